```python
import jax, jax.numpy as jnp
from jax import lax
import numpy as np

D_MODEL = 1024
BATCH = 32
SEQ = 256
DEPTH = 2
DEC_BATCH = 8
DEC_SEQ = 1024
PAST_LEN = 512

GRID_W = 64
GLA_HEADS = 4
GLA_DV = D_MODEL // 8
GLA_DK = GLA_DV // 2
GLA_RANK = 16
GLA_TAU = 16.0
GLA_CHUNK = 32
GMLP_CHUNK = 128
GMLP_CH = 128
GMLP_GROUPS = (D_MODEL // 2) // GMLP_CH
ATTN_HEADS = 8
ATTN_KV_HEADS = 2
ATTN_HEAD_DIM = 64
ATTN_GROUP = ATTN_HEADS // ATTN_KV_HEADS
WINDOW = 128
ATTN_BLOCK = 128
ROPE_BASE = 10000.0
BRANCH_W = D_MODEL // 2
N_BRANCH = 3
N_GROUPS = 4
EXPERTS_PER_GROUP = 4
N_EXPERTS = N_GROUPS * EXPERTS_PER_GROUP
TOP_K_IN_GROUP = 2
EXPERT_FF = D_MODEL // 4
LN_EPS = 1e-5
IN_WIDTHS = (GLA_HEADS * GLA_DK, GLA_HEADS * GLA_DK, GLA_HEADS * GLA_DV, GLA_HEADS * GLA_DV,
             GLA_RANK, GLA_RANK, 2 * GMLP_GROUPS * GMLP_CH,
             ATTN_HEADS * ATTN_HEAD_DIM, ATTN_KV_HEADS * ATTN_HEAD_DIM, ATTN_KV_HEADS * ATTN_HEAD_DIM,
             N_BRANCH * D_MODEL)
IN_COLS = sum(IN_WIDTHS)

kernel_name = "hybrid_gla_gmlp_swa_hmoe_diffusion_step"

F32 = jnp.float32


def split_cols(x, widths):
    out, start = [], 0
    for w in widths:
        out.append(x[..., start:start + w])
        start += w
    return out


def layer_norm(x, g, b):
    xf = x.astype(F32)
    mu = jnp.mean(xf, axis=-1, keepdims=True)
    xc = xf - mu
    var = jnp.mean(xc * xc, axis=-1, keepdims=True)
    return (xc * lax.rsqrt(var + LN_EPS) * g.astype(F32) + b.astype(F32)).astype(x.dtype)


def axial_rope(x, row_pos, col_pos):
    half = x.shape[-1] // 2
    quarter = half // 2
    inv_freq = jnp.power(ROPE_BASE, -jnp.arange(quarter, dtype=F32) / quarter)

    def rotate(xh, pos):
        ang = pos[:, None] * inv_freq[None, :]
        cos = jnp.cos(ang)[None, :, None, :].astype(x.dtype)
        sin = jnp.sin(ang)[None, :, None, :].astype(x.dtype)
        x1, x2 = xh[..., :quarter], xh[..., quarter:]
        return jnp.concatenate([x1 * cos - x2 * sin, x1 * sin + x2 * cos], axis=-1)

    return jnp.concatenate([rotate(x[..., :half], row_pos), rotate(x[..., half:], col_pos)], axis=-1)


def gla_log_decay(lr, w2, b2):
    B, L, _ = lr.shape
    z = (lr @ w2 + b2).astype(F32)
    return (jax.nn.log_sigmoid(z) / GLA_TAU).reshape(B, L, GLA_HEADS, GLA_DK)


def gla_chunked(q, k, v, log_a, s0):
    B, L, H, DK = q.shape
    DV = v.shape[-1]
    C = GLA_CHUNK
    N = L // C
    q = q.astype(F32).reshape(B, N, C, H, DK) * (DK ** -0.5)
    k = k.astype(F32).reshape(B, N, C, H, DK)
    v = v.astype(F32).reshape(B, N, C, H, DV)
    b = jnp.cumsum(log_a.reshape(B, N, C, H, DK), axis=2)
    tri = jnp.tril(jnp.ones((C, C), dtype=bool))
    diff = b[:, :, :, None] - b[:, :, None, :]
    decay = jnp.exp(jnp.where(tri[None, None, :, :, None, None], diff, -jnp.inf))
    scores = jnp.einsum('bnthd,bnshd,bntshd->bnhts', q, k, decay)
    intra = jnp.einsum('bnhts,bnshv->bnthv', scores, v)
    b_last = b[:, :, -1]
    kv_chunk = jnp.einsum('bnshd,bnshv->bnhdv', k * jnp.exp(b_last[:, :, None] - b), v)

    def step(S, inp):
        g, kv = inp
        return jnp.exp(g)[..., None] * S + kv, S

    s_fin, s_in = lax.scan(step, s0.astype(F32),
                           (jnp.moveaxis(b_last, 1, 0), jnp.moveaxis(kv_chunk, 1, 0)))
    s_in = jnp.moveaxis(s_in, 0, 1)
    inter = jnp.einsum('bnthd,bnhdv->bnthv', q * jnp.exp(b), s_in)
    return (intra + inter).reshape(B, L, H, DV), s_fin


def chunk_gmlp(z, ln_g, ln_b, w_s, b_s):
    B, L, _ = z.shape
    u, v = jnp.split(z, 2, axis=-1)
    v = layer_norm(v, ln_g, ln_b)
    v = v.reshape(B, L // GMLP_CHUNK, GMLP_CHUNK, GMLP_GROUPS, GMLP_CH)
    s = jnp.einsum('gts,bnsgc->bntgc', w_s, v) + b_s.T[None, None, :, :, None]
    return u * s.reshape(B, L, GMLP_GROUPS * GMLP_CH)


def context_attention(q, k, v, sink):
    B, S = q.shape[0], q.shape[1]
    nb = S // ATTN_BLOCK
    scale = ATTN_HEAD_DIM ** -0.5
    qb = q.reshape(B, nb, ATTN_BLOCK, ATTN_KV_HEADS, ATTN_GROUP, ATTN_HEAD_DIM).transpose(1, 0, 2, 3, 4, 5)
    sink_col = sink.astype(F32).reshape(ATTN_KV_HEADS, ATTN_GROUP)[None, :, :, None, None]

    def block(qi):
        s = jnp.einsum('bqkgd,bskd->bkgqs', qi, k).astype(F32) * scale
        sink_b = jnp.broadcast_to(sink_col, s.shape[:-1] + (1,))
        p = jax.nn.softmax(jnp.concatenate([s, sink_b], axis=-1), axis=-1)[..., :-1]
        return jnp.einsum('bkgqs,bskd->bqkgd', p.astype(v.dtype), v)

    o = lax.map(block, qb)
    return o.transpose(1, 0, 2, 3, 4, 5).reshape(B, S, ATTN_HEADS * ATTN_HEAD_DIM)


def latent_window_attention(q, k, v, k_ctx, v_ctx, sink):
    B, L = q.shape[0], q.shape[1]
    wb = ATTN_BLOCK
    nb = L // wb
    scale = ATTN_HEAD_DIM ** -0.5
    qb = q.reshape(B, nb, wb, ATTN_KV_HEADS, ATTN_GROUP, ATTN_HEAD_DIM)

    def neighbours(t):
        tb = jnp.pad(t, ((0, 0), (wb, wb), (0, 0), (0, 0))).reshape(
            B, nb + 2, wb, ATTN_KV_HEADS, ATTN_HEAD_DIM)
        return jnp.concatenate([tb[:, :-2], tb[:, 1:-1], tb[:, 2:]], axis=2)

    kn, vn = neighbours(k), neighbours(v)
    s_loc = jnp.einsum('bnqkgd,bnskd->bnkgqs', qb, kn).astype(F32) * scale
    s_ctx = jnp.einsum('bnqkgd,bskd->bnkgqs', qb, k_ctx).astype(F32) * scale
    blk = jnp.arange(nb)[:, None, None]
    q_pos = blk * wb + jnp.arange(wb)[None, :, None]
    k_pos = (blk - 1) * wb + jnp.arange(3 * wb)[None, None, :]
    valid = (jnp.abs(k_pos - q_pos) <= WINDOW) & (k_pos >= 0) & (k_pos < L)
    s_loc = jnp.where(valid[None, :, None, None], s_loc, -jnp.inf)
    sink_col = jnp.broadcast_to(
        sink.astype(F32).reshape(ATTN_KV_HEADS, ATTN_GROUP)[None, None, :, :, None, None],
        s_loc.shape[:-1] + (1,))
    p = jax.nn.softmax(jnp.concatenate([s_loc, s_ctx, sink_col], axis=-1), axis=-1).astype(v.dtype)
    n_loc = 3 * wb
    n_ctx = k_ctx.shape[1]
    o = (jnp.einsum('bnkgqs,bnskd->bnqkgd', p[..., :n_loc], vn)
         + jnp.einsum('bnkgqs,bskd->bnqkgd', p[..., n_loc:n_loc + n_ctx], v_ctx))
    return o.reshape(B, L, ATTN_HEADS * ATTN_HEAD_DIM)


def hier_moe(h, rg_w, rg_b, re_w, re_b, w_gate, w_up, w_down):
    B, L, D = h.shape
    T = B * L
    t = h.reshape(T, D)
    rows = jnp.arange(T)
    g_prob = jax.nn.softmax((t @ rg_w + rg_b).astype(F32), axis=-1)
    g_p, g_i = lax.top_k(g_prob, 1)
    e_logits = (t @ re_w + re_b).astype(F32).reshape(T, N_GROUPS, EXPERTS_PER_GROUP)
    e_prob = jax.nn.softmax(e_logits[rows, g_i[:, 0]], axis=-1)
    e_p, e_i = lax.top_k(e_prob, TOP_K_IN_GROUP)
    w = g_p * e_p / jnp.sum(e_p, axis=-1, keepdims=True)
    ids = g_i * EXPERTS_PER_GROUP + e_i
    combine = jnp.zeros((T, N_EXPERTS), F32).at[rows[:, None], ids].add(w).astype(h.dtype)
    hid = jax.nn.silu(jnp.einsum('td,edf->tef', t, w_gate)) * jnp.einsum('td,edf->tef', t, w_up)
    y = jnp.einsum('tef,efd->td', hid * combine[:, :, None], w_down)
    return y.reshape(B, L, D)


def trunk_layer(x, cond, lp, alpha, ctx):
    B, L, _ = x.shape
    mod = (jax.nn.silu(cond) @ lp["ada_w"] + lp["ada_b"])[:, None, :]
    sh_a, sc_a, gt_a, sh_m, sc_m, gt_m = jnp.split(mod, 6, axis=-1)
    h = x * (1 + sc_a) + sh_a
    aq, ak, av, ag, alr_f, alr_b, bz, cq, ck, cv, mg = split_cols(h @ lp["w_in"], IN_WIDTHS)

    aq = aq.reshape(B, L, GLA_HEADS, GLA_DK)
    ak = ak.reshape(B, L, GLA_HEADS, GLA_DK)
    av = av.reshape(B, L, GLA_HEADS, GLA_DV)
    log_f = gla_log_decay(alr_f, lp["gla_gate_w"][0], lp["gla_gate_b"][0])
    log_b = gla_log_decay(alr_b, lp["gla_gate_w"][1], lp["gla_gate_b"][1])
    if ctx is None:
        s0_f = jnp.zeros((B, GLA_HEADS, GLA_DK, GLA_DV), F32)
        s0_b = s0_f
    else:
        s0_f, s0_b, k_ctx, v_ctx = ctx
    o_f, s_f = gla_chunked(aq, ak, av, log_f, s0_f)
    o_b, s_b = gla_chunked(aq[:, ::-1], ak[:, ::-1], av[:, ::-1], log_b[:, ::-1], s0_b)
    o = o_f + o_b[:, ::-1]
    o = o * lax.rsqrt(jnp.mean(o * o, axis=-1, keepdims=True) + LN_EPS) \
        * lp["gla_norm_g"].astype(F32).reshape(GLA_HEADS, GLA_DV)
    br_a = o.reshape(B, L, GLA_HEADS * GLA_DV).astype(x.dtype) * jax.nn.silu(ag)

    br_b = chunk_gmlp(jax.nn.gelu(bz), lp["gmlp_ln_g"], lp["gmlp_ln_b"], lp["gmlp_ws"], lp["gmlp_bs"])

    cq = cq.reshape(B, L, ATTN_HEADS, ATTN_HEAD_DIM)
    ck = ck.reshape(B, L, ATTN_KV_HEADS, ATTN_HEAD_DIM)
    cv = cv.reshape(B, L, ATTN_KV_HEADS, ATTN_HEAD_DIM)
    if ctx is None:
        br_c = context_attention(cq, ck, cv, lp["attn_sink"])
    else:
        n_rows = L // GRID_W
        row_pos = jnp.repeat(jnp.arange(n_rows, dtype=F32), GRID_W)
        col_pos = (jnp.arange(n_rows * GRID_W) % GRID_W).astype(F32)
        br_c = latent_window_attention(axial_rope(cq, row_pos, col_pos), axial_rope(ck, row_pos, col_pos),
                                       cv, k_ctx, v_ctx, lp["attn_sink"])

    branches = jnp.stack([br_a, br_b, br_c], axis=2)
    proj = jnp.einsum('blnc,ncd->blnd', branches, lp["w_branch"])
    gates = jax.nn.sigmoid(mg).reshape(B, L, N_BRANCH, D_MODEL)
    y = jnp.sum(gates * proj, axis=2) @ lp["w_out"]
    x = layer_norm(alpha * x + gt_a * y, lp["ln1_g"], lp["ln1_b"])

    h2 = x * (1 + sc_m) + sh_m
    y2 = hier_moe(h2, lp["router_group_w"], lp["router_group_b"], lp["router_expert_w"],
                  lp["router_expert_b"], lp["expert_w_gate"], lp["expert_w_up"], lp["expert_w_down"])
    x = layer_norm(alpha * x + gt_m * y2, lp["ln2_g"], lp["ln2_b"])
    if ctx is None:
        return x, (s_f.astype(x.dtype), s_b.astype(x.dtype), ck, cv)
    return x, None


def setup_inputs(seed: int = 0) -> dict:
    key = jax.random.key(seed)
    ks = jax.random.split(key, 40)
    D = D_MODEL
    beta = (8.0 * DEPTH) ** -0.25

    def nrm(k, shape, scale):
        return jax.random.normal(k, shape, F32) * scale

    return {
        "x_prompt": nrm(ks[0], (BATCH, SEQ, D), 1.0),
        "x_sample": nrm(ks[1], (DEC_BATCH, DEC_SEQ, D), 1.0),
        "state_gla": nrm(ks[2], (DEC_BATCH, DEPTH, 2, GLA_HEADS, GLA_DK, GLA_DV), 1.0),
        "cache_k": nrm(ks[3], (DEC_BATCH, DEPTH, PAST_LEN, ATTN_KV_HEADS, ATTN_HEAD_DIM), 1.0),
        "cache_v": nrm(ks[4], (DEC_BATCH, DEPTH, PAST_LEN, ATTN_KV_HEADS, ATTN_HEAD_DIM), 1.0),
        "c": nrm(ks[5], (DEC_BATCH, D), 1.0),
        "c_ctx": nrm(ks[6], (D,), 1.0),
        "ada_w": nrm(ks[7], (DEPTH, D, 6 * D), D ** -0.5),
        "ada_b": nrm(ks[8], (DEPTH, 6 * D), 0.02),
        "w_in": nrm(ks[9], (DEPTH, D, IN_COLS), D ** -0.5),
        "gla_gate_w": nrm(ks[10], (DEPTH, 2, GLA_RANK, GLA_HEADS * GLA_DK), GLA_RANK ** -0.5),
        "gla_gate_b": nrm(ks[11], (DEPTH, 2, GLA_HEADS * GLA_DK), 0.1),
        "gla_norm_g": 1.0 + nrm(ks[12], (DEPTH, GLA_HEADS * GLA_DV), 0.02),
        "gmlp_ln_g": 1.0 + nrm(ks[13], (DEPTH, GMLP_GROUPS * GMLP_CH), 0.02),
        "gmlp_ln_b": nrm(ks[14], (DEPTH, GMLP_GROUPS * GMLP_CH), 0.02),
        "gmlp_ws": nrm(ks[15], (DEPTH, GMLP_GROUPS, GMLP_CHUNK, GMLP_CHUNK), GMLP_CHUNK ** -0.5),
        "gmlp_bs": 1.0 + nrm(ks[16], (DEPTH, GMLP_GROUPS, GMLP_CHUNK), 0.02),
        "attn_sink": nrm(ks[17], (DEPTH, ATTN_HEADS), 1.0),
        "w_branch": nrm(ks[18], (DEPTH, N_BRANCH, BRANCH_W, D), BRANCH_W ** -0.5 * beta),
        "w_out": nrm(ks[19], (DEPTH, D, D), D ** -0.5 * beta),
        "ln1_g": 1.0 + nrm(ks[20], (DEPTH, D), 0.02),
        "ln1_b": nrm(ks[21], (DEPTH, D), 0.02),
        "ln2_g": 1.0 + nrm(ks[22], (DEPTH, D), 0.02),
        "ln2_b": nrm(ks[23], (DEPTH, D), 0.02),
        "router_group_w": nrm(ks[24], (DEPTH, D, N_GROUPS), D ** -0.5),
        "router_group_b": nrm(ks[25], (DEPTH, N_GROUPS), 0.01),
        "router_expert_w": nrm(ks[26], (DEPTH, D, N_EXPERTS), D ** -0.5),
        "router_expert_b": nrm(ks[27], (DEPTH, N_EXPERTS), 0.01),
        "expert_w_gate": nrm(ks[28], (DEPTH, N_EXPERTS, D, EXPERT_FF), D ** -0.5),
        "expert_w_up": nrm(ks[29], (DEPTH, N_EXPERTS, D, EXPERT_FF), D ** -0.5 * beta),
        "expert_w_down": nrm(ks[30], (DEPTH, N_EXPERTS, EXPERT_FF, D), EXPERT_FF ** -0.5 * beta),
    }


def reference(x_prompt, x_sample, state_gla, cache_k, cache_v, c, c_ctx, ada_w, ada_b, w_in,
              gla_gate_w, gla_gate_b, gla_norm_g, gmlp_ln_g, gmlp_ln_b, gmlp_ws, gmlp_bs, attn_sink,
              w_branch, w_out, ln1_g, ln1_b, ln2_g, ln2_b, router_group_w, router_group_b,
              router_expert_w, router_expert_b, expert_w_gate, expert_w_up, expert_w_down):
    alpha = (2.0 * DEPTH) ** 0.25
    xp, xs = x_prompt, x_sample
    states, keys, values = [], [], []
    for l in range(DEPTH):
        lp = {
            "ada_w": ada_w[l], "ada_b": ada_b[l], "w_in": w_in[l],
            "gla_gate_w": gla_gate_w[l], "gla_gate_b": gla_gate_b[l], "gla_norm_g": gla_norm_g[l],
            "gmlp_ln_g": gmlp_ln_g[l], "gmlp_ln_b": gmlp_ln_b[l], "gmlp_ws": gmlp_ws[l],
            "gmlp_bs": gmlp_bs[l], "attn_sink": attn_sink[l], "w_branch": w_branch[l],
            "w_out": w_out[l], "ln1_g": ln1_g[l], "ln1_b": ln1_b[l], "ln2_g": ln2_g[l],
            "ln2_b": ln2_b[l], "router_group_w": router_group_w[l], "router_group_b": router_group_b[l],
            "router_expert_w": router_expert_w[l], "router_expert_b": router_expert_b[l],
            "expert_w_gate": expert_w_gate[l], "expert_w_up": expert_w_up[l],
            "expert_w_down": expert_w_down[l],
        }
        xp, (s_f, s_b, k_l, v_l) = trunk_layer(xp, c_ctx[None, :], lp, alpha, None)
        states.append(jnp.stack([s_f, s_b], axis=1))
        keys.append(k_l)
        values.append(v_l)
        xs, _ = trunk_layer(xs, c, lp, alpha,
                            (state_gla[:, l, 0], state_gla[:, l, 1], cache_k[:, l], cache_v[:, l]))
    new_state_gla = jnp.stack(states, axis=1)
    new_cache_k = jnp.stack(keys, axis=1)
    new_cache_v = jnp.stack(values, axis=1)
    return (xp, xs, new_state_gla, new_cache_k, new_cache_v)
```

```python
import functools

import jax
import jax.numpy as jnp
from jax import lax
from jax.experimental import pallas as pl
from jax.experimental.pallas import tpu as pltpu

F32 = jnp.float32
BF16 = jnp.bfloat16

D_MODEL = 1024
BATCH = 32
SEQ = 256
DEPTH = 2
DEC_BATCH = 8
DEC_SEQ = 1024
PAST_LEN = 512
GRID_W = 64
GLA_HEADS = 4
GLA_DV = 128
GLA_DK = 64
GLA_RANK = 16
GLA_TAU = 16.0
GLA_CHUNK = 32
GMLP_CHUNK = 128
GMLP_CH = 128
GMLP_GROUPS = 4
ATTN_HEADS = 8
ATTN_KV_HEADS = 2
ATTN_HEAD_DIM = 64
WINDOW = 128
ATTN_BLOCK = 128
ROPE_BASE = 10000.0
BRANCH_W = 512
N_BRANCH = 3
N_GROUPS = 4
EXPERTS_PER_GROUP = 4
N_EXPERTS = 16
EXPERT_FF = 256
LN_EPS = 1e-5

LANES = 128
TM = 256
CTX_TOKENS = BATCH * SEQ
LAT_TOKENS = DEC_BATCH * DEC_SEQ
TOKENS = CTX_TOKENS + LAT_TOKENS
NB_CTX = CTX_TOKENS // TM
NB_LAT = LAT_TOKENS // TM
NB = NB_CTX + NB_LAT
LAT_BLOCKS_PER_SEQ = DEC_SEQ // TM
N_SEQ = BATCH + DEC_BATCH
MOD_ROWS = 16
GLA_SUB = 128
CHUNKS_PER_SUB = GLA_SUB // GLA_CHUNK
MOE_TM = 1024

C_AQ = 0
C_AK = C_AQ + 256
C_AV = C_AK + 256
C_AG = C_AV + 512
C_LR = C_AG + 512
C_BZ = C_LR + LANES
C_CQ = C_BZ + 1024
C_CK = C_CQ + 512
C_CV = C_CK + 256
C_MG = C_CV + 256
IN_COLS_P = C_MG + 3072

VMEM_LIMIT = 56 * 1024 * 1024


def _cp(sem):
    return pltpu.CompilerParams(dimension_semantics=sem, vmem_limit_bytes=VMEM_LIMIT)


def _dot(a, b):
    return jnp.dot(a, b, preferred_element_type=F32)


def _dot_nt(a, b):
    return lax.dot_general(a, b, (((1,), (1,)), ((), ())), preferred_element_type=F32)


def _dot_tn(a, b):
    return lax.dot_general(a, b, (((0,), (0,)), ((), ())), preferred_element_type=F32)


def _split(x):
    hi = x.astype(BF16)
    lo = (x - hi.astype(F32)).astype(BF16)
    return hi, lo


def _sigmoid(x):
    return 1.0 / (1.0 + jnp.exp(-x))


def _silu(x):
    return x * _sigmoid(x)


def _layer_norm(x, g, b):
    mu = jnp.mean(x, axis=-1, keepdims=True)
    xc = x - mu
    var = jnp.mean(xc * xc, axis=-1, keepdims=True)
    return xc * lax.rsqrt(var + LN_EPS) * g + b


def _mod_block(i):
    return jnp.where(i < NB_CTX, 0, 1 + (i - NB_CTX) // LAT_BLOCKS_PER_SEQ)


def _mod_kernel(cond_ref, w_ref, b_ref, o_ref):
    s_hi, s_lo = _split(_silu(cond_ref[...]))
    w_hi, w_lo = _split(w_ref[...])
    o_ref[...] = _dot(s_hi, w_hi) + _dot(s_lo, w_hi) + _dot(s_hi, w_lo) + b_ref[...]


def _modulation(cond, ada_w, ada_b):
    tn = 1536
    return pl.pallas_call(
        _mod_kernel,
        grid=(6 * D_MODEL // tn,),
        in_specs=[pl.BlockSpec((MOD_ROWS, D_MODEL), lambda j: (0, 0)),
                  pl.BlockSpec((D_MODEL, tn), lambda j: (0, j)),
                  pl.BlockSpec((1, tn), lambda j: (0, j))],
        out_specs=pl.BlockSpec((MOD_ROWS, tn), lambda j: (0, j)),
        out_shape=jax.ShapeDtypeStruct((MOD_ROWS, 6 * D_MODEL), F32),
        compiler_params=_cp(("arbitrary",)),
        name="modulation",
    )(cond, ada_w, ada_b)


def _rope(x, cos, sin):
    n = x.shape[1]
    lane = lax.broadcasted_iota(jnp.int32, x.shape, 1)
    partner = jnp.where((lane & 31) < 16, pltpu.roll(x, n - 16, 1), pltpu.roll(x, 16, 1))
    return x * cos + partner * sin


def _inproj_kernel(x_ref, mod_ref, cos_ref, sin_ref, w_ref,
                   aq_ref, ak_ref, av_ref, ag_ref, lr_ref, bz_ref,
                   cq_ref, ckd_ref, cvd_ref, ck_ref, cv_ref, mg_ref):
    m = mod_ref[0]
    h = (x_ref[...] * (1.0 + m[:, D_MODEL:2 * D_MODEL]) + m[:, 0:D_MODEL]).astype(BF16)

    def seg(c0, width):
        return _dot(h, w_ref[:, c0:c0 + width])

    aq_ref[...] = seg(C_AQ, 256)
    ak_ref[...] = seg(C_AK, 256)
    av_ref[...] = seg(C_AV, 512).astype(BF16)
    ag_ref[...] = seg(C_AG, 512)
    lr_ref[...] = seg(C_LR, LANES)
    bz_ref[...] = seg(C_BZ, 1024)
    mg_ref[...] = seg(C_MG, 3072)

    cos = cos_ref[...]
    sin = sin_ref[...]
    cq_ref[...] = _rope(seg(C_CQ, 512), cos, sin).astype(BF16)
    ckd = seg(C_CK, 256)
    cvd = seg(C_CV, 256)
    lane = lax.broadcasted_iota(jnp.int32, (TM, LANES), 1)
    ck_ref[...] = jnp.where(lane < 64, ckd[:, :LANES], ckd[:, LANES:])
    cv_ref[...] = jnp.where(lane < 64, cvd[:, :LANES], cvd[:, LANES:])
    ckd_ref[...] = _rope(ckd, cos[:, :256], sin[:, :256]).astype(BF16)
    cvd_ref[...] = cvd.astype(BF16)


def _rope_block(i):
    return jnp.where(i < NB_CTX, 0, 1 + (i - NB_CTX) % LAT_BLOCKS_PER_SEQ)


def _in_projection(x, mod3, cos_t, sin_t, w_in_p):
    def tok(width):
        return pl.BlockSpec((TM, width), lambda i: (i, 0))

    widths = [(256, F32), (256, F32), (512, BF16), (512, F32), (LANES, F32), (1024, F32),
              (512, BF16), (256, BF16), (256, BF16), (LANES, F32), (LANES, F32), (3072, F32)]
    return pl.pallas_call(
        _inproj_kernel,
        grid=(NB,),
        in_specs=[tok(D_MODEL),
                  pl.BlockSpec((1, 1, 6 * D_MODEL), lambda i: (_mod_block(i), 0, 0)),
                  pl.BlockSpec((TM, 512), lambda i: (_rope_block(i), 0)),
                  pl.BlockSpec((TM, 512), lambda i: (_rope_block(i), 0)),
                  pl.BlockSpec((D_MODEL, IN_COLS_P), lambda i: (0, 0))],
        out_specs=[tok(w) for w, _ in widths],
        out_shape=[jax.ShapeDtypeStruct((TOKENS, w), dt) for w, dt in widths],
        compiler_params=_cp(("arbitrary",)),
        name="in_projection",
    )(x, mod3, cos_t, sin_t, w_in_p)


def _gla_direction(q, k, v, g, state_ref, reverse):
    r = lax.broadcasted_iota(jnp.int32, (GLA_SUB, GLA_SUB), 0)
    c = lax.broadcasted_iota(jnp.int32, (GLA_SUB, GLA_SUB), 1)
    same = (r >> 5) == (c >> 5)
    tri = (same & ((c >= r) if reverse else (c <= r)))
    tri_b = jnp.where(tri, 1.0, 0.0).astype(BF16)
    ones_b = jnp.where(same, 1.0, 0.0).astype(BF16)
    g_hi, g_lo = _split(g)
    b = _dot(tri_b, g_hi) + _dot(tri_b, g_lo)
    bl = _dot(ones_b, g_hi) + _dot(ones_b, g_lo)
    qe = q * jnp.exp(b) * (GLA_DK ** -0.5)
    ke = (k * jnp.exp(-b)).astype(BF16)
    kl = k * jnp.exp(bl - b)
    ind = jnp.where((lax.broadcasted_iota(jnp.int32, (GLA_SUB, LANES), 0) >> 5)
                    == lax.broadcasted_iota(jnp.int32, (GLA_SUB, LANES), 1), 1.0, 0.0).astype(BF16)
    dcol = jnp.exp(_dot_tn(g_hi, ind) + _dot_tn(g_lo, ind))

    lane = lax.broadcasted_iota(jnp.int32, (GLA_SUB, LANES), 1)
    lo_half = lane < 64
    row_chunk = lax.broadcasted_iota(jnp.int32, (GLA_SUB, CHUNKS_PER_SUB * LANES), 0) >> 5
    col_chunk = lax.broadcasted_iota(jnp.int32, (GLA_SUB, CHUNKS_PER_SUB * LANES), 1) >> 7
    in_chunk = row_chunk == col_chunk
    blk_r = lax.broadcasted_iota(jnp.int32, (LANES, 2 * GLA_DV), 0) >> 6
    blk_c = lax.broadcasted_iota(jnp.int32, (LANES, 2 * GLA_DV), 1) >> 7
    head_diag = blk_r == blk_c

    outs = []
    for p in range(GLA_HEADS // 2):
        qp = qe[:, p * LANES:(p + 1) * LANES]
        kep = ke[:, p * LANES:(p + 1) * LANES]
        klp = kl[:, p * LANES:(p + 1) * LANES]
        vp = v[:, p * 2 * GLA_DV:(p + 1) * 2 * GLA_DV]
        halves = []
        for hh in range(2):
            qh = jnp.where(lo_half if hh == 0 else ~lo_half, qp, 0.0).astype(BF16)
            s = jnp.where(tri, _dot_nt(qh, kep), 0.0).astype(BF16)
            halves.append(_dot(s, vp[:, hh * GLA_DV:(hh + 1) * GLA_DV]))
        k_exp = jnp.where(in_chunk, jnp.concatenate([klp] * CHUNKS_PER_SUB, axis=1), 0.0).astype(BF16)
        q_exp = jnp.where(in_chunk, jnp.concatenate([qp] * CHUNKS_PER_SUB, axis=1), 0.0).astype(BF16)
        kv = _dot_tn(k_exp, vp)
        s_cur = state_ref[p]
        entering = [None] * CHUNKS_PER_SUB
        order = range(CHUNKS_PER_SUB - 1, -1, -1) if reverse else range(CHUNKS_PER_SUB)
        for n in order:
            entering[n] = s_cur
            dec = dcol[p * LANES:(p + 1) * LANES, n:n + 1]
            kv_n = jnp.where(head_diag, kv[n * LANES:(n + 1) * LANES], 0.0)
            s_cur = dec * s_cur + kv_n
        state_ref[p] = s_cur
        s_stack = jnp.concatenate(entering, axis=0).astype(BF16)
        inter = _dot(q_exp, s_stack)
        outs.append(halves[0] + inter[:, :GLA_DV])
        outs.append(halves[1] + inter[:, GLA_DV:])
    return jnp.concatenate(outs, axis=1)


def _gla_kernel(qf_ref, kf_ref, vf_ref, lrf_ref, qb_ref, kb_ref, vb_ref, lrb_ref,
                gw_ref, gb_ref, s0_ref, of_ref, ob_ref, sfin_ref, state_ref):
    i = pl.program_id(0)
    first = jnp.logical_or(i < NB_CTX, (i - NB_CTX) % LAT_BLOCKS_PER_SEQ == 0)
    last = jnp.logical_or(i < NB_CTX, (i - NB_CTX) % LAT_BLOCKS_PER_SEQ == LAT_BLOCKS_PER_SEQ - 1)

    @pl.when(first)
    def _():
        is_lat = i >= NB_CTX
        zero = jnp.zeros((GLA_DK, GLA_DV), F32)
        for d in range(2):
            for p in range(2):
                a = jnp.where(is_lat, s0_ref[0, d, 2 * p], zero)
                bq = jnp.where(is_lat, s0_ref[0, d, 2 * p + 1], zero)
                state_ref[d, p] = jnp.concatenate(
                    [jnp.concatenate([a, zero], axis=1), jnp.concatenate([zero, bq], axis=1)], axis=0)

    def log_decay(lr_ref, d, r0):
        lr_hi, lr_lo = _split(lr_ref[r0:r0 + GLA_SUB, :])
        w_hi, w_lo = _split(gw_ref[d])
        z = _dot(lr_hi, w_hi) + _dot(lr_lo, w_hi) + _dot(lr_hi, w_lo) + gb_ref[d]
        return (jnp.minimum(z, 0.0) - jnp.log1p(jnp.exp(-jnp.abs(z)))) / GLA_TAU

    n_sub = TM // GLA_SUB
    for j in range(n_sub):
        r0 = j * GLA_SUB
        of_ref[r0:r0 + GLA_SUB, :] = _gla_direction(
            qf_ref[r0:r0 + GLA_SUB, :], kf_ref[r0:r0 + GLA_SUB, :], vf_ref[r0:r0 + GLA_SUB, :],
            log_decay(lrf_ref, 0, r0), state_ref.at[0], False)
        r1 = (n_sub - 1 - j) * GLA_SUB
        ob_ref[r1:r1 + GLA_SUB, :] = _gla_direction(
            qb_ref[r1:r1 + GLA_SUB, :], kb_ref[r1:r1 + GLA_SUB, :], vb_ref[r1:r1 + GLA_SUB, :],
            log_decay(lrb_ref, 1, r1), state_ref.at[1], True)

    @pl.when(last)
    def _():
        for d in range(2):
            for p in range(2):
                sp = state_ref[d, p]
                sfin_ref[0, d, 2 * p] = sp[:GLA_DK, :GLA_DV]
                sfin_ref[0, d, 2 * p + 1] = sp[GLA_DK:, GLA_DV:]


def _bwd_block(i):
    k = (i - NB_CTX) % LAT_BLOCKS_PER_SEQ
    return jnp.where(i < NB_CTX, i, i - k + (LAT_BLOCKS_PER_SEQ - 1 - k))


def _seq_of_block(i):
    return jnp.where(i < NB_CTX, i, BATCH + (i - NB_CTX) // LAT_BLOCKS_PER_SEQ)


def _gla(aq, ak, av, lr, gw_p, gb_p, s0):
    def fwd(width):
        return pl.BlockSpec((TM, width), lambda i: (i, 0))

    def bwd(width):
        return pl.BlockSpec((TM, width), lambda i: (_bwd_block(i), 0))

    state_blk = (1, 2, GLA_HEADS, GLA_DK, GLA_DV)
    return pl.pallas_call(
        _gla_kernel,
        grid=(NB,),
        in_specs=[fwd(256), fwd(256), fwd(512), fwd(LANES), bwd(256), bwd(256), bwd(512), bwd(LANES),
                  pl.BlockSpec((2, LANES, 256), lambda i: (0, 0, 0)),
                  pl.BlockSpec((2, 1, 256), lambda i: (0, 0, 0)),
                  pl.BlockSpec(state_blk, lambda i: (jnp.maximum(_seq_of_block(i) - BATCH, 0), 0, 0, 0, 0))],
        out_specs=[fwd(512), bwd(512),
                   pl.BlockSpec(state_blk, lambda i: (_seq_of_block(i), 0, 0, 0, 0))],
        out_shape=[jax.ShapeDtypeStruct((TOKENS, 512), F32), jax.ShapeDtypeStruct((TOKENS, 512), F32),
                   jax.ShapeDtypeStruct((N_SEQ, 2, GLA_HEADS, GLA_DK, GLA_DV), F32)],
        scratch_shapes=[pltpu.VMEM((2, 2, LANES, 2 * GLA_DV), F32)],
        compiler_params=_cp(("arbitrary",)),
        name="gla_scan",
    )(aq, ak, av, lr, aq, ak, av, lr, gw_p, gb_p, s0)


def _attend(q, keys, values, masks, sink_row):
    rows = q.shape[0]
    lane = lax.broadcasted_iota(jnp.int32, (rows, LANES), 1)
    q32 = q.astype(F32)
    scale = ATTN_HEAD_DIM ** -0.5
    outs = []
    for hh in range(2):
        qh = jnp.where((lane < 64) if hh == 0 else (lane >= 64), q32, 0.0).astype(BF16)
        ss = []
        for kk, mk in zip(keys, masks):
            s = _dot_nt(qh, kk) * scale
            if mk is not None:
                s = jnp.where(mk, s, -jnp.inf)
            ss.append(s)
        m = sink_row[hh]
        for s in ss:
            m = jnp.maximum(m, jnp.max(s, axis=-1, keepdims=True))
        den = jnp.exp(sink_row[hh] - m)
        acc = jnp.zeros((rows, LANES), F32)
        for s, vv in zip(ss, values):
            e = jnp.exp(s - m)
            den = den + jnp.sum(e, axis=-1, keepdims=True)
            acc = acc + _dot(e.astype(BF16), vv)
        outs.append(acc / den)
    return jnp.where(lane < 64, outs[0], outs[1])


def _ctx_attn_kernel(q_ref, kd_ref, vd_ref, sink_ref, o_ref):
    for p in range(ATTN_HEADS // 2):
        g = p // 2
        sink_row = [sink_ref[2 * p][:, :1], sink_ref[2 * p + 1][:, :1]]
        o_ref[:, p * LANES:(p + 1) * LANES] = _attend(
            q_ref[:, p * LANES:(p + 1) * LANES],
            [kd_ref[:, g * LANES:(g + 1) * LANES]], [vd_ref[:, g * LANES:(g + 1) * LANES]],
            [None], sink_row).astype(BF16)


def _context_attention(cq, ckd, cvd, sink_t):
    return pl.pallas_call(
        _ctx_attn_kernel,
        grid=(BATCH,),
        in_specs=[pl.BlockSpec((SEQ, 512), lambda b: (b, 0)),
                  pl.BlockSpec((SEQ, 256), lambda b: (b, 0)),
                  pl.BlockSpec((SEQ, 256), lambda b: (b, 0)),
                  pl.BlockSpec((ATTN_HEADS, 1, LANES), lambda b: (0, 0, 0))],
        out_specs=pl.BlockSpec((SEQ, 512), lambda b: (b, 0)),
        out_shape=jax.ShapeDtypeStruct((CTX_TOKENS, 512), BF16),
        compiler_params=_cp(("arbitrary",)),
        name="context_attention",
    )(cq, ckd, cvd, sink_t)


def _lat_attn_kernel(q_ref, kd_ref, vd_ref, kc_ref, vc_ref, sink_ref, o_ref):
    n = pl.program_id(1)
    nq = DEC_SEQ // ATTN_BLOCK
    prev0 = pl.multiple_of(jnp.maximum(n - 1, 0) * ATTN_BLOCK, ATTN_BLOCK)
    cur0 = pl.multiple_of(n * ATTN_BLOCK, ATTN_BLOCK)
    next0 = pl.multiple_of(jnp.minimum(n + 1, nq - 1) * ATTN_BLOCK, ATTN_BLOCK)
    qi = lax.broadcasted_iota(jnp.int32, (ATTN_BLOCK, ATTN_BLOCK), 0)
    kj = lax.broadcasted_iota(jnp.int32, (ATTN_BLOCK, ATTN_BLOCK), 1)
    m_prev = jnp.logical_and(kj >= qi, n > 0)
    m_next = jnp.logical_and(kj <= qi, n < nq - 1)
    for p in range(ATTN_HEADS // 2):
        g = p // 2
        gl = slice(g * LANES, (g + 1) * LANES)
        sink_row = [sink_ref[2 * p][:, :1], sink_ref[2 * p + 1][:, :1]]
        keys = [kd_ref[pl.ds(prev0, ATTN_BLOCK), gl], kd_ref[pl.ds(cur0, ATTN_BLOCK), gl],
                kd_ref[pl.ds(next0, ATTN_BLOCK), gl], kc_ref[0, :, gl]]
        vals = [vd_ref[pl.ds(prev0, ATTN_BLOCK), gl], vd_ref[pl.ds(cur0, ATTN_BLOCK), gl],
                vd_ref[pl.ds(next0, ATTN_BLOCK), gl], vc_ref[0, :, gl]]
        o_ref[:, p * LANES:(p + 1) * LANES] = _attend(
            q_ref[:, p * LANES:(p + 1) * LANES], keys, vals,
            [m_prev, None, m_next, None], sink_row).astype(BF16)


def _latent_attention(cq, ckd, cvd, kc, vc, sink_t):
    nq = DEC_SEQ // ATTN_BLOCK
    q0 = CTX_TOKENS // ATTN_BLOCK
    s0 = CTX_TOKENS // DEC_SEQ
    return pl.pallas_call(
        _lat_attn_kernel,
        grid=(DEC_BATCH, nq),
        in_specs=[pl.BlockSpec((ATTN_BLOCK, 512), lambda b, n: (q0 + b * nq + n, 0)),
                  pl.BlockSpec((DEC_SEQ, 256), lambda b, n: (s0 + b, 0)),
                  pl.BlockSpec((DEC_SEQ, 256), lambda b, n: (s0 + b, 0)),
                  pl.BlockSpec((1, PAST_LEN, 256), lambda b, n: (b, 0, 0)),
                  pl.BlockSpec((1, PAST_LEN, 256), lambda b, n: (b, 0, 0)),
                  pl.BlockSpec((ATTN_HEADS, 1, LANES), lambda b, n: (0, 0, 0))],
        out_specs=pl.BlockSpec((ATTN_BLOCK, 512), lambda b, n: (b * nq + n, 0)),
        out_shape=jax.ShapeDtypeStruct((LAT_TOKENS, 512), BF16),
        compiler_params=_cp(("arbitrary", "arbitrary")),
        name="latent_attention",
    )(cq, ckd, cvd, kc, vc, sink_t)


def _gelu(x):
    return 0.5 * x * (1.0 + jnp.tanh(0.7978845608028654 * (x + 0.044715 * (x * x * x))))


def _merge_kernel(alpha, x_ref, mod_ref, of_ref, ob_ref, ag_ref, ng_ref, bz_ref, lg_ref, lb_ref,
                  ws_ref, bs_ref, cc_ref, cl_ref, mg_ref, wb_ref, wo_ref, g1_ref, b1_ref,
                  rwh_ref, rwl_ref, rb_ref, x1_ref, h2_ref, cw_ref):
    i = pl.program_id(0)
    m = mod_ref[0]
    o = of_ref[...] + ob_ref[...]
    parts = []
    for h in range(GLA_HEADS):
        oh = o[:, h * GLA_DV:(h + 1) * GLA_DV]
        parts.append(oh * lax.rsqrt(jnp.mean(oh * oh, axis=-1, keepdims=True) + LN_EPS))
    br_a = (jnp.concatenate(parts, axis=1) * ng_ref[...] * _silu(ag_ref[...])).astype(BF16)
    z = _gelu(bz_ref[...])
    u = z[:, :BRANCH_W]
    v = _layer_norm(z[:, BRANCH_W:], lg_ref[...], lb_ref[...]).astype(BF16)
    rows = []
    for c in range(TM // GMLP_CHUNK):
        cols = []
        for g in range(GMLP_GROUPS):
            vv = v[c * GMLP_CHUNK:(c + 1) * GMLP_CHUNK, g * GMLP_CH:(g + 1) * GMLP_CH]
            cols.append(_dot(ws_ref[g], vv))
        rows.append(jnp.concatenate(cols, axis=1) + bs_ref[...])
    br_b = (u * jnp.concatenate(rows, axis=0)).astype(BF16)
    br_c = jnp.where(i < NB_CTX, cc_ref[...], cl_ref[...])
    mg = mg_ref[...]
    y = jnp.zeros((TM, D_MODEL), F32)
    for nbr, br in enumerate((br_a, br_b, br_c)):
        y = y + _sigmoid(mg[:, nbr * D_MODEL:(nbr + 1) * D_MODEL]) * _dot(br, wb_ref[nbr])
    y = _dot(y.astype(BF16), wo_ref[...])
    x1 = _layer_norm(alpha * x_ref[...] + m[:, 2 * D_MODEL:3 * D_MODEL] * y, g1_ref[...], b1_ref[...])
    x1_ref[...] = x1
    h2 = x1 * (1.0 + m[:, 4 * D_MODEL:5 * D_MODEL]) + m[:, 3 * D_MODEL:4 * D_MODEL]
    h2_ref[...] = h2.astype(BF16)
    h_hi, h_lo = _split(h2)
    logits = _dot(h_hi, rwh_ref[...]) + _dot(h_lo, rwh_ref[...]) + _dot(h_hi, rwl_ref[...]) + rb_ref[...]
    lane_i = lax.broadcasted_iota(jnp.int32, (TM, LANES), 1)
    lane = lane_i.astype(F32)
    lane_group = (lane_i >> 2).astype(F32)
    big = float(LANES)
    neg = -jnp.inf
    gl = jnp.where((lane_i >= N_EXPERTS) & (lane_i < N_EXPERTS + N_GROUPS), logits, neg)
    gmax = jnp.max(gl, axis=-1, keepdims=True)
    gsum = jnp.sum(jnp.exp(gl - gmax), axis=-1, keepdims=True)
    g_p = 1.0 / gsum
    g_i = jnp.min(jnp.where(gl == gmax, lane, big), axis=-1, keepdims=True) - float(N_EXPERTS)
    in_group = (lane_i < N_EXPERTS) & (lane_group == g_i)
    el = jnp.where(in_group, logits, neg)
    emax = jnp.max(el, axis=-1, keepdims=True)
    ee = jnp.exp(el - emax)
    e_prob = ee / jnp.sum(ee, axis=-1, keepdims=True)
    p1 = jnp.max(jnp.where(in_group, e_prob, neg), axis=-1, keepdims=True)
    i1 = jnp.min(jnp.where(in_group & (e_prob == p1), lane, big), axis=-1, keepdims=True)
    rest = in_group & (lane != i1)
    p2 = jnp.max(jnp.where(rest, e_prob, neg), axis=-1, keepdims=True)
    i2 = jnp.min(jnp.where(rest & (e_prob == p2), lane, big), axis=-1, keepdims=True)
    tot = p1 + p2
    cw_ref[...] = (jnp.where(lane == i1, g_p * p1 / tot, 0.0)
                   + jnp.where(lane == i2, g_p * p2 / tot, 0.0))


def _merge(alpha, x, mod3, o_f, o_b, ag, norm_g, bz, ln_g, ln_b, ws, bs_t, brc_ctx, brc_lat, mg,
           wb, wo, g1, b1, rw_hi, rw_lo, rb):
    def tok(width):
        return pl.BlockSpec((TM, width), lambda i: (i, 0))

    def const(shape):
        return pl.BlockSpec(shape, lambda i: (0,) * len(shape))

    return pl.pallas_call(
        functools.partial(_merge_kernel, alpha),
        grid=(NB,),
        in_specs=[tok(D_MODEL),
                  pl.BlockSpec((1, 1, 6 * D_MODEL), lambda i: (_mod_block(i), 0, 0)),
                  tok(512), tok(512), tok(512), const((1, 512)),
                  tok(1024), const((1, 512)), const((1, 512)),
                  const((GMLP_GROUPS, GMLP_CHUNK, GMLP_CHUNK)), const((GMLP_CHUNK, 512)),
                  pl.BlockSpec((TM, 512), lambda i: (jnp.minimum(i, NB_CTX - 1), 0)),
                  pl.BlockSpec((TM, 512), lambda i: (jnp.maximum(i - NB_CTX, 0), 0)),
                  tok(3072), const((N_BRANCH, BRANCH_W, D_MODEL)), const((D_MODEL, D_MODEL)),
                  const((1, D_MODEL)), const((1, D_MODEL)),
                  const((D_MODEL, LANES)), const((D_MODEL, LANES)), const((1, LANES))],
        out_specs=[tok(D_MODEL), tok(D_MODEL), tok(LANES)],
        out_shape=[jax.ShapeDtypeStruct((TOKENS, D_MODEL), F32),
                   jax.ShapeDtypeStruct((TOKENS, D_MODEL), BF16),
                   jax.ShapeDtypeStruct((TOKENS, LANES), F32)],
        compiler_params=_cp(("arbitrary",)),
        name="merge_ln1_router",
    )(x, mod3, o_f, o_b, ag, norm_g, bz, ln_g, ln_b, ws, bs_t, brc_ctx, brc_lat, mg,
      wb, wo, g1, b1, rw_hi, rw_lo, rb)


def _moe_kernel(alpha, h2_ref, cw_ref, wgu_ref, wd_ref, x1_ref, mod_ref, g2_ref, b2_ref, o_ref, acc_ref):
    e = pl.program_id(1)

    @pl.when(e == 0)
    def _():
        acc_ref[...] = jnp.zeros_like(acc_ref)

    gu = _dot(h2_ref[...], wgu_ref[0])
    lane = lax.broadcasted_iota(jnp.int32, (MOE_TM, LANES), 1)
    w_e = jnp.sum(jnp.where(lane == e, cw_ref[...], 0.0), axis=-1, keepdims=True)
    hid = _silu(gu[:, :EXPERT_FF]) * gu[:, EXPERT_FF:] * w_e
    acc_ref[...] += _dot(hid.astype(BF16), wd_ref[0])

    @pl.when(e == N_EXPERTS - 1)
    def _():
        m = mod_ref[0]
        o_ref[...] = _layer_norm(alpha * x1_ref[...] + m[:, 5 * D_MODEL:6 * D_MODEL] * acc_ref[...],
                                 g2_ref[...], b2_ref[...])


def _moe_mod_block(i):
    per = MOE_TM // TM
    return _mod_block(i * per)


def _moe(alpha, h2, cw, wgu, wd, x1, mod3, g2, b2):
    return pl.pallas_call(
        functools.partial(_moe_kernel, alpha),
        grid=(TOKENS // MOE_TM, N_EXPERTS),
        in_specs=[pl.BlockSpec((MOE_TM, D_MODEL), lambda i, e: (i, 0)),
                  pl.BlockSpec((MOE_TM, LANES), lambda i, e: (i, 0)),
                  pl.BlockSpec((1, D_MODEL, 2 * EXPERT_FF), lambda i, e: (e, 0, 0)),
                  pl.BlockSpec((1, EXPERT_FF, D_MODEL), lambda i, e: (e, 0, 0)),
                  pl.BlockSpec((MOE_TM, D_MODEL), lambda i, e: (i, 0)),
                  pl.BlockSpec((1, 1, 6 * D_MODEL), lambda i, e: (_moe_mod_block(i), 0, 0)),
                  pl.BlockSpec((1, D_MODEL), lambda i, e: (0, 0)),
                  pl.BlockSpec((1, D_MODEL), lambda i, e: (0, 0))],
        out_specs=pl.BlockSpec((MOE_TM, D_MODEL), lambda i, e: (i, 0)),
        out_shape=jax.ShapeDtypeStruct((TOKENS, D_MODEL), F32),
        scratch_shapes=[pltpu.VMEM((MOE_TM, D_MODEL), F32)],
        compiler_params=_cp(("arbitrary", "arbitrary")),
        name="moe_ln2",
    )(h2, cw, wgu, wd, x1, mod3, g2, b2)


def _rope_tables():
    pos = jnp.arange(DEC_SEQ)
    row = (pos // GRID_W).astype(F32)
    col = (pos % GRID_W).astype(F32)
    quarter = ATTN_HEAD_DIM // 4
    inv_freq = jnp.power(ROPE_BASE, -jnp.arange(quarter, dtype=F32) / quarter)
    j = jnp.arange(ATTN_HEAD_DIM)
    p = jnp.where((j // (2 * quarter))[None, :] == 0, row[:, None], col[:, None])
    ang = p * inv_freq[j % quarter][None, :]
    sign = jnp.where((j % (2 * quarter)) < quarter, -1.0, 1.0).astype(F32)
    cos = jnp.tile(jnp.cos(ang), (1, ATTN_HEADS))
    sin = jnp.tile(jnp.sin(ang) * sign[None, :], (1, ATTN_HEADS))
    cos = jnp.concatenate([jnp.ones((TM, 512), F32), cos], axis=0)
    sin = jnp.concatenate([jnp.zeros((TM, 512), F32), sin], axis=0)
    return cos, sin


def _dup_heads(t):
    h0, h1 = t[..., :ATTN_HEAD_DIM], t[..., ATTN_HEAD_DIM:]
    return jnp.concatenate([h0, h0, h1, h1], axis=-1)


def _reorder_w_in(w):
    widths = (256, 256, 512, 512, GLA_RANK, GLA_RANK, 1024, 512, 128, 128, 3072)
    parts, start = [], 0
    for wd in widths:
        parts.append(w[:, start:start + wd])
        start += wd
    aq, ak, av, ag, lrf, lrb, bz, cq, ck, cv, mg = parts
    lr = jnp.concatenate([lrf, lrb, jnp.zeros((D_MODEL, LANES - 2 * GLA_RANK), w.dtype)], axis=1)
    return jnp.concatenate([aq, ak, av, ag, lr, bz, cq, _dup_heads(ck), _dup_heads(cv), mg],
                           axis=1).astype(BF16)


def kernel(x_prompt, x_sample, state_gla, cache_k, cache_v, c, c_ctx, ada_w, ada_b, w_in, gla_gate_w, gla_gate_b, gla_norm_g, gmlp_ln_g, gmlp_ln_b, gmlp_ws, gmlp_bs, attn_sink, w_branch, w_out, ln1_g, ln1_b, ln2_g, ln2_b, router_group_w, router_group_b, router_expert_w, router_expert_b, expert_w_gate, expert_w_up, expert_w_down):
    alpha = (2.0 * DEPTH) ** 0.25
    x = jnp.concatenate([x_prompt.reshape(CTX_TOKENS, D_MODEL), x_sample.reshape(LAT_TOKENS, D_MODEL)], axis=0)
    cond = jnp.concatenate([c_ctx[None, :], c, jnp.zeros((MOD_ROWS - 1 - DEC_BATCH, D_MODEL), F32)], axis=0)
    cos_t, sin_t = _rope_tables()
    states, keys, values = [], [], []
    for l in range(DEPTH):
        mod3 = _modulation(cond, ada_w[l], ada_b[l][None, :]).reshape(MOD_ROWS, 1, 6 * D_MODEL)
        (aq, ak, av, ag, lr, bz, cq, ckd, cvd, ck, cv, mg) = _in_projection(
            x, mod3, cos_t, sin_t, _reorder_w_in(w_in[l]))

        zpad = jnp.zeros((LANES - 2 * GLA_RANK, 256), F32)
        gw_p = jnp.stack([
            jnp.concatenate([gla_gate_w[l, 0], jnp.zeros((GLA_RANK, 256), F32), zpad], axis=0),
            jnp.concatenate([jnp.zeros((GLA_RANK, 256), F32), gla_gate_w[l, 1], zpad], axis=0)])
        o_f, o_b, s_fin = _gla(aq, ak, av, lr, gw_p, gla_gate_b[l][:, None, :], state_gla[:, l])

        sink_t = jnp.broadcast_to(attn_sink[l][:, None, None], (ATTN_HEADS, 1, LANES))
        brc_ctx = _context_attention(cq, ckd, cvd, sink_t)
        kc = _dup_heads(cache_k[:, l].reshape(DEC_BATCH, PAST_LEN, 128)).astype(BF16)
        vc = _dup_heads(cache_v[:, l].reshape(DEC_BATCH, PAST_LEN, 128)).astype(BF16)
        brc_lat = _latent_attention(cq, ckd, cvd, kc, vc, sink_t)

        rw = jnp.concatenate([router_expert_w[l], router_group_w[l],
                              jnp.zeros((D_MODEL, LANES - N_EXPERTS - N_GROUPS), F32)], axis=1)
        rw_hi = rw.astype(BF16)
        rw_lo = (rw - rw_hi.astype(F32)).astype(BF16)
        rb = jnp.concatenate([router_expert_b[l], router_group_b[l],
                              jnp.zeros((LANES - N_EXPERTS - N_GROUPS,), F32)])[None, :]
        bs_t = jnp.repeat(gmlp_bs[l].T, GMLP_CH, axis=1)
        x1, h2, cw = _merge(alpha, x, mod3, o_f, o_b, ag, gla_norm_g[l][None, :], bz,
                            gmlp_ln_g[l][None, :], gmlp_ln_b[l][None, :], gmlp_ws[l].astype(BF16), bs_t,
                            brc_ctx, brc_lat, mg, w_branch[l].astype(BF16), w_out[l].astype(BF16),
                            ln1_g[l][None, :], ln1_b[l][None, :], rw_hi, rw_lo, rb)

        wgu = jnp.concatenate([expert_w_gate[l], expert_w_up[l]], axis=-1).astype(BF16)
        x = _moe(alpha, h2, cw, wgu, expert_w_down[l].astype(BF16), x1, mod3,
                 ln2_g[l][None, :], ln2_b[l][None, :])

        states.append(s_fin[:BATCH])
        keys.append(ck[:CTX_TOKENS].reshape(BATCH, SEQ, ATTN_KV_HEADS, ATTN_HEAD_DIM))
        values.append(cv[:CTX_TOKENS].reshape(BATCH, SEQ, ATTN_KV_HEADS, ATTN_HEAD_DIM))
    y_prompt = x[:CTX_TOKENS].reshape(BATCH, SEQ, D_MODEL)
    y_sample = x[CTX_TOKENS:].reshape(DEC_BATCH, DEC_SEQ, D_MODEL)
    return (y_prompt, y_sample, jnp.stack(states, axis=1), jnp.stack(keys, axis=1), jnp.stack(values, axis=1))
```

```python
import functools

import jax
import jax.numpy as jnp
from jax import lax
from jax.experimental import pallas as pl
from jax.experimental.pallas import tpu as pltpu

F32 = jnp.float32
BF16 = jnp.bfloat16

D_MODEL = 1024
BATCH = 32
SEQ = 256
DEPTH = 2
DEC_BATCH = 8
DEC_SEQ = 1024
PAST_LEN = 512
GRID_W = 64
GLA_HEADS = 4
GLA_DV = 128
GLA_DK = 64
GLA_RANK = 16
GLA_TAU = 16.0
GLA_CHUNK = 32
GMLP_CHUNK = 128
GMLP_CH = 128
GMLP_GROUPS = 4
ATTN_HEADS = 8
ATTN_KV_HEADS = 2
ATTN_GROUP = ATTN_HEADS // ATTN_KV_HEADS
ATTN_HEAD_DIM = 64
WINDOW = 128
ATTN_BLOCK = 128
ROPE_BASE = 10000.0
BRANCH_W = 512
N_BRANCH = 3
N_GROUPS = 4
EXPERTS_PER_GROUP = 4
N_EXPERTS = 16
EXPERT_FF = 256
LN_EPS = 1e-5

LANES = 128
TM = 256
CTX_TOKENS = BATCH * SEQ
LAT_TOKENS = DEC_BATCH * DEC_SEQ
TOKENS = CTX_TOKENS + LAT_TOKENS
NB_CTX = CTX_TOKENS // TM
NB_LAT = LAT_TOKENS // TM
NB = NB_CTX + NB_LAT
LAT_BLOCKS_PER_SEQ = DEC_SEQ // TM
MOD_ROWS = 16
GLA_SUB = 128
CHUNKS_PER_SUB = GLA_SUB // GLA_CHUNK
MOE_TM = 1024
MOE_EB = 4
MOE_CTX_BLOCKS = CTX_TOKENS // MOE_TM

C_AQ = 0
C_AK = C_AQ + 256
C_AV = C_AK + 256
C_AG = C_AV + 512
C_LR = C_AG + 512
C_BZ = C_LR + LANES
C_CQ = C_BZ + 1024
C_CK = C_CQ + 512
C_CV = C_CK + 256
C_MG = C_CV + LANES
IN_COLS_P = C_MG + 3072

VMEM_LIMIT = 56 * 1024 * 1024


def _cp(sem):
    return pltpu.CompilerParams(dimension_semantics=sem, vmem_limit_bytes=VMEM_LIMIT)


def _dot(a, b):
    return jnp.dot(a, b, preferred_element_type=F32)


def _dot_nt(a, b):
    return lax.dot_general(a, b, (((1,), (1,)), ((), ())), preferred_element_type=F32)


def _dot_tn(a, b):
    return lax.dot_general(a, b, (((0,), (0,)), ((), ())), preferred_element_type=F32)


def _split(x):
    hi = x.astype(BF16)
    lo = (x - hi.astype(F32)).astype(BF16)
    return hi, lo


def _sigmoid(x):
    return 1.0 / (1.0 + jnp.exp(-x))


def _silu(x):
    return x * _sigmoid(x)


def _layer_norm(x, g, b):
    mu = jnp.mean(x, axis=-1, keepdims=True)
    xc = x - mu
    var = jnp.mean(xc * xc, axis=-1, keepdims=True)
    return xc * lax.rsqrt(var + LN_EPS) * g + b


def _mod_block(i):
    return jnp.where(i < NB_CTX, 0, 1 + (i - NB_CTX) // LAT_BLOCKS_PER_SEQ)


def _tok(width):
    return pl.BlockSpec((TM, width), lambda i: (i, 0))


def _ctx_tok(width):
    return pl.BlockSpec((TM, width), lambda i: (jnp.minimum(i, NB_CTX - 1), 0))


def _lat_tok(width):
    return pl.BlockSpec((TM, width), lambda i: (jnp.maximum(i - NB_CTX, 0), 0))


def _layer_spec(l, shape):
    return pl.BlockSpec((None,) + shape, lambda *_: (l,) + (0,) * len(shape))


def _mod_spec():
    return pl.BlockSpec((1, 1, 6 * D_MODEL), lambda i: (_mod_block(i), 0, 0))


def _mod_kernel(cond_ref, w_ref, b_ref, o_ref):
    s_hi, s_lo = _split(_silu(cond_ref[...]))
    w_hi, w_lo = _split(w_ref[...])
    o_ref[...] = _dot(s_hi, w_hi) + _dot(s_lo, w_hi) + _dot(s_hi, w_lo) + b_ref[...]


def _modulation(l, cond, ada_w, ada_b3):
    tn = 1536
    return pl.pallas_call(
        _mod_kernel,
        grid=(6 * D_MODEL // tn,),
        in_specs=[pl.BlockSpec((MOD_ROWS, D_MODEL), lambda j: (0, 0)),
                  pl.BlockSpec((None, D_MODEL, tn), lambda j: (l, 0, j)),
                  pl.BlockSpec((None, 1, tn), lambda j: (l, 0, j))],
        out_specs=pl.BlockSpec((MOD_ROWS, tn), lambda j: (0, j)),
        out_shape=jax.ShapeDtypeStruct((MOD_ROWS, 6 * D_MODEL), F32),
        compiler_params=_cp(("arbitrary",)),
        name="modulation",
    )(cond, ada_w, ada_b3)


def _rope(x, cos, sin):
    n = x.shape[1]
    lane = lax.broadcasted_iota(jnp.int32, x.shape, 1)
    partner = jnp.where((lane & 31) < 16, pltpu.roll(x, n - 16, 1), pltpu.roll(x, 16, 1))
    return x * cos + partner * sin


def _inproj_kernel(xc_ref, xl_ref, mod_ref, cos_ref, sin_ref, w_ref,
                   aq_ref, ak_ref, av_ref, ag_ref, lr_ref, bz_ref,
                   cq_ref, ckd_ref, cvt_ref, ck_ref, cv_ref, mg_ref):
    i = pl.program_id(0)
    m = mod_ref[0]
    x = jnp.where(i < NB_CTX, xc_ref[...], xl_ref[...])
    h = (x * (1.0 + m[:, D_MODEL:2 * D_MODEL]) + m[:, 0:D_MODEL]).astype(BF16)

    def seg(c0, width):
        return _dot(h, w_ref[:, c0:c0 + width])

    aq_ref[...] = seg(C_AQ, 256)
    ak_ref[...] = seg(C_AK, 256)
    av_ref[...] = seg(C_AV, 512).astype(BF16)
    ag_ref[...] = seg(C_AG, 512)
    lr_ref[...] = seg(C_LR, LANES)
    bz_ref[...] = seg(C_BZ, 1024)
    mg_ref[...] = seg(C_MG, 3072)

    cos = cos_ref[...]
    sin = sin_ref[...]
    cq_ref[...] = (_rope(seg(C_CQ, 512), cos, sin) * (ATTN_HEAD_DIM ** -0.5)).astype(BF16)
    ckd = seg(C_CK, 256)
    cv = seg(C_CV, LANES)
    ckd_ref[...] = _rope(ckd, cos[:, :256], sin[:, :256]).astype(BF16)
    cvt_ref[...] = cv.T.astype(BF16)

    @pl.when(i < NB_CTX)
    def _():
        lane = lax.broadcasted_iota(jnp.int32, (TM, LANES), 1)
        ck_ref[...] = jnp.where(lane < 64, ckd[:, :LANES], ckd[:, LANES:])
        cv_ref[...] = cv


def _rope_block(i):
    return jnp.where(i < NB_CTX, 0, 1 + (i - NB_CTX) % LAT_BLOCKS_PER_SEQ)


def _in_projection(l, xc, xl, mod3, cos_t, sin_t, w_in_p):
    widths = [(256, F32), (256, F32), (512, BF16), (512, F32), (LANES, F32), (1024, F32),
              (512, BF16), (256, BF16), (3072, F32)]
    out_specs = [_tok(w) for w, _ in widths]
    out_shape = [jax.ShapeDtypeStruct((TOKENS, w), dt) for w, dt in widths]
    out_specs[8:8] = [pl.BlockSpec((LANES, TM), lambda i: (0, i)), _ctx_tok(LANES), _ctx_tok(LANES)]
    out_shape[8:8] = [jax.ShapeDtypeStruct((LANES, TOKENS), BF16),
                      jax.ShapeDtypeStruct((CTX_TOKENS, LANES), F32),
                      jax.ShapeDtypeStruct((CTX_TOKENS, LANES), F32)]
    return pl.pallas_call(
        _inproj_kernel,
        grid=(NB,),
        in_specs=[_ctx_tok(D_MODEL), _lat_tok(D_MODEL), _mod_spec(),
                  pl.BlockSpec((TM, 512), lambda i: (_rope_block(i), 0)),
                  pl.BlockSpec((TM, 512), lambda i: (_rope_block(i), 0)),
                  _layer_spec(l, (D_MODEL, IN_COLS_P))],
        out_specs=out_specs,
        out_shape=out_shape,
        compiler_params=_cp(("arbitrary",)),
        name="in_projection",
    )(xc, xl, mod3, cos_t, sin_t, w_in_p)


def _gla_direction(q, k, v, g, state_ref, reverse):
    r = lax.broadcasted_iota(jnp.int32, (GLA_SUB, GLA_SUB), 0)
    c = lax.broadcasted_iota(jnp.int32, (GLA_SUB, GLA_SUB), 1)
    same = (r >> 5) == (c >> 5)
    tri = (same & ((c >= r) if reverse else (c <= r)))
    tri_b = jnp.where(tri, 1.0, 0.0).astype(BF16)
    ones_b = jnp.where(same, 1.0, 0.0).astype(BF16)
    g_hi, g_lo = _split(g)
    b = _dot(tri_b, g_hi) + _dot(tri_b, g_lo)
    bl = _dot(ones_b, g_hi) + _dot(ones_b, g_lo)
    qe = q * jnp.exp(b) * (GLA_DK ** -0.5)
    ke = (k * jnp.exp(-b)).astype(BF16)
    kl = k * jnp.exp(bl - b)
    ind = jnp.where((lax.broadcasted_iota(jnp.int32, (GLA_SUB, LANES), 0) >> 5)
                    == lax.broadcasted_iota(jnp.int32, (GLA_SUB, LANES), 1), 1.0, 0.0).astype(BF16)
    dcol = jnp.exp(_dot_tn(g_hi, ind) + _dot_tn(g_lo, ind))

    lane = lax.broadcasted_iota(jnp.int32, (GLA_SUB, LANES), 1)
    lo_half = lane < 64
    row_chunk = lax.broadcasted_iota(jnp.int32, (GLA_SUB, CHUNKS_PER_SUB * LANES), 0) >> 5
    col_chunk = lax.broadcasted_iota(jnp.int32, (GLA_SUB, CHUNKS_PER_SUB * LANES), 1) >> 7
    in_chunk = row_chunk == col_chunk
    blk_r = lax.broadcasted_iota(jnp.int32, (LANES, 2 * GLA_DV), 0) >> 6
    blk_c = lax.broadcasted_iota(jnp.int32, (LANES, 2 * GLA_DV), 1) >> 7
    head_diag = blk_r == blk_c

    outs = []
    for p in range(GLA_HEADS // 2):
        qp = qe[:, p * LANES:(p + 1) * LANES]
        kep = ke[:, p * LANES:(p + 1) * LANES]
        klp = kl[:, p * LANES:(p + 1) * LANES]
        vp = v[:, p * 2 * GLA_DV:(p + 1) * 2 * GLA_DV]
        halves = []
        for hh in range(2):
            qh = jnp.where(lo_half if hh == 0 else ~lo_half, qp, 0.0).astype(BF16)
            s = jnp.where(tri, _dot_nt(qh, kep), 0.0).astype(BF16)
            halves.append(_dot(s, vp[:, hh * GLA_DV:(hh + 1) * GLA_DV]))
        k_exp = jnp.where(in_chunk, jnp.concatenate([klp] * CHUNKS_PER_SUB, axis=1), 0.0).astype(BF16)
        q_exp = jnp.where(in_chunk, jnp.concatenate([qp] * CHUNKS_PER_SUB, axis=1), 0.0).astype(BF16)
        kv = _dot_tn(k_exp, vp)
        s_cur = state_ref[p]
        entering = [None] * CHUNKS_PER_SUB
        order = range(CHUNKS_PER_SUB - 1, -1, -1) if reverse else range(CHUNKS_PER_SUB)
        for n in order:
            entering[n] = s_cur
            dec = dcol[p * LANES:(p + 1) * LANES, n:n + 1]
            kv_n = jnp.where(head_diag, kv[n * LANES:(n + 1) * LANES], 0.0)
            s_cur = dec * s_cur + kv_n
        state_ref[p] = s_cur
        s_stack = jnp.concatenate(entering, axis=0).astype(BF16)
        inter = _dot(q_exp, s_stack)
        outs.append(halves[0] + inter[:, :GLA_DV])
        outs.append(halves[1] + inter[:, GLA_DV:])
    return jnp.concatenate(outs, axis=1)


def _gla_kernel(qf_ref, kf_ref, vf_ref, lrf_ref, qb_ref, kb_ref, vb_ref, lrb_ref,
                gw_ref, gb_ref, s0_ref, of_ref, ob_ref, sfin_ref, state_ref):
    i = pl.program_id(0)
    first = jnp.logical_or(i < NB_CTX, (i - NB_CTX) % LAT_BLOCKS_PER_SEQ == 0)

    @pl.when(first)
    def _():
        is_lat = i >= NB_CTX
        zero = jnp.zeros((GLA_DK, GLA_DV), F32)
        for d in range(2):
            for p in range(2):
                a = jnp.where(is_lat, s0_ref[0, d, 2 * p], zero)
                bq = jnp.where(is_lat, s0_ref[0, d, 2 * p + 1], zero)
                state_ref[d, p] = jnp.concatenate(
                    [jnp.concatenate([a, zero], axis=1), jnp.concatenate([zero, bq], axis=1)], axis=0)

    def log_decay(lr_ref, d, r0):
        lr_hi, lr_lo = _split(lr_ref[r0:r0 + GLA_SUB, :])
        w_hi, w_lo = _split(gw_ref[d])
        z = _dot(lr_hi, w_hi) + _dot(lr_lo, w_hi) + _dot(lr_hi, w_lo) + gb_ref[d]
        return (jnp.minimum(z, 0.0) - jnp.log1p(jnp.exp(-jnp.abs(z)))) / GLA_TAU

    n_sub = TM // GLA_SUB
    for j in range(n_sub):
        r0 = j * GLA_SUB
        of_ref[r0:r0 + GLA_SUB, :] = _gla_direction(
            qf_ref[r0:r0 + GLA_SUB, :], kf_ref[r0:r0 + GLA_SUB, :], vf_ref[r0:r0 + GLA_SUB, :],
            log_decay(lrf_ref, 0, r0), state_ref.at[0], False)
        r1 = (n_sub - 1 - j) * GLA_SUB
        ob_ref[r1:r1 + GLA_SUB, :] = _gla_direction(
            qb_ref[r1:r1 + GLA_SUB, :], kb_ref[r1:r1 + GLA_SUB, :], vb_ref[r1:r1 + GLA_SUB, :],
            log_decay(lrb_ref, 1, r1), state_ref.at[1], True)

    @pl.when(i < NB_CTX)
    def _():
        for d in range(2):
            for p in range(2):
                sp = state_ref[d, p]
                sfin_ref[0, d, 2 * p] = sp[:GLA_DK, :GLA_DV]
                sfin_ref[0, d, 2 * p + 1] = sp[GLA_DK:, GLA_DV:]


def _bwd_block(i):
    k = (i - NB_CTX) % LAT_BLOCKS_PER_SEQ
    return jnp.where(i < NB_CTX, i, i - k + (LAT_BLOCKS_PER_SEQ - 1 - k))


def _gla(l, aq, ak, av, lr, gw_p, gb_p, state_gla):
    def bwd(width):
        return pl.BlockSpec((TM, width), lambda i: (_bwd_block(i), 0))

    def lat_seq(i):
        return jnp.maximum(i - NB_CTX, 0) // LAT_BLOCKS_PER_SEQ

    state_dims = (2, GLA_HEADS, GLA_DK, GLA_DV)
    return pl.pallas_call(
        _gla_kernel,
        grid=(NB,),
        in_specs=[_tok(256), _tok(256), _tok(512), _tok(LANES), bwd(256), bwd(256), bwd(512), bwd(LANES),
                  _layer_spec(l, (2, LANES, 256)), _layer_spec(l, (2, 1, 256)),
                  pl.BlockSpec((1, None) + state_dims, lambda i: (lat_seq(i), l, 0, 0, 0, 0))],
        out_specs=[_tok(512), bwd(512),
                   pl.BlockSpec((1,) + state_dims, lambda i: (jnp.minimum(i, NB_CTX - 1), 0, 0, 0, 0))],
        out_shape=[jax.ShapeDtypeStruct((TOKENS, 512), F32), jax.ShapeDtypeStruct((TOKENS, 512), F32),
                   jax.ShapeDtypeStruct((BATCH,) + state_dims, F32)],
        scratch_shapes=[pltpu.VMEM((2, 2, LANES, 2 * GLA_DV), F32)],
        compiler_params=_cp(("arbitrary",)),
        name="gla_scan",
    )(aq, ak, av, lr, aq, ak, av, lr, gw_p, gb_p, state_gla)


def _attend_group(q_refs, keys, values_t, masks, sink_ref, g, o_ref):
    rows = q_refs[0].shape[0]
    lane = lax.broadcasted_iota(jnp.int32, (rows, LANES), 1)
    stacked = []
    for q in q_refs:
        q32 = q.astype(F32)
        stacked.append(jnp.where(lane < 64, q32, 0.0).astype(BF16))
        stacked.append(jnp.where(lane >= 64, q32, 0.0).astype(BF16))
    qs = jnp.concatenate(stacked, axis=0)
    col_head = lax.broadcasted_iota(jnp.int32, (1, ATTN_GROUP * rows), 1) // rows
    sink = jnp.zeros((1, ATTN_GROUP * rows), F32)
    for hh in range(ATTN_GROUP):
        sink = jnp.where(col_head == hh, sink_ref[ATTN_GROUP * g + hh][:, :1], sink)
    ss = []
    m = sink
    for kk, mk in zip(keys, masks):
        s = _dot_nt(kk, qs)
        if mk is not None:
            s = jnp.where(jnp.concatenate([mk] * ATTN_GROUP, axis=1), s, -jnp.inf)
        ss.append(s)
        m = jnp.maximum(m, jnp.max(s, axis=0, keepdims=True))
    den = jnp.exp(sink - m)
    acc = jnp.zeros((ATTN_HEAD_DIM, ATTN_GROUP * rows), F32)
    for s, vt in zip(ss, values_t):
        e = jnp.exp(s - m)
        den = den + jnp.sum(e, axis=0, keepdims=True)
        acc = acc + _dot(vt, e.astype(BF16))
    o = (acc / den).astype(BF16)
    for hh in range(ATTN_GROUP):
        h = ATTN_GROUP * g + hh
        o_ref[h * ATTN_HEAD_DIM:(h + 1) * ATTN_HEAD_DIM, :] = o[:, hh * rows:(hh + 1) * rows]


def _ctx_attn_kernel(q_ref, kd_ref, vt_ref, sink_ref, o_ref):
    for g in range(ATTN_KV_HEADS):
        gl = slice(g * LANES, (g + 1) * LANES)
        q_refs = [q_ref[:, (2 * g + pp) * LANES:(2 * g + pp + 1) * LANES] for pp in range(2)]
        vt = vt_ref[g * ATTN_HEAD_DIM:(g + 1) * ATTN_HEAD_DIM, :]
        _attend_group(q_refs, [kd_ref[:, gl]], [vt], [None], sink_ref, g, o_ref)


def _context_attention(l, cq, ckd, cvt, sink_t):
    return pl.pallas_call(
        _ctx_attn_kernel,
        grid=(BATCH,),
        in_specs=[pl.BlockSpec((SEQ, 512), lambda b: (b, 0)),
                  pl.BlockSpec((SEQ, 256), lambda b: (b, 0)),
                  pl.BlockSpec((LANES, SEQ), lambda b: (0, b)),
                  _layer_spec(l, (ATTN_HEADS, 1, LANES))],
        out_specs=pl.BlockSpec((512, SEQ), lambda b: (0, b)),
        out_shape=jax.ShapeDtypeStruct((512, CTX_TOKENS), BF16),
        compiler_params=_cp(("arbitrary",)),
        name="context_attention",
    )(cq, ckd, cvt, sink_t)


def _lat_attn_kernel(q_ref, kd_ref, vt_ref, kc_ref, vct_ref, sink_ref, o_ref):
    n = pl.program_id(1)
    nq = DEC_SEQ // ATTN_BLOCK
    prev0 = pl.multiple_of(jnp.maximum(n - 1, 0) * ATTN_BLOCK, ATTN_BLOCK)
    cur0 = pl.multiple_of(n * ATTN_BLOCK, ATTN_BLOCK)
    next0 = pl.multiple_of(jnp.minimum(n + 1, nq - 1) * ATTN_BLOCK, ATTN_BLOCK)
    kj = lax.broadcasted_iota(jnp.int32, (ATTN_BLOCK, ATTN_BLOCK), 0)
    qi = lax.broadcasted_iota(jnp.int32, (ATTN_BLOCK, ATTN_BLOCK), 1)
    m_prev = jnp.logical_and(kj >= qi, n > 0)
    m_next = jnp.logical_and(kj <= qi, n < nq - 1)
    for g in range(ATTN_KV_HEADS):
        gl = slice(g * LANES, (g + 1) * LANES)
        gr = slice(g * ATTN_HEAD_DIM, (g + 1) * ATTN_HEAD_DIM)
        q_refs = [q_ref[:, (2 * g + pp) * LANES:(2 * g + pp + 1) * LANES] for pp in range(2)]
        keys = [kd_ref[pl.ds(prev0, ATTN_BLOCK), gl], kd_ref[pl.ds(cur0, ATTN_BLOCK), gl],
                kd_ref[pl.ds(next0, ATTN_BLOCK), gl], kc_ref[0, :, gl]]
        vals = [vt_ref[gr, pl.ds(prev0, ATTN_BLOCK)], vt_ref[gr, pl.ds(cur0, ATTN_BLOCK)],
                vt_ref[gr, pl.ds(next0, ATTN_BLOCK)], vct_ref[0, gr, :]]
        _attend_group(q_refs, keys, vals, [m_prev, None, m_next, None], sink_ref, g, o_ref)


def _latent_attention(l, cq, ckd, cvt, kc, vct, sink_t):
    nq = DEC_SEQ // ATTN_BLOCK
    q0 = CTX_TOKENS // ATTN_BLOCK
    s0 = CTX_TOKENS // DEC_SEQ
    return pl.pallas_call(
        _lat_attn_kernel,
        grid=(DEC_BATCH, nq),
        in_specs=[pl.BlockSpec((ATTN_BLOCK, 512), lambda b, n: (q0 + b * nq + n, 0)),
                  pl.BlockSpec((DEC_SEQ, 256), lambda b, n: (s0 + b, 0)),
                  pl.BlockSpec((LANES, DEC_SEQ), lambda b, n: (0, s0 + b)),
                  pl.BlockSpec((1, None, PAST_LEN, 256), lambda b, n: (b, l, 0, 0)),
                  pl.BlockSpec((1, None, LANES, PAST_LEN), lambda b, n: (b, l, 0, 0)),
                  _layer_spec(l, (ATTN_HEADS, 1, LANES))],
        out_specs=pl.BlockSpec((512, ATTN_BLOCK), lambda b, n: (0, b * nq + n)),
        out_shape=jax.ShapeDtypeStruct((512, LAT_TOKENS), BF16),
        compiler_params=_cp(("arbitrary", "arbitrary")),
        name="latent_attention",
    )(cq, ckd, cvt, kc, vct, sink_t)


def _gelu(x):
    return 0.5 * x * (1.0 + jnp.tanh(0.7978845608028654 * (x + 0.044715 * (x * x * x))))


def _merge_kernel(alpha, xc_ref, xl_ref, mod_ref, of_ref, ob_ref, ag_ref, ng_ref, bz_ref, lg_ref, lb_ref,
                  ws_ref, bs_ref, cc_ref, cl_ref, mg_ref, wb_ref, wo_ref, g1_ref, b1_ref,
                  rwh_ref, rwl_ref, rb_ref, x1_ref, h2_ref, cw_ref):
    i = pl.program_id(0)
    m = mod_ref[0]
    o = of_ref[...] + ob_ref[...]
    parts = []
    for h in range(GLA_HEADS):
        oh = o[:, h * GLA_DV:(h + 1) * GLA_DV]
        parts.append(oh * lax.rsqrt(jnp.mean(oh * oh, axis=-1, keepdims=True) + LN_EPS))
    br_a = (jnp.concatenate(parts, axis=1) * ng_ref[...] * _silu(ag_ref[...])).astype(BF16)
    z = _gelu(bz_ref[...])
    u = z[:, :BRANCH_W]
    v = _layer_norm(z[:, BRANCH_W:], lg_ref[...], lb_ref[...]).astype(BF16)
    rows = []
    for c in range(TM // GMLP_CHUNK):
        cols = []
        for g in range(GMLP_GROUPS):
            vv = v[c * GMLP_CHUNK:(c + 1) * GMLP_CHUNK, g * GMLP_CH:(g + 1) * GMLP_CH]
            cols.append(_dot(ws_ref[g], vv))
        rows.append(jnp.concatenate(cols, axis=1) + bs_ref[...])
    br_b = (u * jnp.concatenate(rows, axis=0)).astype(BF16)
    br_ct = jnp.where(i < NB_CTX, cc_ref[...], cl_ref[...])
    mg = mg_ref[...]
    projs = (_dot(br_a, wb_ref[0]), _dot(br_b, wb_ref[1]), _dot_tn(br_ct, wb_ref[2]))
    y = jnp.zeros((TM, D_MODEL), F32)
    for nbr, proj in enumerate(projs):
        y = y + _sigmoid(mg[:, nbr * D_MODEL:(nbr + 1) * D_MODEL]) * proj
    y = _dot(y.astype(BF16), wo_ref[...])
    x = jnp.where(i < NB_CTX, xc_ref[...], xl_ref[...])
    x1 = _layer_norm(alpha * x + m[:, 2 * D_MODEL:3 * D_MODEL] * y, g1_ref[...], b1_ref[...])
    x1_ref[...] = x1
    h2 = x1 * (1.0 + m[:, 4 * D_MODEL:5 * D_MODEL]) + m[:, 3 * D_MODEL:4 * D_MODEL]
    h2_ref[...] = h2.astype(BF16)
    h_hi, h_lo = _split(h2)
    logits = _dot(h_hi, rwh_ref[...]) + _dot(h_lo, rwh_ref[...]) + _dot(h_hi, rwl_ref[...]) + rb_ref[...]
    lane_i = lax.broadcasted_iota(jnp.int32, (TM, LANES), 1)
    lane = lane_i.astype(F32)
    lane_group = (lane_i >> 2).astype(F32)
    big = float(LANES)
    neg = -jnp.inf
    gl = jnp.where((lane_i >= N_EXPERTS) & (lane_i < N_EXPERTS + N_GROUPS), logits, neg)
    gmax = jnp.max(gl, axis=-1, keepdims=True)
    gsum = jnp.sum(jnp.exp(gl - gmax), axis=-1, keepdims=True)
    g_p = 1.0 / gsum
    g_i = jnp.min(jnp.where(gl == gmax, lane, big), axis=-1, keepdims=True) - float(N_EXPERTS)
    in_group = (lane_i < N_EXPERTS) & (lane_group == g_i)
    el = jnp.where(in_group, logits, neg)
    emax = jnp.max(el, axis=-1, keepdims=True)
    ee = jnp.exp(el - emax)
    e_prob = ee / jnp.sum(ee, axis=-1, keepdims=True)
    p1 = jnp.max(jnp.where(in_group, e_prob, neg), axis=-1, keepdims=True)
    i1 = jnp.min(jnp.where(in_group & (e_prob == p1), lane, big), axis=-1, keepdims=True)
    rest = in_group & (lane != i1)
    p2 = jnp.max(jnp.where(rest, e_prob, neg), axis=-1, keepdims=True)
    i2 = jnp.min(jnp.where(rest & (e_prob == p2), lane, big), axis=-1, keepdims=True)
    tot = p1 + p2
    cw_ref[...] = (jnp.where(lane == i1, g_p * p1 / tot, 0.0)
                   + jnp.where(lane == i2, g_p * p2 / tot, 0.0))


def _merge(l, alpha, xc, xl, mod3, o_f, o_b, ag, norm_g, bz, ln_g, ln_b, ws, bs_t, brc_ctx, brc_lat, mg,
           wb, wo, g1, b1, rw_hi, rw_lo, rb):
    return pl.pallas_call(
        functools.partial(_merge_kernel, alpha),
        grid=(NB,),
        in_specs=[_ctx_tok(D_MODEL), _lat_tok(D_MODEL), _mod_spec(),
                  _tok(512), _tok(512), _tok(512), _layer_spec(l, (1, 512)),
                  _tok(1024), _layer_spec(l, (1, 512)), _layer_spec(l, (1, 512)),
                  _layer_spec(l, (GMLP_GROUPS, GMLP_CHUNK, GMLP_CHUNK)), _layer_spec(l, (GMLP_CHUNK, 512)),
                  pl.BlockSpec((512, TM), lambda i: (0, jnp.minimum(i, NB_CTX - 1))),
                  pl.BlockSpec((512, TM), lambda i: (0, jnp.maximum(i - NB_CTX, 0))),
                  _tok(3072), _layer_spec(l, (N_BRANCH, BRANCH_W, D_MODEL)), _layer_spec(l, (D_MODEL, D_MODEL)),
                  _layer_spec(l, (1, D_MODEL)), _layer_spec(l, (1, D_MODEL)),
                  _layer_spec(l, (D_MODEL, LANES)), _layer_spec(l, (D_MODEL, LANES)), _layer_spec(l, (1, LANES))],
        out_specs=[_tok(D_MODEL), _tok(D_MODEL), _tok(LANES)],
        out_shape=[jax.ShapeDtypeStruct((TOKENS, D_MODEL), F32),
                   jax.ShapeDtypeStruct((TOKENS, D_MODEL), BF16),
                   jax.ShapeDtypeStruct((TOKENS, LANES), F32)],
        compiler_params=_cp(("arbitrary",)),
        name="merge_ln1_router",
    )(xc, xl, mod3, o_f, o_b, ag, norm_g, bz, ln_g, ln_b, ws, bs_t, brc_ctx, brc_lat, mg,
      wb, wo, g1, b1, rw_hi, rw_lo, rb)


def _moe_kernel(alpha, h2_ref, cw_ref, wg_ref, wu_ref, wd_ref, x1_ref, mod_ref, g2_ref, b2_ref,
                oc_ref, ol_ref, acc_ref):
    i = pl.program_id(0)
    eb = pl.program_id(1)
    h2 = h2_ref[...]
    cw = cw_ref[...]
    lane = lax.broadcasted_iota(jnp.int32, (MOE_TM, LANES), 1)
    hid = []
    for k in range(MOE_EB):
        w_e = jnp.sum(jnp.where(lane == eb * MOE_EB + k, cw, 0.0), axis=-1, keepdims=True)
        hid.append((_silu(_dot(h2, wg_ref[k])) * _dot(h2, wu_ref[k]) * w_e).astype(BF16))
    part = _dot(jnp.concatenate(hid, axis=1), wd_ref[...].reshape(MOE_EB * EXPERT_FF, D_MODEL))

    @pl.when(eb == 0)
    def _():
        acc_ref[...] = part

    @pl.when(eb > 0)
    def _():
        acc_ref[...] += part

    @pl.when(eb == N_EXPERTS // MOE_EB - 1)
    def _():
        m = mod_ref[0]
        y = _layer_norm(alpha * x1_ref[...] + m[:, 5 * D_MODEL:6 * D_MODEL] * acc_ref[...],
                        g2_ref[...], b2_ref[...])

        @pl.when(i < MOE_CTX_BLOCKS)
        def _():
            oc_ref[...] = y

        @pl.when(i >= MOE_CTX_BLOCKS)
        def _():
            ol_ref[...] = y


def _moe(l, alpha, h2, cw, wg, wu, wd, x1, mod3, g2, b2):
    per = MOE_TM // TM
    return pl.pallas_call(
        functools.partial(_moe_kernel, alpha),
        grid=(TOKENS // MOE_TM, N_EXPERTS // MOE_EB),
        in_specs=[pl.BlockSpec((MOE_TM, D_MODEL), lambda i, e: (i, 0)),
                  pl.BlockSpec((MOE_TM, LANES), lambda i, e: (i, 0)),
                  pl.BlockSpec((None, MOE_EB, D_MODEL, EXPERT_FF), lambda i, e: (l, e, 0, 0)),
                  pl.BlockSpec((None, MOE_EB, D_MODEL, EXPERT_FF), lambda i, e: (l, e, 0, 0)),
                  pl.BlockSpec((None, MOE_EB, EXPERT_FF, D_MODEL), lambda i, e: (l, e, 0, 0)),
                  pl.BlockSpec((MOE_TM, D_MODEL), lambda i, e: (i, 0)),
                  pl.BlockSpec((1, 1, 6 * D_MODEL), lambda i, e: (_mod_block(i * per), 0, 0)),
                  pl.BlockSpec((None, 1, D_MODEL), lambda i, e: (l, 0, 0)),
                  pl.BlockSpec((None, 1, D_MODEL), lambda i, e: (l, 0, 0))],
        out_specs=[pl.BlockSpec((MOE_TM, D_MODEL), lambda i, e: (jnp.minimum(i, MOE_CTX_BLOCKS - 1), 0)),
                   pl.BlockSpec((MOE_TM, D_MODEL), lambda i, e: (jnp.maximum(i - MOE_CTX_BLOCKS, 0), 0))],
        out_shape=[jax.ShapeDtypeStruct((CTX_TOKENS, D_MODEL), F32),
                   jax.ShapeDtypeStruct((LAT_TOKENS, D_MODEL), F32)],
        scratch_shapes=[pltpu.VMEM((MOE_TM, D_MODEL), F32)],
        compiler_params=_cp(("arbitrary", "arbitrary")),
        name="moe_ln2",
    )(h2, cw, wg, wu, wd, x1, mod3, g2, b2)


def _rope_tables():
    pos = jnp.arange(DEC_SEQ)
    row = (pos // GRID_W).astype(F32)
    col = (pos % GRID_W).astype(F32)
    quarter = ATTN_HEAD_DIM // 4
    inv_freq = jnp.power(ROPE_BASE, -jnp.arange(quarter, dtype=F32) / quarter)
    j = jnp.arange(ATTN_HEAD_DIM)
    p = jnp.where((j // (2 * quarter))[None, :] == 0, row[:, None], col[:, None])
    ang = p * inv_freq[j % quarter][None, :]
    sign = jnp.where((j % (2 * quarter)) < quarter, -1.0, 1.0).astype(F32)
    cos = jnp.tile(jnp.cos(ang), (1, ATTN_HEADS))
    sin = jnp.tile(jnp.sin(ang) * sign[None, :], (1, ATTN_HEADS))
    cos = jnp.concatenate([jnp.ones((TM, 512), F32), cos], axis=0)
    sin = jnp.concatenate([jnp.zeros((TM, 512), F32), sin], axis=0)
    return cos, sin


def _dup_heads(t):
    h0, h1 = t[..., :ATTN_HEAD_DIM], t[..., ATTN_HEAD_DIM:]
    return jnp.concatenate([h0, h0, h1, h1], axis=-1)


def _reorder_w_in(w):
    widths = (256, 256, 512, 512, GLA_RANK, GLA_RANK, 1024, 512, 128, 128, 3072)
    parts, start = [], 0
    for wd in widths:
        parts.append(w[..., start:start + wd])
        start += wd
    aq, ak, av, ag, lrf, lrb, bz, cq, ck, cv, mg = parts
    pad = jnp.zeros(w.shape[:-1] + (LANES - 2 * GLA_RANK,), w.dtype)
    return jnp.concatenate([aq, ak, av, ag, lrf, lrb, pad, bz, cq, _dup_heads(ck), cv, mg],
                           axis=-1).astype(BF16)


def kernel(x_prompt, x_sample, state_gla, cache_k, cache_v, c, c_ctx, ada_w, ada_b, w_in, gla_gate_w, gla_gate_b, gla_norm_g, gmlp_ln_g, gmlp_ln_b, gmlp_ws, gmlp_bs, attn_sink, w_branch, w_out, ln1_g, ln1_b, ln2_g, ln2_b, router_group_w, router_group_b, router_expert_w, router_expert_b, expert_w_gate, expert_w_up, expert_w_down):
    alpha = (2.0 * DEPTH) ** 0.25
    xc = x_prompt.reshape(CTX_TOKENS, D_MODEL)
    xl = x_sample.reshape(LAT_TOKENS, D_MODEL)
    cond = jnp.concatenate([c_ctx[None, :], c, jnp.zeros((MOD_ROWS - 1 - DEC_BATCH, D_MODEL), F32)], axis=0)
    cos_t, sin_t = _rope_tables()

    ada_b3 = ada_b[:, None, :]
    w_in_p = _reorder_w_in(w_in)
    zrow = jnp.zeros((DEPTH, GLA_RANK, 256), F32)
    zpad = jnp.zeros((DEPTH, LANES - 2 * GLA_RANK, 256), F32)
    gw_p = jnp.stack([jnp.concatenate([gla_gate_w[:, 0], zrow, zpad], axis=1),
                      jnp.concatenate([zrow, gla_gate_w[:, 1], zpad], axis=1)], axis=1)
    gb_p = gla_gate_b[:, :, None, :]
    sink_t = jnp.broadcast_to(attn_sink[:, :, None, None], (DEPTH, ATTN_HEADS, 1, LANES))
    kc = _dup_heads(cache_k.reshape(DEC_BATCH, DEPTH, PAST_LEN, 128)).astype(BF16)
    vct = jnp.swapaxes(cache_v.reshape(DEC_BATCH, DEPTH, PAST_LEN, 128), 2, 3).astype(BF16)
    rw = jnp.concatenate([router_expert_w, router_group_w,
                          jnp.zeros((DEPTH, D_MODEL, LANES - N_EXPERTS - N_GROUPS), F32)], axis=2)
    rw_hi = rw.astype(BF16)
    rw_lo = (rw - rw_hi.astype(F32)).astype(BF16)
    rb = jnp.concatenate([router_expert_b, router_group_b,
                          jnp.zeros((DEPTH, LANES - N_EXPERTS - N_GROUPS), F32)], axis=1)[:, None, :]
    bs_t = jnp.repeat(jnp.swapaxes(gmlp_bs, 1, 2), GMLP_CH, axis=2)
    ws_b = gmlp_ws.astype(BF16)
    wb_b = w_branch.astype(BF16)
    wo_b = w_out.astype(BF16)
    wg_b = expert_w_gate.astype(BF16)
    wu_b = expert_w_up.astype(BF16)
    wd_b = expert_w_down.astype(BF16)
    row = lambda t: t[:, None, :]

    states, keys, values = [], [], []
    for l in range(DEPTH):
        mod3 = _modulation(l, cond, ada_w, ada_b3).reshape(MOD_ROWS, 1, 6 * D_MODEL)
        (aq, ak, av, ag, lr, bz, cq, ckd, cvt, ck, cv, mg) = _in_projection(
            l, xc, xl, mod3, cos_t, sin_t, w_in_p)
        o_f, o_b, s_fin = _gla(l, aq, ak, av, lr, gw_p, gb_p, state_gla)
        brc_ctx = _context_attention(l, cq, ckd, cvt, sink_t)
        brc_lat = _latent_attention(l, cq, ckd, cvt, kc, vct, sink_t)
        x1, h2, cw = _merge(l, alpha, xc, xl, mod3, o_f, o_b, ag, row(gla_norm_g), bz,
                            row(gmlp_ln_g), row(gmlp_ln_b), ws_b, bs_t, brc_ctx, brc_lat, mg, wb_b, wo_b,
                            row(ln1_g), row(ln1_b), rw_hi, rw_lo, rb)
        xc, xl = _moe(l, alpha, h2, cw, wg_b, wu_b, wd_b, x1, mod3, row(ln2_g), row(ln2_b))
        states.append(s_fin)
        keys.append(ck.reshape(BATCH, SEQ, ATTN_KV_HEADS, ATTN_HEAD_DIM))
        values.append(cv.reshape(BATCH, SEQ, ATTN_KV_HEADS, ATTN_HEAD_DIM))
    return (xc.reshape(BATCH, SEQ, D_MODEL), xl.reshape(DEC_BATCH, DEC_SEQ, D_MODEL),
            jnp.stack(states, axis=1), jnp.stack(keys, axis=1), jnp.stack(values, axis=1))
```

```python
import functools

import jax
import jax.numpy as jnp
from jax import lax
from jax.experimental import pallas as pl
from jax.experimental.pallas import tpu as pltpu

F32 = jnp.float32
BF16 = jnp.bfloat16

D_MODEL = 1024
BATCH = 32
SEQ = 256
DEPTH = 2
DEC_BATCH = 8
DEC_SEQ = 1024
PAST_LEN = 512
GRID_W = 64
GLA_HEADS = 4
GLA_DV = 128
GLA_DK = 64
GLA_RANK = 16
GLA_TAU = 16.0
GLA_CHUNK = 32
GMLP_CHUNK = 128
GMLP_CH = 128
GMLP_GROUPS = 4
ATTN_HEADS = 8
ATTN_KV_HEADS = 2
ATTN_GROUP = ATTN_HEADS // ATTN_KV_HEADS
ATTN_HEAD_DIM = 64
WINDOW = 128
ATTN_BLOCK = 128
ROPE_BASE = 10000.0
BRANCH_W = 512
N_BRANCH = 3
N_GROUPS = 4
EXPERTS_PER_GROUP = 4
N_EXPERTS = 16
EXPERT_FF = 256
LN_EPS = 1e-5

LANES = 128
TM = 256
CTX_TOKENS = BATCH * SEQ
LAT_TOKENS = DEC_BATCH * DEC_SEQ
TOKENS = CTX_TOKENS + LAT_TOKENS
NB_CTX = CTX_TOKENS // TM
NB_LAT = LAT_TOKENS // TM
NB = NB_CTX + NB_LAT
LAT_BLOCKS_PER_SEQ = DEC_SEQ // TM
MOD_ROWS = 16
GLA_SUB = 128
CHUNKS_PER_SUB = GLA_SUB // GLA_CHUNK
MERGE_ROWS = TM
MOE_TM = 1024
MOE_EB = 4
MOE_CTX_BLOCKS = CTX_TOKENS // MOE_TM

VMEM_LIMIT = 56 * 1024 * 1024


def _cp(sem):
    return pltpu.CompilerParams(dimension_semantics=sem, vmem_limit_bytes=VMEM_LIMIT)


def _dot(a, b):
    return jnp.dot(a, b, preferred_element_type=F32)


def _dot_nt(a, b):
    return lax.dot_general(a, b, (((1,), (1,)), ((), ())), preferred_element_type=F32)


def _dot_tn(a, b):
    return lax.dot_general(a, b, (((0,), (0,)), ((), ())), preferred_element_type=F32)


def _split(x):
    hi = x.astype(BF16)
    lo = (x - hi.astype(F32)).astype(BF16)
    return hi, lo


def _sigmoid(x):
    return 0.5 * (jnp.tanh(0.5 * x) + 1.0)


def _silu(x):
    return x * _sigmoid(x)


def _layer_norm(x, g, b):
    mu = jnp.mean(x, axis=-1, keepdims=True)
    xc = x - mu
    var = jnp.mean(xc * xc, axis=-1, keepdims=True)
    return xc * lax.rsqrt(var + LN_EPS) * g + b


def _mod_block(i):
    return jnp.where(i < NB_CTX, 0, 1 + (i - NB_CTX) // LAT_BLOCKS_PER_SEQ)


def _tok(width):
    return pl.BlockSpec((TM, width), lambda i: (i, 0))


def _ctx_tok(width):
    return pl.BlockSpec((TM, width), lambda i: (jnp.minimum(i, NB_CTX - 1), 0))


def _lat_tok(width):
    return pl.BlockSpec((TM, width), lambda i: (jnp.maximum(i - NB_CTX, 0), 0))


def _layer_spec(l, shape):
    return pl.BlockSpec((None,) + shape, lambda *_: (l,) + (0,) * len(shape))


def _mod_spec():
    return pl.BlockSpec((1, 1, 6 * D_MODEL), lambda i: (_mod_block(i), 0, 0))


def _mod_kernel(cond_ref, w_ref, b_ref, o_ref):
    s_hi, s_lo = _split(_silu(cond_ref[...]))
    w_hi, w_lo = _split(w_ref[...])
    o_ref[...] = _dot(s_hi, w_hi) + _dot(s_lo, w_hi) + _dot(s_hi, w_lo) + b_ref[...]


def _modulation(l, cond, ada_w, ada_b3):
    tn = 1536
    return pl.pallas_call(
        _mod_kernel,
        grid=(6 * D_MODEL // tn,),
        in_specs=[pl.BlockSpec((MOD_ROWS, D_MODEL), lambda j: (0, 0)),
                  pl.BlockSpec((None, D_MODEL, tn), lambda j: (l, 0, j)),
                  pl.BlockSpec((None, 1, tn), lambda j: (l, 0, j))],
        out_specs=pl.BlockSpec((MOD_ROWS, tn), lambda j: (0, j)),
        out_shape=jax.ShapeDtypeStruct((MOD_ROWS, 6 * D_MODEL), F32),
        compiler_params=_cp(("arbitrary",)),
        name="modulation",
    )(cond, ada_w, ada_b3)


def _rope(x, cos, sin):
    n = x.shape[1]
    lane = lax.broadcasted_iota(jnp.int32, x.shape, 1)
    partner = jnp.where((lane & 31) < 16, pltpu.roll(x, n - 16, 1), pltpu.roll(x, 16, 1))
    return x * cos + partner * sin


def _inproj_kernel(xc_ref, xl_ref, mod_ref, cos_ref, sin_ref, gw_ref, gb_ref,
                   waq_ref, wak_ref, wav_ref, wag_ref, wlr_ref, wbz_ref, wcq_ref, wck_ref, wcv_ref, wmg_ref,
                   qef_ref, kef_ref, klf_ref, qeb_ref, keb_ref, klb_ref, dec_ref,
                   av_ref, avt_ref, ag_ref, bz_ref,
                   cq_ref, ckd_ref, cvt_ref, ck_ref, cv_ref, mg_ref):
    i = pl.program_id(0)
    m = mod_ref[0]
    x = jnp.where(i < NB_CTX, xc_ref[...], xl_ref[...])
    h = (x * (1.0 + m[:, D_MODEL:2 * D_MODEL]) + m[:, 0:D_MODEL]).astype(BF16)

    aq = _dot(h, waq_ref[...])
    ak = _dot(h, wak_ref[...])
    lr = _dot(h, wlr_ref[...]).astype(BF16)
    zs = [_dot(lr, gw_ref[d]) + gb_ref[d] for d in range(2)]
    mg_ref[...] = _dot(h, wmg_ref[...])
    r = lax.broadcasted_iota(jnp.int32, (TM, TM), 0)
    c = lax.broadcasted_iota(jnp.int32, (TM, TM), 1)
    same = (r >> 5) == (c >> 5)
    bs = []
    for d in range(2):
        g = (jnp.minimum(zs[d], 0.0) - jnp.log(1.0 + jnp.exp(-jnp.abs(zs[d])))) / GLA_TAU
        tri_b = jnp.where(same & ((c <= r) if d == 0 else (c >= r)), 1.0, 0.0).astype(BF16)
        g_hi, g_lo = _split(g)
        bs.append(_dot(tri_b, g_hi) + _dot(tri_b, g_lo))
    bz_ref[...] = _dot(h, wbz_ref[...])
    ag_ref[...] = _dot(h, wag_ref[...])
    av = _dot(h, wav_ref[...])
    av_ref[...] = av.astype(BF16)
    avt_ref[...] = av.T.astype(BF16)
    decs = []
    for d, (qe_ref, ke_ref, kl_ref) in enumerate(((qef_ref, kef_ref, klf_ref), (qeb_ref, keb_ref, klb_ref))):
        b = bs[d]
        last = GLA_CHUNK - 1 if d == 0 else 0
        tot = [b[n * GLA_CHUNK + last:n * GLA_CHUNK + last + 1, :] for n in range(TM // GLA_CHUNK)]
        bl = jnp.concatenate([jnp.broadcast_to(t, (GLA_CHUNK, t.shape[1])) for t in tot], axis=0)
        qe_ref[...] = (aq * jnp.exp(b) * (GLA_DK ** -0.5)).astype(BF16)
        ke_ref[...] = (ak * jnp.exp(-b)).astype(BF16)
        kl_ref[...] = (ak * jnp.exp(bl - b)).astype(BF16)
        decs.append(jnp.exp(jnp.concatenate(tot, axis=0)))
    dec_ref[...] = jnp.concatenate(decs, axis=1)

    cos = cos_ref[...]
    sin = sin_ref[...]
    cq_ref[...] = (_rope(_dot(h, wcq_ref[...]), cos, sin) * (ATTN_HEAD_DIM ** -0.5)).astype(BF16)
    ck = _dot(h, wck_ref[...])
    cv = _dot(h, wcv_ref[...])
    kr = _rope(ck, cos[:, :LANES], sin[:, :LANES])
    swapped = pltpu.roll(kr, ATTN_HEAD_DIM, 1)
    lane = lax.broadcasted_iota(jnp.int32, (TM, LANES), 1)
    ckd_ref[...] = jnp.concatenate([jnp.where(lane < 64, kr, swapped), jnp.where(lane < 64, swapped, kr)],
                                   axis=1).astype(BF16)
    cvt_ref[...] = cv.T.astype(BF16)

    @pl.when(i < NB_CTX)
    def _():
        ck_ref[...] = ck
        cv_ref[...] = cv


def _rope_block(i):
    return jnp.where(i < NB_CTX, 0, 1 + (i - NB_CTX) % LAT_BLOCKS_PER_SEQ)


def _in_projection(l, xc, xl, mod3, cos_t, sin_t, gw_p, gb_p, w_parts):
    def tok_out(width, dt):
        return _tok(width), jax.ShapeDtypeStruct((TOKENS, width), dt)

    def feat_out(width):
        return (pl.BlockSpec((width, TM), lambda i: (0, i)), jax.ShapeDtypeStruct((width, TOKENS), BF16))

    def ctx_out(width):
        return _ctx_tok(width), jax.ShapeDtypeStruct((CTX_TOKENS, width), F32)

    dec_out = (pl.BlockSpec((TM // GLA_CHUNK, 512), lambda i: (i, 0)),
               jax.ShapeDtypeStruct((TOKENS // GLA_CHUNK, 512), F32))
    outs = [tok_out(256, BF16)] * 6 + [dec_out] + [
        tok_out(512, BF16), feat_out(512), tok_out(512, F32), tok_out(1024, F32), tok_out(512, BF16),
        tok_out(256, BF16), feat_out(LANES), ctx_out(LANES), ctx_out(LANES), tok_out(3072, F32)]
    return pl.pallas_call(
        _inproj_kernel,
        grid=(NB,),
        in_specs=[_ctx_tok(D_MODEL), _lat_tok(D_MODEL), _mod_spec(),
                  pl.BlockSpec((TM, 512), lambda i: (_rope_block(i), 0)),
                  pl.BlockSpec((TM, 512), lambda i: (_rope_block(i), 0)),
                  _layer_spec(l, (2, LANES, 256)), _layer_spec(l, (2, 1, 256))]
                 + [_layer_spec(l, (D_MODEL, w.shape[-1])) for w in w_parts],
        out_specs=[o[0] for o in outs],
        out_shape=[o[1] for o in outs],
        compiler_params=_cp(("arbitrary",)),
        name="in_projection",
    )(xc, xl, mod3, cos_t, sin_t, gw_p, gb_p, *w_parts)


def _gla_direction(qe, ke, kl, v, vt, dec, masks, state_ref, reverse):
    tri, half_masks, chunk_mask, head_diag = masks
    tri = tri[int(reverse)]
    outs = []
    for p in range(GLA_HEADS // 2):
        ps = slice(p * LANES, (p + 1) * LANES)
        qp = qe[:, ps]
        kep = ke[:, ps]
        klp = kl[:, ps]
        vp = v[:, p * 2 * GLA_DV:(p + 1) * 2 * GLA_DV]
        vtp = vt[p * 2 * GLA_DV:(p + 1) * 2 * GLA_DV, :]
        halves = []
        for hh in range(2):
            s = jnp.where(tri, _dot_nt(qp * half_masks[hh], kep), 0.0).astype(BF16)
            halves.append(_dot(s, vp[:, hh * GLA_DV:(hh + 1) * GLA_DV]))
        k_exp = jnp.concatenate([klp] * CHUNKS_PER_SUB, axis=1) * chunk_mask
        q_exp = jnp.concatenate([qp] * CHUNKS_PER_SUB, axis=1) * chunk_mask
        kvt = _dot(vtp, k_exp)
        st = state_ref[p]
        entering = [None] * CHUNKS_PER_SUB
        order = range(CHUNKS_PER_SUB - 1, -1, -1) if reverse else range(CHUNKS_PER_SUB)
        for n in order:
            entering[n] = st
            st = dec[n:n + 1, ps] * st + jnp.where(head_diag, kvt[:, n * LANES:(n + 1) * LANES], 0.0)
        state_ref[p] = st
        st_stack = jnp.concatenate(entering, axis=1).astype(BF16)
        inter = _dot_nt(q_exp, st_stack)
        outs.append(halves[0] + inter[:, :GLA_DV])
        outs.append(halves[1] + inter[:, GLA_DV:])
    return jnp.concatenate(outs, axis=1)


def _gla_mask_constants():
    r = jnp.arange(GLA_SUB)[:, None]
    c = jnp.arange(GLA_SUB)[None, :]
    same = (r // GLA_CHUNK) == (c // GLA_CHUNK)
    tri = jnp.stack([same & (c <= r), same & (c >= r)]).astype(F32)
    lane = jnp.arange(LANES)[None, :]
    halves = jnp.stack([jnp.broadcast_to(lane < 64, (GLA_SUB, LANES)),
                        jnp.broadcast_to(lane >= 64, (GLA_SUB, LANES))]).astype(BF16)
    col_chunk = jnp.arange(CHUNKS_PER_SUB * LANES)[None, :] // LANES
    chunk_mask = ((r // GLA_CHUNK) == col_chunk).astype(BF16)
    head_diag = ((jnp.arange(2 * GLA_DV)[:, None] // GLA_DV) == (lane // GLA_DK)).astype(F32)
    return tri, halves, chunk_mask, head_diag


def _gla_kernel(tri_ref, half_ref, cmask_ref, hdiag_ref,
                qf_ref, kef_ref, klf_ref, vf_ref, vtf_ref, decf_ref,
                qb_ref, keb_ref, klb_ref, vb_ref, vtb_ref, decb_ref,
                s0_ref, of_ref, ob_ref, sfin_ref, state_ref):
    i = pl.program_id(0)
    @pl.when(i < NB_CTX)
    def _():
        state_ref[...] = jnp.zeros_like(state_ref)

    @pl.when(jnp.logical_and(i >= NB_CTX, (i - NB_CTX) % LAT_BLOCKS_PER_SEQ == 0))
    def _():
        zero = jnp.zeros((GLA_DK, GLA_DV), F32)
        for d in range(2):
            for p in range(2):
                pair = jnp.concatenate(
                    [jnp.concatenate([s0_ref[0, d, 2 * p], zero], axis=1),
                     jnp.concatenate([zero, s0_ref[0, d, 2 * p + 1]], axis=1)], axis=0)
                state_ref[d, p] = pair.T

    masks = ((tri_ref[0] > 0.5, tri_ref[1] > 0.5), (half_ref[0], half_ref[1]), cmask_ref[...],
             hdiag_ref[...] > 0.5)
    n_sub = TM // GLA_SUB
    for j in range(n_sub):
        r0 = j * GLA_SUB
        rs = slice(r0, r0 + GLA_SUB)
        of_ref[rs, :] = _gla_direction(
            qf_ref[rs, :], kef_ref[rs, :], klf_ref[rs, :], vf_ref[rs, :], vtf_ref[:, rs],
            decf_ref[j * CHUNKS_PER_SUB:(j + 1) * CHUNKS_PER_SUB, :256], masks, state_ref.at[0], False)
        jb = n_sub - 1 - j
        rb = slice(jb * GLA_SUB, (jb + 1) * GLA_SUB)
        ob_ref[rb, :] = _gla_direction(
            qb_ref[rb, :], keb_ref[rb, :], klb_ref[rb, :], vb_ref[rb, :], vtb_ref[:, rb],
            decb_ref[jb * CHUNKS_PER_SUB:(jb + 1) * CHUNKS_PER_SUB, 256:], masks, state_ref.at[1], True)

    @pl.when(i < NB_CTX)
    def _():
        for d in range(2):
            for p in range(2):
                sp = state_ref[d, p].T
                sfin_ref[0, d, 2 * p] = sp[:GLA_DK, :GLA_DV]
                sfin_ref[0, d, 2 * p + 1] = sp[GLA_DK:, GLA_DV:]


def _bwd_block(i):
    k = (i - NB_CTX) % LAT_BLOCKS_PER_SEQ
    return jnp.where(i < NB_CTX, i, i - k + (LAT_BLOCKS_PER_SEQ - 1 - k))


def _gla(l, qef, kef, klf, qeb, keb, klb, dec, av, avt, state_gla):
    def bwd(width):
        return pl.BlockSpec((TM, width), lambda i: (_bwd_block(i), 0))

    def lat_seq(i):
        return jnp.maximum(i - NB_CTX, 0) // LAT_BLOCKS_PER_SEQ

    vt_fwd = pl.BlockSpec((512, TM), lambda i: (0, i))
    vt_bwd = pl.BlockSpec((512, TM), lambda i: (0, _bwd_block(i)))
    dec_fwd = pl.BlockSpec((TM // GLA_CHUNK, 512), lambda i: (i, 0))
    dec_bwd = pl.BlockSpec((TM // GLA_CHUNK, 512), lambda i: (_bwd_block(i), 0))
    state_dims = (2, GLA_HEADS, GLA_DK, GLA_DV)
    consts = _gla_mask_constants()
    return pl.pallas_call(
        _gla_kernel,
        grid=(NB,),
        in_specs=[pl.BlockSpec(t.shape, lambda i, nd=t.ndim: (0,) * nd) for t in consts]
                 + [_tok(256), _tok(256), _tok(256), _tok(512), vt_fwd, dec_fwd,
                  bwd(256), bwd(256), bwd(256), bwd(512), vt_bwd, dec_bwd,
                  pl.BlockSpec((1, None) + state_dims, lambda i: (lat_seq(i), l, 0, 0, 0, 0))],
        out_specs=[_tok(512), bwd(512),
                   pl.BlockSpec((1,) + state_dims, lambda i: (jnp.minimum(i, NB_CTX - 1), 0, 0, 0, 0))],
        out_shape=[jax.ShapeDtypeStruct((TOKENS, 512), F32), jax.ShapeDtypeStruct((TOKENS, 512), F32),
                   jax.ShapeDtypeStruct((BATCH,) + state_dims, F32)],
        scratch_shapes=[pltpu.VMEM((2, 2, 2 * GLA_DV, LANES), F32)],
        compiler_params=_cp(("arbitrary",)),
        name="gla_scan",
    )(*consts, qef, kef, klf, av, avt, dec, qeb, keb, klb, av, avt, dec, state_gla)


def _attend_group(q_refs, keys, values_t, masks, sink_ref, g, o_ref):
    rows = q_refs[0].shape[0]
    lane = lax.broadcasted_iota(jnp.int32, (rows, LANES), 1)
    stacked = []
    for q in q_refs:
        q32 = q.astype(F32)
        stacked.append(jnp.where(lane < 64, q32, 0.0).astype(BF16))
        stacked.append(jnp.where(lane >= 64, q32, 0.0).astype(BF16))
    qs = jnp.concatenate(stacked, axis=0)
    col_head = lax.broadcasted_iota(jnp.int32, (1, ATTN_GROUP * rows), 1) // rows
    sink = jnp.zeros((1, ATTN_GROUP * rows), F32)
    for hh in range(ATTN_GROUP):
        sink = jnp.where(col_head == hh, sink_ref[ATTN_GROUP * g + hh][:, :1], sink)
    ss = []
    m = sink
    for kk, mk in zip(keys, masks):
        s = _dot_nt(kk, qs)
        if mk is not None:
            s = jnp.where(jnp.concatenate([mk] * ATTN_GROUP, axis=1), s, -jnp.inf)
        ss.append(s)
        m = jnp.maximum(m, jnp.max(s, axis=0, keepdims=True))
    den = jnp.exp(sink - m)
    acc = jnp.zeros((ATTN_HEAD_DIM, ATTN_GROUP * rows), F32)
    for s, vt in zip(ss, values_t):
        e = jnp.exp(s - m)
        den = den + jnp.sum(e, axis=0, keepdims=True)
        acc = acc + _dot(vt, e.astype(BF16))
    o = (acc / den).astype(BF16)
    for hh in range(ATTN_GROUP):
        h = ATTN_GROUP * g + hh
        o_ref[h * ATTN_HEAD_DIM:(h + 1) * ATTN_HEAD_DIM, :] = o[:, hh * rows:(hh + 1) * rows]


def _ctx_attn_kernel(q_ref, kd_ref, vt_ref, sink_ref, o_ref):
    for g in range(ATTN_KV_HEADS):
        gl = slice(g * LANES, (g + 1) * LANES)
        q_refs = [q_ref[:, (2 * g + pp) * LANES:(2 * g + pp + 1) * LANES] for pp in range(2)]
        vt = vt_ref[g * ATTN_HEAD_DIM:(g + 1) * ATTN_HEAD_DIM, :]
        _attend_group(q_refs, [kd_ref[:, gl]], [vt], [None], sink_ref, g, o_ref)


def _context_attention(l, cq, ckd, cvt, sink_t):
    return pl.pallas_call(
        _ctx_attn_kernel,
        grid=(BATCH,),
        in_specs=[pl.BlockSpec((SEQ, 512), lambda b: (b, 0)),
                  pl.BlockSpec((SEQ, 256), lambda b: (b, 0)),
                  pl.BlockSpec((LANES, SEQ), lambda b: (0, b)),
                  _layer_spec(l, (ATTN_HEADS, 1, LANES))],
        out_specs=pl.BlockSpec((512, SEQ), lambda b: (0, b)),
        out_shape=jax.ShapeDtypeStruct((512, CTX_TOKENS), BF16),
        compiler_params=_cp(("arbitrary",)),
        name="context_attention",
    )(cq, ckd, cvt, sink_t)


def _lat_attn_kernel(q_ref, kd_ref, vt_ref, kc_ref, vct_ref, sink_ref, o_ref):
    n = pl.program_id(1)
    nq = DEC_SEQ // ATTN_BLOCK
    prev0 = pl.multiple_of(jnp.maximum(n - 1, 0) * ATTN_BLOCK, ATTN_BLOCK)
    cur0 = pl.multiple_of(n * ATTN_BLOCK, ATTN_BLOCK)
    next0 = pl.multiple_of(jnp.minimum(n + 1, nq - 1) * ATTN_BLOCK, ATTN_BLOCK)
    kj = lax.broadcasted_iota(jnp.int32, (ATTN_BLOCK, ATTN_BLOCK), 0)
    qi = lax.broadcasted_iota(jnp.int32, (ATTN_BLOCK, ATTN_BLOCK), 1)
    m_prev = jnp.logical_and(kj >= qi, n > 0)
    m_next = jnp.logical_and(kj <= qi, n < nq - 1)
    for g in range(ATTN_KV_HEADS):
        gl = slice(g * LANES, (g + 1) * LANES)
        gr = slice(g * ATTN_HEAD_DIM, (g + 1) * ATTN_HEAD_DIM)
        q_refs = [q_ref[:, (2 * g + pp) * LANES:(2 * g + pp + 1) * LANES] for pp in range(2)]
        keys = [kd_ref[pl.ds(prev0, ATTN_BLOCK), gl], kd_ref[pl.ds(cur0, ATTN_BLOCK), gl],
                kd_ref[pl.ds(next0, ATTN_BLOCK), gl], kc_ref[0, :, gl]]
        vals = [vt_ref[gr, pl.ds(prev0, ATTN_BLOCK)], vt_ref[gr, pl.ds(cur0, ATTN_BLOCK)],
                vt_ref[gr, pl.ds(next0, ATTN_BLOCK)], vct_ref[0, gr, :]]
        _attend_group(q_refs, keys, vals, [m_prev, None, m_next, None], sink_ref, g, o_ref)


def _latent_attention(l, cq, ckd, cvt, kc, vct, sink_t):
    nq = DEC_SEQ // ATTN_BLOCK
    q0 = CTX_TOKENS // ATTN_BLOCK
    s0 = CTX_TOKENS // DEC_SEQ
    return pl.pallas_call(
        _lat_attn_kernel,
        grid=(DEC_BATCH, nq),
        in_specs=[pl.BlockSpec((ATTN_BLOCK, 512), lambda b, n: (q0 + b * nq + n, 0)),
                  pl.BlockSpec((DEC_SEQ, 256), lambda b, n: (s0 + b, 0)),
                  pl.BlockSpec((LANES, DEC_SEQ), lambda b, n: (0, s0 + b)),
                  pl.BlockSpec((1, None, PAST_LEN, 256), lambda b, n: (b, l, 0, 0)),
                  pl.BlockSpec((1, None, LANES, PAST_LEN), lambda b, n: (b, l, 0, 0)),
                  _layer_spec(l, (ATTN_HEADS, 1, LANES))],
        out_specs=pl.BlockSpec((512, ATTN_BLOCK), lambda b, n: (0, b * nq + n)),
        out_shape=jax.ShapeDtypeStruct((512, LAT_TOKENS), BF16),
        compiler_params=_cp(("arbitrary", "arbitrary")),
        name="latent_attention",
    )(cq, ckd, cvt, kc, vct, sink_t)


def _gelu(x):
    return 0.5 * x * (1.0 + jnp.tanh(0.7978845608028654 * (x + 0.044715 * (x * x * x))))


def _merge_kernel(alpha, xc_ref, xl_ref, mod_ref, of_ref, ob_ref, ag_ref, ng_ref, bz_ref, lg_ref, lb_ref,
                  ws_ref, bs_ref, cc_ref, cl_ref, mg_ref, wb_ref, wo_ref, g1_ref, b1_ref,
                  rwh_ref, rwl_ref, rb_ref, x1_ref, h2_ref, cw_ref):
    for part in range(TM // MERGE_ROWS):
        rows = slice(part * MERGE_ROWS, (part + 1) * MERGE_ROWS)
        _merge_rows(alpha, rows, xc_ref, xl_ref, mod_ref, of_ref, ob_ref, ag_ref, ng_ref, bz_ref, lg_ref,
                    lb_ref, ws_ref, bs_ref, cc_ref, cl_ref, mg_ref, wb_ref, wo_ref, g1_ref, b1_ref,
                    rwh_ref, rwl_ref, rb_ref, x1_ref, h2_ref, cw_ref)


def _merge_rows(alpha, rows, xc_ref, xl_ref, mod_ref, of_ref, ob_ref, ag_ref, ng_ref, bz_ref, lg_ref, lb_ref,
                ws_ref, bs_ref, cc_ref, cl_ref, mg_ref, wb_ref, wo_ref, g1_ref, b1_ref,
                rwh_ref, rwl_ref, rb_ref, x1_ref, h2_ref, cw_ref):
    i = pl.program_id(0)
    m = mod_ref[0]
    n_rows = rows.stop - rows.start
    o = of_ref[rows, :] + ob_ref[rows, :]
    parts = []
    for h in range(GLA_HEADS):
        oh = o[:, h * GLA_DV:(h + 1) * GLA_DV]
        parts.append(oh * lax.rsqrt(jnp.mean(oh * oh, axis=-1, keepdims=True) + LN_EPS))
    br_a = (jnp.concatenate(parts, axis=1) * ng_ref[...] * _silu(ag_ref[rows, :])).astype(BF16)
    z = _gelu(bz_ref[rows, :])
    u = z[:, :BRANCH_W]
    v = _layer_norm(z[:, BRANCH_W:], lg_ref[...], lb_ref[...]).astype(BF16)
    chunks = []
    for c in range(n_rows // GMLP_CHUNK):
        vc = v[c * GMLP_CHUNK:(c + 1) * GMLP_CHUNK]
        cols = [_dot(ws_ref[g], vc[:, g * GMLP_CH:(g + 1) * GMLP_CH]) for g in range(GMLP_GROUPS)]
        chunks.append(jnp.concatenate(cols, axis=1) + bs_ref[...])
    br_b = (u * jnp.concatenate(chunks, axis=0)).astype(BF16)
    br_ct = jnp.where(i < NB_CTX, cc_ref[:, rows], cl_ref[:, rows])
    projs = (_dot(br_a, wb_ref[0]), _dot(br_b, wb_ref[1]), _dot_tn(br_ct, wb_ref[2]))
    y = jnp.zeros((n_rows, D_MODEL), F32)
    for nbr, proj in enumerate(projs):
        y = y + (jnp.tanh(mg_ref[rows, nbr * D_MODEL:(nbr + 1) * D_MODEL]) + 1.0) * proj
    y = _dot(y.astype(BF16), wo_ref[...])
    x = jnp.where(i < NB_CTX, xc_ref[rows, :], xl_ref[rows, :])
    x1 = _layer_norm(alpha * x + m[:, 2 * D_MODEL:3 * D_MODEL] * y, g1_ref[...], b1_ref[...])
    x1_ref[rows, :] = x1
    h2 = x1 * (1.0 + m[:, 4 * D_MODEL:5 * D_MODEL]) + m[:, 3 * D_MODEL:4 * D_MODEL]
    h2_ref[rows, :] = h2.astype(BF16)
    h_hi, h_lo = _split(h2)
    logits = _dot(h_hi, rwh_ref[...]) + _dot(h_lo, rwh_ref[...]) + _dot(h_hi, rwl_ref[...]) + rb_ref[...]
    lane_i = lax.broadcasted_iota(jnp.int32, (n_rows, LANES), 1)
    lane = lane_i.astype(F32)
    lane_group = (lane_i >> 2).astype(F32)
    big = float(LANES)
    neg = -jnp.inf
    gl = jnp.where((lane_i >= N_EXPERTS) & (lane_i < N_EXPERTS + N_GROUPS), logits, neg)
    gmax = jnp.max(gl, axis=-1, keepdims=True)
    gsum = jnp.sum(jnp.exp(gl - gmax), axis=-1, keepdims=True)
    g_p = 1.0 / gsum
    g_i = jnp.min(jnp.where(gl == gmax, lane, big), axis=-1, keepdims=True) - float(N_EXPERTS)
    in_group = (lane_i < N_EXPERTS) & (lane_group == g_i)
    el = jnp.where(in_group, logits, neg)
    emax = jnp.max(el, axis=-1, keepdims=True)
    ee = jnp.exp(el - emax)
    e_prob = ee / jnp.sum(ee, axis=-1, keepdims=True)
    p1 = jnp.max(jnp.where(in_group, e_prob, neg), axis=-1, keepdims=True)
    i1 = jnp.min(jnp.where(in_group & (e_prob == p1), lane, big), axis=-1, keepdims=True)
    rest = in_group & (lane != i1)
    p2 = jnp.max(jnp.where(rest, e_prob, neg), axis=-1, keepdims=True)
    i2 = jnp.min(jnp.where(rest & (e_prob == p2), lane, big), axis=-1, keepdims=True)
    tot = p1 + p2
    cw_ref[rows, :] = (jnp.where(lane == i1, g_p * p1 / tot, 0.0)
                       + jnp.where(lane == i2, g_p * p2 / tot, 0.0))


def _merge(l, alpha, xc, xl, mod3, o_f, o_b, ag, norm_g, bz, ln_g, ln_b, ws, bs_t, brc_ctx, brc_lat, mg,
           wb, wo, g1, b1, rw_hi, rw_lo, rb):
    return pl.pallas_call(
        functools.partial(_merge_kernel, alpha),
        grid=(NB,),
        in_specs=[_ctx_tok(D_MODEL), _lat_tok(D_MODEL), _mod_spec(),
                  _tok(512), _tok(512), _tok(512), _layer_spec(l, (1, 512)),
                  _tok(1024), _layer_spec(l, (1, 512)), _layer_spec(l, (1, 512)),
                  _layer_spec(l, (GMLP_GROUPS, GMLP_CHUNK, GMLP_CHUNK)), _layer_spec(l, (GMLP_CHUNK, 512)),
                  pl.BlockSpec((512, TM), lambda i: (0, jnp.minimum(i, NB_CTX - 1))),
                  pl.BlockSpec((512, TM), lambda i: (0, jnp.maximum(i - NB_CTX, 0))),
                  _tok(3072), _layer_spec(l, (N_BRANCH, BRANCH_W, D_MODEL)), _layer_spec(l, (D_MODEL, D_MODEL)),
                  _layer_spec(l, (1, D_MODEL)), _layer_spec(l, (1, D_MODEL)),
                  _layer_spec(l, (D_MODEL, LANES)), _layer_spec(l, (D_MODEL, LANES)), _layer_spec(l, (1, LANES))],
        out_specs=[_tok(D_MODEL), _tok(D_MODEL), _tok(LANES)],
        out_shape=[jax.ShapeDtypeStruct((TOKENS, D_MODEL), F32),
                   jax.ShapeDtypeStruct((TOKENS, D_MODEL), BF16),
                   jax.ShapeDtypeStruct((TOKENS, LANES), F32)],
        compiler_params=_cp(("arbitrary",)),
        name="merge_ln1_router",
    )(xc, xl, mod3, o_f, o_b, ag, norm_g, bz, ln_g, ln_b, ws, bs_t, brc_ctx, brc_lat, mg,
      wb, wo, g1, b1, rw_hi, rw_lo, rb)


def _moe_kernel(alpha, h2_ref, cw_ref, wg_ref, wu_ref, wd_ref, x1_ref, mod_ref, g2_ref, b2_ref,
                oc_ref, ol_ref, acc_ref):
    i = pl.program_id(0)
    eb = pl.program_id(1)
    h2 = h2_ref[...]
    cw = cw_ref[...]
    lane = lax.broadcasted_iota(jnp.int32, (MOE_TM, LANES), 1)
    hid = []
    for k in range(MOE_EB):
        w_e = jnp.sum(jnp.where(lane == eb * MOE_EB + k, cw, 0.0), axis=-1, keepdims=True)
        hid.append((_silu(_dot(h2, wg_ref[k])) * _dot(h2, wu_ref[k]) * w_e).astype(BF16))
    part = _dot(jnp.concatenate(hid, axis=1), wd_ref[...].reshape(MOE_EB * EXPERT_FF, D_MODEL))

    @pl.when(eb == 0)
    def _():
        acc_ref[...] = part

    @pl.when(eb > 0)
    def _():
        acc_ref[...] += part

    @pl.when(eb == N_EXPERTS // MOE_EB - 1)
    def _():
        m = mod_ref[0]
        y = _layer_norm(alpha * x1_ref[...] + m[:, 5 * D_MODEL:6 * D_MODEL] * acc_ref[...],
                        g2_ref[...], b2_ref[...])

        @pl.when(i < MOE_CTX_BLOCKS)
        def _():
            oc_ref[...] = y

        @pl.when(i >= MOE_CTX_BLOCKS)
        def _():
            ol_ref[...] = y


def _moe(l, alpha, h2, cw, wg, wu, wd, x1, mod3, g2, b2):
    per = MOE_TM // TM
    return pl.pallas_call(
        functools.partial(_moe_kernel, alpha),
        grid=(TOKENS // MOE_TM, N_EXPERTS // MOE_EB),
        in_specs=[pl.BlockSpec((MOE_TM, D_MODEL), lambda i, e: (i, 0)),
                  pl.BlockSpec((MOE_TM, LANES), lambda i, e: (i, 0)),
                  pl.BlockSpec((None, MOE_EB, D_MODEL, EXPERT_FF), lambda i, e: (l, e, 0, 0)),
                  pl.BlockSpec((None, MOE_EB, D_MODEL, EXPERT_FF), lambda i, e: (l, e, 0, 0)),
                  pl.BlockSpec((None, MOE_EB, EXPERT_FF, D_MODEL), lambda i, e: (l, e, 0, 0)),
                  pl.BlockSpec((MOE_TM, D_MODEL), lambda i, e: (i, 0)),
                  pl.BlockSpec((1, 1, 6 * D_MODEL), lambda i, e: (_mod_block(i * per), 0, 0)),
                  pl.BlockSpec((None, 1, D_MODEL), lambda i, e: (l, 0, 0)),
                  pl.BlockSpec((None, 1, D_MODEL), lambda i, e: (l, 0, 0))],
        out_specs=[pl.BlockSpec((MOE_TM, D_MODEL), lambda i, e: (jnp.minimum(i, MOE_CTX_BLOCKS - 1), 0)),
                   pl.BlockSpec((MOE_TM, D_MODEL), lambda i, e: (jnp.maximum(i - MOE_CTX_BLOCKS, 0), 0))],
        out_shape=[jax.ShapeDtypeStruct((CTX_TOKENS, D_MODEL), F32),
                   jax.ShapeDtypeStruct((LAT_TOKENS, D_MODEL), F32)],
        scratch_shapes=[pltpu.VMEM((MOE_TM, D_MODEL), F32)],
        compiler_params=_cp(("arbitrary", "arbitrary")),
        name="moe_ln2",
    )(h2, cw, wg, wu, wd, x1, mod3, g2, b2)


def _rope_tables():
    pos = jnp.arange(DEC_SEQ)
    row = (pos // GRID_W).astype(F32)
    col = (pos % GRID_W).astype(F32)
    quarter = ATTN_HEAD_DIM // 4
    inv_freq = jnp.power(ROPE_BASE, -jnp.arange(quarter, dtype=F32) / quarter)
    j = jnp.arange(ATTN_HEAD_DIM)
    p = jnp.where((j // (2 * quarter))[None, :] == 0, row[:, None], col[:, None])
    ang = p * inv_freq[j % quarter][None, :]
    sign = jnp.where((j % (2 * quarter)) < quarter, -1.0, 1.0).astype(F32)
    cos = jnp.tile(jnp.cos(ang), (1, ATTN_HEADS))
    sin = jnp.tile(jnp.sin(ang) * sign[None, :], (1, ATTN_HEADS))
    cos = jnp.concatenate([jnp.ones((TM, 512), F32), cos], axis=0)
    sin = jnp.concatenate([jnp.zeros((TM, 512), F32), sin], axis=0)
    return cos, sin


def _dup_heads(t):
    h0, h1 = t[..., :ATTN_HEAD_DIM], t[..., ATTN_HEAD_DIM:]
    return jnp.concatenate([h0, h0, h1, h1], axis=-1)


def _split_w_in(w):
    widths = (256, 256, 512, 512, GLA_RANK, GLA_RANK, 1024, 512, 128, 128, 3072)
    parts, start = [], 0
    for wd in widths:
        parts.append(w[..., start:start + wd].astype(BF16))
        start += wd
    aq, ak, av, ag, lrf, lrb, bz, cq, ck, cv, mg = parts
    mg = mg * 0.5
    pad = jnp.zeros(w.shape[:-1] + (LANES - 2 * GLA_RANK,), BF16)
    return [aq, ak, av, ag, jnp.concatenate([lrf, lrb, pad], axis=-1), bz, cq, ck, cv, mg]


def kernel(x_prompt, x_sample, state_gla, cache_k, cache_v, c, c_ctx, ada_w, ada_b, w_in, gla_gate_w, gla_gate_b, gla_norm_g, gmlp_ln_g, gmlp_ln_b, gmlp_ws, gmlp_bs, attn_sink, w_branch, w_out, ln1_g, ln1_b, ln2_g, ln2_b, router_group_w, router_group_b, router_expert_w, router_expert_b, expert_w_gate, expert_w_up, expert_w_down):
    alpha = (2.0 * DEPTH) ** 0.25
    xc = x_prompt.reshape(CTX_TOKENS, D_MODEL)
    xl = x_sample.reshape(LAT_TOKENS, D_MODEL)
    cond = jnp.concatenate([c_ctx[None, :], c, jnp.zeros((MOD_ROWS - 1 - DEC_BATCH, D_MODEL), F32)], axis=0)
    cos_t, sin_t = _rope_tables()

    ada_b3 = ada_b[:, None, :]
    w_parts = _split_w_in(w_in)
    zrow = jnp.zeros((DEPTH, GLA_RANK, 256), F32)
    zpad = jnp.zeros((DEPTH, LANES - 2 * GLA_RANK, 256), F32)
    gw_p = jnp.stack([jnp.concatenate([gla_gate_w[:, 0], zrow, zpad], axis=1),
                      jnp.concatenate([zrow, gla_gate_w[:, 1], zpad], axis=1)], axis=1).astype(BF16)
    gb_p = gla_gate_b[:, :, None, :]
    sink_t = jnp.broadcast_to(attn_sink[:, :, None, None], (DEPTH, ATTN_HEADS, 1, LANES))
    kc = _dup_heads(cache_k.reshape(DEC_BATCH, DEPTH, PAST_LEN, 128)).astype(BF16)
    vct = jnp.swapaxes(cache_v.reshape(DEC_BATCH, DEPTH, PAST_LEN, 128), 2, 3).astype(BF16)
    rw = jnp.concatenate([router_expert_w, router_group_w,
                          jnp.zeros((DEPTH, D_MODEL, LANES - N_EXPERTS - N_GROUPS), F32)], axis=2)
    rw_hi = rw.astype(BF16)
    rw_lo = (rw - rw_hi.astype(F32)).astype(BF16)
    rb = jnp.concatenate([router_expert_b, router_group_b,
                          jnp.zeros((DEPTH, LANES - N_EXPERTS - N_GROUPS), F32)], axis=1)[:, None, :]
    bs_t = jnp.repeat(jnp.swapaxes(gmlp_bs, 1, 2), GMLP_CH, axis=2)
    ws_b = gmlp_ws.astype(BF16)
    wb_b = (0.5 * w_branch).astype(BF16)
    wo_b = w_out.astype(BF16)
    wg_b = expert_w_gate.astype(BF16)
    wu_b = expert_w_up.astype(BF16)
    wd_b = expert_w_down.astype(BF16)
    row = lambda t: t[:, None, :]

    states, keys, values = [], [], []
    for l in range(DEPTH):
        mod3 = _modulation(l, cond, ada_w, ada_b3).reshape(MOD_ROWS, 1, 6 * D_MODEL)
        (qef, kef, klf, qeb, keb, klb, dec, av, avt, ag, bz, cq, ckd, cvt, ck, cv, mg) = _in_projection(
            l, xc, xl, mod3, cos_t, sin_t, gw_p, gb_p, w_parts)
        o_f, o_b, s_fin = _gla(l, qef, kef, klf, qeb, keb, klb, dec, av, avt, state_gla)
        brc_ctx = _context_attention(l, cq, ckd, cvt, sink_t)
        brc_lat = _latent_attention(l, cq, ckd, cvt, kc, vct, sink_t)
        x1, h2, cw = _merge(l, alpha, xc, xl, mod3, o_f, o_b, ag, row(gla_norm_g), bz,
                            row(gmlp_ln_g), row(gmlp_ln_b), ws_b, bs_t, brc_ctx, brc_lat, mg, wb_b, wo_b,
                            row(ln1_g), row(ln1_b), rw_hi, rw_lo, rb)
        xc, xl = _moe(l, alpha, h2, cw, wg_b, wu_b, wd_b, x1, mod3, row(ln2_g), row(ln2_b))
        states.append(s_fin)
        keys.append(ck.reshape(BATCH, SEQ, ATTN_KV_HEADS, ATTN_HEAD_DIM))
        values.append(cv.reshape(BATCH, SEQ, ATTN_KV_HEADS, ATTN_HEAD_DIM))
    return (xc.reshape(BATCH, SEQ, D_MODEL), xl.reshape(DEC_BATCH, DEC_SEQ, D_MODEL),
            jnp.stack(states, axis=1), jnp.stack(keys, axis=1), jnp.stack(values, axis=1))
```

```python
import functools

import jax
import jax.numpy as jnp
from jax import lax
from jax.experimental import pallas as pl
from jax.experimental.pallas import tpu as pltpu

F32 = jnp.float32
BF16 = jnp.bfloat16

D_MODEL = 1024
BATCH = 32
SEQ = 256
DEPTH = 2
DEC_BATCH = 8
DEC_SEQ = 1024
PAST_LEN = 512
GRID_W = 64
GLA_HEADS = 4
GLA_DV = 128
GLA_DK = 64
GLA_RANK = 16
GLA_TAU = 16.0
GLA_CHUNK = 32
GMLP_CHUNK = 128
GMLP_CH = 128
GMLP_GROUPS = 4
ATTN_HEADS = 8
ATTN_KV_HEADS = 2
ATTN_GROUP = ATTN_HEADS // ATTN_KV_HEADS
ATTN_HEAD_DIM = 64
WINDOW = 128
ATTN_BLOCK = 128
ROPE_BASE = 10000.0
BRANCH_W = 512
N_BRANCH = 3
N_GROUPS = 4
EXPERTS_PER_GROUP = 4
N_EXPERTS = 16
EXPERT_FF = 256
LN_EPS = 1e-5

LANES = 128
TM = 256
CTX_TOKENS = BATCH * SEQ
LAT_TOKENS = DEC_BATCH * DEC_SEQ
TOKENS = CTX_TOKENS + LAT_TOKENS
NB_CTX = CTX_TOKENS // TM
NB_LAT = LAT_TOKENS // TM
NB = NB_CTX + NB_LAT
LAT_BLOCKS_PER_SEQ = DEC_SEQ // TM
MOD_ROWS = 16
GLA_SUB = 128
CHUNKS_PER_SUB = GLA_SUB // GLA_CHUNK
MERGE_ROWS = 128
MERGE_STAGES = 5
ROUTER_ROWS = 24
HS_ALIGN = 16
HS_W = D_MODEL + 3 * LANES
STAGE_ROWS = TM + N_GROUPS * HS_ALIGN + TM
_MAX_PAD = NB * (HS_ALIGN - 1)
HS_CAP = -(-(TOKENS + _MAX_PAD + 2 * TM) // TM) * TM
MOE_TILES = (TOKENS + N_GROUPS * _MAX_PAD) // TM + 2 * N_GROUPS

VMEM_LIMIT = 56 * 1024 * 1024


def _cp(sem):
    return pltpu.CompilerParams(dimension_semantics=sem, vmem_limit_bytes=VMEM_LIMIT)


def _dot(a, b):
    return jnp.dot(a, b, preferred_element_type=F32)


def _dot_nt(a, b):
    return lax.dot_general(a, b, (((1,), (1,)), ((), ())), preferred_element_type=F32)


def _dot_tn(a, b):
    return lax.dot_general(a, b, (((0,), (0,)), ((), ())), preferred_element_type=F32)


def _split(x):
    hi = x.astype(BF16)
    lo = (x - hi.astype(F32)).astype(BF16)
    return hi, lo


def _sigmoid(x):
    return 0.5 * (jnp.tanh(0.5 * x) + 1.0)


def _silu(x):
    return x * _sigmoid(x)


def _layer_norm(x, g, b):
    mu = jnp.mean(x, axis=-1, keepdims=True)
    xc = x - mu
    var = jnp.mean(xc * xc, axis=-1, keepdims=True)
    return xc * lax.rsqrt(var + LN_EPS) * g + b


def _mod_block(i):
    return jnp.where(i < NB_CTX, 0, 1 + (i - NB_CTX) // LAT_BLOCKS_PER_SEQ)


def _tok(width):
    return pl.BlockSpec((TM, width), lambda i: (i, 0))


def _ctx_tok(width):
    return pl.BlockSpec((TM, width), lambda i: (jnp.minimum(i, NB_CTX - 1), 0))


def _lat_tok(width):
    return pl.BlockSpec((TM, width), lambda i: (jnp.maximum(i - NB_CTX, 0), 0))


def _layer_spec(l, shape):
    return pl.BlockSpec((None,) + shape, lambda *_: (l,) + (0,) * len(shape))


def _mod_spec():
    return pl.BlockSpec((1, 1, 6 * D_MODEL), lambda i: (_mod_block(i), 0, 0))


def _mod_kernel(cond_ref, w_ref, b_ref, o_ref):
    s_hi, s_lo = _split(_silu(cond_ref[...]))
    w_hi, w_lo = _split(w_ref[...])
    o_ref[...] = _dot(s_hi, w_hi) + _dot(s_lo, w_hi) + _dot(s_hi, w_lo) + b_ref[...]


def _modulation(l, cond, ada_w, ada_b3):
    tn = 1536
    return pl.pallas_call(
        _mod_kernel,
        grid=(6 * D_MODEL // tn,),
        in_specs=[pl.BlockSpec((MOD_ROWS, D_MODEL), lambda j: (0, 0)),
                  pl.BlockSpec((None, D_MODEL, tn), lambda j: (l, 0, j)),
                  pl.BlockSpec((None, 1, tn), lambda j: (l, 0, j))],
        out_specs=pl.BlockSpec((MOD_ROWS, tn), lambda j: (0, j)),
        out_shape=jax.ShapeDtypeStruct((MOD_ROWS, 6 * D_MODEL), F32),
        compiler_params=_cp(("arbitrary",)),
        name="modulation",
    )(cond, ada_w, ada_b3)


def _rope(x, cos, sin):
    n = x.shape[1]
    lane = lax.broadcasted_iota(jnp.int32, x.shape, 1)
    partner = jnp.where((lane & 31) < 16, pltpu.roll(x, n - 16, 1), pltpu.roll(x, 16, 1))
    return x * cos + partner * sin


def _inproj_kernel(xc_ref, xl_ref, mod_ref, cos_ref, sin_ref, gw_ref, gb_ref,
                   waq_ref, wak_ref, wav_ref, wag_ref, wlr_ref, wbz_ref, wcq_ref, wck_ref, wcv_ref, wmg_ref,
                   qef_ref, kef_ref, klf_ref, qeb_ref, keb_ref, klb_ref, dec_ref,
                   av_ref, avt_ref, ag_ref, bz_ref,
                   cq_ref, ckd_ref, cvt_ref, ck_ref, cv_ref, mg_ref):
    i = pl.program_id(0)
    m = mod_ref[0]
    x = jnp.where(i < NB_CTX, xc_ref[...], xl_ref[...])
    h = (x * (1.0 + m[:, D_MODEL:2 * D_MODEL]) + m[:, 0:D_MODEL]).astype(BF16)

    aq = _dot(h, waq_ref[...])
    ak = _dot(h, wak_ref[...])
    lr = _dot(h, wlr_ref[...]).astype(BF16)
    zs = [_dot(lr, gw_ref[d]) + gb_ref[d] for d in range(2)]
    mg_ref[...] = _dot(h, wmg_ref[...])
    r = lax.broadcasted_iota(jnp.int32, (TM, TM), 0)
    c = lax.broadcasted_iota(jnp.int32, (TM, TM), 1)
    same = (r >> 5) == (c >> 5)
    bs = []
    for d in range(2):
        g = (jnp.minimum(zs[d], 0.0) - jnp.log(1.0 + jnp.exp(-jnp.abs(zs[d])))) / GLA_TAU
        tri_b = jnp.where(same & ((c <= r) if d == 0 else (c >= r)), 1.0, 0.0).astype(BF16)
        g_hi, g_lo = _split(g)
        bs.append(_dot(tri_b, g_hi) + _dot(tri_b, g_lo))
    bz_ref[...] = _dot(h, wbz_ref[...])
    ag_ref[...] = _dot(h, wag_ref[...])
    av = _dot(h, wav_ref[...])
    av_ref[...] = av.astype(BF16)
    avt_ref[...] = av.T.astype(BF16)
    decs = []
    for d, (qe_ref, ke_ref, kl_ref) in enumerate(((qef_ref, kef_ref, klf_ref), (qeb_ref, keb_ref, klb_ref))):
        b = bs[d]
        last = GLA_CHUNK - 1 if d == 0 else 0
        tot = [b[n * GLA_CHUNK + last:n * GLA_CHUNK + last + 1, :] for n in range(TM // GLA_CHUNK)]
        bl = jnp.concatenate([jnp.broadcast_to(t, (GLA_CHUNK, t.shape[1])) for t in tot], axis=0)
        qe_ref[...] = (aq * jnp.exp(b) * (GLA_DK ** -0.5)).astype(BF16)
        ke_ref[...] = (ak * jnp.exp(-b)).astype(BF16)
        kl_ref[...] = (ak * jnp.exp(bl - b)).astype(BF16)
        decs.append(jnp.exp(jnp.concatenate(tot, axis=0)))
    dec_ref[...] = jnp.concatenate(decs, axis=1)

    cos = cos_ref[...]
    sin = sin_ref[...]
    cq_ref[...] = (_rope(_dot(h, wcq_ref[...]), cos, sin) * (ATTN_HEAD_DIM ** -0.5)).astype(BF16)
    ck = _dot(h, wck_ref[...])
    cv = _dot(h, wcv_ref[...])
    kr = _rope(ck, cos[:, :LANES], sin[:, :LANES])
    swapped = pltpu.roll(kr, ATTN_HEAD_DIM, 1)
    lane = lax.broadcasted_iota(jnp.int32, (TM, LANES), 1)
    ckd_ref[...] = jnp.concatenate([jnp.where(lane < 64, kr, swapped), jnp.where(lane < 64, swapped, kr)],
                                   axis=1).astype(BF16)
    cvt_ref[...] = cv.T.astype(BF16)

    @pl.when(i < NB_CTX)
    def _():
        ck_ref[...] = ck
        cv_ref[...] = cv


def _rope_block(i):
    return jnp.where(i < NB_CTX, 0, 1 + (i - NB_CTX) % LAT_BLOCKS_PER_SEQ)


def _in_projection(l, xc, xl, mod3, cos_t, sin_t, gw_p, gb_p, w_parts):
    def tok_out(width, dt):
        return _tok(width), jax.ShapeDtypeStruct((TOKENS, width), dt)

    def feat_out(width):
        return (pl.BlockSpec((width, TM), lambda i: (0, i)), jax.ShapeDtypeStruct((width, TOKENS), BF16))

    def ctx_out(width):
        return _ctx_tok(width), jax.ShapeDtypeStruct((CTX_TOKENS, width), F32)

    dec_out = (pl.BlockSpec((TM // GLA_CHUNK, 512), lambda i: (i, 0)),
               jax.ShapeDtypeStruct((TOKENS // GLA_CHUNK, 512), F32))
    outs = [tok_out(256, BF16)] * 6 + [dec_out] + [
        tok_out(512, BF16), feat_out(512), tok_out(512, F32), tok_out(1024, F32), tok_out(512, BF16),
        tok_out(256, BF16), feat_out(LANES), ctx_out(LANES), ctx_out(LANES), tok_out(3072, F32)]
    return pl.pallas_call(
        _inproj_kernel,
        grid=(NB,),
        in_specs=[_ctx_tok(D_MODEL), _lat_tok(D_MODEL), _mod_spec(),
                  pl.BlockSpec((TM, 512), lambda i: (_rope_block(i), 0)),
                  pl.BlockSpec((TM, 512), lambda i: (_rope_block(i), 0)),
                  _layer_spec(l, (2, LANES, 256)), _layer_spec(l, (2, 1, 256))]
                 + [_layer_spec(l, (D_MODEL, w.shape[-1])) for w in w_parts],
        out_specs=[o[0] for o in outs],
        out_shape=[o[1] for o in outs],
        compiler_params=_cp(("arbitrary",)),
        name="in_projection",
    )(xc, xl, mod3, cos_t, sin_t, gw_p, gb_p, *w_parts)


def _gla_direction(qe, ke, kl, v, vt, dec, masks, state_ref, reverse):
    tri, half_masks, chunk_mask, head_diag = masks
    tri = tri[int(reverse)]
    outs = []
    for p in range(GLA_HEADS // 2):
        ps = slice(p * LANES, (p + 1) * LANES)
        qp = qe[:, ps]
        kep = ke[:, ps]
        klp = kl[:, ps]
        vp = v[:, p * 2 * GLA_DV:(p + 1) * 2 * GLA_DV]
        vtp = vt[p * 2 * GLA_DV:(p + 1) * 2 * GLA_DV, :]
        halves = []
        for hh in range(2):
            s = jnp.where(tri, _dot_nt(qp * half_masks[hh], kep), 0.0).astype(BF16)
            halves.append(_dot(s, vp[:, hh * GLA_DV:(hh + 1) * GLA_DV]))
        k_exp = jnp.concatenate([klp] * CHUNKS_PER_SUB, axis=1) * chunk_mask
        q_exp = jnp.concatenate([qp] * CHUNKS_PER_SUB, axis=1) * chunk_mask
        kvt = _dot(vtp, k_exp)
        st = state_ref[p]
        entering = [None] * CHUNKS_PER_SUB
        order = range(CHUNKS_PER_SUB - 1, -1, -1) if reverse else range(CHUNKS_PER_SUB)
        for n in order:
            entering[n] = st
            st = dec[n:n + 1, ps] * st + jnp.where(head_diag, kvt[:, n * LANES:(n + 1) * LANES], 0.0)
        state_ref[p] = st
        st_stack = jnp.concatenate(entering, axis=1).astype(BF16)
        inter = _dot_nt(q_exp, st_stack)
        outs.append(halves[0] + inter[:, :GLA_DV])
        outs.append(halves[1] + inter[:, GLA_DV:])
    return jnp.concatenate(outs, axis=1)


def _gla_mask_constants():
    r = jnp.arange(GLA_SUB)[:, None]
    c = jnp.arange(GLA_SUB)[None, :]
    same = (r // GLA_CHUNK) == (c // GLA_CHUNK)
    tri = jnp.stack([same & (c <= r), same & (c >= r)]).astype(F32)
    lane = jnp.arange(LANES)[None, :]
    halves = jnp.stack([jnp.broadcast_to(lane < 64, (GLA_SUB, LANES)),
                        jnp.broadcast_to(lane >= 64, (GLA_SUB, LANES))]).astype(BF16)
    col_chunk = jnp.arange(CHUNKS_PER_SUB * LANES)[None, :] // LANES
    chunk_mask = ((r // GLA_CHUNK) == col_chunk).astype(BF16)
    head_diag = ((jnp.arange(2 * GLA_DV)[:, None] // GLA_DV) == (lane // GLA_DK)).astype(F32)
    return tri, halves, chunk_mask, head_diag


def _gla_kernel(tri_ref, half_ref, cmask_ref, hdiag_ref,
                qf_ref, kef_ref, klf_ref, vf_ref, vtf_ref, decf_ref,
                qb_ref, keb_ref, klb_ref, vb_ref, vtb_ref, decb_ref,
                s0_ref, of_ref, ob_ref, sfin_ref, state_ref):
    i = pl.program_id(0)
    @pl.when(i < NB_CTX)
    def _():
        state_ref[...] = jnp.zeros_like(state_ref)

    @pl.when(jnp.logical_and(i >= NB_CTX, (i - NB_CTX) % LAT_BLOCKS_PER_SEQ == 0))
    def _():
        zero = jnp.zeros((GLA_DK, GLA_DV), F32)
        for d in range(2):
            for p in range(2):
                pair = jnp.concatenate(
                    [jnp.concatenate([s0_ref[0, d, 2 * p], zero], axis=1),
                     jnp.concatenate([zero, s0_ref[0, d, 2 * p + 1]], axis=1)], axis=0)
                state_ref[d, p] = pair.T

    masks = ((tri_ref[0] > 0.5, tri_ref[1] > 0.5), (half_ref[0], half_ref[1]), cmask_ref[...],
             hdiag_ref[...] > 0.5)
    n_sub = TM // GLA_SUB
    for j in range(n_sub):
        r0 = j * GLA_SUB
        rs = slice(r0, r0 + GLA_SUB)
        of_ref[rs, :] = _gla_direction(
            qf_ref[rs, :], kef_ref[rs, :], klf_ref[rs, :], vf_ref[rs, :], vtf_ref[:, rs],
            decf_ref[j * CHUNKS_PER_SUB:(j + 1) * CHUNKS_PER_SUB, :256], masks, state_ref.at[0], False)
        jb = n_sub - 1 - j
        rb = slice(jb * GLA_SUB, (jb + 1) * GLA_SUB)
        ob_ref[rb, :] = _gla_direction(
            qb_ref[rb, :], keb_ref[rb, :], klb_ref[rb, :], vb_ref[rb, :], vtb_ref[:, rb],
            decb_ref[jb * CHUNKS_PER_SUB:(jb + 1) * CHUNKS_PER_SUB, 256:], masks, state_ref.at[1], True)

    @pl.when(i < NB_CTX)
    def _():
        for d in range(2):
            for p in range(2):
                sp = state_ref[d, p].T
                sfin_ref[0, d, 2 * p] = sp[:GLA_DK, :GLA_DV]
                sfin_ref[0, d, 2 * p + 1] = sp[GLA_DK:, GLA_DV:]


def _bwd_block(i):
    k = (i - NB_CTX) % LAT_BLOCKS_PER_SEQ
    return jnp.where(i < NB_CTX, i, i - k + (LAT_BLOCKS_PER_SEQ - 1 - k))


def _gla(l, qef, kef, klf, qeb, keb, klb, dec, av, avt, state_gla):
    def bwd(width):
        return pl.BlockSpec((TM, width), lambda i: (_bwd_block(i), 0))

    def lat_seq(i):
        return jnp.maximum(i - NB_CTX, 0) // LAT_BLOCKS_PER_SEQ

    vt_fwd = pl.BlockSpec((512, TM), lambda i: (0, i))
    vt_bwd = pl.BlockSpec((512, TM), lambda i: (0, _bwd_block(i)))
    dec_fwd = pl.BlockSpec((TM // GLA_CHUNK, 512), lambda i: (i, 0))
    dec_bwd = pl.BlockSpec((TM // GLA_CHUNK, 512), lambda i: (_bwd_block(i), 0))
    state_dims = (2, GLA_HEADS, GLA_DK, GLA_DV)
    consts = _gla_mask_constants()
    return pl.pallas_call(
        _gla_kernel,
        grid=(NB,),
        in_specs=[pl.BlockSpec(t.shape, lambda i, nd=t.ndim: (0,) * nd) for t in consts]
                 + [_tok(256), _tok(256), _tok(256), _tok(512), vt_fwd, dec_fwd,
                  bwd(256), bwd(256), bwd(256), bwd(512), vt_bwd, dec_bwd,
                  pl.BlockSpec((1, None) + state_dims, lambda i: (lat_seq(i), l, 0, 0, 0, 0))],
        out_specs=[_tok(512), bwd(512),
                   pl.BlockSpec((1,) + state_dims, lambda i: (jnp.minimum(i, NB_CTX - 1), 0, 0, 0, 0))],
        out_shape=[jax.ShapeDtypeStruct((TOKENS, 512), F32), jax.ShapeDtypeStruct((TOKENS, 512), F32),
                   jax.ShapeDtypeStruct((BATCH,) + state_dims, F32)],
        scratch_shapes=[pltpu.VMEM((2, 2, 2 * GLA_DV, LANES), F32)],
        compiler_params=_cp(("arbitrary",)),
        name="gla_scan",
    )(*consts, qef, kef, klf, av, avt, dec, qeb, keb, klb, av, avt, dec, state_gla)


def _attend_group(q_refs, keys, values_t, masks, sink_ref, g, o_ref):
    rows = q_refs[0].shape[0]
    lane = lax.broadcasted_iota(jnp.int32, (rows, LANES), 1)
    stacked = []
    for q in q_refs:
        q32 = q.astype(F32)
        stacked.append(jnp.where(lane < 64, q32, 0.0).astype(BF16))
        stacked.append(jnp.where(lane >= 64, q32, 0.0).astype(BF16))
    qs = jnp.concatenate(stacked, axis=0)
    col_head = lax.broadcasted_iota(jnp.int32, (1, ATTN_GROUP * rows), 1) // rows
    sink = jnp.zeros((1, ATTN_GROUP * rows), F32)
    for hh in range(ATTN_GROUP):
        sink = jnp.where(col_head == hh, sink_ref[ATTN_GROUP * g + hh][:, :1], sink)
    ss = []
    m = sink
    for kk, mk in zip(keys, masks):
        s = _dot_nt(kk, qs)
        if mk is not None:
            s = jnp.where(jnp.concatenate([mk] * ATTN_GROUP, axis=1), s, -jnp.inf)
        ss.append(s)
        m = jnp.maximum(m, jnp.max(s, axis=0, keepdims=True))
    den = jnp.exp(sink - m)
    acc = jnp.zeros((ATTN_HEAD_DIM, ATTN_GROUP * rows), F32)
    for s, vt in zip(ss, values_t):
        e = jnp.exp(s - m)
        den = den + jnp.sum(e, axis=0, keepdims=True)
        acc = acc + _dot(vt, e.astype(BF16))
    o = (acc / den).astype(BF16)
    for hh in range(ATTN_GROUP):
        h = ATTN_GROUP * g + hh
        o_ref[h * ATTN_HEAD_DIM:(h + 1) * ATTN_HEAD_DIM, :] = o[:, hh * rows:(hh + 1) * rows]


def _ctx_attn_kernel(q_ref, kd_ref, vt_ref, sink_ref, o_ref):
    for g in range(ATTN_KV_HEADS):
        gl = slice(g * LANES, (g + 1) * LANES)
        q_refs = [q_ref[:, (2 * g + pp) * LANES:(2 * g + pp + 1) * LANES] for pp in range(2)]
        vt = vt_ref[g * ATTN_HEAD_DIM:(g + 1) * ATTN_HEAD_DIM, :]
        _attend_group(q_refs, [kd_ref[:, gl]], [vt], [None], sink_ref, g, o_ref)


def _context_attention(l, cq, ckd, cvt, sink_t):
    return pl.pallas_call(
        _ctx_attn_kernel,
        grid=(BATCH,),
        in_specs=[pl.BlockSpec((SEQ, 512), lambda b: (b, 0)),
                  pl.BlockSpec((SEQ, 256), lambda b: (b, 0)),
                  pl.BlockSpec((LANES, SEQ), lambda b: (0, b)),
                  _layer_spec(l, (ATTN_HEADS, 1, LANES))],
        out_specs=pl.BlockSpec((512, SEQ), lambda b: (0, b)),
        out_shape=jax.ShapeDtypeStruct((512, CTX_TOKENS), BF16),
        compiler_params=_cp(("arbitrary",)),
        name="context_attention",
    )(cq, ckd, cvt, sink_t)


def _lat_attn_kernel(q_ref, kd_ref, vt_ref, kc_ref, vct_ref, sink_ref, o_ref):
    n = pl.program_id(1)
    nq = DEC_SEQ // ATTN_BLOCK
    prev0 = pl.multiple_of(jnp.maximum(n - 1, 0) * ATTN_BLOCK, ATTN_BLOCK)
    cur0 = pl.multiple_of(n * ATTN_BLOCK, ATTN_BLOCK)
    next0 = pl.multiple_of(jnp.minimum(n + 1, nq - 1) * ATTN_BLOCK, ATTN_BLOCK)
    kj = lax.broadcasted_iota(jnp.int32, (ATTN_BLOCK, ATTN_BLOCK), 0)
    qi = lax.broadcasted_iota(jnp.int32, (ATTN_BLOCK, ATTN_BLOCK), 1)
    m_prev = jnp.logical_and(kj >= qi, n > 0)
    m_next = jnp.logical_and(kj <= qi, n < nq - 1)
    for g in range(ATTN_KV_HEADS):
        gl = slice(g * LANES, (g + 1) * LANES)
        gr = slice(g * ATTN_HEAD_DIM, (g + 1) * ATTN_HEAD_DIM)
        q_refs = [q_ref[:, (2 * g + pp) * LANES:(2 * g + pp + 1) * LANES] for pp in range(2)]
        keys = [kd_ref[pl.ds(prev0, ATTN_BLOCK), gl], kd_ref[pl.ds(cur0, ATTN_BLOCK), gl],
                kd_ref[pl.ds(next0, ATTN_BLOCK), gl], kc_ref[0, :, gl]]
        vals = [vt_ref[gr, pl.ds(prev0, ATTN_BLOCK)], vt_ref[gr, pl.ds(cur0, ATTN_BLOCK)],
                vt_ref[gr, pl.ds(next0, ATTN_BLOCK)], vct_ref[0, gr, :]]
        _attend_group(q_refs, keys, vals, [m_prev, None, m_next, None], sink_ref, g, o_ref)


def _latent_attention(l, cq, ckd, cvt, kc, vct, sink_t):
    nq = DEC_SEQ // ATTN_BLOCK
    q0 = CTX_TOKENS // ATTN_BLOCK
    s0 = CTX_TOKENS // DEC_SEQ
    return pl.pallas_call(
        _lat_attn_kernel,
        grid=(DEC_BATCH, nq),
        in_specs=[pl.BlockSpec((ATTN_BLOCK, 512), lambda b, n: (q0 + b * nq + n, 0)),
                  pl.BlockSpec((DEC_SEQ, 256), lambda b, n: (s0 + b, 0)),
                  pl.BlockSpec((LANES, DEC_SEQ), lambda b, n: (0, s0 + b)),
                  pl.BlockSpec((1, None, PAST_LEN, 256), lambda b, n: (b, l, 0, 0)),
                  pl.BlockSpec((1, None, LANES, PAST_LEN), lambda b, n: (b, l, 0, 0)),
                  _layer_spec(l, (ATTN_HEADS, 1, LANES))],
        out_specs=pl.BlockSpec((512, ATTN_BLOCK), lambda b, n: (0, b * nq + n)),
        out_shape=jax.ShapeDtypeStruct((512, LAT_TOKENS), BF16),
        compiler_params=_cp(("arbitrary", "arbitrary")),
        name="latent_attention",
    )(cq, ckd, cvt, kc, vct, sink_t)


def _gelu(x):
    return 0.5 * x * (1.0 + jnp.tanh(0.7978845608028654 * (x + 0.044715 * (x * x * x))))


def _merge_kernel(alpha, xc_ref, xl_ref, mod_ref, of_ref, ob_ref, ag_ref, ng_ref, bz_ref, lg_ref, lb_ref,
                  ws_ref, bs_ref, cc_ref, cl_ref, mg_ref, wb_ref, wo_ref, g1_ref, b1_ref,
                  rwh_ref, rwl_ref, rb_ref, utri_ref,
                  x1_ref, slot_ref, meta_ref, hs_ref,
                  h2_ref, cw_ref, gid_ref, stage_ref, cnt_ref, sem_ref):
    parts = [_merge_rows(alpha, slice(p * MERGE_ROWS, (p + 1) * MERGE_ROWS), xc_ref, xl_ref, mod_ref,
                         of_ref, ob_ref, ag_ref, ng_ref, bz_ref, lg_ref, lb_ref, ws_ref, bs_ref, cc_ref,
                         cl_ref, mg_ref, wb_ref, wo_ref, g1_ref, b1_ref, rwh_ref, rwl_ref, rb_ref,
                         x1_ref, h2_ref, cw_ref, gid_ref) for p in range(TM // MERGE_ROWS)]
    for step in range(MERGE_STAGES + len(parts) - 1):
        for lag, part in enumerate(parts):
            if 0 <= step - lag < MERGE_STAGES:
                next(part)
    _dispatch(h2_ref, cw_ref, gid_ref, utri_ref, slot_ref, meta_ref, hs_ref, stage_ref, cnt_ref, sem_ref)


def _aligned(row):
    return row if isinstance(row, int) else pl.multiple_of(row, HS_ALIGN)


def _window_copy(stage_ref, buf, src_row, hs_ref, dst_row, sem_ref):
    return pltpu.make_async_copy(stage_ref.at[buf, pl.ds(_aligned(src_row), TM), :],
                                 hs_ref.at[pl.ds(_aligned(dst_row), TM), :], sem_ref.at[0])


def _dispatch(h2_ref, cw_ref, gid_ref, utri_ref, slot_ref, meta_ref, hs_ref, stage_ref, cnt_ref, sem_ref):
    i = pl.program_id(0)

    @pl.when(i == 0)
    def _():
        for g in range(N_GROUPS):
            cnt_ref[g] = 0

    gid = gid_ref[0:1, :]
    grp_i = lax.broadcasted_iota(jnp.int32, (8, TM), 0)
    onehot = jnp.where(grp_i.astype(F32) == gid, 1.0, 0.0)
    rank = _dot(onehot.astype(BF16), utri_ref[...])
    ncol = (rank[:, TM - 1:TM] + onehot[:, TM - 1:TM]).astype(jnp.int32)
    padc = ((ncol + (HS_ALIGN - 1)) // HS_ALIGN) * HS_ALIGN
    row8 = lax.broadcasted_iota(jnp.int32, (8, 1), 0)
    startc = jnp.zeros((8, 1), jnp.int32)
    run = jnp.zeros((1, 1), jnp.int32)
    for g in range(1, N_GROUPS):
        run = run + padc[g - 1:g, :]
        startc = jnp.where(row8 == g, run, startc)
    slot = jnp.sum(onehot * (rank + startc.astype(F32)), axis=0, keepdims=True)
    back = jnp.sum(onehot * (rank + (row8 * TM).astype(F32)), axis=0, keepdims=True)
    slot_ref[...] = jnp.broadcast_to(back, (LANES, TM)).T
    sel = jnp.where(lax.broadcasted_iota(jnp.int32, (STAGE_ROWS, TM), 0).astype(F32) == slot, 1.0, 0.0)
    cw = cw_ref[...]
    cw_hi = cw.astype(BF16)
    r1 = cw - cw_hi.astype(F32)
    cw_mid = r1.astype(BF16)
    cw_lo = (r1 - cw_mid.astype(F32)).astype(BF16)
    rowdata = jnp.concatenate([h2_ref[...], cw_hi, cw_mid, cw_lo], axis=1)
    buf = i % 2
    stage_ref[buf] = _dot(sel.astype(BF16), rowdata).astype(BF16)

    @pl.when(i > 0)
    def _():
        for g in range(N_GROUPS):
            _window_copy(stage_ref, buf, 0, hs_ref, 0, sem_ref).wait()

    lane = lax.broadcasted_iota(jnp.int32, (8, LANES), 1)
    meta = jnp.zeros((8, LANES), jnp.int32)
    for g in range(N_GROUPS):
        dst = g * HS_CAP + cnt_ref[g]
        _window_copy(stage_ref, buf, startc[g, 0], hs_ref, dst, sem_ref).start()
        cnt_ref[g] = cnt_ref[g] + padc[g, 0]
        meta = jnp.where(lane == g, dst, meta)
        meta = jnp.where(lane == N_GROUPS + g, cnt_ref[g], meta)
    meta_ref[0] = meta

    @pl.when(i == NB - 1)
    def _():
        for g in range(N_GROUPS):
            _window_copy(stage_ref, buf, 0, hs_ref, 0, sem_ref).wait()
        for g in range(N_GROUPS):
            for w in range(2):
                _window_copy(stage_ref, buf, STAGE_ROWS - TM, hs_ref, g * HS_CAP + cnt_ref[g] + w * TM,
                             sem_ref).start()
        for _ in range(2 * N_GROUPS):
            _window_copy(stage_ref, buf, 0, hs_ref, 0, sem_ref).wait()


def _merge_rows(alpha, rows, xc_ref, xl_ref, mod_ref, of_ref, ob_ref, ag_ref, ng_ref, bz_ref, lg_ref, lb_ref,
                ws_ref, bs_ref, cc_ref, cl_ref, mg_ref, wb_ref, wo_ref, g1_ref, b1_ref,
                rwh_ref, rwl_ref, rb_ref, x1_ref, h2_ref, cw_ref, gid_ref):
    i = pl.program_id(0)
    m = mod_ref[0]
    n_rows = rows.stop - rows.start
    o = of_ref[rows, :] + ob_ref[rows, :]
    parts = []
    for h in range(GLA_HEADS):
        oh = o[:, h * GLA_DV:(h + 1) * GLA_DV]
        parts.append(oh * lax.rsqrt(jnp.mean(oh * oh, axis=-1, keepdims=True) + LN_EPS))
    br_a = (jnp.concatenate(parts, axis=1) * ng_ref[...] * _silu(ag_ref[rows, :])).astype(BF16)
    z = _gelu(bz_ref[rows, :])
    u = z[:, :BRANCH_W]
    v = _layer_norm(z[:, BRANCH_W:], lg_ref[...], lb_ref[...]).astype(BF16)
    yield
    chunks = []
    for c in range(n_rows // GMLP_CHUNK):
        vc = v[c * GMLP_CHUNK:(c + 1) * GMLP_CHUNK]
        cols = [_dot(ws_ref[g], vc[:, g * GMLP_CH:(g + 1) * GMLP_CH]) for g in range(GMLP_GROUPS)]
        chunks.append(jnp.concatenate(cols, axis=1) + bs_ref[...])
    br_b = (u * jnp.concatenate(chunks, axis=0)).astype(BF16)
    br_ct = jnp.where(i < NB_CTX, cc_ref[:, rows], cl_ref[:, rows])
    projs = (_dot(br_a, wb_ref[0]), _dot(br_b, wb_ref[1]), _dot_tn(br_ct, wb_ref[2]))
    yield
    y = jnp.zeros((n_rows, D_MODEL), F32)
    for nbr, proj in enumerate(projs):
        y = y + (jnp.tanh(mg_ref[rows, nbr * D_MODEL:(nbr + 1) * D_MODEL]) + 1.0) * proj
    y = y.astype(BF16)
    yield
    y = _dot(y, wo_ref[...])
    yield
    x = jnp.where(i < NB_CTX, xc_ref[rows, :], xl_ref[rows, :])
    x1 = _layer_norm(alpha * x + m[:, 2 * D_MODEL:3 * D_MODEL] * y, g1_ref[...], b1_ref[...])
    x1_ref[rows, :] = x1
    h2 = x1 * (1.0 + m[:, 4 * D_MODEL:5 * D_MODEL]) + m[:, 3 * D_MODEL:4 * D_MODEL]
    h2_ref[rows, :] = h2.astype(BF16)
    h_hi, h_lo = _split(h2)
    logits = (_dot_nt(rwh_ref[...], h_hi) + _dot_nt(rwh_ref[...], h_lo) + _dot_nt(rwl_ref[...], h_hi)
              + rb_ref[:, :n_rows])[:ROUTER_ROWS]
    row_i = lax.broadcasted_iota(jnp.int32, (ROUTER_ROWS, n_rows), 0)
    row = row_i.astype(F32)
    row_group = (row_i >> 2).astype(F32)
    big = float(LANES)
    neg = -jnp.inf
    gl = jnp.where((row_i >= N_EXPERTS) & (row_i < N_EXPERTS + N_GROUPS), logits, neg)
    gmax = jnp.max(gl, axis=0, keepdims=True)
    gsum = jnp.sum(jnp.exp(gl - gmax), axis=0, keepdims=True)
    g_p = 1.0 / gsum
    g_i = jnp.min(jnp.where(gl == gmax, row, big), axis=0, keepdims=True) - float(N_EXPERTS)
    in_group = (row_i < N_EXPERTS) & (row_group == g_i)
    el = jnp.where(in_group, logits, neg)
    emax = jnp.max(el, axis=0, keepdims=True)
    ee = jnp.exp(el - emax)
    e_prob = ee / jnp.sum(ee, axis=0, keepdims=True)
    p1 = jnp.max(jnp.where(in_group, e_prob, neg), axis=0, keepdims=True)
    i1 = jnp.min(jnp.where(in_group & (e_prob == p1), row, big), axis=0, keepdims=True)
    rest = in_group & (row != i1)
    p2 = jnp.max(jnp.where(rest, e_prob, neg), axis=0, keepdims=True)
    i2 = jnp.min(jnp.where(rest & (e_prob == p2), row, big), axis=0, keepdims=True)
    tot = p1 + p2
    cw_t = (jnp.where(row == i1, g_p * p1 / tot, 0.0) + jnp.where(row == i2, g_p * p2 / tot, 0.0))
    cw_ref[rows, :] = jnp.concatenate([cw_t, jnp.zeros((LANES - ROUTER_ROWS, n_rows), F32)], axis=0).T
    gid_ref[0:1, rows] = g_i
    yield


def _merge(l, alpha, xc, xl, mod3, o_f, o_b, ag, norm_g, bz, ln_g, ln_b, ws, bs_t, brc_ctx, brc_lat, mg,
           wb, wo, g1, b1, rw_hi, rw_lo, rb):
    utri = (jnp.arange(TM)[:, None] < jnp.arange(TM)[None, :]).astype(BF16)
    return pl.pallas_call(
        functools.partial(_merge_kernel, alpha),
        grid=(NB,),
        in_specs=[_ctx_tok(D_MODEL), _lat_tok(D_MODEL), _mod_spec(),
                  _tok(512), _tok(512), _tok(512), _layer_spec(l, (1, 512)),
                  _tok(1024), _layer_spec(l, (1, 512)), _layer_spec(l, (1, 512)),
                  _layer_spec(l, (GMLP_GROUPS, GMLP_CHUNK, GMLP_CHUNK)), _layer_spec(l, (GMLP_CHUNK, 512)),
                  pl.BlockSpec((512, TM), lambda i: (0, jnp.minimum(i, NB_CTX - 1))),
                  pl.BlockSpec((512, TM), lambda i: (0, jnp.maximum(i - NB_CTX, 0))),
                  _tok(3072), _layer_spec(l, (N_BRANCH, BRANCH_W, D_MODEL)), _layer_spec(l, (D_MODEL, D_MODEL)),
                  _layer_spec(l, (1, D_MODEL)), _layer_spec(l, (1, D_MODEL)),
                  _layer_spec(l, (LANES, D_MODEL)), _layer_spec(l, (LANES, D_MODEL)), _layer_spec(l, (LANES, TM)),
                  pl.BlockSpec((TM, TM), lambda i: (0, 0))],
        out_specs=[_tok(D_MODEL), _tok(LANES), pl.BlockSpec((1, 8, LANES), lambda i: (i, 0, 0)),
                   pl.BlockSpec(memory_space=pl.ANY)],
        out_shape=[jax.ShapeDtypeStruct((TOKENS, D_MODEL), F32),
                   jax.ShapeDtypeStruct((TOKENS, LANES), F32),
                   jax.ShapeDtypeStruct((NB, 8, LANES), jnp.int32),
                   jax.ShapeDtypeStruct((N_GROUPS * HS_CAP, HS_W), BF16)],
        scratch_shapes=[pltpu.VMEM((TM, D_MODEL), BF16), pltpu.VMEM((TM, LANES), F32), pltpu.VMEM((8, TM), F32),
                        pltpu.VMEM((2, STAGE_ROWS, HS_W), BF16), pltpu.SMEM((N_GROUPS,), jnp.int32),
                        pltpu.SemaphoreType.DMA((1,))],
        compiler_params=_cp(("arbitrary",)),
        name="merge_ln1_router",
    )(xc, xl, mod3, o_f, o_b, ag, norm_g, bz, ln_g, ln_b, ws, bs_t, brc_ctx, brc_lat, mg,
      wb, wo, g1, b1, rw_hi, rw_lo, rb, utri)


def _experts_kernel(blk_ref, grp_ref, ntile_ref, hs_ref, wg_ref, wu_ref, wd_ref, ys_ref):
    s = pl.program_id(0)

    @pl.when(s < ntile_ref[0])
    def _():
        g = grp_ref[s]
        x = hs_ref[:, :D_MODEL]
        cw = (hs_ref[:, D_MODEL:D_MODEL + LANES].astype(F32)
              + hs_ref[:, D_MODEL + LANES:D_MODEL + 2 * LANES].astype(F32)
              + hs_ref[:, D_MODEL + 2 * LANES:].astype(F32))
        lane = lax.broadcasted_iota(jnp.int32, (TM, LANES), 1)
        hid = []
        for k in range(EXPERTS_PER_GROUP):
            w_e = jnp.sum(jnp.where(lane == g * EXPERTS_PER_GROUP + k, cw, 0.0), axis=-1, keepdims=True)
            hid.append((_silu(_dot(x, wg_ref[k])) * _dot(x, wu_ref[k]) * w_e).astype(BF16))
        y = _dot(jnp.concatenate(hid, axis=1), wd_ref[...].reshape(EXPERTS_PER_GROUP * EXPERT_FF, D_MODEL))
        ys_ref[...] = y.astype(BF16)


def _experts(l, tile_blk, tile_grp, n_tiles, hs, wg, wu, wd):
    e = EXPERTS_PER_GROUP
    return pl.pallas_call(
        _experts_kernel,
        grid_spec=pltpu.PrefetchScalarGridSpec(
            num_scalar_prefetch=3,
            grid=(MOE_TILES,),
            in_specs=[pl.BlockSpec((TM, HS_W), lambda s, blk, grp, nt: (blk[s], 0)),
                      pl.BlockSpec((None, e, D_MODEL, EXPERT_FF), lambda s, blk, grp, nt: (l, grp[s], 0, 0)),
                      pl.BlockSpec((None, e, D_MODEL, EXPERT_FF), lambda s, blk, grp, nt: (l, grp[s], 0, 0)),
                      pl.BlockSpec((None, e, EXPERT_FF, D_MODEL), lambda s, blk, grp, nt: (l, grp[s], 0, 0))],
            out_specs=pl.BlockSpec((TM, D_MODEL), lambda s, blk, grp, nt: (blk[s], 0)),
        ),
        out_shape=jax.ShapeDtypeStruct((N_GROUPS * HS_CAP, D_MODEL), BF16),
        compiler_params=_cp(("arbitrary",)),
        name="moe_experts",
    )(tile_blk, tile_grp, n_tiles, hs, wg, wu, wd)


def _tile_table(final_counts):
    nt = (final_counts + 2 * TM - 1) // TM
    ends = jnp.cumsum(nt)
    total = ends[-1]
    s = jnp.minimum(jnp.arange(MOE_TILES), total - 1)
    grp = jnp.sum((s[:, None] >= ends[None, :]).astype(jnp.int32), axis=1)
    first = ends - nt
    blk = grp * (HS_CAP // TM) + s - first[grp]
    return blk.astype(jnp.int32), grp.astype(jnp.int32), total.reshape(1).astype(jnp.int32)


def _window_fetch(ys_ref, offs_ref, blk, win_ref, buf, sem_ref):
    return [pltpu.make_async_copy(
        ys_ref.at[pl.ds(pl.multiple_of(offs_ref[blk * N_GROUPS + g], HS_ALIGN), TM), :],
        win_ref.at[buf, pl.ds(g * TM, TM), :], sem_ref.at[buf]) for g in range(N_GROUPS)]


def _combine_kernel(alpha, offs_ref, ys_ref, slot_ref, x1_ref, mod_ref, g2_ref, b2_ref,
                    oc_ref, ol_ref, win_ref, sem_ref):
    i = pl.program_id(0)
    buf = i % 2

    @pl.when(i == 0)
    def _():
        for cp in _window_fetch(ys_ref, offs_ref, 0, win_ref, 0, sem_ref):
            cp.start()

    @pl.when(i + 1 < NB)
    def _():
        for cp in _window_fetch(ys_ref, offs_ref, i + 1, win_ref, 1 - buf, sem_ref):
            cp.start()

    for cp in _window_fetch(ys_ref, offs_ref, i, win_ref, buf, sem_ref):
        cp.wait()
    slot = jnp.concatenate([slot_ref[...]] * (N_GROUPS * TM // LANES), axis=1)
    sel = jnp.where(lax.broadcasted_iota(jnp.int32, (TM, N_GROUPS * TM), 1).astype(F32) == slot, 1.0, 0.0)
    y = _dot(sel.astype(BF16), win_ref[buf])
    m = mod_ref[0]
    out = _layer_norm(alpha * x1_ref[...] + m[:, 5 * D_MODEL:6 * D_MODEL] * y, g2_ref[...], b2_ref[...])

    @pl.when(i < NB_CTX)
    def _():
        oc_ref[...] = out

    @pl.when(i >= NB_CTX)
    def _():
        ol_ref[...] = out


def _combine(l, alpha, offs, ys, slot, x1, mod3, g2, b2):
    return pl.pallas_call(
        functools.partial(_combine_kernel, alpha),
        grid_spec=pltpu.PrefetchScalarGridSpec(
            num_scalar_prefetch=1,
            grid=(NB,),
            in_specs=[pl.BlockSpec(memory_space=pl.ANY),
                      pl.BlockSpec((TM, LANES), lambda i, offs: (i, 0)),
                      pl.BlockSpec((TM, D_MODEL), lambda i, offs: (i, 0)),
                      pl.BlockSpec((1, 1, 6 * D_MODEL), lambda i, offs: (_mod_block(i), 0, 0)),
                      pl.BlockSpec((None, 1, D_MODEL), lambda i, offs: (l, 0, 0)),
                      pl.BlockSpec((None, 1, D_MODEL), lambda i, offs: (l, 0, 0))],
            out_specs=[pl.BlockSpec((TM, D_MODEL), lambda i, offs: (jnp.minimum(i, NB_CTX - 1), 0)),
                       pl.BlockSpec((TM, D_MODEL), lambda i, offs: (jnp.maximum(i - NB_CTX, 0), 0))],
            scratch_shapes=[pltpu.VMEM((2, N_GROUPS * TM, D_MODEL), BF16), pltpu.SemaphoreType.DMA((2,))],
        ),
        out_shape=[jax.ShapeDtypeStruct((CTX_TOKENS, D_MODEL), F32),
                   jax.ShapeDtypeStruct((LAT_TOKENS, D_MODEL), F32)],
        compiler_params=_cp(("arbitrary",)),
        name="combine_ln2",
    )(offs, ys, slot, x1, mod3, g2, b2)


def _rope_tables():
    pos = jnp.arange(DEC_SEQ)
    row = (pos // GRID_W).astype(F32)
    col = (pos % GRID_W).astype(F32)
    quarter = ATTN_HEAD_DIM // 4
    inv_freq = jnp.power(ROPE_BASE, -jnp.arange(quarter, dtype=F32) / quarter)
    j = jnp.arange(ATTN_HEAD_DIM)
    p = jnp.where((j // (2 * quarter))[None, :] == 0, row[:, None], col[:, None])
    ang = p * inv_freq[j % quarter][None, :]
    sign = jnp.where((j % (2 * quarter)) < quarter, -1.0, 1.0).astype(F32)
    cos = jnp.tile(jnp.cos(ang), (1, ATTN_HEADS))
    sin = jnp.tile(jnp.sin(ang) * sign[None, :], (1, ATTN_HEADS))
    cos = jnp.concatenate([jnp.ones((TM, 512), F32), cos], axis=0)
    sin = jnp.concatenate([jnp.zeros((TM, 512), F32), sin], axis=0)
    return cos, sin


def _dup_heads(t):
    h0, h1 = t[..., :ATTN_HEAD_DIM], t[..., ATTN_HEAD_DIM:]
    return jnp.concatenate([h0, h0, h1, h1], axis=-1)


def _split_w_in(w):
    widths = (256, 256, 512, 512, GLA_RANK, GLA_RANK, 1024, 512, 128, 128, 3072)
    parts, start = [], 0
    for wd in widths:
        parts.append(w[..., start:start + wd].astype(BF16))
        start += wd
    aq, ak, av, ag, lrf, lrb, bz, cq, ck, cv, mg = parts
    mg = mg * 0.5
    pad = jnp.zeros(w.shape[:-1] + (LANES - 2 * GLA_RANK,), BF16)
    return [aq, ak, av, ag, jnp.concatenate([lrf, lrb, pad], axis=-1), bz, cq, ck, cv, mg]


def kernel(x_prompt, x_sample, state_gla, cache_k, cache_v, c, c_ctx, ada_w, ada_b, w_in, gla_gate_w, gla_gate_b, gla_norm_g, gmlp_ln_g, gmlp_ln_b, gmlp_ws, gmlp_bs, attn_sink, w_branch, w_out, ln1_g, ln1_b, ln2_g, ln2_b, router_group_w, router_group_b, router_expert_w, router_expert_b, expert_w_gate, expert_w_up, expert_w_down):
    alpha = (2.0 * DEPTH) ** 0.25
    xc = x_prompt.reshape(CTX_TOKENS, D_MODEL)
    xl = x_sample.reshape(LAT_TOKENS, D_MODEL)
    cond = jnp.concatenate([c_ctx[None, :], c, jnp.zeros((MOD_ROWS - 1 - DEC_BATCH, D_MODEL), F32)], axis=0)
    cos_t, sin_t = _rope_tables()

    ada_b3 = ada_b[:, None, :]
    w_parts = _split_w_in(w_in)
    zrow = jnp.zeros((DEPTH, GLA_RANK, 256), F32)
    zpad = jnp.zeros((DEPTH, LANES - 2 * GLA_RANK, 256), F32)
    gw_p = jnp.stack([jnp.concatenate([gla_gate_w[:, 0], zrow, zpad], axis=1),
                      jnp.concatenate([zrow, gla_gate_w[:, 1], zpad], axis=1)], axis=1).astype(BF16)
    gb_p = gla_gate_b[:, :, None, :]
    sink_t = jnp.broadcast_to(attn_sink[:, :, None, None], (DEPTH, ATTN_HEADS, 1, LANES))
    kc = _dup_heads(cache_k.reshape(DEC_BATCH, DEPTH, PAST_LEN, 128)).astype(BF16)
    vct = jnp.swapaxes(cache_v.reshape(DEC_BATCH, DEPTH, PAST_LEN, 128), 2, 3).astype(BF16)
    rw = jnp.swapaxes(jnp.concatenate(
        [router_expert_w, router_group_w,
         jnp.zeros((DEPTH, D_MODEL, LANES - N_EXPERTS - N_GROUPS), F32)], axis=2), 1, 2)
    rw_hi = rw.astype(BF16)
    rw_lo = (rw - rw_hi.astype(F32)).astype(BF16)
    rb = jnp.concatenate([router_expert_b, router_group_b,
                          jnp.zeros((DEPTH, LANES - N_EXPERTS - N_GROUPS), F32)], axis=1)
    rb = jnp.broadcast_to(rb[:, :, None], (DEPTH, LANES, TM))
    bs_t = jnp.repeat(jnp.swapaxes(gmlp_bs, 1, 2), GMLP_CH, axis=2)
    ws_b = gmlp_ws.astype(BF16)
    wb_b = (0.5 * w_branch).astype(BF16)
    wo_b = w_out.astype(BF16)
    wg_b = expert_w_gate.astype(BF16)
    wu_b = expert_w_up.astype(BF16)
    wd_b = expert_w_down.astype(BF16)
    row = lambda t: t[:, None, :]

    states, keys, values = [], [], []
    for l in range(DEPTH):
        mod3 = _modulation(l, cond, ada_w, ada_b3).reshape(MOD_ROWS, 1, 6 * D_MODEL)
        (qef, kef, klf, qeb, keb, klb, dec, av, avt, ag, bz, cq, ckd, cvt, ck, cv, mg) = _in_projection(
            l, xc, xl, mod3, cos_t, sin_t, gw_p, gb_p, w_parts)
        o_f, o_b, s_fin = _gla(l, qef, kef, klf, qeb, keb, klb, dec, av, avt, state_gla)
        brc_ctx = _context_attention(l, cq, ckd, cvt, sink_t)
        brc_lat = _latent_attention(l, cq, ckd, cvt, kc, vct, sink_t)
        x1, slot, meta, hs = _merge(l, alpha, xc, xl, mod3, o_f, o_b, ag, row(gla_norm_g), bz,
                                    row(gmlp_ln_g), row(gmlp_ln_b), ws_b, bs_t, brc_ctx, brc_lat, mg, wb_b,
                                    wo_b, row(ln1_g), row(ln1_b), rw_hi, rw_lo, rb)
        offs = meta[:, 0, :N_GROUPS].reshape(NB * N_GROUPS)
        tile_blk, tile_grp, n_tiles = _tile_table(meta[NB - 1, 0, N_GROUPS:2 * N_GROUPS])
        ys = _experts(l, tile_blk, tile_grp, n_tiles, hs, wg_b, wu_b, wd_b)
        xc, xl = _combine(l, alpha, offs, ys, slot, x1, mod3, row(ln2_g), row(ln2_b))
        states.append(s_fin)
        keys.append(ck.reshape(BATCH, SEQ, ATTN_KV_HEADS, ATTN_HEAD_DIM))
        values.append(cv.reshape(BATCH, SEQ, ATTN_KV_HEADS, ATTN_HEAD_DIM))
    return (xc.reshape(BATCH, SEQ, D_MODEL), xl.reshape(DEC_BATCH, DEC_SEQ, D_MODEL),
            jnp.stack(states, axis=1), jnp.stack(keys, axis=1), jnp.stack(values, axis=1))
```

```python
import functools

import jax
import jax.numpy as jnp
import numpy as np
from jax import lax
from jax.experimental import pallas as pl
from jax.experimental.pallas import tpu as pltpu

F32 = jnp.float32
BF16 = jnp.bfloat16

D_MODEL = 1024
BATCH = 32
SEQ = 256
DEPTH = 2
DEC_BATCH = 8
DEC_SEQ = 1024
PAST_LEN = 512
GRID_W = 64
GLA_HEADS = 4
GLA_DV = 128
GLA_DK = 64
GLA_RANK = 16
GLA_TAU = 16.0
GLA_CHUNK = 32
GMLP_CHUNK = 128
GMLP_CH = 128
GMLP_GROUPS = 4
ATTN_HEADS = 8
ATTN_KV_HEADS = 2
ATTN_GROUP = ATTN_HEADS // ATTN_KV_HEADS
ATTN_HEAD_DIM = 64
WINDOW = 128
ATTN_BLOCK = 128
ROPE_BASE = 10000.0
BRANCH_W = 512
N_BRANCH = 3
N_GROUPS = 4
EXPERTS_PER_GROUP = 4
N_EXPERTS = 16
EXPERT_FF = 256
LN_EPS = 1e-5

LANES = 128
TM = 256
CTX_TOKENS = BATCH * SEQ
LAT_TOKENS = DEC_BATCH * DEC_SEQ
TOKENS = CTX_TOKENS + LAT_TOKENS
NB_CTX = CTX_TOKENS // TM
NB_LAT = LAT_TOKENS // TM
NB = NB_CTX + NB_LAT
LAT_BLOCKS_PER_SEQ = DEC_SEQ // TM
MOD_ROWS = 16
GLA_SUB = 128
CHUNKS_PER_SUB = GLA_SUB // GLA_CHUNK
MERGE_ROWS = 128
MERGE_STAGES = 5
ROUTER_ROWS = 24
HS_ALIGN = 16
SMALL_WIN = 128
HS_W = D_MODEL + 3 * LANES
STAGE_ROWS = TM + N_GROUPS * HS_ALIGN + TM
_MAX_PAD = NB * (HS_ALIGN - 1)
HS_CAP = -(-(TOKENS + _MAX_PAD + 2 * TM) // TM) * TM
MOE_TILES = (TOKENS + N_GROUPS * _MAX_PAD) // TM + 2 * N_GROUPS

IN_COLS = 6432
WA_COLS = 1664
WA_AQ, WA_AK, WA_AV, WA_AG = slice(0, 256), slice(256, 512), slice(512, 1024), slice(1024, 1536)
WA_LR = slice(1536, 1664)
WB_START = 1568
WB_BZ, WB_CQ, WB_CK, WB_CV = slice(0, 1024), slice(1024, 1536), slice(1536, 1664), slice(1664, 1792)
WB_MG = slice(1792, 4864)

VMEM_LIMIT = 56 * 1024 * 1024


def _cp(sem):
    return pltpu.CompilerParams(dimension_semantics=sem, vmem_limit_bytes=VMEM_LIMIT)


def _dot(a, b):
    return jnp.dot(a, b, preferred_element_type=F32)


def _dot_nt(a, b):
    return lax.dot_general(a, b, (((1,), (1,)), ((), ())), preferred_element_type=F32)


def _dot_tn(a, b):
    return lax.dot_general(a, b, (((0,), (0,)), ((), ())), preferred_element_type=F32)


def _split(x):
    hi = x.astype(BF16)
    lo = (x - hi.astype(F32)).astype(BF16)
    return hi, lo


def _sigmoid(x):
    return 0.5 * (jnp.tanh(0.5 * x) + 1.0)


def _silu(x):
    return x * _sigmoid(x)


def _layer_norm(x, g, b):
    mu = jnp.mean(x, axis=-1, keepdims=True)
    xc = x - mu
    var = jnp.mean(xc * xc, axis=-1, keepdims=True)
    return xc * lax.rsqrt(var + LN_EPS) * g + b


def _mod_block(i):
    return jnp.where(i < NB_CTX, 0, 1 + (i - NB_CTX) // LAT_BLOCKS_PER_SEQ)


def _tok(width):
    return pl.BlockSpec((TM, width), lambda i: (i, 0))


def _ctx_tok(width):
    return pl.BlockSpec((TM, width), lambda i: (jnp.minimum(i, NB_CTX - 1), 0))


def _lat_tok(width):
    return pl.BlockSpec((TM, width), lambda i: (jnp.maximum(i - NB_CTX, 0), 0))


def _layer_spec(l, shape):
    return pl.BlockSpec((None,) + shape, lambda *_: (l,) + (0,) * len(shape))


def _mod_spec():
    return pl.BlockSpec((1, 1, 6 * D_MODEL), lambda i: (_mod_block(i), 0, 0))


def _mod_kernel(cond_ref, w_ref, b_ref, o_ref):
    s_hi, s_lo = _split(_silu(cond_ref[...]))
    w_hi, w_lo = _split(w_ref[...])
    o_ref[...] = _dot(s_hi, w_hi) + _dot(s_lo, w_hi) + _dot(s_hi, w_lo) + b_ref[...]


def _modulation(l, cond, ada_w, ada_b3):
    tn = 1536
    return pl.pallas_call(
        _mod_kernel,
        grid=(6 * D_MODEL // tn,),
        in_specs=[pl.BlockSpec((MOD_ROWS, D_MODEL), lambda j: (0, 0)),
                  pl.BlockSpec((None, D_MODEL, tn), lambda j: (l, 0, j)),
                  pl.BlockSpec((None, 1, tn), lambda j: (l, 0, j))],
        out_specs=pl.BlockSpec((MOD_ROWS, tn), lambda j: (0, j)),
        out_shape=jax.ShapeDtypeStruct((MOD_ROWS, 6 * D_MODEL), F32),
        compiler_params=_cp(("arbitrary",)),
        name="modulation",
    )(cond, ada_w, ada_b3)


def _rope(x, cos, sin):
    n = x.shape[1]
    lane = lax.broadcasted_iota(jnp.int32, x.shape, 1)
    partner = jnp.where((lane & 31) < 16, pltpu.roll(x, n - 16, 1), pltpu.roll(x, 16, 1))
    return x * cos + partner * sin


def _inproj_kernel(xc_ref, xl_ref, mod_ref, cos_ref, sin_ref, gw_ref, gb_ref,
                   wa_ref, wb_ref,
                   qef_ref, kef_ref, klf_ref, qeb_ref, keb_ref, klb_ref, dec_ref,
                   av_ref, avt_ref, ag_ref, bz_ref,
                   cq_ref, ckd_ref, cvt_ref, ck_ref, cv_ref, mg_ref):
    i = pl.program_id(0)
    m = mod_ref[0]
    x = jnp.where(i < NB_CTX, xc_ref[...], xl_ref[...])
    h = (x * (1.0 + m[:, D_MODEL:2 * D_MODEL]) + m[:, 0:D_MODEL]).astype(BF16)

    aq = _dot(h, wa_ref[:, WA_AQ])
    ak = _dot(h, wa_ref[:, WA_AK])
    lr = _dot(h, wa_ref[:, WA_LR]).astype(BF16)
    zs = [_dot(lr, gw_ref[d]) + gb_ref[d] for d in range(2)]
    mg_ref[...] = _dot(h, wb_ref[:, WB_MG])
    r = lax.broadcasted_iota(jnp.int32, (TM, TM), 0)
    c = lax.broadcasted_iota(jnp.int32, (TM, TM), 1)
    same = (r >> 5) == (c >> 5)
    bs = []
    for d in range(2):
        g = (jnp.minimum(zs[d], 0.0) - jnp.log(1.0 + jnp.exp(-jnp.abs(zs[d])))) / GLA_TAU
        tri_b = jnp.where(same & ((c <= r) if d == 0 else (c >= r)), 1.0, 0.0).astype(BF16)
        g_hi, g_lo = _split(g)
        bs.append(_dot(tri_b, g_hi) + _dot(tri_b, g_lo))
    bz_ref[...] = _dot(h, wb_ref[:, WB_BZ])
    ag_ref[...] = _dot(h, wa_ref[:, WA_AG])
    av = _dot(h, wa_ref[:, WA_AV])
    av_ref[...] = av.astype(BF16)
    avt_ref[...] = av.T.astype(BF16)
    decs = []
    for d, (qe_ref, ke_ref, kl_ref) in enumerate(((qef_ref, kef_ref, klf_ref), (qeb_ref, keb_ref, klb_ref))):
        b = bs[d]
        last = GLA_CHUNK - 1 if d == 0 else 0
        tot = [b[n * GLA_CHUNK + last:n * GLA_CHUNK + last + 1, :] for n in range(TM // GLA_CHUNK)]
        bl = jnp.concatenate([jnp.broadcast_to(t, (GLA_CHUNK, t.shape[1])) for t in tot], axis=0)
        qe_ref[...] = (aq * jnp.exp(b) * (GLA_DK ** -0.5)).astype(BF16)
        ke_ref[...] = (ak * jnp.exp(-b)).astype(BF16)
        kl_ref[...] = (ak * jnp.exp(bl - b)).astype(BF16)
        decs.append(jnp.exp(jnp.concatenate(tot, axis=0)))
    dec_ref[...] = jnp.concatenate(decs, axis=1)

    cos = cos_ref[...]
    sin = sin_ref[...]
    cq_ref[...] = (_rope(_dot(h, wb_ref[:, WB_CQ]), cos, sin) * (ATTN_HEAD_DIM ** -0.5)).astype(BF16)
    ck = _dot(h, wb_ref[:, WB_CK])
    cv = _dot(h, wb_ref[:, WB_CV])
    kr = _rope(ck, cos[:, :LANES], sin[:, :LANES])
    swapped = pltpu.roll(kr, ATTN_HEAD_DIM, 1)
    lane = lax.broadcasted_iota(jnp.int32, (TM, LANES), 1)
    ckd_ref[...] = jnp.concatenate([jnp.where(lane < 64, kr, swapped), jnp.where(lane < 64, swapped, kr)],
                                   axis=1).astype(BF16)
    cvt_ref[...] = cv.T.astype(BF16)

    @pl.when(i < NB_CTX)
    def _():
        ck_ref[...] = ck
        cv_ref[...] = cv


def _rope_block(i):
    return jnp.where(i < NB_CTX, 0, 1 + (i - NB_CTX) % LAT_BLOCKS_PER_SEQ)


def _in_projection(l, xc, xl, mod3, cos_t, sin_t, gw_p, gb_p, w_parts):
    def tok_out(width, dt):
        return _tok(width), jax.ShapeDtypeStruct((TOKENS, width), dt)

    def feat_out(width):
        return (pl.BlockSpec((width, TM), lambda i: (0, i)), jax.ShapeDtypeStruct((width, TOKENS), BF16))

    def ctx_out(width):
        return _ctx_tok(width), jax.ShapeDtypeStruct((CTX_TOKENS, width), F32)

    dec_out = (pl.BlockSpec((TM // GLA_CHUNK, 512), lambda i: (i, 0)),
               jax.ShapeDtypeStruct((TOKENS // GLA_CHUNK, 512), F32))
    outs = [tok_out(256, BF16)] * 6 + [dec_out] + [
        tok_out(512, BF16), feat_out(512), tok_out(512, F32), tok_out(1024, F32), tok_out(512, BF16),
        tok_out(256, BF16), feat_out(LANES), ctx_out(LANES), ctx_out(LANES), tok_out(3072, F32)]
    return pl.pallas_call(
        _inproj_kernel,
        grid=(NB,),
        in_specs=[_ctx_tok(D_MODEL), _lat_tok(D_MODEL), _mod_spec(),
                  pl.BlockSpec((TM, 512), lambda i: (_rope_block(i), 0)),
                  pl.BlockSpec((TM, 512), lambda i: (_rope_block(i), 0)),
                  _layer_spec(l, (2, LANES, 256)), _layer_spec(l, (2, 1, 256))]
                 + [_layer_spec(l, (D_MODEL, w.shape[-1])) for w in w_parts],
        out_specs=[o[0] for o in outs],
        out_shape=[o[1] for o in outs],
        compiler_params=_cp(("arbitrary",)),
        name="in_projection",
    )(xc, xl, mod3, cos_t, sin_t, gw_p, gb_p, *w_parts)


def _gla_direction(qe, ke, kl, v, vt, dec, masks, state_ref, reverse):
    tri, half_masks, chunk_mask, head_diag = masks
    tri = tri[int(reverse)]
    outs = []
    for p in range(GLA_HEADS // 2):
        ps = slice(p * LANES, (p + 1) * LANES)
        qp = qe[:, ps]
        kep = ke[:, ps]
        klp = kl[:, ps]
        vp = v[:, p * 2 * GLA_DV:(p + 1) * 2 * GLA_DV]
        vtp = vt[p * 2 * GLA_DV:(p + 1) * 2 * GLA_DV, :]
        halves = []
        for hh in range(2):
            s = jnp.where(tri, _dot_nt(qp * half_masks[hh], kep), 0.0).astype(BF16)
            halves.append(_dot(s, vp[:, hh * GLA_DV:(hh + 1) * GLA_DV]))
        k_exp = jnp.concatenate([klp] * CHUNKS_PER_SUB, axis=1) * chunk_mask
        q_exp = jnp.concatenate([qp] * CHUNKS_PER_SUB, axis=1) * chunk_mask
        kvt = _dot(vtp, k_exp)
        st = state_ref[p]
        entering = [None] * CHUNKS_PER_SUB
        order = range(CHUNKS_PER_SUB - 1, -1, -1) if reverse else range(CHUNKS_PER_SUB)
        for n in order:
            entering[n] = st
            st = dec[n:n + 1, ps] * st + jnp.where(head_diag, kvt[:, n * LANES:(n + 1) * LANES], 0.0)
        state_ref[p] = st
        st_stack = jnp.concatenate(entering, axis=1).astype(BF16)
        inter = _dot_nt(q_exp, st_stack)
        outs.append(halves[0] + inter[:, :GLA_DV])
        outs.append(halves[1] + inter[:, GLA_DV:])
    return jnp.concatenate(outs, axis=1)


def _gla_mask_constants():
    r = np.arange(GLA_SUB)[:, None]
    c = np.arange(GLA_SUB)[None, :]
    same = (r // GLA_CHUNK) == (c // GLA_CHUNK)
    tri = np.stack([same & (c <= r), same & (c >= r)]).astype(np.float32)
    lane = np.arange(LANES)[None, :]
    halves = np.stack([np.broadcast_to(lane < 64, (GLA_SUB, LANES)),
                       np.broadcast_to(lane >= 64, (GLA_SUB, LANES))]).astype(np.float32)
    col_chunk = np.arange(CHUNKS_PER_SUB * LANES)[None, :] // LANES
    chunk_mask = ((r // GLA_CHUNK) == col_chunk).astype(np.float32)
    head_diag = ((np.arange(2 * GLA_DV)[:, None] // GLA_DV) == (lane // GLA_DK)).astype(np.float32)
    return (jnp.asarray(tri), jnp.asarray(halves, dtype=BF16), jnp.asarray(chunk_mask, dtype=BF16),
            jnp.asarray(head_diag))


def _gla_kernel(tri_ref, half_ref, cmask_ref, hdiag_ref,
                qf_ref, kef_ref, klf_ref, vf_ref, vtf_ref, decf_ref,
                qb_ref, keb_ref, klb_ref, vb_ref, vtb_ref, decb_ref,
                s0_ref, of_ref, ob_ref, sfin_ref, state_ref):
    i = pl.program_id(0)
    @pl.when(i < NB_CTX)
    def _():
        state_ref[...] = jnp.zeros_like(state_ref)

    @pl.when(jnp.logical_and(i >= NB_CTX, (i - NB_CTX) % LAT_BLOCKS_PER_SEQ == 0))
    def _():
        zero = jnp.zeros((GLA_DK, GLA_DV), F32)
        for d in range(2):
            for p in range(2):
                pair = jnp.concatenate(
                    [jnp.concatenate([s0_ref[0, d, 2 * p], zero], axis=1),
                     jnp.concatenate([zero, s0_ref[0, d, 2 * p + 1]], axis=1)], axis=0)
                state_ref[d, p] = pair.T

    masks = ((tri_ref[0] > 0.5, tri_ref[1] > 0.5), (half_ref[0], half_ref[1]), cmask_ref[...],
             hdiag_ref[...] > 0.5)
    n_sub = TM // GLA_SUB
    for j in range(n_sub):
        r0 = j * GLA_SUB
        rs = slice(r0, r0 + GLA_SUB)
        of_ref[rs, :] = _gla_direction(
            qf_ref[rs, :], kef_ref[rs, :], klf_ref[rs, :], vf_ref[rs, :], vtf_ref[:, rs],
            decf_ref[j * CHUNKS_PER_SUB:(j + 1) * CHUNKS_PER_SUB, :256], masks, state_ref.at[0], False)
        jb = n_sub - 1 - j
        rb = slice(jb * GLA_SUB, (jb + 1) * GLA_SUB)
        ob_ref[rb, :] = _gla_direction(
            qb_ref[rb, :], keb_ref[rb, :], klb_ref[rb, :], vb_ref[rb, :], vtb_ref[:, rb],
            decb_ref[jb * CHUNKS_PER_SUB:(jb + 1) * CHUNKS_PER_SUB, 256:], masks, state_ref.at[1], True)

    @pl.when(i < NB_CTX)
    def _():
        for d in range(2):
            for p in range(2):
                sp = state_ref[d, p].T
                sfin_ref[0, d, 2 * p] = sp[:GLA_DK, :GLA_DV]
                sfin_ref[0, d, 2 * p + 1] = sp[GLA_DK:, GLA_DV:]


def _bwd_block(i):
    k = (i - NB_CTX) % LAT_BLOCKS_PER_SEQ
    return jnp.where(i < NB_CTX, i, i - k + (LAT_BLOCKS_PER_SEQ - 1 - k))


def _gla(l, qef, kef, klf, qeb, keb, klb, dec, av, avt, state_gla):
    def bwd(width):
        return pl.BlockSpec((TM, width), lambda i: (_bwd_block(i), 0))

    def lat_seq(i):
        return jnp.maximum(i - NB_CTX, 0) // LAT_BLOCKS_PER_SEQ

    vt_fwd = pl.BlockSpec((512, TM), lambda i: (0, i))
    vt_bwd = pl.BlockSpec((512, TM), lambda i: (0, _bwd_block(i)))
    dec_fwd = pl.BlockSpec((TM // GLA_CHUNK, 512), lambda i: (i, 0))
    dec_bwd = pl.BlockSpec((TM // GLA_CHUNK, 512), lambda i: (_bwd_block(i), 0))
    state_dims = (2, GLA_HEADS, GLA_DK, GLA_DV)
    consts = _gla_mask_constants()
    return pl.pallas_call(
        _gla_kernel,
        grid=(NB,),
        in_specs=[pl.BlockSpec(t.shape, lambda i, nd=t.ndim: (0,) * nd) for t in consts]
                 + [_tok(256), _tok(256), _tok(256), _tok(512), vt_fwd, dec_fwd,
                  bwd(256), bwd(256), bwd(256), bwd(512), vt_bwd, dec_bwd,
                  pl.BlockSpec((1, None) + state_dims, lambda i: (lat_seq(i), l, 0, 0, 0, 0))],
        out_specs=[_tok(512), bwd(512),
                   pl.BlockSpec((1,) + state_dims, lambda i: (jnp.minimum(i, NB_CTX - 1), 0, 0, 0, 0))],
        out_shape=[jax.ShapeDtypeStruct((TOKENS, 512), F32), jax.ShapeDtypeStruct((TOKENS, 512), F32),
                   jax.ShapeDtypeStruct((BATCH,) + state_dims, F32)],
        scratch_shapes=[pltpu.VMEM((2, 2, 2 * GLA_DV, LANES), F32)],
        compiler_params=_cp(("arbitrary",)),
        name="gla_scan",
    )(*consts, qef, kef, klf, av, avt, dec, qeb, keb, klb, av, avt, dec, state_gla)


def _attend_group(q_refs, keys, values_t, masks, sink_ref, g, o_ref):
    rows = q_refs[0].shape[0]
    lane = lax.broadcasted_iota(jnp.int32, (rows, LANES), 1)
    stacked = []
    for q in q_refs:
        q32 = q.astype(F32)
        stacked.append(jnp.where(lane < 64, q32, 0.0).astype(BF16))
        stacked.append(jnp.where(lane >= 64, q32, 0.0).astype(BF16))
    qs = jnp.concatenate(stacked, axis=0)
    col_head = lax.broadcasted_iota(jnp.int32, (1, ATTN_GROUP * rows), 1) // rows
    sink = jnp.zeros((1, ATTN_GROUP * rows), F32)
    for hh in range(ATTN_GROUP):
        sink = jnp.where(col_head == hh, sink_ref[ATTN_GROUP * g + hh][:, :1], sink)
    ss = []
    m = sink
    for kk, mk in zip(keys, masks):
        s = _dot_nt(kk, qs)
        if mk is not None:
            s = jnp.where(jnp.concatenate([mk] * ATTN_GROUP, axis=1), s, -jnp.inf)
        ss.append(s)
        m = jnp.maximum(m, jnp.max(s, axis=0, keepdims=True))
    den = jnp.exp(sink - m)
    acc = jnp.zeros((ATTN_HEAD_DIM, ATTN_GROUP * rows), F32)
    for s, vt in zip(ss, values_t):
        e = jnp.exp(s - m)
        den = den + jnp.sum(e, axis=0, keepdims=True)
        acc = acc + _dot(vt, e.astype(BF16))
    o = (acc / den).astype(BF16)
    for hh in range(ATTN_GROUP):
        h = ATTN_GROUP * g + hh
        o_ref[h * ATTN_HEAD_DIM:(h + 1) * ATTN_HEAD_DIM, :] = o[:, hh * rows:(hh + 1) * rows]


def _ctx_attn_kernel(q_ref, kd_ref, vt_ref, sink_ref, o_ref):
    for g in range(ATTN_KV_HEADS):
        gl = slice(g * LANES, (g + 1) * LANES)
        q_refs = [q_ref[:, (2 * g + pp) * LANES:(2 * g + pp + 1) * LANES] for pp in range(2)]
        vt = vt_ref[g * ATTN_HEAD_DIM:(g + 1) * ATTN_HEAD_DIM, :]
        _attend_group(q_refs, [kd_ref[:, gl]], [vt], [None], sink_ref, g, o_ref)


def _context_attention(l, cq, ckd, cvt, sink_t):
    return pl.pallas_call(
        _ctx_attn_kernel,
        grid=(BATCH,),
        in_specs=[pl.BlockSpec((SEQ, 512), lambda b: (b, 0)),
                  pl.BlockSpec((SEQ, 256), lambda b: (b, 0)),
                  pl.BlockSpec((LANES, SEQ), lambda b: (0, b)),
                  _layer_spec(l, (ATTN_HEADS, 1, LANES))],
        out_specs=pl.BlockSpec((512, SEQ), lambda b: (0, b)),
        out_shape=jax.ShapeDtypeStruct((512, CTX_TOKENS), BF16),
        compiler_params=_cp(("arbitrary",)),
        name="context_attention",
    )(cq, ckd, cvt, sink_t)


def _lat_attn_kernel(q_ref, kd_ref, vt_ref, kc_ref, vct_ref, sink_ref, o_ref):
    n = pl.program_id(1)
    nq = DEC_SEQ // ATTN_BLOCK
    prev0 = pl.multiple_of(jnp.maximum(n - 1, 0) * ATTN_BLOCK, ATTN_BLOCK)
    cur0 = pl.multiple_of(n * ATTN_BLOCK, ATTN_BLOCK)
    next0 = pl.multiple_of(jnp.minimum(n + 1, nq - 1) * ATTN_BLOCK, ATTN_BLOCK)
    kj = lax.broadcasted_iota(jnp.int32, (ATTN_BLOCK, ATTN_BLOCK), 0)
    qi = lax.broadcasted_iota(jnp.int32, (ATTN_BLOCK, ATTN_BLOCK), 1)
    m_prev = jnp.logical_and(kj >= qi, n > 0)
    m_next = jnp.logical_and(kj <= qi, n < nq - 1)
    for g in range(ATTN_KV_HEADS):
        gl = slice(g * LANES, (g + 1) * LANES)
        gr = slice(g * ATTN_HEAD_DIM, (g + 1) * ATTN_HEAD_DIM)
        q_refs = [q_ref[:, (2 * g + pp) * LANES:(2 * g + pp + 1) * LANES] for pp in range(2)]
        keys = [kd_ref[pl.ds(prev0, ATTN_BLOCK), gl], kd_ref[pl.ds(cur0, ATTN_BLOCK), gl],
                kd_ref[pl.ds(next0, ATTN_BLOCK), gl], kc_ref[0, :, gl]]
        vals = [vt_ref[gr, pl.ds(prev0, ATTN_BLOCK)], vt_ref[gr, pl.ds(cur0, ATTN_BLOCK)],
                vt_ref[gr, pl.ds(next0, ATTN_BLOCK)], vct_ref[0, gr, :]]
        _attend_group(q_refs, keys, vals, [m_prev, None, m_next, None], sink_ref, g, o_ref)


def _latent_attention(l, cq, ckd, cvt, kc, vct, sink_t):
    nq = DEC_SEQ // ATTN_BLOCK
    q0 = CTX_TOKENS // ATTN_BLOCK
    s0 = CTX_TOKENS // DEC_SEQ
    return pl.pallas_call(
        _lat_attn_kernel,
        grid=(DEC_BATCH, nq),
        in_specs=[pl.BlockSpec((ATTN_BLOCK, 512), lambda b, n: (q0 + b * nq + n, 0)),
                  pl.BlockSpec((DEC_SEQ, 256), lambda b, n: (s0 + b, 0)),
                  pl.BlockSpec((LANES, DEC_SEQ), lambda b, n: (0, s0 + b)),
                  pl.BlockSpec((1, None, PAST_LEN, 256), lambda b, n: (b, l, 0, 0)),
                  pl.BlockSpec((1, None, LANES, PAST_LEN), lambda b, n: (b, l, 0, 0)),
                  _layer_spec(l, (ATTN_HEADS, 1, LANES))],
        out_specs=pl.BlockSpec((512, ATTN_BLOCK), lambda b, n: (0, b * nq + n)),
        out_shape=jax.ShapeDtypeStruct((512, LAT_TOKENS), BF16),
        compiler_params=_cp(("arbitrary", "arbitrary")),
        name="latent_attention",
    )(cq, ckd, cvt, kc, vct, sink_t)


def _gelu(x):
    return 0.5 * x * (1.0 + jnp.tanh(0.7978845608028654 * (x + 0.044715 * (x * x * x))))


def _merge_kernel(alpha, xc_ref, xl_ref, mod_ref, of_ref, ob_ref, ag_ref, ng_ref, bz_ref, lg_ref, lb_ref,
                  ws_ref, bs_ref, cc_ref, cl_ref, mg_ref, wb_ref, wo_ref, g1_ref, b1_ref,
                  rwh_ref, rwl_ref, rb_ref, utri_ref,
                  x1_ref, slot_ref, meta_ref, hs_ref,
                  h2_ref, cw_ref, gid_ref, stage_ref, cnt_ref, sem_ref):
    parts = [_merge_rows(alpha, slice(p * MERGE_ROWS, (p + 1) * MERGE_ROWS), xc_ref, xl_ref, mod_ref,
                         of_ref, ob_ref, ag_ref, ng_ref, bz_ref, lg_ref, lb_ref, ws_ref, bs_ref, cc_ref,
                         cl_ref, mg_ref, wb_ref, wo_ref, g1_ref, b1_ref, rwh_ref, rwl_ref, rb_ref,
                         x1_ref, h2_ref, cw_ref, gid_ref) for p in range(TM // MERGE_ROWS)]
    for step in range(MERGE_STAGES + len(parts) - 1):
        for lag, part in enumerate(parts):
            if 0 <= step - lag < MERGE_STAGES:
                next(part)
    _dispatch(h2_ref, cw_ref, gid_ref, utri_ref, slot_ref, meta_ref, hs_ref, stage_ref, cnt_ref, sem_ref)


def _aligned(row):
    return row if isinstance(row, int) else pl.multiple_of(row, HS_ALIGN)


def _window_copy(stage_ref, buf, src_row, hs_ref, dst_row, sem_ref, rows):
    return pltpu.make_async_copy(stage_ref.at[buf, pl.ds(_aligned(src_row), rows), :],
                                 hs_ref.at[pl.ds(_aligned(dst_row), rows), :], sem_ref.at[0])


def _by_window(small, fn):
    pl.when(small)(lambda: fn(SMALL_WIN))
    pl.when(jnp.logical_not(small))(lambda: fn(TM))


def _dispatch(h2_ref, cw_ref, gid_ref, utri_ref, slot_ref, meta_ref, hs_ref, stage_ref, cnt_ref, sem_ref):
    i = pl.program_id(0)

    @pl.when(i == 0)
    def _():
        for g in range(N_GROUPS + 1):
            cnt_ref[g] = 0

    gid = gid_ref[0:1, :]
    grp_i = lax.broadcasted_iota(jnp.int32, (8, TM), 0)
    onehot = jnp.where(grp_i.astype(F32) == gid, 1.0, 0.0)
    rank = _dot(onehot.astype(BF16), utri_ref[...])
    ncol = (rank[:, TM - 1:TM] + onehot[:, TM - 1:TM]).astype(jnp.int32)
    padc = ((ncol + (HS_ALIGN - 1)) // HS_ALIGN) * HS_ALIGN
    row8 = lax.broadcasted_iota(jnp.int32, (8, 1), 0)
    startc = jnp.zeros((8, 1), jnp.int32)
    run = jnp.zeros((1, 1), jnp.int32)
    for g in range(1, N_GROUPS):
        run = run + padc[g - 1:g, :]
        startc = jnp.where(row8 == g, run, startc)
    slot = jnp.sum(onehot * (rank + startc.astype(F32)), axis=0, keepdims=True)
    back = jnp.sum(onehot * (rank + (row8 * TM).astype(F32)), axis=0, keepdims=True)
    slot_ref[...] = jnp.broadcast_to(back, (LANES, TM)).T
    sel = jnp.where(lax.broadcasted_iota(jnp.int32, (STAGE_ROWS, TM), 0).astype(F32) == slot, 1.0, 0.0)
    cw = cw_ref[...]
    cw_hi = cw.astype(BF16)
    r1 = cw - cw_hi.astype(F32)
    cw_mid = r1.astype(BF16)
    cw_lo = (r1 - cw_mid.astype(F32)).astype(BF16)
    rowdata = jnp.concatenate([h2_ref[...], cw_hi, cw_mid, cw_lo], axis=1)
    buf = i % 2
    stage_ref[buf] = _dot(sel.astype(BF16), rowdata).astype(BF16)

    small = jnp.max(ncol, axis=0, keepdims=True)[0, 0] <= SMALL_WIN
    prev_small = cnt_ref[N_GROUPS] == 1

    def wait_windows(rows):
        for _ in range(N_GROUPS):
            _window_copy(stage_ref, buf, 0, hs_ref, 0, sem_ref, rows).wait()

    @pl.when(i > 0)
    def _():
        _by_window(prev_small, wait_windows)

    lane = lax.broadcasted_iota(jnp.int32, (8, LANES), 1)
    meta = jnp.where(lane == 2 * N_GROUPS, small.astype(jnp.int32), jnp.zeros((8, LANES), jnp.int32))
    starts, dsts = [], []
    for g in range(N_GROUPS):
        starts.append(startc[g, 0])
        dsts.append(g * HS_CAP + cnt_ref[g])
        cnt_ref[g] = cnt_ref[g] + padc[g, 0]
        meta = jnp.where(lane == g, dsts[g], meta)
        meta = jnp.where(lane == N_GROUPS + g, cnt_ref[g], meta)
    meta_ref[0] = meta
    cnt_ref[N_GROUPS] = small.astype(jnp.int32)

    def start_windows(rows):
        for g in range(N_GROUPS):
            _window_copy(stage_ref, buf, starts[g], hs_ref, dsts[g], sem_ref, rows).start()

    _by_window(small, start_windows)

    @pl.when(i == NB - 1)
    def _():
        _by_window(small, wait_windows)
        for g in range(N_GROUPS):
            for w in range(2):
                _window_copy(stage_ref, buf, STAGE_ROWS - TM, hs_ref, g * HS_CAP + cnt_ref[g] + w * TM,
                             sem_ref, TM).start()
        for _ in range(2 * N_GROUPS):
            _window_copy(stage_ref, buf, 0, hs_ref, 0, sem_ref, TM).wait()


def _merge_rows(alpha, rows, xc_ref, xl_ref, mod_ref, of_ref, ob_ref, ag_ref, ng_ref, bz_ref, lg_ref, lb_ref,
                ws_ref, bs_ref, cc_ref, cl_ref, mg_ref, wb_ref, wo_ref, g1_ref, b1_ref,
                rwh_ref, rwl_ref, rb_ref, x1_ref, h2_ref, cw_ref, gid_ref):
    i = pl.program_id(0)
    m = mod_ref[0]
    n_rows = rows.stop - rows.start
    o = of_ref[rows, :] + ob_ref[rows, :]
    parts = []
    for h in range(GLA_HEADS):
        oh = o[:, h * GLA_DV:(h + 1) * GLA_DV]
        parts.append(oh * lax.rsqrt(jnp.mean(oh * oh, axis=-1, keepdims=True) + LN_EPS))
    br_a = (jnp.concatenate(parts, axis=1) * ng_ref[...] * _silu(ag_ref[rows, :])).astype(BF16)
    z = _gelu(bz_ref[rows, :])
    u = z[:, :BRANCH_W]
    v = _layer_norm(z[:, BRANCH_W:], lg_ref[...], lb_ref[...]).astype(BF16)
    yield
    chunks = []
    for c in range(n_rows // GMLP_CHUNK):
        vc = v[c * GMLP_CHUNK:(c + 1) * GMLP_CHUNK]
        cols = [_dot(ws_ref[g], vc[:, g * GMLP_CH:(g + 1) * GMLP_CH]) for g in range(GMLP_GROUPS)]
        chunks.append(jnp.concatenate(cols, axis=1) + bs_ref[...])
    br_b = (u * jnp.concatenate(chunks, axis=0)).astype(BF16)
    br_ct = jnp.where(i < NB_CTX, cc_ref[:, rows], cl_ref[:, rows])
    projs = (_dot(br_a, wb_ref[0]), _dot(br_b, wb_ref[1]), _dot_tn(br_ct, wb_ref[2]))
    yield
    y = jnp.zeros((n_rows, D_MODEL), F32)
    for nbr, proj in enumerate(projs):
        y = y + (jnp.tanh(mg_ref[rows, nbr * D_MODEL:(nbr + 1) * D_MODEL]) + 1.0) * proj
    y = y.astype(BF16)
    yield
    y = _dot(y, wo_ref[...])
    yield
    x = jnp.where(i < NB_CTX, xc_ref[rows, :], xl_ref[rows, :])
    x1 = _layer_norm(alpha * x + m[:, 2 * D_MODEL:3 * D_MODEL] * y, g1_ref[...], b1_ref[...])
    x1_ref[rows, :] = x1
    h2 = x1 * (1.0 + m[:, 4 * D_MODEL:5 * D_MODEL]) + m[:, 3 * D_MODEL:4 * D_MODEL]
    h2_ref[rows, :] = h2.astype(BF16)
    h_hi, h_lo = _split(h2)
    logits = (_dot_nt(rwh_ref[...], h_hi) + _dot_nt(rwh_ref[...], h_lo) + _dot_nt(rwl_ref[...], h_hi)
              + rb_ref[:, :n_rows])[:ROUTER_ROWS]
    row_i = lax.broadcasted_iota(jnp.int32, (ROUTER_ROWS, n_rows), 0)
    row = row_i.astype(F32)
    row_group = (row_i >> 2).astype(F32)
    big = float(LANES)
    neg = -jnp.inf
    gl = jnp.where((row_i >= N_EXPERTS) & (row_i < N_EXPERTS + N_GROUPS), logits, neg)
    gmax = jnp.max(gl, axis=0, keepdims=True)
    gsum = jnp.sum(jnp.exp(gl - gmax), axis=0, keepdims=True)
    g_p = 1.0 / gsum
    g_i = jnp.min(jnp.where(gl == gmax, row, big), axis=0, keepdims=True) - float(N_EXPERTS)
    in_group = (row_i < N_EXPERTS) & (row_group == g_i)
    el = jnp.where(in_group, logits, neg)
    emax = jnp.max(el, axis=0, keepdims=True)
    ee = jnp.exp(el - emax)
    e_prob = ee / jnp.sum(ee, axis=0, keepdims=True)
    p1 = jnp.max(jnp.where(in_group, e_prob, neg), axis=0, keepdims=True)
    i1 = jnp.min(jnp.where(in_group & (e_prob == p1), row, big), axis=0, keepdims=True)
    rest = in_group & (row != i1)
    p2 = jnp.max(jnp.where(rest, e_prob, neg), axis=0, keepdims=True)
    i2 = jnp.min(jnp.where(rest & (e_prob == p2), row, big), axis=0, keepdims=True)
    tot = p1 + p2
    cw_t = (jnp.where(row == i1, g_p * p1 / tot, 0.0) + jnp.where(row == i2, g_p * p2 / tot, 0.0))
    cw_ref[rows, :] = jnp.concatenate([cw_t, jnp.zeros((LANES - ROUTER_ROWS, n_rows), F32)], axis=0).T
    gid_ref[0:1, rows] = g_i
    yield


def _merge(l, alpha, xc, xl, mod3, o_f, o_b, ag, norm_g, bz, ln_g, ln_b, ws, bs_t, brc_ctx, brc_lat, mg,
           wb, wo, g1, b1, rw_hi, rw_lo, rb):
    utri = jnp.asarray(np.arange(TM)[:, None] < np.arange(TM)[None, :], dtype=BF16)
    return pl.pallas_call(
        functools.partial(_merge_kernel, alpha),
        grid=(NB,),
        in_specs=[_ctx_tok(D_MODEL), _lat_tok(D_MODEL), _mod_spec(),
                  _tok(512), _tok(512), _tok(512), _layer_spec(l, (1, 512)),
                  _tok(1024), _layer_spec(l, (1, 512)), _layer_spec(l, (1, 512)),
                  _layer_spec(l, (GMLP_GROUPS, GMLP_CHUNK, GMLP_CHUNK)), _layer_spec(l, (GMLP_CHUNK, 512)),
                  pl.BlockSpec((512, TM), lambda i: (0, jnp.minimum(i, NB_CTX - 1))),
                  pl.BlockSpec((512, TM), lambda i: (0, jnp.maximum(i - NB_CTX, 0))),
                  _tok(3072), _layer_spec(l, (N_BRANCH, BRANCH_W, D_MODEL)), _layer_spec(l, (D_MODEL, D_MODEL)),
                  _layer_spec(l, (1, D_MODEL)), _layer_spec(l, (1, D_MODEL)),
                  _layer_spec(l, (LANES, D_MODEL)), _layer_spec(l, (LANES, D_MODEL)), _layer_spec(l, (LANES, TM)),
                  pl.BlockSpec((TM, TM), lambda i: (0, 0))],
        out_specs=[_tok(D_MODEL), _tok(LANES), pl.BlockSpec((1, 8, LANES), lambda i: (i, 0, 0)),
                   pl.BlockSpec(memory_space=pl.ANY)],
        out_shape=[jax.ShapeDtypeStruct((TOKENS, D_MODEL), F32),
                   jax.ShapeDtypeStruct((TOKENS, LANES), F32),
                   jax.ShapeDtypeStruct((NB, 8, LANES), jnp.int32),
                   jax.ShapeDtypeStruct((N_GROUPS * HS_CAP, HS_W), BF16)],
        scratch_shapes=[pltpu.VMEM((TM, D_MODEL), BF16), pltpu.VMEM((TM, LANES), F32), pltpu.VMEM((8, TM), F32),
                        pltpu.VMEM((2, STAGE_ROWS, HS_W), BF16), pltpu.SMEM((N_GROUPS + 1,), jnp.int32),
                        pltpu.SemaphoreType.DMA((1,))],
        compiler_params=_cp(("arbitrary",)),
        name="merge_ln1_router",
    )(xc, xl, mod3, o_f, o_b, ag, norm_g, bz, ln_g, ln_b, ws, bs_t, brc_ctx, brc_lat, mg,
      wb, wo, g1, b1, rw_hi, rw_lo, rb, utri)


def _experts_kernel(blk_ref, grp_ref, ntile_ref, hs_ref, wg32_ref, wu32_ref, wd32_ref, ys_ref,
                    wg_ref, wu_ref, wd_ref):
    s = pl.program_id(0)

    @pl.when(jnp.logical_or(s == 0, grp_ref[s] != grp_ref[jnp.maximum(s - 1, 0)]))
    def _():
        wg_ref[...] = wg32_ref[...].astype(BF16)
        wu_ref[...] = wu32_ref[...].astype(BF16)
        wd_ref[...] = wd32_ref[...].astype(BF16)

    @pl.when(s < ntile_ref[0])
    def _():
        g = grp_ref[s]
        x = hs_ref[:, :D_MODEL]
        cw = (hs_ref[:, D_MODEL:D_MODEL + LANES].astype(F32)
              + hs_ref[:, D_MODEL + LANES:D_MODEL + 2 * LANES].astype(F32)
              + hs_ref[:, D_MODEL + 2 * LANES:].astype(F32))
        lane = lax.broadcasted_iota(jnp.int32, (TM, LANES), 1)
        hid = []
        for k in range(EXPERTS_PER_GROUP):
            w_e = jnp.sum(jnp.where(lane == g * EXPERTS_PER_GROUP + k, cw, 0.0), axis=-1, keepdims=True)
            hid.append((_silu(_dot(x, wg_ref[k])) * _dot(x, wu_ref[k]) * w_e).astype(BF16))
        y = _dot(jnp.concatenate(hid, axis=1), wd_ref[...].reshape(EXPERTS_PER_GROUP * EXPERT_FF, D_MODEL))
        ys_ref[...] = y.astype(BF16)


def _experts(l, tile_blk, tile_grp, n_tiles, hs, wg, wu, wd):
    e = EXPERTS_PER_GROUP
    return pl.pallas_call(
        _experts_kernel,
        grid_spec=pltpu.PrefetchScalarGridSpec(
            num_scalar_prefetch=3,
            grid=(MOE_TILES,),
            in_specs=[pl.BlockSpec((TM, HS_W), lambda s, blk, grp, nt: (blk[s], 0)),
                      pl.BlockSpec((None, e, D_MODEL, EXPERT_FF), lambda s, blk, grp, nt: (l, grp[s], 0, 0)),
                      pl.BlockSpec((None, e, D_MODEL, EXPERT_FF), lambda s, blk, grp, nt: (l, grp[s], 0, 0)),
                      pl.BlockSpec((None, e, EXPERT_FF, D_MODEL), lambda s, blk, grp, nt: (l, grp[s], 0, 0))],
            out_specs=pl.BlockSpec((TM, D_MODEL), lambda s, blk, grp, nt: (blk[s], 0)),
            scratch_shapes=[pltpu.VMEM((e, D_MODEL, EXPERT_FF), BF16), pltpu.VMEM((e, D_MODEL, EXPERT_FF), BF16),
                            pltpu.VMEM((e, EXPERT_FF, D_MODEL), BF16)],
        ),
        out_shape=jax.ShapeDtypeStruct((N_GROUPS * HS_CAP, D_MODEL), BF16),
        compiler_params=_cp(("arbitrary",)),
        name="moe_experts",
    )(tile_blk, tile_grp, n_tiles, hs, wg, wu, wd)


def _tile_table(final_counts):
    nt = (final_counts + 2 * TM - 1) // TM
    ends = jnp.cumsum(nt)
    total = ends[-1]
    s = jnp.minimum(jnp.arange(MOE_TILES), total - 1)
    grp = jnp.sum((s[:, None] >= ends[None, :]).astype(jnp.int32), axis=1)
    first = ends - nt
    blk = grp * (HS_CAP // TM) + s - first[grp]
    return blk.astype(jnp.int32), grp.astype(jnp.int32), total.reshape(1).astype(jnp.int32)


def _window_fetch(ys_ref, offs_ref, blk, win_ref, buf, sem_ref, rows):
    return [pltpu.make_async_copy(
        ys_ref.at[pl.ds(pl.multiple_of(offs_ref[blk * N_GROUPS + g], HS_ALIGN), rows), :],
        win_ref.at[buf, pl.ds(g * rows, rows), :], sem_ref.at[buf]) for g in range(N_GROUPS)]


def _combine_kernel(alpha, offs_ref, small_ref, ys_ref, slot_ref, x1_ref, mod_ref, g2_ref, b2_ref,
                    oc_ref, ol_ref, win_ref, y_ref, sem_ref):
    i = pl.program_id(0)
    buf = i % 2

    def start(blk, to_buf):
        def go(rows):
            for cp in _window_fetch(ys_ref, offs_ref, blk, win_ref, to_buf, sem_ref, rows):
                cp.start()
        _by_window(small_ref[blk] == 1, go)

    pl.when(i == 0)(lambda: start(0, 0))
    pl.when(i + 1 < NB)(lambda: start(i + 1, 1 - buf))

    def gather(rows):
        for cp in _window_fetch(ys_ref, offs_ref, i, win_ref, buf, sem_ref, rows):
            cp.wait()
        back = slot_ref[...]
        packed = back - jnp.floor(back * (1.0 / TM)) * float(TM - rows)
        slot = jnp.concatenate([packed] * (N_GROUPS * rows // LANES), axis=1)
        lane = lax.broadcasted_iota(jnp.int32, (TM, N_GROUPS * rows), 1).astype(F32)
        sel = jnp.where(lane == slot, 1.0, 0.0).astype(BF16)
        y_ref[...] = _dot(sel, win_ref[buf, :N_GROUPS * rows, :])

    _by_window(small_ref[i] == 1, gather)
    m = mod_ref[0]
    out = _layer_norm(alpha * x1_ref[...] + m[:, 5 * D_MODEL:6 * D_MODEL] * y_ref[...],
                      g2_ref[...], b2_ref[...])

    @pl.when(i < NB_CTX)
    def _():
        oc_ref[...] = out

    @pl.when(i >= NB_CTX)
    def _():
        ol_ref[...] = out


def _combine(l, alpha, offs, small, ys, slot, x1, mod3, g2, b2):
    return pl.pallas_call(
        functools.partial(_combine_kernel, alpha),
        grid_spec=pltpu.PrefetchScalarGridSpec(
            num_scalar_prefetch=2,
            grid=(NB,),
            in_specs=[pl.BlockSpec(memory_space=pl.ANY),
                      pl.BlockSpec((TM, LANES), lambda i, *_: (i, 0)),
                      pl.BlockSpec((TM, D_MODEL), lambda i, *_: (i, 0)),
                      pl.BlockSpec((1, 1, 6 * D_MODEL), lambda i, *_: (_mod_block(i), 0, 0)),
                      pl.BlockSpec((None, 1, D_MODEL), lambda i, *_: (l, 0, 0)),
                      pl.BlockSpec((None, 1, D_MODEL), lambda i, *_: (l, 0, 0))],
            out_specs=[pl.BlockSpec((TM, D_MODEL), lambda i, *_: (jnp.minimum(i, NB_CTX - 1), 0)),
                       pl.BlockSpec((TM, D_MODEL), lambda i, *_: (jnp.maximum(i - NB_CTX, 0), 0))],
            scratch_shapes=[pltpu.VMEM((2, N_GROUPS * TM, D_MODEL), BF16), pltpu.VMEM((TM, D_MODEL), F32),
                            pltpu.SemaphoreType.DMA((2,))],
        ),
        out_shape=[jax.ShapeDtypeStruct((CTX_TOKENS, D_MODEL), F32),
                   jax.ShapeDtypeStruct((LAT_TOKENS, D_MODEL), F32)],
        compiler_params=_cp(("arbitrary",)),
        name="combine_ln2",
    )(offs, small, ys, slot, x1, mod3, g2, b2)


def _rope_tables():
    f32 = np.float32
    pos = np.arange(DEC_SEQ)
    row = (pos // GRID_W).astype(f32)
    col = (pos % GRID_W).astype(f32)
    quarter = ATTN_HEAD_DIM // 4
    inv_freq = np.power(f32(ROPE_BASE), -np.arange(quarter, dtype=f32) / f32(quarter)).astype(f32)
    j = np.arange(ATTN_HEAD_DIM)
    p = np.where((j // (2 * quarter))[None, :] == 0, row[:, None], col[:, None]).astype(f32)
    ang = (p * inv_freq[j % quarter][None, :]).astype(f32)
    sign = np.where((j % (2 * quarter)) < quarter, -1.0, 1.0).astype(f32)
    cos = np.tile(np.cos(ang).astype(f32), (1, ATTN_HEADS))
    sin = np.tile((np.sin(ang) * sign[None, :]).astype(f32), (1, ATTN_HEADS))
    cos = np.concatenate([np.ones((TM, 512), f32), cos], axis=0)
    sin = np.concatenate([np.zeros((TM, 512), f32), sin], axis=0)
    return jnp.asarray(cos), jnp.asarray(sin)


def _dup_heads(t):
    h0, h1 = t[..., :ATTN_HEAD_DIM], t[..., ATTN_HEAD_DIM:]
    return jnp.concatenate([h0, h0, h1, h1], axis=-1)


def _split_w_in(w):
    gate_scale = np.where(np.arange(IN_COLS - WB_START) >= WB_MG.start, 0.5, 1.0).astype(np.float32)
    return [w[..., :WA_COLS].astype(BF16), (w[..., WB_START:] * gate_scale).astype(BF16)]


def kernel(x_prompt, x_sample, state_gla, cache_k, cache_v, c, c_ctx, ada_w, ada_b, w_in, gla_gate_w, gla_gate_b, gla_norm_g, gmlp_ln_g, gmlp_ln_b, gmlp_ws, gmlp_bs, attn_sink, w_branch, w_out, ln1_g, ln1_b, ln2_g, ln2_b, router_group_w, router_group_b, router_expert_w, router_expert_b, expert_w_gate, expert_w_up, expert_w_down):
    alpha = (2.0 * DEPTH) ** 0.25
    xc = x_prompt.reshape(CTX_TOKENS, D_MODEL)
    xl = x_sample.reshape(LAT_TOKENS, D_MODEL)
    cond = jnp.concatenate([c_ctx[None, :], c, jnp.zeros((MOD_ROWS - 1 - DEC_BATCH, D_MODEL), F32)], axis=0)
    cos_t, sin_t = _rope_tables()

    ada_b3 = ada_b[:, None, :]
    w_parts = _split_w_in(w_in)
    zrow = jnp.zeros((DEPTH, GLA_RANK, 256), F32)
    zpad = jnp.zeros((DEPTH, LANES - 2 * GLA_RANK, 256), F32)
    gw_p = jnp.stack([jnp.concatenate([gla_gate_w[:, 0], zrow, zpad], axis=1),
                      jnp.concatenate([zrow, gla_gate_w[:, 1], zpad], axis=1)], axis=1).astype(BF16)
    gb_p = gla_gate_b[:, :, None, :]
    sink_t = jnp.broadcast_to(attn_sink[:, :, None, None], (DEPTH, ATTN_HEADS, 1, LANES))
    kc = _dup_heads(cache_k.reshape(DEC_BATCH, DEPTH, PAST_LEN, 128)).astype(BF16)
    vct = jnp.swapaxes(cache_v.reshape(DEC_BATCH, DEPTH, PAST_LEN, 128), 2, 3).astype(BF16)
    rw = jnp.swapaxes(jnp.concatenate(
        [router_expert_w, router_group_w,
         jnp.zeros((DEPTH, D_MODEL, LANES - N_EXPERTS - N_GROUPS), F32)], axis=2), 1, 2)
    rw_hi = rw.astype(BF16)
    rw_lo = (rw - rw_hi.astype(F32)).astype(BF16)
    rb = jnp.concatenate([router_expert_b, router_group_b,
                          jnp.zeros((DEPTH, LANES - N_EXPERTS - N_GROUPS), F32)], axis=1)
    rb = jnp.broadcast_to(rb[:, :, None], (DEPTH, LANES, TM))
    bs_t = jnp.repeat(jnp.swapaxes(gmlp_bs, 1, 2), GMLP_CH, axis=2)
    ws_b = gmlp_ws.astype(BF16)
    wb_b = (0.5 * w_branch).astype(BF16)
    wo_b = w_out.astype(BF16)
    row = lambda t: t[:, None, :]

    states, keys, values = [], [], []
    for l in range(DEPTH):
        mod3 = _modulation(l, cond, ada_w, ada_b3).reshape(MOD_ROWS, 1, 6 * D_MODEL)
        (qef, kef, klf, qeb, keb, klb, dec, av, avt, ag, bz, cq, ckd, cvt, ck, cv, mg) = _in_projection(
            l, xc, xl, mod3, cos_t, sin_t, gw_p, gb_p, w_parts)
        o_f, o_b, s_fin = _gla(l, qef, kef, klf, qeb, keb, klb, dec, av, avt, state_gla)
        brc_ctx = _context_attention(l, cq, ckd, cvt, sink_t)
        brc_lat = _latent_attention(l, cq, ckd, cvt, kc, vct, sink_t)
        x1, slot, meta, hs = _merge(l, alpha, xc, xl, mod3, o_f, o_b, ag, row(gla_norm_g), bz,
                                    row(gmlp_ln_g), row(gmlp_ln_b), ws_b, bs_t, brc_ctx, brc_lat, mg, wb_b,
                                    wo_b, row(ln1_g), row(ln1_b), rw_hi, rw_lo, rb)
        offs = meta[:, 0, :N_GROUPS].reshape(NB * N_GROUPS)
        tile_blk, tile_grp, n_tiles = _tile_table(meta[NB - 1, 0, N_GROUPS:2 * N_GROUPS])
        ys = _experts(l, tile_blk, tile_grp, n_tiles, hs, expert_w_gate, expert_w_up, expert_w_down)
        xc, xl = _combine(l, alpha, offs, meta[:, 0, 2 * N_GROUPS], ys, slot, x1, mod3,
                          row(ln2_g), row(ln2_b))
        states.append(s_fin)
        keys.append(ck.reshape(BATCH, SEQ, ATTN_KV_HEADS, ATTN_HEAD_DIM))
        values.append(cv.reshape(BATCH, SEQ, ATTN_KV_HEADS, ATTN_HEAD_DIM))
    return (xc.reshape(BATCH, SEQ, D_MODEL), xl.reshape(DEC_BATCH, DEC_SEQ, D_MODEL),
            jnp.stack(states, axis=1), jnp.stack(keys, axis=1), jnp.stack(values, axis=1))
```

```python
import functools

import jax
import jax.numpy as jnp
import numpy as np
from jax import lax
from jax.experimental import pallas as pl
from jax.experimental.pallas import tpu as pltpu

F32 = jnp.float32
BF16 = jnp.bfloat16

D_MODEL = 1024
BATCH = 32
SEQ = 256
DEPTH = 2
DEC_BATCH = 8
DEC_SEQ = 1024
PAST_LEN = 512
GRID_W = 64
GLA_HEADS = 4
GLA_DV = 128
GLA_DK = 64
GLA_RANK = 16
GLA_TAU = 16.0
GLA_CHUNK = 32
GMLP_CHUNK = 128
GMLP_CH = 128
GMLP_GROUPS = 4
ATTN_HEADS = 8
ATTN_KV_HEADS = 2
ATTN_GROUP = ATTN_HEADS // ATTN_KV_HEADS
ATTN_HEAD_DIM = 64
WINDOW = 128
ATTN_BLOCK = 128
ROPE_BASE = 10000.0
BRANCH_W = 512
N_BRANCH = 3
N_GROUPS = 4
EXPERTS_PER_GROUP = 4
N_EXPERTS = 16
EXPERT_FF = 256
LN_EPS = 1e-5

LANES = 128
TM = 256
CTX_TOKENS = BATCH * SEQ
LAT_TOKENS = DEC_BATCH * DEC_SEQ
TOKENS = CTX_TOKENS + LAT_TOKENS
NB_CTX = CTX_TOKENS // TM
NB_LAT = LAT_TOKENS // TM
NB = NB_CTX + NB_LAT
LAT_BLOCKS_PER_SEQ = DEC_SEQ // TM
MOD_ROWS = 16
GLA_SUB = 128
CHUNKS_PER_SUB = GLA_SUB // GLA_CHUNK
MERGE_ROWS = 128
MERGE_STAGES = 5
ROUTER_ROWS = 24
HS_ALIGN = 16
SMALL_WIN = 128
HS_W = D_MODEL + 3 * LANES
STAGE_ROWS = TM + N_GROUPS * HS_ALIGN + TM
_MAX_PAD = NB * (HS_ALIGN - 1)
HS_CAP = -(-(TOKENS + _MAX_PAD + 2 * TM) // TM) * TM
MOE_TILES = (TOKENS + N_GROUPS * _MAX_PAD) // TM + 2 * N_GROUPS

IN_COLS = 6432
WA_COLS = 1664
WA_AQ, WA_AK, WA_AV, WA_AG = slice(0, 256), slice(256, 512), slice(512, 1024), slice(1024, 1536)
WA_LR = slice(1536, 1664)
WB_START = 1568
WB_BZ, WB_CQ, WB_CK, WB_CV = slice(0, 1024), slice(1024, 1536), slice(1536, 1664), slice(1664, 1792)
WB_MG = slice(1792, 4864)

VMEM_LIMIT = 56 * 1024 * 1024


def _cp(sem):
    return pltpu.CompilerParams(dimension_semantics=sem, vmem_limit_bytes=VMEM_LIMIT)


def _dot(a, b):
    return jnp.dot(a, b, preferred_element_type=F32)


def _dot_nt(a, b):
    return lax.dot_general(a, b, (((1,), (1,)), ((), ())), preferred_element_type=F32)


def _dot_tn(a, b):
    return lax.dot_general(a, b, (((0,), (0,)), ((), ())), preferred_element_type=F32)


def _split(x):
    hi = x.astype(BF16)
    lo = (x - hi.astype(F32)).astype(BF16)
    return hi, lo


def _sigmoid(x):
    return 0.5 * (jnp.tanh(0.5 * x) + 1.0)


def _silu(x):
    return x * _sigmoid(x)


def _layer_norm(x, g, b):
    mu = jnp.mean(x, axis=-1, keepdims=True)
    xc = x - mu
    var = jnp.mean(xc * xc, axis=-1, keepdims=True)
    return xc * lax.rsqrt(var + LN_EPS) * g + b


def _mod_block(i):
    return jnp.where(i < NB_CTX, 0, 1 + (i - NB_CTX) // LAT_BLOCKS_PER_SEQ)


def _tok(width):
    return pl.BlockSpec((TM, width), lambda i: (i, 0))


def _ctx_tok(width):
    return pl.BlockSpec((TM, width), lambda i: (jnp.minimum(i, NB_CTX - 1), 0))


def _lat_tok(width):
    return pl.BlockSpec((TM, width), lambda i: (jnp.maximum(i - NB_CTX, 0), 0))


def _layer_spec(l, shape):
    return pl.BlockSpec((None,) + shape, lambda *_: (l,) + (0,) * len(shape))


def _mod_spec(l):
    return pl.BlockSpec((None, 1, 1, 6 * D_MODEL), lambda i, *_: (l, _mod_block(i), 0, 0))


def _mod_kernel(cond_ref, w_ref, b_ref, o_ref):
    s_hi, s_lo = _split(_silu(cond_ref[...]))
    w_hi, w_lo = _split(w_ref[...])
    o_ref[...] = _dot(s_hi, w_hi) + _dot(s_lo, w_hi) + _dot(s_hi, w_lo) + b_ref[...]


def _modulation(cond, ada_w, ada_b3):
    tn = 1536
    return pl.pallas_call(
        _mod_kernel,
        grid=(DEPTH, 6 * D_MODEL // tn),
        in_specs=[pl.BlockSpec((MOD_ROWS, D_MODEL), lambda l, j: (0, 0)),
                  pl.BlockSpec((None, D_MODEL, tn), lambda l, j: (l, 0, j)),
                  pl.BlockSpec((None, 1, tn), lambda l, j: (l, 0, j))],
        out_specs=pl.BlockSpec((None, MOD_ROWS, tn), lambda l, j: (l, 0, j)),
        out_shape=jax.ShapeDtypeStruct((DEPTH, MOD_ROWS, 6 * D_MODEL), F32),
        compiler_params=_cp(("arbitrary", "arbitrary")),
        name="modulation",
    )(cond, ada_w, ada_b3)


def _rope(x, cos, sin):
    n = x.shape[1]
    lane = lax.broadcasted_iota(jnp.int32, x.shape, 1)
    partner = jnp.where((lane & 31) < 16, pltpu.roll(x, n - 16, 1), pltpu.roll(x, 16, 1))
    return x * cos + partner * sin


def _inproj_kernel(xc_ref, xl_ref, mod_ref, cos_ref, sin_ref, gw_ref, gb_ref, lg_ref, lb_ref,
                   wa_ref, wb_ref,
                   qef_ref, kef_ref, klf_ref, qeb_ref, keb_ref, klb_ref, dec_ref,
                   av_ref, avt_ref, sag_ref, zu_ref, zv_ref,
                   cq_ref, ckd_ref, cvt_ref, ck_ref, cv_ref, gate_ref):
    i = pl.program_id(0)
    m = mod_ref[0]
    x = jnp.where(i < NB_CTX, xc_ref[...], xl_ref[...])
    h = (x * (1.0 + m[:, D_MODEL:2 * D_MODEL]) + m[:, 0:D_MODEL]).astype(BF16)

    aq = _dot(h, wa_ref[:, WA_AQ])
    ak = _dot(h, wa_ref[:, WA_AK])
    lr = _dot(h, wa_ref[:, WA_LR]).astype(BF16)
    zs = [_dot(lr, gw_ref[d]) + gb_ref[d] for d in range(2)]
    for nbr in range(N_BRANCH):
        cols = slice(WB_MG.start + nbr * D_MODEL, WB_MG.start + (nbr + 1) * D_MODEL)
        gate_ref[:, nbr * D_MODEL:(nbr + 1) * D_MODEL] = jnp.tanh(_dot(h, wb_ref[:, cols])) + 1.0
    r = lax.broadcasted_iota(jnp.int32, (TM, TM), 0)
    c = lax.broadcasted_iota(jnp.int32, (TM, TM), 1)
    same = (r >> 5) == (c >> 5)
    bs = []
    for d in range(2):
        g = (jnp.minimum(zs[d], 0.0) - jnp.log(1.0 + jnp.exp(-jnp.abs(zs[d])))) / GLA_TAU
        tri_b = jnp.where(same & ((c <= r) if d == 0 else (c >= r)), 1.0, 0.0).astype(BF16)
        g_hi, g_lo = _split(g)
        bs.append(_dot(tri_b, g_hi) + _dot(tri_b, g_lo))
    z = _gelu(_dot(h, wb_ref[:, WB_BZ]))
    zu_ref[...] = z[:, :BRANCH_W]
    zv_ref[...] = _layer_norm(z[:, BRANCH_W:], lg_ref[...], lb_ref[...]).astype(BF16)
    sag_ref[...] = _silu(_dot(h, wa_ref[:, WA_AG]))
    av = _dot(h, wa_ref[:, WA_AV])
    av_ref[...] = av.astype(BF16)
    avt_ref[...] = av.T.astype(BF16)
    decs = []
    for d, (qe_ref, ke_ref, kl_ref) in enumerate(((qef_ref, kef_ref, klf_ref), (qeb_ref, keb_ref, klb_ref))):
        b = bs[d]
        last = GLA_CHUNK - 1 if d == 0 else 0
        tot = [b[n * GLA_CHUNK + last:n * GLA_CHUNK + last + 1, :] for n in range(TM // GLA_CHUNK)]
        bl = jnp.concatenate([jnp.broadcast_to(t, (GLA_CHUNK, t.shape[1])) for t in tot], axis=0)
        qe_ref[...] = (aq * jnp.exp(b) * (GLA_DK ** -0.5)).astype(BF16)
        ke_ref[...] = (ak * jnp.exp(-b)).astype(BF16)
        kl_ref[...] = (ak * jnp.exp(bl - b)).astype(BF16)
        decs.append(jnp.exp(jnp.concatenate(tot, axis=0)))
    dec_ref[...] = jnp.concatenate(decs, axis=1)

    cos = cos_ref[...]
    sin = sin_ref[...]
    cq_ref[...] = (_rope(_dot(h, wb_ref[:, WB_CQ]), cos, sin) * (ATTN_HEAD_DIM ** -0.5)).astype(BF16)
    ck = _dot(h, wb_ref[:, WB_CK])
    cv = _dot(h, wb_ref[:, WB_CV])
    kr = _rope(ck, cos[:, :LANES], sin[:, :LANES])
    swapped = pltpu.roll(kr, ATTN_HEAD_DIM, 1)
    lane = lax.broadcasted_iota(jnp.int32, (TM, LANES), 1)
    ckd_ref[...] = jnp.concatenate([jnp.where(lane < 64, kr, swapped), jnp.where(lane < 64, swapped, kr)],
                                   axis=1).astype(BF16)
    cvt_ref[...] = cv.T.astype(BF16)

    @pl.when(i < NB_CTX)
    def _():
        ck_ref[...] = ck
        cv_ref[...] = cv


def _rope_block(i):
    return jnp.where(i < NB_CTX, 0, 1 + (i - NB_CTX) % LAT_BLOCKS_PER_SEQ)


def _in_projection(l, xc, xl, mod3, cos_t, sin_t, gw_p, gb_p, ln_g, ln_b, w_parts):
    def tok_out(width, dt):
        return _tok(width), jax.ShapeDtypeStruct((TOKENS, width), dt)

    def feat_out(width):
        return (pl.BlockSpec((width, TM), lambda i: (0, i)), jax.ShapeDtypeStruct((width, TOKENS), BF16))

    def ctx_out(width):
        return _ctx_tok(width), jax.ShapeDtypeStruct((CTX_TOKENS, width), F32)

    dec_out = (pl.BlockSpec((TM // GLA_CHUNK, 512), lambda i: (i, 0)),
               jax.ShapeDtypeStruct((TOKENS // GLA_CHUNK, 512), F32))
    outs = [tok_out(256, BF16)] * 6 + [dec_out] + [
        tok_out(512, BF16), feat_out(512), tok_out(512, F32), tok_out(512, F32), tok_out(512, BF16),
        tok_out(512, BF16), tok_out(256, BF16), feat_out(LANES), ctx_out(LANES), ctx_out(LANES),
        tok_out(3072, F32)]
    return pl.pallas_call(
        _inproj_kernel,
        grid=(NB,),
        in_specs=[_ctx_tok(D_MODEL), _lat_tok(D_MODEL), _mod_spec(l),
                  pl.BlockSpec((TM, 512), lambda i: (_rope_block(i), 0)),
                  pl.BlockSpec((TM, 512), lambda i: (_rope_block(i), 0)),
                  _layer_spec(l, (2, LANES, 256)), _layer_spec(l, (2, 1, 256)),
                  _layer_spec(l, (1, 512)), _layer_spec(l, (1, 512))]
                 + [_layer_spec(l, (D_MODEL, w.shape[-1])) for w in w_parts],
        out_specs=[o[0] for o in outs],
        out_shape=[o[1] for o in outs],
        compiler_params=_cp(("arbitrary",)),
        name="in_projection",
    )(xc, xl, mod3, cos_t, sin_t, gw_p, gb_p, ln_g, ln_b, *w_parts)


def _gla_direction(qe, ke, kl, v, vt, dec, masks, state_ref, reverse):
    tri, half_masks, chunk_mask, head_diag = masks
    tri = tri[int(reverse)]
    outs = []
    for p in range(GLA_HEADS // 2):
        ps = slice(p * LANES, (p + 1) * LANES)
        qp = qe[:, ps]
        kep = ke[:, ps]
        klp = kl[:, ps]
        vp = v[:, p * 2 * GLA_DV:(p + 1) * 2 * GLA_DV]
        vtp = vt[p * 2 * GLA_DV:(p + 1) * 2 * GLA_DV, :]
        halves = []
        for hh in range(2):
            s = jnp.where(tri, _dot_nt(qp * half_masks[hh], kep), 0.0).astype(BF16)
            halves.append(_dot(s, vp[:, hh * GLA_DV:(hh + 1) * GLA_DV]))
        k_exp = jnp.concatenate([klp] * CHUNKS_PER_SUB, axis=1) * chunk_mask
        q_exp = jnp.concatenate([qp] * CHUNKS_PER_SUB, axis=1) * chunk_mask
        kvt = _dot(vtp, k_exp)
        st = state_ref[p]
        entering = [None] * CHUNKS_PER_SUB
        order = range(CHUNKS_PER_SUB - 1, -1, -1) if reverse else range(CHUNKS_PER_SUB)
        for n in order:
            entering[n] = st
            st = dec[n:n + 1, ps] * st + jnp.where(head_diag, kvt[:, n * LANES:(n + 1) * LANES], 0.0)
        state_ref[p] = st
        st_stack = jnp.concatenate(entering, axis=1).astype(BF16)
        inter = _dot_nt(q_exp, st_stack)
        outs.append(halves[0] + inter[:, :GLA_DV])
        outs.append(halves[1] + inter[:, GLA_DV:])
    return jnp.concatenate(outs, axis=1)


def _gla_mask_constants():
    r = np.arange(GLA_SUB)[:, None]
    c = np.arange(GLA_SUB)[None, :]
    same = (r // GLA_CHUNK) == (c // GLA_CHUNK)
    tri = np.stack([same & (c <= r), same & (c >= r)]).astype(np.float32)
    lane = np.arange(LANES)[None, :]
    halves = np.stack([np.broadcast_to(lane < 64, (GLA_SUB, LANES)),
                       np.broadcast_to(lane >= 64, (GLA_SUB, LANES))]).astype(np.float32)
    col_chunk = np.arange(CHUNKS_PER_SUB * LANES)[None, :] // LANES
    chunk_mask = ((r // GLA_CHUNK) == col_chunk).astype(np.float32)
    head_diag = ((np.arange(2 * GLA_DV)[:, None] // GLA_DV) == (lane // GLA_DK)).astype(np.float32)
    return (jnp.asarray(tri), jnp.asarray(halves, dtype=BF16), jnp.asarray(chunk_mask, dtype=BF16),
            jnp.asarray(head_diag))


def _gla_kernel(tri_ref, half_ref, cmask_ref, hdiag_ref,
                qf_ref, kef_ref, klf_ref, vf_ref, vtf_ref, decf_ref,
                qb_ref, keb_ref, klb_ref, vb_ref, vtb_ref, decb_ref,
                s0_ref, of_ref, ob_ref, sfin_ref, state_ref):
    i = pl.program_id(0)
    @pl.when(i < NB_CTX)
    def _():
        state_ref[...] = jnp.zeros_like(state_ref)

    @pl.when(jnp.logical_and(i >= NB_CTX, (i - NB_CTX) % LAT_BLOCKS_PER_SEQ == 0))
    def _():
        zero = jnp.zeros((GLA_DK, GLA_DV), F32)
        for d in range(2):
            for p in range(2):
                pair = jnp.concatenate(
                    [jnp.concatenate([s0_ref[0, d, 2 * p], zero], axis=1),
                     jnp.concatenate([zero, s0_ref[0, d, 2 * p + 1]], axis=1)], axis=0)
                state_ref[d, p] = pair.T

    masks = ((tri_ref[0] > 0.5, tri_ref[1] > 0.5), (half_ref[0], half_ref[1]), cmask_ref[...],
             hdiag_ref[...] > 0.5)
    n_sub = TM // GLA_SUB
    for j in range(n_sub):
        r0 = j * GLA_SUB
        rs = slice(r0, r0 + GLA_SUB)
        of_ref[rs, :] = _gla_direction(
            qf_ref[rs, :], kef_ref[rs, :], klf_ref[rs, :], vf_ref[rs, :], vtf_ref[:, rs],
            decf_ref[j * CHUNKS_PER_SUB:(j + 1) * CHUNKS_PER_SUB, :256], masks, state_ref.at[0], False)
        jb = n_sub - 1 - j
        rb = slice(jb * GLA_SUB, (jb + 1) * GLA_SUB)
        ob_ref[rb, :] = _gla_direction(
            qb_ref[rb, :], keb_ref[rb, :], klb_ref[rb, :], vb_ref[rb, :], vtb_ref[:, rb],
            decb_ref[jb * CHUNKS_PER_SUB:(jb + 1) * CHUNKS_PER_SUB, 256:], masks, state_ref.at[1], True)

    @pl.when(i < NB_CTX)
    def _():
        for d in range(2):
            for p in range(2):
                sp = state_ref[d, p].T
                sfin_ref[0, d, 2 * p] = sp[:GLA_DK, :GLA_DV]
                sfin_ref[0, d, 2 * p + 1] = sp[GLA_DK:, GLA_DV:]


def _bwd_block(i):
    k = (i - NB_CTX) % LAT_BLOCKS_PER_SEQ
    return jnp.where(i < NB_CTX, i, i - k + (LAT_BLOCKS_PER_SEQ - 1 - k))


def _gla(l, qef, kef, klf, qeb, keb, klb, dec, av, avt, state_gla):
    def bwd(width):
        return pl.BlockSpec((TM, width), lambda i: (_bwd_block(i), 0))

    def lat_seq(i):
        return jnp.maximum(i - NB_CTX, 0) // LAT_BLOCKS_PER_SEQ

    vt_fwd = pl.BlockSpec((512, TM), lambda i: (0, i))
    vt_bwd = pl.BlockSpec((512, TM), lambda i: (0, _bwd_block(i)))
    dec_fwd = pl.BlockSpec((TM // GLA_CHUNK, 512), lambda i: (i, 0))
    dec_bwd = pl.BlockSpec((TM // GLA_CHUNK, 512), lambda i: (_bwd_block(i), 0))
    state_dims = (2, GLA_HEADS, GLA_DK, GLA_DV)
    consts = _gla_mask_constants()
    return pl.pallas_call(
        _gla_kernel,
        grid=(NB,),
        in_specs=[pl.BlockSpec(t.shape, lambda i, nd=t.ndim: (0,) * nd) for t in consts]
                 + [_tok(256), _tok(256), _tok(256), _tok(512), vt_fwd, dec_fwd,
                  bwd(256), bwd(256), bwd(256), bwd(512), vt_bwd, dec_bwd,
                  pl.BlockSpec((1, None) + state_dims, lambda i: (lat_seq(i), l, 0, 0, 0, 0))],
        out_specs=[_tok(512), bwd(512),
                   pl.BlockSpec((1,) + state_dims, lambda i: (jnp.minimum(i, NB_CTX - 1), 0, 0, 0, 0))],
        out_shape=[jax.ShapeDtypeStruct((TOKENS, 512), F32), jax.ShapeDtypeStruct((TOKENS, 512), F32),
                   jax.ShapeDtypeStruct((BATCH,) + state_dims, F32)],
        scratch_shapes=[pltpu.VMEM((2, 2, 2 * GLA_DV, LANES), F32)],
        compiler_params=_cp(("arbitrary",)),
        name="gla_scan",
    )(*consts, qef, kef, klf, av, avt, dec, qeb, keb, klb, av, avt, dec, state_gla)


def _attend_group(q_refs, keys, values_t, masks, sink_ref, g, o_ref):
    rows = q_refs[0].shape[0]
    lane = lax.broadcasted_iota(jnp.int32, (rows, LANES), 1)
    stacked = []
    for q in q_refs:
        q32 = q.astype(F32)
        stacked.append(jnp.where(lane < 64, q32, 0.0).astype(BF16))
        stacked.append(jnp.where(lane >= 64, q32, 0.0).astype(BF16))
    qs = jnp.concatenate(stacked, axis=0)
    col_head = lax.broadcasted_iota(jnp.int32, (1, ATTN_GROUP * rows), 1) // rows
    sink = jnp.zeros((1, ATTN_GROUP * rows), F32)
    for hh in range(ATTN_GROUP):
        sink = jnp.where(col_head == hh, sink_ref[ATTN_GROUP * g + hh][:, :1], sink)
    ss = []
    m = sink
    for kk, mk in zip(keys, masks):
        s = _dot_nt(kk, qs)
        if mk is not None:
            s = jnp.where(jnp.concatenate([mk] * ATTN_GROUP, axis=1), s, -jnp.inf)
        ss.append(s)
        m = jnp.maximum(m, jnp.max(s, axis=0, keepdims=True))
    den = jnp.exp(sink - m)
    acc = jnp.zeros((ATTN_HEAD_DIM, ATTN_GROUP * rows), F32)
    for s, vt in zip(ss, values_t):
        e = jnp.exp(s - m)
        den = den + jnp.sum(e, axis=0, keepdims=True)
        acc = acc + _dot(vt, e.astype(BF16))
    o = (acc / den).astype(BF16)
    for hh in range(ATTN_GROUP):
        h = ATTN_GROUP * g + hh
        o_ref[h * ATTN_HEAD_DIM:(h + 1) * ATTN_HEAD_DIM, :] = o[:, hh * rows:(hh + 1) * rows]


def _ctx_attn_kernel(q_ref, kd_ref, vt_ref, sink_ref, o_ref):
    for g in range(ATTN_KV_HEADS):
        gl = slice(g * LANES, (g + 1) * LANES)
        q_refs = [q_ref[:, (2 * g + pp) * LANES:(2 * g + pp + 1) * LANES] for pp in range(2)]
        vt = vt_ref[g * ATTN_HEAD_DIM:(g + 1) * ATTN_HEAD_DIM, :]
        _attend_group(q_refs, [kd_ref[:, gl]], [vt], [None], sink_ref, g, o_ref)


def _context_attention(l, cq, ckd, cvt, sink_t):
    return pl.pallas_call(
        _ctx_attn_kernel,
        grid=(BATCH,),
        in_specs=[pl.BlockSpec((SEQ, 512), lambda b: (b, 0)),
                  pl.BlockSpec((SEQ, 256), lambda b: (b, 0)),
                  pl.BlockSpec((LANES, SEQ), lambda b: (0, b)),
                  _layer_spec(l, (ATTN_HEADS, 1, LANES))],
        out_specs=pl.BlockSpec((512, SEQ), lambda b: (0, b)),
        out_shape=jax.ShapeDtypeStruct((512, CTX_TOKENS), BF16),
        compiler_params=_cp(("arbitrary",)),
        name="context_attention",
    )(cq, ckd, cvt, sink_t)


def _lat_attn_kernel(q_ref, kd_ref, vt_ref, kc_ref, vct_ref, sink_ref, o_ref):
    n = pl.program_id(1)
    nq = DEC_SEQ // ATTN_BLOCK
    prev0 = pl.multiple_of(jnp.maximum(n - 1, 0) * ATTN_BLOCK, ATTN_BLOCK)
    cur0 = pl.multiple_of(n * ATTN_BLOCK, ATTN_BLOCK)
    next0 = pl.multiple_of(jnp.minimum(n + 1, nq - 1) * ATTN_BLOCK, ATTN_BLOCK)
    kj = lax.broadcasted_iota(jnp.int32, (ATTN_BLOCK, ATTN_BLOCK), 0)
    qi = lax.broadcasted_iota(jnp.int32, (ATTN_BLOCK, ATTN_BLOCK), 1)
    m_prev = jnp.logical_and(kj >= qi, n > 0)
    m_next = jnp.logical_and(kj <= qi, n < nq - 1)
    for g in range(ATTN_KV_HEADS):
        gl = slice(g * LANES, (g + 1) * LANES)
        gr = slice(g * ATTN_HEAD_DIM, (g + 1) * ATTN_HEAD_DIM)
        q_refs = [q_ref[:, (2 * g + pp) * LANES:(2 * g + pp + 1) * LANES] for pp in range(2)]
        keys = [kd_ref[pl.ds(prev0, ATTN_BLOCK), gl], kd_ref[pl.ds(cur0, ATTN_BLOCK), gl],
                kd_ref[pl.ds(next0, ATTN_BLOCK), gl], kc_ref[0, :, gl]]
        vals = [vt_ref[gr, pl.ds(prev0, ATTN_BLOCK)], vt_ref[gr, pl.ds(cur0, ATTN_BLOCK)],
                vt_ref[gr, pl.ds(next0, ATTN_BLOCK)], vct_ref[0, gr, :]]
        _attend_group(q_refs, keys, vals, [m_prev, None, m_next, None], sink_ref, g, o_ref)


def _latent_attention(l, cq, ckd, cvt, kc, vct, sink_t):
    nq = DEC_SEQ // ATTN_BLOCK
    q0 = CTX_TOKENS // ATTN_BLOCK
    s0 = CTX_TOKENS // DEC_SEQ
    return pl.pallas_call(
        _lat_attn_kernel,
        grid=(DEC_BATCH, nq),
        in_specs=[pl.BlockSpec((ATTN_BLOCK, 512), lambda b, n: (q0 + b * nq + n, 0)),
                  pl.BlockSpec((DEC_SEQ, 256), lambda b, n: (s0 + b, 0)),
                  pl.BlockSpec((LANES, DEC_SEQ), lambda b, n: (0, s0 + b)),
                  pl.BlockSpec((1, None, PAST_LEN, 256), lambda b, n: (b, l, 0, 0)),
                  pl.BlockSpec((1, None, LANES, PAST_LEN), lambda b, n: (b, l, 0, 0)),
                  _layer_spec(l, (ATTN_HEADS, 1, LANES))],
        out_specs=pl.BlockSpec((512, ATTN_BLOCK), lambda b, n: (0, b * nq + n)),
        out_shape=jax.ShapeDtypeStruct((512, LAT_TOKENS), BF16),
        compiler_params=_cp(("arbitrary", "arbitrary")),
        name="latent_attention",
    )(cq, ckd, cvt, kc, vct, sink_t)


def _gelu(x):
    return 0.5 * x * (1.0 + jnp.tanh(0.7978845608028654 * (x + 0.044715 * (x * x * x))))


def _merge_kernel(alpha, xc_ref, xl_ref, mod_ref, of_ref, ob_ref, sag_ref, ng_ref, zu_ref, zv_ref,
                  ws_ref, bs_ref, cc_ref, cl_ref, gate_ref, wb_ref, wo_ref, g1_ref, b1_ref,
                  rwh_ref, rwl_ref, rb_ref, utri_ref,
                  x1_ref, slot_ref, meta_ref, hs_ref,
                  h2_ref, cw_ref, gid_ref, stage_ref, cnt_ref, sem_ref):
    parts = [_merge_rows(alpha, slice(p * MERGE_ROWS, (p + 1) * MERGE_ROWS), xc_ref, xl_ref, mod_ref,
                         of_ref, ob_ref, sag_ref, ng_ref, zu_ref, zv_ref, ws_ref, bs_ref, cc_ref,
                         cl_ref, gate_ref, wb_ref, wo_ref, g1_ref, b1_ref, rwh_ref, rwl_ref, rb_ref,
                         x1_ref, h2_ref, cw_ref, gid_ref) for p in range(TM // MERGE_ROWS)]
    for step in range(MERGE_STAGES + len(parts) - 1):
        for lag, part in enumerate(parts):
            if 0 <= step - lag < MERGE_STAGES:
                next(part)
    _dispatch(h2_ref, cw_ref, gid_ref, utri_ref, slot_ref, meta_ref, hs_ref, stage_ref, cnt_ref, sem_ref)


def _aligned(row):
    return row if isinstance(row, int) else pl.multiple_of(row, HS_ALIGN)


def _window_copy(stage_ref, buf, src_row, hs_ref, dst_row, sem_ref, rows):
    return pltpu.make_async_copy(stage_ref.at[buf, pl.ds(_aligned(src_row), rows), :],
                                 hs_ref.at[pl.ds(_aligned(dst_row), rows), :], sem_ref.at[0])


def _by_window(small, fn):
    pl.when(small)(lambda: fn(SMALL_WIN))
    pl.when(jnp.logical_not(small))(lambda: fn(TM))


def _dispatch(h2_ref, cw_ref, gid_ref, utri_ref, slot_ref, meta_ref, hs_ref, stage_ref, cnt_ref, sem_ref):
    i = pl.program_id(0)

    @pl.when(i == 0)
    def _():
        for g in range(N_GROUPS + 1):
            cnt_ref[g] = 0

    gid = gid_ref[0:1, :]
    grp_i = lax.broadcasted_iota(jnp.int32, (8, TM), 0)
    onehot = jnp.where(grp_i.astype(F32) == gid, 1.0, 0.0)
    rank = _dot(onehot.astype(BF16), utri_ref[...])
    ncol = (rank[:, TM - 1:TM] + onehot[:, TM - 1:TM]).astype(jnp.int32)
    padc = ((ncol + (HS_ALIGN - 1)) // HS_ALIGN) * HS_ALIGN
    row8 = lax.broadcasted_iota(jnp.int32, (8, 1), 0)
    startc = jnp.zeros((8, 1), jnp.int32)
    run = jnp.zeros((1, 1), jnp.int32)
    for g in range(1, N_GROUPS):
        run = run + padc[g - 1:g, :]
        startc = jnp.where(row8 == g, run, startc)
    slot = jnp.sum(onehot * (rank + startc.astype(F32)), axis=0, keepdims=True)
    back = jnp.sum(onehot * (rank + (row8 * TM).astype(F32)), axis=0, keepdims=True)
    slot_ref[...] = jnp.broadcast_to(back, (LANES, TM)).T
    sel = jnp.where(lax.broadcasted_iota(jnp.int32, (STAGE_ROWS, TM), 0).astype(F32) == slot, 1.0, 0.0)
    cw = cw_ref[...]
    cw_hi = cw.astype(BF16)
    r1 = cw - cw_hi.astype(F32)
    cw_mid = r1.astype(BF16)
    cw_lo = (r1 - cw_mid.astype(F32)).astype(BF16)
    rowdata = jnp.concatenate([h2_ref[...], cw_hi, cw_mid, cw_lo], axis=1)
    buf = i % 2
    stage_ref[buf] = _dot(sel.astype(BF16), rowdata).astype(BF16)

    small = jnp.max(ncol, axis=0, keepdims=True)[0, 0] <= SMALL_WIN
    prev_small = cnt_ref[N_GROUPS] == 1

    def wait_windows(rows):
        for _ in range(N_GROUPS):
            _window_copy(stage_ref, buf, 0, hs_ref, 0, sem_ref, rows).wait()

    @pl.when(i > 0)
    def _():
        _by_window(prev_small, wait_windows)

    lane = lax.broadcasted_iota(jnp.int32, (8, LANES), 1)
    meta = jnp.where(lane == 2 * N_GROUPS, small.astype(jnp.int32), jnp.zeros((8, LANES), jnp.int32))
    starts, dsts = [], []
    for g in range(N_GROUPS):
        starts.append(startc[g, 0])
        dsts.append(g * HS_CAP + cnt_ref[g])
        cnt_ref[g] = cnt_ref[g] + padc[g, 0]
        meta = jnp.where(lane == g, dsts[g], meta)
        meta = jnp.where(lane == N_GROUPS + g, cnt_ref[g], meta)
    meta_ref[0] = meta
    cnt_ref[N_GROUPS] = small.astype(jnp.int32)

    def start_windows(rows):
        for g in range(N_GROUPS):
            _window_copy(stage_ref, buf, starts[g], hs_ref, dsts[g], sem_ref, rows).start()

    _by_window(small, start_windows)

    @pl.when(i == NB - 1)
    def _():
        _by_window(small, wait_windows)
        for g in range(N_GROUPS):
            for w in range(2):
                _window_copy(stage_ref, buf, STAGE_ROWS - TM, hs_ref, g * HS_CAP + cnt_ref[g] + w * TM,
                             sem_ref, TM).start()
        for _ in range(2 * N_GROUPS):
            _window_copy(stage_ref, buf, 0, hs_ref, 0, sem_ref, TM).wait()


def _merge_rows(alpha, rows, xc_ref, xl_ref, mod_ref, of_ref, ob_ref, sag_ref, ng_ref, zu_ref, zv_ref,
                ws_ref, bs_ref, cc_ref, cl_ref, gate_ref, wb_ref, wo_ref, g1_ref, b1_ref,
                rwh_ref, rwl_ref, rb_ref, x1_ref, h2_ref, cw_ref, gid_ref):
    i = pl.program_id(0)
    m = mod_ref[0]
    n_rows = rows.stop - rows.start
    o = of_ref[rows, :] + ob_ref[rows, :]
    parts = []
    for h in range(GLA_HEADS):
        oh = o[:, h * GLA_DV:(h + 1) * GLA_DV]
        parts.append(oh * lax.rsqrt(jnp.mean(oh * oh, axis=-1, keepdims=True) + LN_EPS))
    br_a = (jnp.concatenate(parts, axis=1) * ng_ref[...] * sag_ref[rows, :]).astype(BF16)
    u = zu_ref[rows, :]
    v = zv_ref[rows, :]
    yield
    chunks = []
    for c in range(n_rows // GMLP_CHUNK):
        vc = v[c * GMLP_CHUNK:(c + 1) * GMLP_CHUNK]
        cols = [_dot(ws_ref[g], vc[:, g * GMLP_CH:(g + 1) * GMLP_CH]) for g in range(GMLP_GROUPS)]
        chunks.append(jnp.concatenate(cols, axis=1) + bs_ref[...])
    br_b = (u * jnp.concatenate(chunks, axis=0)).astype(BF16)
    br_ct = jnp.where(i < NB_CTX, cc_ref[:, rows], cl_ref[:, rows])
    projs = (_dot(br_a, wb_ref[0]), _dot(br_b, wb_ref[1]), _dot_tn(br_ct, wb_ref[2]))
    yield
    y = jnp.zeros((n_rows, D_MODEL), F32)
    for nbr, proj in enumerate(projs):
        y = y + gate_ref[rows, nbr * D_MODEL:(nbr + 1) * D_MODEL] * proj
    y = y.astype(BF16)
    yield
    y = _dot(y, wo_ref[...])
    yield
    x = jnp.where(i < NB_CTX, xc_ref[rows, :], xl_ref[rows, :])
    x1 = _layer_norm(alpha * x + m[:, 2 * D_MODEL:3 * D_MODEL] * y, g1_ref[...], b1_ref[...])
    x1_ref[rows, :] = x1
    h2 = x1 * (1.0 + m[:, 4 * D_MODEL:5 * D_MODEL]) + m[:, 3 * D_MODEL:4 * D_MODEL]
    h2_ref[rows, :] = h2.astype(BF16)
    h_hi, h_lo = _split(h2)
    logits = (_dot_nt(rwh_ref[...], h_hi) + _dot_nt(rwh_ref[...], h_lo) + _dot_nt(rwl_ref[...], h_hi)
              + rb_ref[:, :n_rows])[:ROUTER_ROWS]
    row_i = lax.broadcasted_iota(jnp.int32, (ROUTER_ROWS, n_rows), 0)
    row = row_i.astype(F32)
    row_group = (row_i >> 2).astype(F32)
    big = float(LANES)
    neg = -jnp.inf
    gl = jnp.where((row_i >= N_EXPERTS) & (row_i < N_EXPERTS + N_GROUPS), logits, neg)
    gmax = jnp.max(gl, axis=0, keepdims=True)
    gsum = jnp.sum(jnp.exp(gl - gmax), axis=0, keepdims=True)
    g_p = 1.0 / gsum
    g_i = jnp.min(jnp.where(gl == gmax, row, big), axis=0, keepdims=True) - float(N_EXPERTS)
    in_group = (row_i < N_EXPERTS) & (row_group == g_i)
    el = jnp.where(in_group, logits, neg)
    emax = jnp.max(el, axis=0, keepdims=True)
    ee = jnp.exp(el - emax)
    e_prob = ee / jnp.sum(ee, axis=0, keepdims=True)
    p1 = jnp.max(jnp.where(in_group, e_prob, neg), axis=0, keepdims=True)
    i1 = jnp.min(jnp.where(in_group & (e_prob == p1), row, big), axis=0, keepdims=True)
    rest = in_group & (row != i1)
    p2 = jnp.max(jnp.where(rest, e_prob, neg), axis=0, keepdims=True)
    i2 = jnp.min(jnp.where(rest & (e_prob == p2), row, big), axis=0, keepdims=True)
    tot = p1 + p2
    cw_t = (jnp.where(row == i1, g_p * p1 / tot, 0.0) + jnp.where(row == i2, g_p * p2 / tot, 0.0))
    cw_ref[rows, :] = jnp.concatenate([cw_t, jnp.zeros((LANES - ROUTER_ROWS, n_rows), F32)], axis=0).T
    gid_ref[0:1, rows] = g_i
    yield


def _merge(l, alpha, xc, xl, mod3, o_f, o_b, sag, norm_g, zu, zv, ws, bs_t, brc_ctx, brc_lat, gates,
           wb, wo, g1, b1, rw_hi, rw_lo, rb):
    utri = jnp.asarray(np.arange(TM)[:, None] < np.arange(TM)[None, :], dtype=BF16)
    return pl.pallas_call(
        functools.partial(_merge_kernel, alpha),
        grid=(NB,),
        in_specs=[_ctx_tok(D_MODEL), _lat_tok(D_MODEL), _mod_spec(l),
                  _tok(512), _tok(512), _tok(512), _layer_spec(l, (1, 512)),
                  _tok(512), _tok(512),
                  _layer_spec(l, (GMLP_GROUPS, GMLP_CHUNK, GMLP_CHUNK)), _layer_spec(l, (GMLP_CHUNK, 512)),
                  pl.BlockSpec((512, TM), lambda i: (0, jnp.minimum(i, NB_CTX - 1))),
                  pl.BlockSpec((512, TM), lambda i: (0, jnp.maximum(i - NB_CTX, 0))),
                  _tok(3072), _layer_spec(l, (N_BRANCH, BRANCH_W, D_MODEL)), _layer_spec(l, (D_MODEL, D_MODEL)),
                  _layer_spec(l, (1, D_MODEL)), _layer_spec(l, (1, D_MODEL)),
                  _layer_spec(l, (LANES, D_MODEL)), _layer_spec(l, (LANES, D_MODEL)), _layer_spec(l, (LANES, TM)),
                  pl.BlockSpec((TM, TM), lambda i: (0, 0))],
        out_specs=[_tok(D_MODEL), _tok(LANES), pl.BlockSpec((1, 8, LANES), lambda i: (i, 0, 0)),
                   pl.BlockSpec(memory_space=pl.ANY)],
        out_shape=[jax.ShapeDtypeStruct((TOKENS, D_MODEL), F32),
                   jax.ShapeDtypeStruct((TOKENS, LANES), F32),
                   jax.ShapeDtypeStruct((NB, 8, LANES), jnp.int32),
                   jax.ShapeDtypeStruct((N_GROUPS * HS_CAP, HS_W), BF16)],
        scratch_shapes=[pltpu.VMEM((TM, D_MODEL), BF16), pltpu.VMEM((TM, LANES), F32), pltpu.VMEM((8, TM), F32),
                        pltpu.VMEM((2, STAGE_ROWS, HS_W), BF16), pltpu.SMEM((N_GROUPS + 1,), jnp.int32),
                        pltpu.SemaphoreType.DMA((1,))],
        compiler_params=_cp(("arbitrary",)),
        name="merge_ln1_router",
    )(xc, xl, mod3, o_f, o_b, sag, norm_g, zu, zv, ws, bs_t, brc_ctx, brc_lat, gates,
      wb, wo, g1, b1, rw_hi, rw_lo, rb, utri)


def _experts_kernel(blk_ref, grp_ref, ntile_ref, hs_ref, wg32_ref, wu32_ref, wd32_ref, ys_ref,
                    wg_ref, wu_ref, wd_ref):
    s = pl.program_id(0)

    @pl.when(jnp.logical_or(s == 0, grp_ref[s] != grp_ref[jnp.maximum(s - 1, 0)]))
    def _():
        wg_ref[...] = wg32_ref[...].astype(BF16)
        wu_ref[...] = wu32_ref[...].astype(BF16)
        wd_ref[...] = wd32_ref[...].astype(BF16)

    @pl.when(s < ntile_ref[0])
    def _():
        g = grp_ref[s]
        x = hs_ref[:, :D_MODEL]
        cw = (hs_ref[:, D_MODEL:D_MODEL + LANES].astype(F32)
              + hs_ref[:, D_MODEL + LANES:D_MODEL + 2 * LANES].astype(F32)
              + hs_ref[:, D_MODEL + 2 * LANES:].astype(F32))
        lane = lax.broadcasted_iota(jnp.int32, (TM, LANES), 1)
        hid = []
        for k in range(EXPERTS_PER_GROUP):
            w_e = jnp.sum(jnp.where(lane == g * EXPERTS_PER_GROUP + k, cw, 0.0), axis=-1, keepdims=True)
            hid.append((_silu(_dot(x, wg_ref[k])) * _dot(x, wu_ref[k]) * w_e).astype(BF16))
        y = _dot(jnp.concatenate(hid, axis=1), wd_ref[...].reshape(EXPERTS_PER_GROUP * EXPERT_FF, D_MODEL))
        ys_ref[...] = y.astype(BF16)


def _experts(l, tile_blk, tile_grp, n_tiles, hs, wg, wu, wd):
    e = EXPERTS_PER_GROUP
    return pl.pallas_call(
        _experts_kernel,
        grid_spec=pltpu.PrefetchScalarGridSpec(
            num_scalar_prefetch=3,
            grid=(MOE_TILES,),
            in_specs=[pl.BlockSpec((TM, HS_W), lambda s, blk, grp, nt: (blk[s], 0)),
                      pl.BlockSpec((None, e, D_MODEL, EXPERT_FF), lambda s, blk, grp, nt: (l, grp[s], 0, 0)),
                      pl.BlockSpec((None, e, D_MODEL, EXPERT_FF), lambda s, blk, grp, nt: (l, grp[s], 0, 0)),
                      pl.BlockSpec((None, e, EXPERT_FF, D_MODEL), lambda s, blk, grp, nt: (l, grp[s], 0, 0))],
            out_specs=pl.BlockSpec((TM, D_MODEL), lambda s, blk, grp, nt: (blk[s], 0)),
            scratch_shapes=[pltpu.VMEM((e, D_MODEL, EXPERT_FF), BF16), pltpu.VMEM((e, D_MODEL, EXPERT_FF), BF16),
                            pltpu.VMEM((e, EXPERT_FF, D_MODEL), BF16)],
        ),
        out_shape=jax.ShapeDtypeStruct((N_GROUPS * HS_CAP, D_MODEL), BF16),
        compiler_params=_cp(("arbitrary",)),
        name="moe_experts",
    )(tile_blk, tile_grp, n_tiles, hs, wg, wu, wd)


def _tile_table(final_counts):
    nt = (final_counts + 2 * TM - 1) // TM
    ends = jnp.cumsum(nt)
    total = ends[-1]
    s = jnp.minimum(jnp.arange(MOE_TILES), total - 1)
    grp = jnp.sum((s[:, None] >= ends[None, :]).astype(jnp.int32), axis=1)
    first = ends - nt
    blk = grp * (HS_CAP // TM) + s - first[grp]
    return blk.astype(jnp.int32), grp.astype(jnp.int32), total.reshape(1).astype(jnp.int32)


def _window_fetch(ys_ref, offs_ref, blk, win_ref, buf, sem_ref, rows):
    return [pltpu.make_async_copy(
        ys_ref.at[pl.ds(pl.multiple_of(offs_ref[blk * N_GROUPS + g], HS_ALIGN), rows), :],
        win_ref.at[buf, pl.ds(g * rows, rows), :], sem_ref.at[buf]) for g in range(N_GROUPS)]


def _combine_kernel(alpha, offs_ref, small_ref, ys_ref, slot_ref, x1_ref, mod_ref, g2_ref, b2_ref,
                    oc_ref, ol_ref, win_ref, y_ref, sem_ref):
    i = pl.program_id(0)
    buf = i % 2

    def start(blk, to_buf):
        def go(rows):
            for cp in _window_fetch(ys_ref, offs_ref, blk, win_ref, to_buf, sem_ref, rows):
                cp.start()
        _by_window(small_ref[blk] == 1, go)

    pl.when(i == 0)(lambda: start(0, 0))
    pl.when(i + 1 < NB)(lambda: start(i + 1, 1 - buf))

    def gather(rows):
        for cp in _window_fetch(ys_ref, offs_ref, i, win_ref, buf, sem_ref, rows):
            cp.wait()
        back = slot_ref[...]
        packed = back - jnp.floor(back * (1.0 / TM)) * float(TM - rows)
        slot = jnp.concatenate([packed] * (N_GROUPS * rows // LANES), axis=1)
        lane = lax.broadcasted_iota(jnp.int32, (TM, N_GROUPS * rows), 1).astype(F32)
        sel = jnp.where(lane == slot, 1.0, 0.0).astype(BF16)
        y_ref[...] = _dot(sel, win_ref[buf, :N_GROUPS * rows, :])

    _by_window(small_ref[i] == 1, gather)
    m = mod_ref[0]
    out = _layer_norm(alpha * x1_ref[...] + m[:, 5 * D_MODEL:6 * D_MODEL] * y_ref[...],
                      g2_ref[...], b2_ref[...])

    @pl.when(i < NB_CTX)
    def _():
        oc_ref[...] = out

    @pl.when(i >= NB_CTX)
    def _():
        ol_ref[...] = out


def _combine(l, alpha, offs, small, ys, slot, x1, mod3, g2, b2):
    return pl.pallas_call(
        functools.partial(_combine_kernel, alpha),
        grid_spec=pltpu.PrefetchScalarGridSpec(
            num_scalar_prefetch=2,
            grid=(NB,),
            in_specs=[pl.BlockSpec(memory_space=pl.ANY),
                      pl.BlockSpec((TM, LANES), lambda i, *_: (i, 0)),
                      pl.BlockSpec((TM, D_MODEL), lambda i, *_: (i, 0)),
                      _mod_spec(l),
                      pl.BlockSpec((None, 1, D_MODEL), lambda i, *_: (l, 0, 0)),
                      pl.BlockSpec((None, 1, D_MODEL), lambda i, *_: (l, 0, 0))],
            out_specs=[pl.BlockSpec((TM, D_MODEL), lambda i, *_: (jnp.minimum(i, NB_CTX - 1), 0)),
                       pl.BlockSpec((TM, D_MODEL), lambda i, *_: (jnp.maximum(i - NB_CTX, 0), 0))],
            scratch_shapes=[pltpu.VMEM((2, N_GROUPS * TM, D_MODEL), BF16), pltpu.VMEM((TM, D_MODEL), F32),
                            pltpu.SemaphoreType.DMA((2,))],
        ),
        out_shape=[jax.ShapeDtypeStruct((CTX_TOKENS, D_MODEL), F32),
                   jax.ShapeDtypeStruct((LAT_TOKENS, D_MODEL), F32)],
        compiler_params=_cp(("arbitrary",)),
        name="combine_ln2",
    )(offs, small, ys, slot, x1, mod3, g2, b2)


def _rope_tables():
    f32 = np.float32
    pos = np.arange(DEC_SEQ)
    row = (pos // GRID_W).astype(f32)
    col = (pos % GRID_W).astype(f32)
    quarter = ATTN_HEAD_DIM // 4
    inv_freq = np.power(f32(ROPE_BASE), -np.arange(quarter, dtype=f32) / f32(quarter)).astype(f32)
    j = np.arange(ATTN_HEAD_DIM)
    p = np.where((j // (2 * quarter))[None, :] == 0, row[:, None], col[:, None]).astype(f32)
    ang = (p * inv_freq[j % quarter][None, :]).astype(f32)
    sign = np.where((j % (2 * quarter)) < quarter, -1.0, 1.0).astype(f32)
    cos = np.tile(np.cos(ang).astype(f32), (1, ATTN_HEADS))
    sin = np.tile((np.sin(ang) * sign[None, :]).astype(f32), (1, ATTN_HEADS))
    cos = np.concatenate([np.ones((TM, 512), f32), cos], axis=0)
    sin = np.concatenate([np.zeros((TM, 512), f32), sin], axis=0)
    return jnp.asarray(cos), jnp.asarray(sin)


def _dup_heads(t):
    h0, h1 = t[..., :ATTN_HEAD_DIM], t[..., ATTN_HEAD_DIM:]
    return jnp.concatenate([h0, h0, h1, h1], axis=-1)


def _split_w_in(w):
    gate_scale = np.where(np.arange(IN_COLS - WB_START) >= WB_MG.start, 0.5, 1.0).astype(np.float32)
    return [w[..., :WA_COLS].astype(BF16), (w[..., WB_START:] * gate_scale).astype(BF16)]


def kernel(x_prompt, x_sample, state_gla, cache_k, cache_v, c, c_ctx, ada_w, ada_b, w_in, gla_gate_w, gla_gate_b, gla_norm_g, gmlp_ln_g, gmlp_ln_b, gmlp_ws, gmlp_bs, attn_sink, w_branch, w_out, ln1_g, ln1_b, ln2_g, ln2_b, router_group_w, router_group_b, router_expert_w, router_expert_b, expert_w_gate, expert_w_up, expert_w_down):
    alpha = (2.0 * DEPTH) ** 0.25
    xc = x_prompt.reshape(CTX_TOKENS, D_MODEL)
    xl = x_sample.reshape(LAT_TOKENS, D_MODEL)
    cond = jnp.concatenate([c_ctx[None, :], c, jnp.zeros((MOD_ROWS - 1 - DEC_BATCH, D_MODEL), F32)], axis=0)
    cos_t, sin_t = _rope_tables()

    ada_b3 = ada_b[:, None, :]
    w_parts = _split_w_in(w_in)
    zrow = jnp.zeros((DEPTH, GLA_RANK, 256), F32)
    zpad = jnp.zeros((DEPTH, LANES - 2 * GLA_RANK, 256), F32)
    gw_p = jnp.stack([jnp.concatenate([gla_gate_w[:, 0], zrow, zpad], axis=1),
                      jnp.concatenate([zrow, gla_gate_w[:, 1], zpad], axis=1)], axis=1).astype(BF16)
    gb_p = gla_gate_b[:, :, None, :]
    sink_t = jnp.broadcast_to(attn_sink[:, :, None, None], (DEPTH, ATTN_HEADS, 1, LANES))
    kc = _dup_heads(cache_k.reshape(DEC_BATCH, DEPTH, PAST_LEN, 128)).astype(BF16)
    vct = jnp.swapaxes(cache_v.reshape(DEC_BATCH, DEPTH, PAST_LEN, 128), 2, 3).astype(BF16)
    rw = jnp.swapaxes(jnp.concatenate(
        [router_expert_w, router_group_w,
         jnp.zeros((DEPTH, D_MODEL, LANES - N_EXPERTS - N_GROUPS), F32)], axis=2), 1, 2)
    rw_hi = rw.astype(BF16)
    rw_lo = (rw - rw_hi.astype(F32)).astype(BF16)
    rb = jnp.concatenate([router_expert_b, router_group_b,
                          jnp.zeros((DEPTH, LANES - N_EXPERTS - N_GROUPS), F32)], axis=1)
    rb = jnp.broadcast_to(rb[:, :, None], (DEPTH, LANES, TM))
    bs_t = jnp.repeat(jnp.swapaxes(gmlp_bs, 1, 2), GMLP_CH, axis=2)
    ws_b = gmlp_ws.astype(BF16)
    wb_b = (0.5 * w_branch).astype(BF16)
    wo_b = w_out.astype(BF16)
    row = lambda t: t[:, None, :]

    mod3 = _modulation(cond, ada_w, ada_b3).reshape(DEPTH, MOD_ROWS, 1, 6 * D_MODEL)
    states, keys, values = [], [], []
    for l in range(DEPTH):
        (qef, kef, klf, qeb, keb, klb, dec, av, avt, sag, zu, zv, cq, ckd, cvt, ck, cv, gates) = _in_projection(
            l, xc, xl, mod3, cos_t, sin_t, gw_p, gb_p, row(gmlp_ln_g), row(gmlp_ln_b), w_parts)
        o_f, o_b, s_fin = _gla(l, qef, kef, klf, qeb, keb, klb, dec, av, avt, state_gla)
        brc_ctx = _context_attention(l, cq, ckd, cvt, sink_t)
        brc_lat = _latent_attention(l, cq, ckd, cvt, kc, vct, sink_t)
        x1, slot, meta, hs = _merge(l, alpha, xc, xl, mod3, o_f, o_b, sag, row(gla_norm_g), zu, zv,
                                    ws_b, bs_t, brc_ctx, brc_lat, gates, wb_b,
                                    wo_b, row(ln1_g), row(ln1_b), rw_hi, rw_lo, rb)
        offs = meta[:, 0, :N_GROUPS].reshape(NB * N_GROUPS)
        tile_blk, tile_grp, n_tiles = _tile_table(meta[NB - 1, 0, N_GROUPS:2 * N_GROUPS])
        ys = _experts(l, tile_blk, tile_grp, n_tiles, hs, expert_w_gate, expert_w_up, expert_w_down)
        xc, xl = _combine(l, alpha, offs, meta[:, 0, 2 * N_GROUPS], ys, slot, x1, mod3,
                          row(ln2_g), row(ln2_b))
        states.append(s_fin)
        keys.append(ck.reshape(BATCH, SEQ, ATTN_KV_HEADS, ATTN_HEAD_DIM))
        values.append(cv.reshape(BATCH, SEQ, ATTN_KV_HEADS, ATTN_HEAD_DIM))
    return (xc.reshape(BATCH, SEQ, D_MODEL), xl.reshape(DEC_BATCH, DEC_SEQ, D_MODEL),
            jnp.stack(states, axis=1), jnp.stack(keys, axis=1), jnp.stack(values, axis=1))
```

```python
import functools

import jax
import jax.numpy as jnp
import numpy as np
from jax import lax
from jax.experimental import pallas as pl
from jax.experimental.pallas import tpu as pltpu

F32 = jnp.float32
BF16 = jnp.bfloat16

D_MODEL = 1024
BATCH = 32
SEQ = 256
DEPTH = 2
DEC_BATCH = 8
DEC_SEQ = 1024
PAST_LEN = 512
GRID_W = 64
GLA_HEADS = 4
GLA_DV = 128
GLA_DK = 64
GLA_RANK = 16
GLA_TAU = 16.0
GLA_CHUNK = 32
GMLP_CHUNK = 128
GMLP_CH = 128
GMLP_GROUPS = 4
ATTN_HEADS = 8
ATTN_KV_HEADS = 2
ATTN_GROUP = ATTN_HEADS // ATTN_KV_HEADS
ATTN_HEAD_DIM = 64
WINDOW = 128
ATTN_BLOCK = 128
ROPE_BASE = 10000.0
BRANCH_W = 512
N_BRANCH = 3
N_GROUPS = 4
EXPERTS_PER_GROUP = 4
N_EXPERTS = 16
EXPERT_FF = 256
LN_EPS = 1e-5

LANES = 128
TM = 256
CTX_TOKENS = BATCH * SEQ
LAT_TOKENS = DEC_BATCH * DEC_SEQ
TOKENS = CTX_TOKENS + LAT_TOKENS
NB_CTX = CTX_TOKENS // TM
NB_LAT = LAT_TOKENS // TM
NB = NB_CTX + NB_LAT
LAT_BLOCKS_PER_SEQ = DEC_SEQ // TM
MOD_ROWS = 16
GLA_SUB = 128
CHUNKS_PER_SUB = GLA_SUB // GLA_CHUNK
MERGE_ROWS = 128
MERGE_STAGES = 5
ROUTER_ROWS = 24
HS_ALIGN = 16
SMALL_WIN = 128
HS_W = D_MODEL + 3 * LANES
STAGE_ROWS = TM + N_GROUPS * HS_ALIGN + TM
_MAX_PAD = NB * (HS_ALIGN - 1)
HS_CAP = -(-(TOKENS + _MAX_PAD + 2 * TM) // TM) * TM
MOE_TILES = (TOKENS + N_GROUPS * _MAX_PAD) // TM + 2 * N_GROUPS

IN_COLS = 6432
WA_COLS = 1664
WA_AQ, WA_AK, WA_AV, WA_AG = slice(0, 256), slice(256, 512), slice(512, 1024), slice(1024, 1536)
WA_LR = slice(1536, 1664)
WB_START = 1568
WB_BZ, WB_CQ, WB_CK, WB_CV = slice(0, 1024), slice(1024, 1536), slice(1536, 1664), slice(1664, 1792)
WB_MG = slice(1792, 4864)

VMEM_LIMIT = 56 * 1024 * 1024


def _cp(sem):
    return pltpu.CompilerParams(dimension_semantics=sem, vmem_limit_bytes=VMEM_LIMIT)


def _dot(a, b):
    return jnp.dot(a, b, preferred_element_type=F32)


def _dot_nt(a, b):
    return lax.dot_general(a, b, (((1,), (1,)), ((), ())), preferred_element_type=F32)


def _dot_tn(a, b):
    return lax.dot_general(a, b, (((0,), (0,)), ((), ())), preferred_element_type=F32)


def _split(x):
    hi = x.astype(BF16)
    lo = (x - hi.astype(F32)).astype(BF16)
    return hi, lo


def _sigmoid(x):
    return 0.5 * (jnp.tanh(0.5 * x) + 1.0)


def _silu(x):
    return x * _sigmoid(x)


def _layer_norm(x, g, b):
    mu = jnp.mean(x, axis=-1, keepdims=True)
    xc = x - mu
    var = jnp.mean(xc * xc, axis=-1, keepdims=True)
    return xc * lax.rsqrt(var + LN_EPS) * g + b


def _mod_block(i):
    return jnp.where(i < NB_CTX, 0, 1 + (i - NB_CTX) // LAT_BLOCKS_PER_SEQ)


def _tok(width):
    return pl.BlockSpec((TM, width), lambda i: (i, 0))


def _ctx_tok(width):
    return pl.BlockSpec((TM, width), lambda i: (jnp.minimum(i, NB_CTX - 1), 0))


def _lat_tok(width):
    return pl.BlockSpec((TM, width), lambda i: (jnp.maximum(i - NB_CTX, 0), 0))


def _layer_spec(l, shape):
    return pl.BlockSpec((None,) + shape, lambda *_: (l,) + (0,) * len(shape))


def _mod_spec(l):
    return pl.BlockSpec((None, 1, 1, 6 * D_MODEL), lambda i, *_: (l, _mod_block(i), 0, 0))


def _mod_kernel(cond_ref, w_ref, b_ref, o_ref):
    s_hi, s_lo = _split(_silu(cond_ref[...]))
    w_hi, w_lo = _split(w_ref[...])
    o_ref[...] = _dot(s_hi, w_hi) + _dot(s_lo, w_hi) + _dot(s_hi, w_lo) + b_ref[...]


def _modulation(cond, ada_w, ada_b3):
    tn = 1536
    return pl.pallas_call(
        _mod_kernel,
        grid=(DEPTH, 6 * D_MODEL // tn),
        in_specs=[pl.BlockSpec((MOD_ROWS, D_MODEL), lambda l, j: (0, 0)),
                  pl.BlockSpec((None, D_MODEL, tn), lambda l, j: (l, 0, j)),
                  pl.BlockSpec((None, 1, tn), lambda l, j: (l, 0, j))],
        out_specs=pl.BlockSpec((None, MOD_ROWS, tn), lambda l, j: (l, 0, j)),
        out_shape=jax.ShapeDtypeStruct((DEPTH, MOD_ROWS, 6 * D_MODEL), F32),
        compiler_params=_cp(("arbitrary", "arbitrary")),
        name="modulation",
    )(cond, ada_w, ada_b3)


def _rope(x, cos, sin):
    n = x.shape[1]
    lane = lax.broadcasted_iota(jnp.int32, x.shape, 1)
    partner = jnp.where((lane & 31) < 16, pltpu.roll(x, n - 16, 1), pltpu.roll(x, 16, 1))
    return x * cos + partner * sin


def _inproj_kernel(xc_ref, xl_ref, mod_ref, cos_ref, sin_ref, gw_ref, gb_ref, lg_ref, lb_ref,
                   wa_ref, wb_ref,
                   qef_ref, kef_ref, klf_ref, qeb_ref, keb_ref, klb_ref, dec_ref,
                   av_ref, avt_ref, sag_ref, zu_ref, zv_ref,
                   cq_ref, ckd_ref, cvt_ref, ck_ref, cv_ref, gate_ref):
    i = pl.program_id(0)
    m = mod_ref[0]
    x = jnp.where(i < NB_CTX, xc_ref[...], xl_ref[...])
    h = (x * (1.0 + m[:, D_MODEL:2 * D_MODEL]) + m[:, 0:D_MODEL]).astype(BF16)

    aq = _dot(h, wa_ref[:, WA_AQ])
    ak = _dot(h, wa_ref[:, WA_AK])
    lr = _dot(h, wa_ref[:, WA_LR]).astype(BF16)
    zs = [_dot(lr, gw_ref[d]) + gb_ref[d] for d in range(2)]
    for nbr in range(N_BRANCH):
        cols = slice(WB_MG.start + nbr * D_MODEL, WB_MG.start + (nbr + 1) * D_MODEL)
        gate_ref[:, nbr * D_MODEL:(nbr + 1) * D_MODEL] = jnp.tanh(_dot(h, wb_ref[:, cols])) + 1.0
    r = lax.broadcasted_iota(jnp.int32, (TM, TM), 0)
    c = lax.broadcasted_iota(jnp.int32, (TM, TM), 1)
    same = (r >> 5) == (c >> 5)
    bs = []
    for d in range(2):
        g = (jnp.minimum(zs[d], 0.0) - jnp.log(1.0 + jnp.exp(-jnp.abs(zs[d])))) / GLA_TAU
        tri_b = jnp.where(same & ((c <= r) if d == 0 else (c >= r)), 1.0, 0.0).astype(BF16)
        g_hi, g_lo = _split(g)
        bs.append(_dot(tri_b, g_hi) + _dot(tri_b, g_lo))
    z = _gelu(_dot(h, wb_ref[:, WB_BZ]))
    zu_ref[...] = z[:, :BRANCH_W]
    zv_ref[...] = _layer_norm(z[:, BRANCH_W:], lg_ref[...], lb_ref[...]).astype(BF16)
    sag_ref[...] = _silu(_dot(h, wa_ref[:, WA_AG]))
    av = _dot(h, wa_ref[:, WA_AV])
    av_ref[...] = av.astype(BF16)
    avt_ref[...] = av.T.astype(BF16)
    decs = []
    for d, (qe_ref, ke_ref, kl_ref) in enumerate(((qef_ref, kef_ref, klf_ref), (qeb_ref, keb_ref, klb_ref))):
        b = bs[d]
        last = GLA_CHUNK - 1 if d == 0 else 0
        tot = [b[n * GLA_CHUNK + last:n * GLA_CHUNK + last + 1, :] for n in range(TM // GLA_CHUNK)]
        bl = jnp.concatenate([jnp.broadcast_to(t, (GLA_CHUNK, t.shape[1])) for t in tot], axis=0)
        qe_ref[...] = (aq * jnp.exp(b) * (GLA_DK ** -0.5)).astype(BF16)
        ke_ref[...] = (ak * jnp.exp(-b)).astype(BF16)
        kl_ref[...] = (ak * jnp.exp(bl - b)).astype(BF16)
        decs.append(jnp.exp(jnp.concatenate(tot, axis=0)))
    dec_ref[...] = jnp.concatenate(decs, axis=1)

    cos = cos_ref[...]
    sin = sin_ref[...]
    cq_ref[...] = (_rope(_dot(h, wb_ref[:, WB_CQ]), cos, sin) * (ATTN_HEAD_DIM ** -0.5)).astype(BF16)
    ck = _dot(h, wb_ref[:, WB_CK])
    cv = _dot(h, wb_ref[:, WB_CV])
    kr = _rope(ck, cos[:, :LANES], sin[:, :LANES])
    swapped = pltpu.roll(kr, ATTN_HEAD_DIM, 1)
    lane = lax.broadcasted_iota(jnp.int32, (TM, LANES), 1)
    ckd_ref[...] = jnp.concatenate([jnp.where(lane < 64, kr, swapped), jnp.where(lane < 64, swapped, kr)],
                                   axis=1).astype(BF16)
    cvt_ref[...] = cv.T.astype(BF16)

    @pl.when(i < NB_CTX)
    def _():
        ck_ref[...] = ck
        cv_ref[...] = cv


def _rope_block(i):
    return jnp.where(i < NB_CTX, 0, 1 + (i - NB_CTX) % LAT_BLOCKS_PER_SEQ)


def _in_projection(l, xc, xl, mod3, cos_t, sin_t, gw_p, gb_p, ln_g, ln_b, w_parts):
    def tok_out(width, dt):
        return _tok(width), jax.ShapeDtypeStruct((TOKENS, width), dt)

    def feat_out(width):
        return (pl.BlockSpec((width, TM), lambda i: (0, i)), jax.ShapeDtypeStruct((width, TOKENS), BF16))

    def ctx_out(width):
        return _ctx_tok(width), jax.ShapeDtypeStruct((CTX_TOKENS, width), F32)

    dec_out = (pl.BlockSpec((TM // GLA_CHUNK, 512), lambda i: (i, 0)),
               jax.ShapeDtypeStruct((TOKENS // GLA_CHUNK, 512), F32))
    outs = [tok_out(256, BF16)] * 6 + [dec_out] + [
        tok_out(512, BF16), feat_out(512), tok_out(512, F32), tok_out(512, F32), tok_out(512, BF16),
        tok_out(512, BF16), tok_out(256, BF16), feat_out(LANES), ctx_out(LANES), ctx_out(LANES),
        tok_out(3072, F32)]
    return pl.pallas_call(
        _inproj_kernel,
        grid=(NB,),
        in_specs=[_ctx_tok(D_MODEL), _lat_tok(D_MODEL), _mod_spec(l),
                  pl.BlockSpec((TM, 512), lambda i: (_rope_block(i), 0)),
                  pl.BlockSpec((TM, 512), lambda i: (_rope_block(i), 0)),
                  _layer_spec(l, (2, LANES, 256)), _layer_spec(l, (2, 1, 256)),
                  _layer_spec(l, (1, 512)), _layer_spec(l, (1, 512))]
                 + [_layer_spec(l, (D_MODEL, w.shape[-1])) for w in w_parts],
        out_specs=[o[0] for o in outs],
        out_shape=[o[1] for o in outs],
        compiler_params=_cp(("arbitrary",)),
        name="in_projection",
    )(xc, xl, mod3, cos_t, sin_t, gw_p, gb_p, ln_g, ln_b, *w_parts)


def _gla_direction(qe, ke, kl, v, vt, dec, masks, state_ref, reverse):
    tri, half_masks, chunk_mask, head_diag = masks
    tri = tri[int(reverse)]
    outs = []
    for p in range(GLA_HEADS // 2):
        ps = slice(p * LANES, (p + 1) * LANES)
        qp = qe[:, ps]
        kep = ke[:, ps]
        klp = kl[:, ps]
        vp = v[:, p * 2 * GLA_DV:(p + 1) * 2 * GLA_DV]
        vtp = vt[p * 2 * GLA_DV:(p + 1) * 2 * GLA_DV, :]
        halves = []
        for hh in range(2):
            s = jnp.where(tri, _dot_nt(qp * half_masks[hh], kep), 0.0).astype(BF16)
            halves.append(_dot(s, vp[:, hh * GLA_DV:(hh + 1) * GLA_DV]))
        k_exp = jnp.concatenate([klp] * CHUNKS_PER_SUB, axis=1) * chunk_mask
        q_exp = jnp.concatenate([qp] * CHUNKS_PER_SUB, axis=1) * chunk_mask
        kvt = _dot(vtp, k_exp)
        st = state_ref[p]
        entering = [None] * CHUNKS_PER_SUB
        order = range(CHUNKS_PER_SUB - 1, -1, -1) if reverse else range(CHUNKS_PER_SUB)
        for n in order:
            entering[n] = st
            st = dec[n:n + 1, ps] * st + jnp.where(head_diag, kvt[:, n * LANES:(n + 1) * LANES], 0.0)
        state_ref[p] = st
        st_stack = jnp.concatenate(entering, axis=1).astype(BF16)
        inter = _dot_nt(q_exp, st_stack)
        outs.append(halves[0] + inter[:, :GLA_DV])
        outs.append(halves[1] + inter[:, GLA_DV:])
    return jnp.concatenate(outs, axis=1)


def _gla_mask_constants():
    r = np.arange(GLA_SUB)[:, None]
    c = np.arange(GLA_SUB)[None, :]
    same = (r // GLA_CHUNK) == (c // GLA_CHUNK)
    tri = np.stack([same & (c <= r), same & (c >= r)]).astype(np.float32)
    lane = np.arange(LANES)[None, :]
    halves = np.stack([np.broadcast_to(lane < 64, (GLA_SUB, LANES)),
                       np.broadcast_to(lane >= 64, (GLA_SUB, LANES))]).astype(np.float32)
    col_chunk = np.arange(CHUNKS_PER_SUB * LANES)[None, :] // LANES
    chunk_mask = ((r // GLA_CHUNK) == col_chunk).astype(np.float32)
    head_diag = ((np.arange(2 * GLA_DV)[:, None] // GLA_DV) == (lane // GLA_DK)).astype(np.float32)
    return (jnp.asarray(tri), jnp.asarray(halves, dtype=BF16), jnp.asarray(chunk_mask, dtype=BF16),
            jnp.asarray(head_diag))


def _gla_kernel(tri_ref, half_ref, cmask_ref, hdiag_ref,
                qf_ref, kef_ref, klf_ref, vf_ref, vtf_ref, decf_ref,
                qb_ref, keb_ref, klb_ref, vb_ref, vtb_ref, decb_ref,
                s0_ref, of_ref, ob_ref, sfin_ref, state_ref):
    i = pl.program_id(0)
    @pl.when(i < NB_CTX)
    def _():
        state_ref[...] = jnp.zeros_like(state_ref)

    @pl.when(jnp.logical_and(i >= NB_CTX, (i - NB_CTX) % LAT_BLOCKS_PER_SEQ == 0))
    def _():
        zero = jnp.zeros((GLA_DK, GLA_DV), F32)
        for d in range(2):
            for p in range(2):
                pair = jnp.concatenate(
                    [jnp.concatenate([s0_ref[0, d, 2 * p], zero], axis=1),
                     jnp.concatenate([zero, s0_ref[0, d, 2 * p + 1]], axis=1)], axis=0)
                state_ref[d, p] = pair.T

    masks = ((tri_ref[0] > 0.5, tri_ref[1] > 0.5), (half_ref[0], half_ref[1]), cmask_ref[...],
             hdiag_ref[...] > 0.5)
    n_sub = TM // GLA_SUB
    for j in range(n_sub):
        r0 = j * GLA_SUB
        rs = slice(r0, r0 + GLA_SUB)
        of_ref[rs, :] = _gla_direction(
            qf_ref[rs, :], kef_ref[rs, :], klf_ref[rs, :], vf_ref[rs, :], vtf_ref[:, rs],
            decf_ref[j * CHUNKS_PER_SUB:(j + 1) * CHUNKS_PER_SUB, :256], masks, state_ref.at[0], False)
        jb = n_sub - 1 - j
        rb = slice(jb * GLA_SUB, (jb + 1) * GLA_SUB)
        ob_ref[rb, :] = _gla_direction(
            qb_ref[rb, :], keb_ref[rb, :], klb_ref[rb, :], vb_ref[rb, :], vtb_ref[:, rb],
            decb_ref[jb * CHUNKS_PER_SUB:(jb + 1) * CHUNKS_PER_SUB, 256:], masks, state_ref.at[1], True)

    @pl.when(i < NB_CTX)
    def _():
        for d in range(2):
            for p in range(2):
                sp = state_ref[d, p].T
                sfin_ref[0, d, 2 * p] = sp[:GLA_DK, :GLA_DV]
                sfin_ref[0, d, 2 * p + 1] = sp[GLA_DK:, GLA_DV:]


def _bwd_block(i):
    k = (i - NB_CTX) % LAT_BLOCKS_PER_SEQ
    return jnp.where(i < NB_CTX, i, i - k + (LAT_BLOCKS_PER_SEQ - 1 - k))


def _gla(l, qef, kef, klf, qeb, keb, klb, dec, av, avt, state_gla):
    def bwd(width):
        return pl.BlockSpec((TM, width), lambda i: (_bwd_block(i), 0))

    def lat_seq(i):
        return jnp.maximum(i - NB_CTX, 0) // LAT_BLOCKS_PER_SEQ

    vt_fwd = pl.BlockSpec((512, TM), lambda i: (0, i))
    vt_bwd = pl.BlockSpec((512, TM), lambda i: (0, _bwd_block(i)))
    dec_fwd = pl.BlockSpec((TM // GLA_CHUNK, 512), lambda i: (i, 0))
    dec_bwd = pl.BlockSpec((TM // GLA_CHUNK, 512), lambda i: (_bwd_block(i), 0))
    state_dims = (2, GLA_HEADS, GLA_DK, GLA_DV)
    consts = _gla_mask_constants()
    return pl.pallas_call(
        _gla_kernel,
        grid=(NB,),
        in_specs=[pl.BlockSpec(t.shape, lambda i, nd=t.ndim: (0,) * nd) for t in consts]
                 + [_tok(256), _tok(256), _tok(256), _tok(512), vt_fwd, dec_fwd,
                  bwd(256), bwd(256), bwd(256), bwd(512), vt_bwd, dec_bwd,
                  pl.BlockSpec((1, None) + state_dims, lambda i: (lat_seq(i), l, 0, 0, 0, 0))],
        out_specs=[_tok(512), bwd(512),
                   pl.BlockSpec((1,) + state_dims, lambda i: (jnp.minimum(i, NB_CTX - 1), 0, 0, 0, 0))],
        out_shape=[jax.ShapeDtypeStruct((TOKENS, 512), F32), jax.ShapeDtypeStruct((TOKENS, 512), F32),
                   jax.ShapeDtypeStruct((BATCH,) + state_dims, F32)],
        scratch_shapes=[pltpu.VMEM((2, 2, 2 * GLA_DV, LANES), F32)],
        compiler_params=_cp(("arbitrary",)),
        name="gla_scan",
    )(*consts, qef, kef, klf, av, avt, dec, qeb, keb, klb, av, avt, dec, state_gla)


def _attend_group(q_refs, keys, values_t, masks, sink_ref, g, o_ref):
    rows = q_refs[0].shape[0]
    lane = lax.broadcasted_iota(jnp.int32, (rows, LANES), 1)
    stacked = []
    for q in q_refs:
        q32 = q.astype(F32)
        stacked.append(jnp.where(lane < 64, q32, 0.0).astype(BF16))
        stacked.append(jnp.where(lane >= 64, q32, 0.0).astype(BF16))
    qs = jnp.concatenate(stacked, axis=0)
    col_head = lax.broadcasted_iota(jnp.int32, (1, ATTN_GROUP * rows), 1) // rows
    sink = jnp.zeros((1, ATTN_GROUP * rows), F32)
    for hh in range(ATTN_GROUP):
        sink = jnp.where(col_head == hh, sink_ref[ATTN_GROUP * g + hh][:, :1], sink)
    ss = []
    m = sink
    for kk, mk in zip(keys, masks):
        s = _dot_nt(kk, qs)
        if mk is not None:
            s = jnp.where(jnp.concatenate([mk] * ATTN_GROUP, axis=1), s, -jnp.inf)
        ss.append(s)
        m = jnp.maximum(m, jnp.max(s, axis=0, keepdims=True))
    den = jnp.exp(sink - m)
    acc = jnp.zeros((ATTN_HEAD_DIM, ATTN_GROUP * rows), F32)
    for s, vt in zip(ss, values_t):
        e = jnp.exp(s - m)
        den = den + jnp.sum(e, axis=0, keepdims=True)
        acc = acc + _dot(vt, e.astype(BF16))
    o = (acc / den).astype(BF16)
    for hh in range(ATTN_GROUP):
        h = ATTN_GROUP * g + hh
        o_ref[h * ATTN_HEAD_DIM:(h + 1) * ATTN_HEAD_DIM, :] = o[:, hh * rows:(hh + 1) * rows]


def _ctx_attn_kernel(q_ref, kd_ref, vt_ref, sink_ref, o_ref):
    for g in range(ATTN_KV_HEADS):
        gl = slice(g * LANES, (g + 1) * LANES)
        q_refs = [q_ref[:, (2 * g + pp) * LANES:(2 * g + pp + 1) * LANES] for pp in range(2)]
        vt = vt_ref[g * ATTN_HEAD_DIM:(g + 1) * ATTN_HEAD_DIM, :]
        _attend_group(q_refs, [kd_ref[:, gl]], [vt], [None], sink_ref, g, o_ref)


def _context_attention(l, cq, ckd, cvt, sink_t):
    return pl.pallas_call(
        _ctx_attn_kernel,
        grid=(BATCH,),
        in_specs=[pl.BlockSpec((SEQ, 512), lambda b: (b, 0)),
                  pl.BlockSpec((SEQ, 256), lambda b: (b, 0)),
                  pl.BlockSpec((LANES, SEQ), lambda b: (0, b)),
                  _layer_spec(l, (ATTN_HEADS, 1, LANES))],
        out_specs=pl.BlockSpec((512, SEQ), lambda b: (0, b)),
        out_shape=jax.ShapeDtypeStruct((512, CTX_TOKENS), BF16),
        compiler_params=_cp(("arbitrary",)),
        name="context_attention",
    )(cq, ckd, cvt, sink_t)


def _lat_attn_kernel(q_ref, kd_ref, vt_ref, kc_ref, vct_ref, sink_ref, o_ref):
    n = pl.program_id(1)
    nq = DEC_SEQ // ATTN_BLOCK
    prev0 = pl.multiple_of(jnp.maximum(n - 1, 0) * ATTN_BLOCK, ATTN_BLOCK)
    cur0 = pl.multiple_of(n * ATTN_BLOCK, ATTN_BLOCK)
    next0 = pl.multiple_of(jnp.minimum(n + 1, nq - 1) * ATTN_BLOCK, ATTN_BLOCK)
    kj = lax.broadcasted_iota(jnp.int32, (ATTN_BLOCK, ATTN_BLOCK), 0)
    qi = lax.broadcasted_iota(jnp.int32, (ATTN_BLOCK, ATTN_BLOCK), 1)
    m_prev = jnp.logical_and(kj >= qi, n > 0)
    m_next = jnp.logical_and(kj <= qi, n < nq - 1)
    for g in range(ATTN_KV_HEADS):
        gl = slice(g * LANES, (g + 1) * LANES)
        gr = slice(g * ATTN_HEAD_DIM, (g + 1) * ATTN_HEAD_DIM)
        q_refs = [q_ref[:, (2 * g + pp) * LANES:(2 * g + pp + 1) * LANES] for pp in range(2)]
        keys = [kd_ref[pl.ds(prev0, ATTN_BLOCK), gl], kd_ref[pl.ds(cur0, ATTN_BLOCK), gl],
                kd_ref[pl.ds(next0, ATTN_BLOCK), gl], kc_ref[0, :, gl]]
        vals = [vt_ref[gr, pl.ds(prev0, ATTN_BLOCK)], vt_ref[gr, pl.ds(cur0, ATTN_BLOCK)],
                vt_ref[gr, pl.ds(next0, ATTN_BLOCK)], vct_ref[0, gr, :]]
        _attend_group(q_refs, keys, vals, [m_prev, None, m_next, None], sink_ref, g, o_ref)


def _latent_attention(l, cq, ckd, cvt, kc, vct, sink_t):
    nq = DEC_SEQ // ATTN_BLOCK
    q0 = CTX_TOKENS // ATTN_BLOCK
    s0 = CTX_TOKENS // DEC_SEQ
    return pl.pallas_call(
        _lat_attn_kernel,
        grid=(DEC_BATCH, nq),
        in_specs=[pl.BlockSpec((ATTN_BLOCK, 512), lambda b, n: (q0 + b * nq + n, 0)),
                  pl.BlockSpec((DEC_SEQ, 256), lambda b, n: (s0 + b, 0)),
                  pl.BlockSpec((LANES, DEC_SEQ), lambda b, n: (0, s0 + b)),
                  pl.BlockSpec((1, None, PAST_LEN, 256), lambda b, n: (b, l, 0, 0)),
                  pl.BlockSpec((1, None, LANES, PAST_LEN), lambda b, n: (b, l, 0, 0)),
                  _layer_spec(l, (ATTN_HEADS, 1, LANES))],
        out_specs=pl.BlockSpec((512, ATTN_BLOCK), lambda b, n: (0, b * nq + n)),
        out_shape=jax.ShapeDtypeStruct((512, LAT_TOKENS), BF16),
        compiler_params=_cp(("arbitrary", "arbitrary")),
        name="latent_attention",
    )(cq, ckd, cvt, kc, vct, sink_t)


def _gelu(x):
    return 0.5 * x * (1.0 + jnp.tanh(0.7978845608028654 * (x + 0.044715 * (x * x * x))))


def _merge_kernel(alpha, xc_ref, xl_ref, mod_ref, of_ref, ob_ref, sag_ref, ng_ref, zu_ref, zv_ref,
                  ws_ref, bs_ref, cc_ref, cl_ref, gate_ref, wb_ref, wo_ref, g1_ref, b1_ref,
                  rwh_ref, rwl_ref, rb_ref, utri_ref,
                  x1_ref, slot_ref, meta_ref, hs_ref,
                  h2_ref, cw_ref, gid_ref, stage_ref, cnt_ref, sem_ref):
    parts = [_merge_rows(alpha, slice(p * MERGE_ROWS, (p + 1) * MERGE_ROWS), xc_ref, xl_ref, mod_ref,
                         of_ref, ob_ref, sag_ref, ng_ref, zu_ref, zv_ref, ws_ref, bs_ref, cc_ref,
                         cl_ref, gate_ref, wb_ref, wo_ref, g1_ref, b1_ref, rwh_ref, rwl_ref, rb_ref,
                         x1_ref, h2_ref, cw_ref, gid_ref) for p in range(TM // MERGE_ROWS)]
    for step in range(MERGE_STAGES + len(parts) - 1):
        for lag, part in enumerate(parts):
            if 0 <= step - lag < MERGE_STAGES:
                next(part)
    _dispatch(h2_ref, cw_ref, gid_ref, utri_ref, slot_ref, meta_ref, hs_ref, stage_ref, cnt_ref, sem_ref)


def _aligned(row):
    return row if isinstance(row, int) else pl.multiple_of(row, HS_ALIGN)


def _window_copy(stage_ref, buf, src_row, hs_ref, dst_row, sem_ref, rows):
    return pltpu.make_async_copy(stage_ref.at[buf, pl.ds(_aligned(src_row), rows), :],
                                 hs_ref.at[pl.ds(_aligned(dst_row), rows), :], sem_ref.at[0])


def _by_window(small, fn):
    pl.when(small)(lambda: fn(SMALL_WIN))
    pl.when(jnp.logical_not(small))(lambda: fn(TM))


def _dispatch(h2_ref, cw_ref, gid_ref, utri_ref, slot_ref, meta_ref, hs_ref, stage_ref, cnt_ref, sem_ref):
    i = pl.program_id(0)

    @pl.when(i == 0)
    def _():
        for g in range(N_GROUPS + 1):
            cnt_ref[g] = 0

    gid = gid_ref[0:1, :]
    grp_i = lax.broadcasted_iota(jnp.int32, (8, TM), 0)
    onehot = jnp.where(grp_i.astype(F32) == gid, 1.0, 0.0)
    rank = _dot(onehot.astype(BF16), utri_ref[...])
    ncol = (rank[:, TM - 1:TM] + onehot[:, TM - 1:TM]).astype(jnp.int32)
    padc = ((ncol + (HS_ALIGN - 1)) // HS_ALIGN) * HS_ALIGN
    row8 = lax.broadcasted_iota(jnp.int32, (8, 1), 0)
    startc = jnp.zeros((8, 1), jnp.int32)
    run = jnp.zeros((1, 1), jnp.int32)
    for g in range(1, N_GROUPS):
        run = run + padc[g - 1:g, :]
        startc = jnp.where(row8 == g, run, startc)
    slot = jnp.sum(onehot * (rank + startc.astype(F32)), axis=0, keepdims=True)
    back = jnp.sum(onehot * (rank + (row8 * TM).astype(F32)), axis=0, keepdims=True)
    slot_ref[...] = jnp.broadcast_to(back, (LANES, TM)).T
    sel = jnp.where(lax.broadcasted_iota(jnp.int32, (STAGE_ROWS, TM), 0).astype(F32) == slot, 1.0, 0.0)
    cw = cw_ref[...]
    cw_hi = cw.astype(BF16)
    r1 = cw - cw_hi.astype(F32)
    cw_mid = r1.astype(BF16)
    cw_lo = (r1 - cw_mid.astype(F32)).astype(BF16)
    rowdata = jnp.concatenate([h2_ref[...], cw_hi, cw_mid, cw_lo], axis=1)
    buf = i % 2
    stage_ref[buf] = _dot(sel.astype(BF16), rowdata).astype(BF16)

    small = jnp.max(ncol, axis=0, keepdims=True)[0, 0] <= SMALL_WIN
    prev_small = cnt_ref[N_GROUPS] == 1

    def wait_windows(rows):
        for _ in range(N_GROUPS):
            _window_copy(stage_ref, buf, 0, hs_ref, 0, sem_ref, rows).wait()

    @pl.when(i > 0)
    def _():
        _by_window(prev_small, wait_windows)

    lane = lax.broadcasted_iota(jnp.int32, (8, LANES), 1)
    meta = jnp.where(lane == 2 * N_GROUPS, small.astype(jnp.int32), jnp.zeros((8, LANES), jnp.int32))
    starts, dsts = [], []
    for g in range(N_GROUPS):
        starts.append(startc[g, 0])
        dsts.append(g * HS_CAP + cnt_ref[g])
        cnt_ref[g] = cnt_ref[g] + padc[g, 0]
        meta = jnp.where(lane == g, dsts[g], meta)
        meta = jnp.where(lane == N_GROUPS + g, cnt_ref[g], meta)
    meta_ref[0] = meta
    cnt_ref[N_GROUPS] = small.astype(jnp.int32)

    def start_windows(rows):
        for g in range(N_GROUPS):
            _window_copy(stage_ref, buf, starts[g], hs_ref, dsts[g], sem_ref, rows).start()

    _by_window(small, start_windows)

    @pl.when(i == NB - 1)
    def _():
        _by_window(small, wait_windows)
        for g in range(N_GROUPS):
            for w in range(2):
                _window_copy(stage_ref, buf, STAGE_ROWS - TM, hs_ref, g * HS_CAP + cnt_ref[g] + w * TM,
                             sem_ref, TM).start()
        for _ in range(2 * N_GROUPS):
            _window_copy(stage_ref, buf, 0, hs_ref, 0, sem_ref, TM).wait()
        ends, total = [], 0
        for g in range(N_GROUPS):
            total = total + lax.shift_right_logical(cnt_ref[g] + (2 * TM - 1), TM.bit_length() - 1)
            ends.append(total)
        step = jnp.minimum(lax.broadcasted_iota(jnp.int32, (1, LANES), 1), total - 1)
        grp = jnp.zeros((1, LANES), jnp.int32)
        first = jnp.zeros((1, LANES), jnp.int32)
        for g in range(N_GROUPS - 1):
            grp = grp + jnp.where(step >= ends[g], 1, 0)
            first = jnp.where(step >= ends[g], ends[g], first)
        meta_ref[0, 1:2, :] = grp * (HS_CAP // TM) + step - first
        meta_ref[0, 2:3, :] = grp
        meta_ref[0, 3:4, :] = jnp.zeros((1, LANES), jnp.int32) + total


def _merge_rows(alpha, rows, xc_ref, xl_ref, mod_ref, of_ref, ob_ref, sag_ref, ng_ref, zu_ref, zv_ref,
                ws_ref, bs_ref, cc_ref, cl_ref, gate_ref, wb_ref, wo_ref, g1_ref, b1_ref,
                rwh_ref, rwl_ref, rb_ref, x1_ref, h2_ref, cw_ref, gid_ref):
    i = pl.program_id(0)
    m = mod_ref[0]
    n_rows = rows.stop - rows.start
    o = of_ref[rows, :] + ob_ref[rows, :]
    parts = []
    for h in range(GLA_HEADS):
        oh = o[:, h * GLA_DV:(h + 1) * GLA_DV]
        parts.append(oh * lax.rsqrt(jnp.mean(oh * oh, axis=-1, keepdims=True) + LN_EPS))
    br_a = (jnp.concatenate(parts, axis=1) * ng_ref[...] * sag_ref[rows, :]).astype(BF16)
    u = zu_ref[rows, :]
    v = zv_ref[rows, :]
    yield
    chunks = []
    for c in range(n_rows // GMLP_CHUNK):
        vc = v[c * GMLP_CHUNK:(c + 1) * GMLP_CHUNK]
        cols = [_dot(ws_ref[g], vc[:, g * GMLP_CH:(g + 1) * GMLP_CH]) for g in range(GMLP_GROUPS)]
        chunks.append(jnp.concatenate(cols, axis=1) + bs_ref[...])
    br_b = (u * jnp.concatenate(chunks, axis=0)).astype(BF16)
    br_ct = jnp.where(i < NB_CTX, cc_ref[:, rows], cl_ref[:, rows])
    projs = (_dot(br_a, wb_ref[0]), _dot(br_b, wb_ref[1]), _dot_tn(br_ct, wb_ref[2]))
    yield
    y = jnp.zeros((n_rows, D_MODEL), F32)
    for nbr, proj in enumerate(projs):
        y = y + gate_ref[rows, nbr * D_MODEL:(nbr + 1) * D_MODEL] * proj
    y = y.astype(BF16)
    yield
    y = _dot(y, wo_ref[...])
    yield
    x = jnp.where(i < NB_CTX, xc_ref[rows, :], xl_ref[rows, :])
    x1 = _layer_norm(alpha * x + m[:, 2 * D_MODEL:3 * D_MODEL] * y, g1_ref[...], b1_ref[...])
    x1_ref[rows, :] = x1
    h2 = x1 * (1.0 + m[:, 4 * D_MODEL:5 * D_MODEL]) + m[:, 3 * D_MODEL:4 * D_MODEL]
    h2_ref[rows, :] = h2.astype(BF16)
    h_hi, h_lo = _split(h2)
    logits = (_dot_nt(rwh_ref[...], h_hi) + _dot_nt(rwh_ref[...], h_lo) + _dot_nt(rwl_ref[...], h_hi)
              + rb_ref[:, :n_rows])[:ROUTER_ROWS]
    row_i = lax.broadcasted_iota(jnp.int32, (ROUTER_ROWS, n_rows), 0)
    row = row_i.astype(F32)
    row_group = (row_i >> 2).astype(F32)
    big = float(LANES)
    neg = -jnp.inf
    gl = jnp.where((row_i >= N_EXPERTS) & (row_i < N_EXPERTS + N_GROUPS), logits, neg)
    gmax = jnp.max(gl, axis=0, keepdims=True)
    gsum = jnp.sum(jnp.exp(gl - gmax), axis=0, keepdims=True)
    g_p = 1.0 / gsum
    g_i = jnp.min(jnp.where(gl == gmax, row, big), axis=0, keepdims=True) - float(N_EXPERTS)
    in_group = (row_i < N_EXPERTS) & (row_group == g_i)
    el = jnp.where(in_group, logits, neg)
    emax = jnp.max(el, axis=0, keepdims=True)
    ee = jnp.exp(el - emax)
    e_prob = ee / jnp.sum(ee, axis=0, keepdims=True)
    p1 = jnp.max(jnp.where(in_group, e_prob, neg), axis=0, keepdims=True)
    i1 = jnp.min(jnp.where(in_group & (e_prob == p1), row, big), axis=0, keepdims=True)
    rest = in_group & (row != i1)
    p2 = jnp.max(jnp.where(rest, e_prob, neg), axis=0, keepdims=True)
    i2 = jnp.min(jnp.where(rest & (e_prob == p2), row, big), axis=0, keepdims=True)
    tot = p1 + p2
    cw_t = (jnp.where(row == i1, g_p * p1 / tot, 0.0) + jnp.where(row == i2, g_p * p2 / tot, 0.0))
    cw_ref[rows, :] = jnp.concatenate([cw_t, jnp.zeros((LANES - ROUTER_ROWS, n_rows), F32)], axis=0).T
    gid_ref[0:1, rows] = g_i
    yield


def _merge(l, alpha, xc, xl, mod3, o_f, o_b, sag, norm_g, zu, zv, ws, bs_t, brc_ctx, brc_lat, gates,
           wb, wo, g1, b1, rw_hi, rw_lo, rb):
    utri = jnp.asarray(np.arange(TM)[:, None] < np.arange(TM)[None, :], dtype=BF16)
    return pl.pallas_call(
        functools.partial(_merge_kernel, alpha),
        grid=(NB,),
        in_specs=[_ctx_tok(D_MODEL), _lat_tok(D_MODEL), _mod_spec(l),
                  _tok(512), _tok(512), _tok(512), _layer_spec(l, (1, 512)),
                  _tok(512), _tok(512),
                  _layer_spec(l, (GMLP_GROUPS, GMLP_CHUNK, GMLP_CHUNK)), _layer_spec(l, (GMLP_CHUNK, 512)),
                  pl.BlockSpec((512, TM), lambda i: (0, jnp.minimum(i, NB_CTX - 1))),
                  pl.BlockSpec((512, TM), lambda i: (0, jnp.maximum(i - NB_CTX, 0))),
                  _tok(3072), _layer_spec(l, (N_BRANCH, BRANCH_W, D_MODEL)), _layer_spec(l, (D_MODEL, D_MODEL)),
                  _layer_spec(l, (1, D_MODEL)), _layer_spec(l, (1, D_MODEL)),
                  _layer_spec(l, (LANES, D_MODEL)), _layer_spec(l, (LANES, D_MODEL)), _layer_spec(l, (LANES, TM)),
                  pl.BlockSpec((TM, TM), lambda i: (0, 0))],
        out_specs=[_tok(D_MODEL), _tok(LANES), pl.BlockSpec((1, 8, LANES), lambda i: (i, 0, 0)),
                   pl.BlockSpec(memory_space=pl.ANY)],
        out_shape=[jax.ShapeDtypeStruct((TOKENS, D_MODEL), F32),
                   jax.ShapeDtypeStruct((TOKENS, LANES), F32),
                   jax.ShapeDtypeStruct((NB, 8, LANES), jnp.int32),
                   jax.ShapeDtypeStruct((N_GROUPS * HS_CAP, HS_W), BF16)],
        scratch_shapes=[pltpu.VMEM((TM, D_MODEL), BF16), pltpu.VMEM((TM, LANES), F32), pltpu.VMEM((8, TM), F32),
                        pltpu.VMEM((2, STAGE_ROWS, HS_W), BF16), pltpu.SMEM((N_GROUPS + 1,), jnp.int32),
                        pltpu.SemaphoreType.DMA((1,))],
        compiler_params=_cp(("arbitrary",)),
        name="merge_ln1_router",
    )(xc, xl, mod3, o_f, o_b, sag, norm_g, zu, zv, ws, bs_t, brc_ctx, brc_lat, gates,
      wb, wo, g1, b1, rw_hi, rw_lo, rb, utri)


def _experts_kernel(blk_ref, grp_ref, ntile_ref, hs_ref, wg32_ref, wu32_ref, wd32_ref, ys_ref,
                    wg_ref, wu_ref, wd_ref):
    s = pl.program_id(0)

    @pl.when(jnp.logical_or(s == 0, grp_ref[s] != grp_ref[jnp.maximum(s - 1, 0)]))
    def _():
        wg_ref[...] = wg32_ref[...].astype(BF16)
        wu_ref[...] = wu32_ref[...].astype(BF16)
        wd_ref[...] = wd32_ref[...].astype(BF16)

    @pl.when(s < ntile_ref[0])
    def _():
        g = grp_ref[s]
        x = hs_ref[:, :D_MODEL]
        cw = (hs_ref[:, D_MODEL:D_MODEL + LANES].astype(F32)
              + hs_ref[:, D_MODEL + LANES:D_MODEL + 2 * LANES].astype(F32)
              + hs_ref[:, D_MODEL + 2 * LANES:].astype(F32))
        lane = lax.broadcasted_iota(jnp.int32, (TM, LANES), 1)
        hid = []
        for k in range(EXPERTS_PER_GROUP):
            w_e = jnp.sum(jnp.where(lane == g * EXPERTS_PER_GROUP + k, cw, 0.0), axis=-1, keepdims=True)
            hid.append((_silu(_dot(x, wg_ref[k])) * _dot(x, wu_ref[k]) * w_e).astype(BF16))
        y = _dot(jnp.concatenate(hid, axis=1), wd_ref[...].reshape(EXPERTS_PER_GROUP * EXPERT_FF, D_MODEL))
        ys_ref[...] = y.astype(BF16)


def _experts(l, tile_blk, tile_grp, n_tiles, hs, wg, wu, wd):
    e = EXPERTS_PER_GROUP
    return pl.pallas_call(
        _experts_kernel,
        grid_spec=pltpu.PrefetchScalarGridSpec(
            num_scalar_prefetch=3,
            grid=(MOE_TILES,),
            in_specs=[pl.BlockSpec((TM, HS_W), lambda s, blk, grp, nt: (blk[s], 0)),
                      pl.BlockSpec((None, e, D_MODEL, EXPERT_FF), lambda s, blk, grp, nt: (l, grp[s], 0, 0)),
                      pl.BlockSpec((None, e, D_MODEL, EXPERT_FF), lambda s, blk, grp, nt: (l, grp[s], 0, 0)),
                      pl.BlockSpec((None, e, EXPERT_FF, D_MODEL), lambda s, blk, grp, nt: (l, grp[s], 0, 0))],
            out_specs=pl.BlockSpec((TM, D_MODEL), lambda s, blk, grp, nt: (blk[s], 0)),
            scratch_shapes=[pltpu.VMEM((e, D_MODEL, EXPERT_FF), BF16), pltpu.VMEM((e, D_MODEL, EXPERT_FF), BF16),
                            pltpu.VMEM((e, EXPERT_FF, D_MODEL), BF16)],
        ),
        out_shape=jax.ShapeDtypeStruct((N_GROUPS * HS_CAP, D_MODEL), BF16),
        compiler_params=_cp(("arbitrary",)),
        name="moe_experts",
    )(tile_blk, tile_grp, n_tiles, hs, wg, wu, wd)


def _window_fetch(ys_ref, offs_ref, blk, win_ref, buf, sem_ref, rows):
    return [pltpu.make_async_copy(
        ys_ref.at[pl.ds(pl.multiple_of(offs_ref[blk * N_GROUPS + g], HS_ALIGN), rows), :],
        win_ref.at[buf, pl.ds(g * rows, rows), :], sem_ref.at[buf]) for g in range(N_GROUPS)]


def _combine_kernel(alpha, offs_ref, small_ref, ys_ref, slot_ref, x1_ref, mod_ref, g2_ref, b2_ref,
                    oc_ref, ol_ref, win_ref, y_ref, sem_ref):
    i = pl.program_id(0)
    buf = i % 2

    def start(blk, to_buf):
        def go(rows):
            for cp in _window_fetch(ys_ref, offs_ref, blk, win_ref, to_buf, sem_ref, rows):
                cp.start()
        _by_window(small_ref[blk] == 1, go)

    pl.when(i == 0)(lambda: start(0, 0))
    pl.when(i + 1 < NB)(lambda: start(i + 1, 1 - buf))

    def gather(rows):
        for cp in _window_fetch(ys_ref, offs_ref, i, win_ref, buf, sem_ref, rows):
            cp.wait()
        back = slot_ref[...]
        packed = back - jnp.floor(back * (1.0 / TM)) * float(TM - rows)
        slot = jnp.concatenate([packed] * (N_GROUPS * rows // LANES), axis=1)
        lane = lax.broadcasted_iota(jnp.int32, (TM, N_GROUPS * rows), 1).astype(F32)
        sel = jnp.where(lane == slot, 1.0, 0.0).astype(BF16)
        y_ref[...] = _dot(sel, win_ref[buf, :N_GROUPS * rows, :])

    _by_window(small_ref[i] == 1, gather)
    m = mod_ref[0]
    out = _layer_norm(alpha * x1_ref[...] + m[:, 5 * D_MODEL:6 * D_MODEL] * y_ref[...],
                      g2_ref[...], b2_ref[...])

    @pl.when(i < NB_CTX)
    def _():
        oc_ref[...] = out

    @pl.when(i >= NB_CTX)
    def _():
        ol_ref[...] = out


def _combine(l, alpha, offs, small, ys, slot, x1, mod3, g2, b2):
    return pl.pallas_call(
        functools.partial(_combine_kernel, alpha),
        grid_spec=pltpu.PrefetchScalarGridSpec(
            num_scalar_prefetch=2,
            grid=(NB,),
            in_specs=[pl.BlockSpec(memory_space=pl.ANY),
                      pl.BlockSpec((TM, LANES), lambda i, *_: (i, 0)),
                      pl.BlockSpec((TM, D_MODEL), lambda i, *_: (i, 0)),
                      _mod_spec(l),
                      pl.BlockSpec((None, 1, D_MODEL), lambda i, *_: (l, 0, 0)),
                      pl.BlockSpec((None, 1, D_MODEL), lambda i, *_: (l, 0, 0))],
            out_specs=[pl.BlockSpec((TM, D_MODEL), lambda i, *_: (jnp.minimum(i, NB_CTX - 1), 0)),
                       pl.BlockSpec((TM, D_MODEL), lambda i, *_: (jnp.maximum(i - NB_CTX, 0), 0))],
            scratch_shapes=[pltpu.VMEM((2, N_GROUPS * TM, D_MODEL), BF16), pltpu.VMEM((TM, D_MODEL), F32),
                            pltpu.SemaphoreType.DMA((2,))],
        ),
        out_shape=[jax.ShapeDtypeStruct((CTX_TOKENS, D_MODEL), F32),
                   jax.ShapeDtypeStruct((LAT_TOKENS, D_MODEL), F32)],
        compiler_params=_cp(("arbitrary",)),
        name="combine_ln2",
    )(offs, small, ys, slot, x1, mod3, g2, b2)


def _rope_tables():
    f32 = np.float32
    pos = np.arange(DEC_SEQ)
    row = (pos // GRID_W).astype(f32)
    col = (pos % GRID_W).astype(f32)
    quarter = ATTN_HEAD_DIM // 4
    inv_freq = np.power(f32(ROPE_BASE), -np.arange(quarter, dtype=f32) / f32(quarter)).astype(f32)
    j = np.arange(ATTN_HEAD_DIM)
    p = np.where((j // (2 * quarter))[None, :] == 0, row[:, None], col[:, None]).astype(f32)
    ang = (p * inv_freq[j % quarter][None, :]).astype(f32)
    sign = np.where((j % (2 * quarter)) < quarter, -1.0, 1.0).astype(f32)
    cos = np.tile(np.cos(ang).astype(f32), (1, ATTN_HEADS))
    sin = np.tile((np.sin(ang) * sign[None, :]).astype(f32), (1, ATTN_HEADS))
    cos = np.concatenate([np.ones((TM, 512), f32), cos], axis=0)
    sin = np.concatenate([np.zeros((TM, 512), f32), sin], axis=0)
    return jnp.asarray(cos), jnp.asarray(sin)


def _dup_heads(t):
    h0, h1 = t[..., :ATTN_HEAD_DIM], t[..., ATTN_HEAD_DIM:]
    return jnp.concatenate([h0, h0, h1, h1], axis=-1)


def _w_in_slabs_kernel(w_ref, scale_ref, wa_ref, wb_ref):
    w = w_ref[...]
    wa_ref[...] = w[:, :WA_COLS].astype(BF16)
    wb_ref[...] = (w[:, WB_START:] * scale_ref[...]).astype(BF16)


def _split_w_in(w):
    rows = 256
    gate_scale = np.where(np.arange(IN_COLS - WB_START) >= WB_MG.start, 0.5, 1.0).astype(np.float32)
    return pl.pallas_call(
        _w_in_slabs_kernel,
        grid=(DEPTH, D_MODEL // rows),
        in_specs=[pl.BlockSpec((None, rows, IN_COLS), lambda l, r: (l, r, 0)),
                  pl.BlockSpec((1, IN_COLS - WB_START), lambda l, r: (0, 0))],
        out_specs=[pl.BlockSpec((None, rows, WA_COLS), lambda l, r: (l, r, 0)),
                   pl.BlockSpec((None, rows, IN_COLS - WB_START), lambda l, r: (l, r, 0))],
        out_shape=[jax.ShapeDtypeStruct((DEPTH, D_MODEL, WA_COLS), BF16),
                   jax.ShapeDtypeStruct((DEPTH, D_MODEL, IN_COLS - WB_START), BF16)],
        compiler_params=_cp(("arbitrary", "arbitrary")),
        name="w_in_slabs",
    )(w, jnp.asarray(gate_scale)[None, :])


def kernel(x_prompt, x_sample, state_gla, cache_k, cache_v, c, c_ctx, ada_w, ada_b, w_in, gla_gate_w, gla_gate_b, gla_norm_g, gmlp_ln_g, gmlp_ln_b, gmlp_ws, gmlp_bs, attn_sink, w_branch, w_out, ln1_g, ln1_b, ln2_g, ln2_b, router_group_w, router_group_b, router_expert_w, router_expert_b, expert_w_gate, expert_w_up, expert_w_down):
    alpha = (2.0 * DEPTH) ** 0.25
    xc = x_prompt.reshape(CTX_TOKENS, D_MODEL)
    xl = x_sample.reshape(LAT_TOKENS, D_MODEL)
    cond = jnp.concatenate([c_ctx[None, :], c, jnp.zeros((MOD_ROWS - 1 - DEC_BATCH, D_MODEL), F32)], axis=0)
    cos_t, sin_t = _rope_tables()

    ada_b3 = ada_b[:, None, :]
    w_parts = _split_w_in(w_in)
    zrow = jnp.zeros((DEPTH, GLA_RANK, 256), F32)
    zpad = jnp.zeros((DEPTH, LANES - 2 * GLA_RANK, 256), F32)
    gw_p = jnp.stack([jnp.concatenate([gla_gate_w[:, 0], zrow, zpad], axis=1),
                      jnp.concatenate([zrow, gla_gate_w[:, 1], zpad], axis=1)], axis=1).astype(BF16)
    gb_p = gla_gate_b[:, :, None, :]
    sink_t = jnp.broadcast_to(attn_sink[:, :, None, None], (DEPTH, ATTN_HEADS, 1, LANES))
    kc = _dup_heads(cache_k.reshape(DEC_BATCH, DEPTH, PAST_LEN, 128)).astype(BF16)
    vct = jnp.swapaxes(cache_v.reshape(DEC_BATCH, DEPTH, PAST_LEN, 128), 2, 3).astype(BF16)
    rw = jnp.swapaxes(jnp.concatenate(
        [router_expert_w, router_group_w,
         jnp.zeros((DEPTH, D_MODEL, LANES - N_EXPERTS - N_GROUPS), F32)], axis=2), 1, 2)
    rw_hi = rw.astype(BF16)
    rw_lo = (rw - rw_hi.astype(F32)).astype(BF16)
    rb = jnp.concatenate([router_expert_b, router_group_b,
                          jnp.zeros((DEPTH, LANES - N_EXPERTS - N_GROUPS), F32)], axis=1)
    rb = jnp.broadcast_to(rb[:, :, None], (DEPTH, LANES, TM))
    bs_t = jnp.repeat(jnp.swapaxes(gmlp_bs, 1, 2), GMLP_CH, axis=2)
    ws_b = gmlp_ws.astype(BF16)
    wb_b = (0.5 * w_branch).astype(BF16)
    wo_b = w_out.astype(BF16)
    row = lambda t: t[:, None, :]

    mod3 = _modulation(cond, ada_w, ada_b3).reshape(DEPTH, MOD_ROWS, 1, 6 * D_MODEL)
    states, keys, values = [], [], []
    for l in range(DEPTH):
        (qef, kef, klf, qeb, keb, klb, dec, av, avt, sag, zu, zv, cq, ckd, cvt, ck, cv, gates) = _in_projection(
            l, xc, xl, mod3, cos_t, sin_t, gw_p, gb_p, row(gmlp_ln_g), row(gmlp_ln_b), w_parts)
        o_f, o_b, s_fin = _gla(l, qef, kef, klf, qeb, keb, klb, dec, av, avt, state_gla)
        brc_ctx = _context_attention(l, cq, ckd, cvt, sink_t)
        brc_lat = _latent_attention(l, cq, ckd, cvt, kc, vct, sink_t)
        x1, slot, meta, hs = _merge(l, alpha, xc, xl, mod3, o_f, o_b, sag, row(gla_norm_g), zu, zv,
                                    ws_b, bs_t, brc_ctx, brc_lat, gates, wb_b,
                                    wo_b, row(ln1_g), row(ln1_b), rw_hi, rw_lo, rb)
        offs = meta[:, 0, :N_GROUPS].reshape(NB * N_GROUPS)
        tiles = meta[NB - 1]
        ys = _experts(l, tiles[1, :MOE_TILES], tiles[2, :MOE_TILES], tiles[3, :1], hs,
                      expert_w_gate, expert_w_up, expert_w_down)
        xc, xl = _combine(l, alpha, offs, meta[:, 0, 2 * N_GROUPS], ys, slot, x1, mod3,
                          row(ln2_g), row(ln2_b))
        states.append(s_fin)
        keys.append(ck.reshape(BATCH, SEQ, ATTN_KV_HEADS, ATTN_HEAD_DIM))
        values.append(cv.reshape(BATCH, SEQ, ATTN_KV_HEADS, ATTN_HEAD_DIM))
    return (xc.reshape(BATCH, SEQ, D_MODEL), xl.reshape(DEC_BATCH, DEC_SEQ, D_MODEL),
            jnp.stack(states, axis=1), jnp.stack(keys, axis=1), jnp.stack(values, axis=1))
```

```python
import functools

import jax
import jax.numpy as jnp
import numpy as np
from jax import lax
from jax.experimental import pallas as pl
from jax.experimental.pallas import tpu as pltpu

F32 = jnp.float32
BF16 = jnp.bfloat16

D_MODEL = 1024
BATCH = 32
SEQ = 256
DEPTH = 2
DEC_BATCH = 8
DEC_SEQ = 1024
PAST_LEN = 512
GRID_W = 64
GLA_HEADS = 4
GLA_DV = 128
GLA_DK = 64
GLA_RANK = 16
GLA_TAU = 16.0
GLA_CHUNK = 32
GMLP_CHUNK = 128
GMLP_CH = 128
GMLP_GROUPS = 4
ATTN_HEADS = 8
ATTN_KV_HEADS = 2
ATTN_GROUP = ATTN_HEADS // ATTN_KV_HEADS
ATTN_HEAD_DIM = 64
WINDOW = 128
ATTN_BLOCK = 128
ROPE_BASE = 10000.0
BRANCH_W = 512
N_BRANCH = 3
N_GROUPS = 4
EXPERTS_PER_GROUP = 4
N_EXPERTS = 16
EXPERT_FF = 256
LN_EPS = 1e-5

LANES = 128
TM = 256
CTX_TOKENS = BATCH * SEQ
LAT_TOKENS = DEC_BATCH * DEC_SEQ
TOKENS = CTX_TOKENS + LAT_TOKENS
NB_CTX = CTX_TOKENS // TM
NB_LAT = LAT_TOKENS // TM
NB = NB_CTX + NB_LAT
LAT_BLOCKS_PER_SEQ = DEC_SEQ // TM
MOD_ROWS = 16
GLA_SUB = 128
CHUNKS_PER_SUB = GLA_SUB // GLA_CHUNK
MERGE_ROWS = 128
MERGE_STAGES = 5
ROUTER_ROWS = 24
HS_ALIGN = 16
SMALL_WIN = 128
HS_W = D_MODEL + 3 * LANES
STAGE_ROWS = TM + N_GROUPS * HS_ALIGN + TM
_MAX_PAD = NB * (HS_ALIGN - 1)
HS_CAP = -(-(TOKENS + _MAX_PAD + 2 * TM) // TM) * TM
MOE_TILES = (TOKENS + N_GROUPS * _MAX_PAD) // TM + 2 * N_GROUPS

IN_COLS = 6432
W_AQ, W_AK, W_AV, W_AG = slice(0, 256), slice(256, 512), slice(512, 1024), slice(1024, 1536)
W_LR = slice(1536, 1664)
W_BZ, W_CQ, W_CK, W_CV = slice(1568, 2592), slice(2592, 3104), slice(3104, 3232), slice(3232, 3360)
W_MG = slice(3360, 6432)

VMEM_LIMIT = 56 * 1024 * 1024


def _cp(sem):
    return pltpu.CompilerParams(dimension_semantics=sem, vmem_limit_bytes=VMEM_LIMIT)


def _dot(a, b):
    return jnp.dot(a, b, preferred_element_type=F32)


def _dot_nt(a, b):
    return lax.dot_general(a, b, (((1,), (1,)), ((), ())), preferred_element_type=F32)


def _dot_tn(a, b):
    return lax.dot_general(a, b, (((0,), (0,)), ((), ())), preferred_element_type=F32)


def _split(x):
    hi = x.astype(BF16)
    lo = (x - hi.astype(F32)).astype(BF16)
    return hi, lo


def _sigmoid(x):
    return 0.5 * (jnp.tanh(0.5 * x) + 1.0)


def _silu(x):
    return x * _sigmoid(x)


def _layer_norm(x, g, b):
    mu = jnp.mean(x, axis=-1, keepdims=True)
    xc = x - mu
    var = jnp.mean(xc * xc, axis=-1, keepdims=True)
    return xc * lax.rsqrt(var + LN_EPS) * g + b


def _mod_block(i):
    return jnp.where(i < NB_CTX, 0, 1 + (i - NB_CTX) // LAT_BLOCKS_PER_SEQ)


def _tok(width):
    return pl.BlockSpec((TM, width), lambda i: (i, 0))


def _ctx_tok(width):
    return pl.BlockSpec((TM, width), lambda i: (jnp.minimum(i, NB_CTX - 1), 0))


def _lat_tok(width):
    return pl.BlockSpec((TM, width), lambda i: (jnp.maximum(i - NB_CTX, 0), 0))


def _layer_spec(l, shape):
    return pl.BlockSpec((None,) + shape, lambda *_: (l,) + (0,) * len(shape))


def _mod_spec(l):
    return pl.BlockSpec((None, 1, 1, 6 * D_MODEL), lambda i, *_: (l, _mod_block(i), 0, 0))


def _mod_kernel(cond_ref, w_ref, b_ref, o_ref):
    s_hi, s_lo = _split(_silu(cond_ref[...]))
    w_hi, w_lo = _split(w_ref[...])
    o_ref[...] = _dot(s_hi, w_hi) + _dot(s_lo, w_hi) + _dot(s_hi, w_lo) + b_ref[...]


def _modulation(cond, ada_w, ada_b3):
    tn = 1536
    return pl.pallas_call(
        _mod_kernel,
        grid=(DEPTH, 6 * D_MODEL // tn),
        in_specs=[pl.BlockSpec((MOD_ROWS, D_MODEL), lambda l, j: (0, 0)),
                  pl.BlockSpec((None, D_MODEL, tn), lambda l, j: (l, 0, j)),
                  pl.BlockSpec((None, 1, tn), lambda l, j: (l, 0, j))],
        out_specs=pl.BlockSpec((None, MOD_ROWS, tn), lambda l, j: (l, 0, j)),
        out_shape=jax.ShapeDtypeStruct((DEPTH, MOD_ROWS, 6 * D_MODEL), F32),
        compiler_params=_cp(("arbitrary", "arbitrary")),
        name="modulation",
    )(cond, ada_w, ada_b3)


def _rope(x, cos, sin):
    n = x.shape[1]
    lane = lax.broadcasted_iota(jnp.int32, x.shape, 1)
    partner = jnp.where((lane & 31) < 16, pltpu.roll(x, n - 16, 1), pltpu.roll(x, 16, 1))
    return x * cos + partner * sin


def _inproj_kernel(xc_ref, xl_ref, mod_ref, cos_ref, sin_ref, gw_ref, gb_ref, lg_ref, lb_ref,
                   wt_ref,
                   qef_ref, kef_ref, klf_ref, qeb_ref, keb_ref, klb_ref, dec_ref,
                   av_ref, avt_ref, sag_ref, zu_ref, zv_ref,
                   cq_ref, ckd_ref, cvt_ref, ck_ref, cv_ref, gate_ref):
    i = pl.program_id(0)
    m = mod_ref[0]
    x = jnp.where(i < NB_CTX, xc_ref[...], xl_ref[...])
    h = (x * (1.0 + m[:, D_MODEL:2 * D_MODEL]) + m[:, 0:D_MODEL]).astype(BF16)

    def proj(rows):
        return _dot_nt(h, wt_ref[rows, :])

    aq = proj(W_AQ)
    ak = proj(W_AK)
    lr = proj(W_LR).astype(BF16)
    zs = [_dot(lr, gw_ref[d]) + gb_ref[d] for d in range(2)]
    for nbr in range(N_BRANCH):
        cols = slice(W_MG.start + nbr * D_MODEL, W_MG.start + (nbr + 1) * D_MODEL)
        gate_ref[:, nbr * D_MODEL:(nbr + 1) * D_MODEL] = jnp.tanh(proj(cols)) + 1.0
    r = lax.broadcasted_iota(jnp.int32, (TM, TM), 0)
    c = lax.broadcasted_iota(jnp.int32, (TM, TM), 1)
    same = (r >> 5) == (c >> 5)
    bs = []
    for d in range(2):
        g = (jnp.minimum(zs[d], 0.0) - jnp.log(1.0 + jnp.exp(-jnp.abs(zs[d])))) / GLA_TAU
        tri_b = jnp.where(same & ((c <= r) if d == 0 else (c >= r)), 1.0, 0.0).astype(BF16)
        g_hi, g_lo = _split(g)
        bs.append(_dot(tri_b, g_hi) + _dot(tri_b, g_lo))
    z = _gelu(proj(W_BZ))
    zu_ref[...] = z[:, :BRANCH_W]
    zv_ref[...] = _layer_norm(z[:, BRANCH_W:], lg_ref[...], lb_ref[...]).astype(BF16)
    sag_ref[...] = _silu(proj(W_AG))
    av = proj(W_AV)
    av_ref[...] = av.astype(BF16)
    avt_ref[...] = av.T.astype(BF16)
    decs = []
    for d, (qe_ref, ke_ref, kl_ref) in enumerate(((qef_ref, kef_ref, klf_ref), (qeb_ref, keb_ref, klb_ref))):
        b = bs[d]
        last = GLA_CHUNK - 1 if d == 0 else 0
        tot = [b[n * GLA_CHUNK + last:n * GLA_CHUNK + last + 1, :] for n in range(TM // GLA_CHUNK)]
        bl = jnp.concatenate([jnp.broadcast_to(t, (GLA_CHUNK, t.shape[1])) for t in tot], axis=0)
        qe_ref[...] = (aq * jnp.exp(b) * (GLA_DK ** -0.5)).astype(BF16)
        ke_ref[...] = (ak * jnp.exp(-b)).astype(BF16)
        kl_ref[...] = (ak * jnp.exp(bl - b)).astype(BF16)
        decs.append(jnp.exp(jnp.concatenate(tot, axis=0)))
    dec_ref[...] = jnp.concatenate(decs, axis=1)

    cos = cos_ref[...]
    sin = sin_ref[...]
    cq_ref[...] = (_rope(proj(W_CQ), cos, sin) * (ATTN_HEAD_DIM ** -0.5)).astype(BF16)
    ck = proj(W_CK)
    cv = proj(W_CV)
    kr = _rope(ck, cos[:, :LANES], sin[:, :LANES])
    swapped = pltpu.roll(kr, ATTN_HEAD_DIM, 1)
    lane = lax.broadcasted_iota(jnp.int32, (TM, LANES), 1)
    ckd_ref[...] = jnp.concatenate([jnp.where(lane < 64, kr, swapped), jnp.where(lane < 64, swapped, kr)],
                                   axis=1).astype(BF16)
    cvt_ref[...] = cv.T.astype(BF16)

    @pl.when(i < NB_CTX)
    def _():
        ck_ref[...] = ck
        cv_ref[...] = cv


def _rope_block(i):
    return jnp.where(i < NB_CTX, 0, 1 + (i - NB_CTX) % LAT_BLOCKS_PER_SEQ)


def _in_projection(l, xc, xl, mod3, cos_t, sin_t, gw_p, gb_p, ln_g, ln_b, w_parts):
    def tok_out(width, dt):
        return _tok(width), jax.ShapeDtypeStruct((TOKENS, width), dt)

    def feat_out(width):
        return (pl.BlockSpec((width, TM), lambda i: (0, i)), jax.ShapeDtypeStruct((width, TOKENS), BF16))

    def ctx_out(width):
        return _ctx_tok(width), jax.ShapeDtypeStruct((CTX_TOKENS, width), F32)

    dec_out = (pl.BlockSpec((TM // GLA_CHUNK, 512), lambda i: (i, 0)),
               jax.ShapeDtypeStruct((TOKENS // GLA_CHUNK, 512), F32))
    outs = [tok_out(256, BF16)] * 6 + [dec_out] + [
        tok_out(512, BF16), feat_out(512), tok_out(512, F32), tok_out(512, F32), tok_out(512, BF16),
        tok_out(512, BF16), tok_out(256, BF16), feat_out(LANES), ctx_out(LANES), ctx_out(LANES),
        tok_out(3072, F32)]
    return pl.pallas_call(
        _inproj_kernel,
        grid=(NB,),
        in_specs=[_ctx_tok(D_MODEL), _lat_tok(D_MODEL), _mod_spec(l),
                  pl.BlockSpec((TM, 512), lambda i: (_rope_block(i), 0)),
                  pl.BlockSpec((TM, 512), lambda i: (_rope_block(i), 0)),
                  _layer_spec(l, (2, LANES, 256)), _layer_spec(l, (2, 1, 256)),
                  _layer_spec(l, (1, 512)), _layer_spec(l, (1, 512))]
                 + [_layer_spec(l, w.shape[1:]) for w in w_parts],
        out_specs=[o[0] for o in outs],
        out_shape=[o[1] for o in outs],
        compiler_params=_cp(("arbitrary",)),
        name="in_projection",
    )(xc, xl, mod3, cos_t, sin_t, gw_p, gb_p, ln_g, ln_b, *w_parts)


def _gla_direction(qe, ke, kl, v, vt, dec, masks, state_ref, reverse):
    tri, half_masks, chunk_mask, head_diag = masks
    tri = tri[int(reverse)]
    outs = []
    for p in range(GLA_HEADS // 2):
        ps = slice(p * LANES, (p + 1) * LANES)
        qp = qe[:, ps]
        kep = ke[:, ps]
        klp = kl[:, ps]
        vp = v[:, p * 2 * GLA_DV:(p + 1) * 2 * GLA_DV]
        vtp = vt[p * 2 * GLA_DV:(p + 1) * 2 * GLA_DV, :]
        halves = []
        for hh in range(2):
            s = jnp.where(tri, _dot_nt(qp * half_masks[hh], kep), 0.0).astype(BF16)
            halves.append(_dot(s, vp[:, hh * GLA_DV:(hh + 1) * GLA_DV]))
        k_exp = jnp.concatenate([klp] * CHUNKS_PER_SUB, axis=1) * chunk_mask
        q_exp = jnp.concatenate([qp] * CHUNKS_PER_SUB, axis=1) * chunk_mask
        kvt = _dot(vtp, k_exp)
        st = state_ref[p]
        entering = [None] * CHUNKS_PER_SUB
        order = range(CHUNKS_PER_SUB - 1, -1, -1) if reverse else range(CHUNKS_PER_SUB)
        for n in order:
            entering[n] = st
            st = dec[n:n + 1, ps] * st + jnp.where(head_diag, kvt[:, n * LANES:(n + 1) * LANES], 0.0)
        state_ref[p] = st
        st_stack = jnp.concatenate(entering, axis=1).astype(BF16)
        inter = _dot_nt(q_exp, st_stack)
        outs.append(halves[0] + inter[:, :GLA_DV])
        outs.append(halves[1] + inter[:, GLA_DV:])
    return jnp.concatenate(outs, axis=1)


def _gla_mask_constants():
    r = np.arange(GLA_SUB)[:, None]
    c = np.arange(GLA_SUB)[None, :]
    same = (r // GLA_CHUNK) == (c // GLA_CHUNK)
    tri = np.stack([same & (c <= r), same & (c >= r)]).astype(np.float32)
    lane = np.arange(LANES)[None, :]
    halves = np.stack([np.broadcast_to(lane < 64, (GLA_SUB, LANES)),
                       np.broadcast_to(lane >= 64, (GLA_SUB, LANES))]).astype(np.float32)
    col_chunk = np.arange(CHUNKS_PER_SUB * LANES)[None, :] // LANES
    chunk_mask = ((r // GLA_CHUNK) == col_chunk).astype(np.float32)
    head_diag = ((np.arange(2 * GLA_DV)[:, None] // GLA_DV) == (lane // GLA_DK)).astype(np.float32)
    return (jnp.asarray(tri), jnp.asarray(halves, dtype=BF16), jnp.asarray(chunk_mask, dtype=BF16),
            jnp.asarray(head_diag))


def _gla_kernel(tri_ref, half_ref, cmask_ref, hdiag_ref,
                qf_ref, kef_ref, klf_ref, vf_ref, vtf_ref, decf_ref,
                qb_ref, keb_ref, klb_ref, vb_ref, vtb_ref, decb_ref,
                s0_ref, of_ref, ob_ref, sfin_ref, state_ref):
    i = pl.program_id(0)
    @pl.when(i < NB_CTX)
    def _():
        state_ref[...] = jnp.zeros_like(state_ref)

    @pl.when(jnp.logical_and(i >= NB_CTX, (i - NB_CTX) % LAT_BLOCKS_PER_SEQ == 0))
    def _():
        zero = jnp.zeros((GLA_DK, GLA_DV), F32)
        for d in range(2):
            for p in range(2):
                pair = jnp.concatenate(
                    [jnp.concatenate([s0_ref[0, d, 2 * p], zero], axis=1),
                     jnp.concatenate([zero, s0_ref[0, d, 2 * p + 1]], axis=1)], axis=0)
                state_ref[d, p] = pair.T

    masks = ((tri_ref[0] > 0.5, tri_ref[1] > 0.5), (half_ref[0], half_ref[1]), cmask_ref[...],
             hdiag_ref[...] > 0.5)
    n_sub = TM // GLA_SUB
    for j in range(n_sub):
        r0 = j * GLA_SUB
        rs = slice(r0, r0 + GLA_SUB)
        of_ref[rs, :] = _gla_direction(
            qf_ref[rs, :], kef_ref[rs, :], klf_ref[rs, :], vf_ref[rs, :], vtf_ref[:, rs],
            decf_ref[j * CHUNKS_PER_SUB:(j + 1) * CHUNKS_PER_SUB, :256], masks, state_ref.at[0], False)
        jb = n_sub - 1 - j
        rb = slice(jb * GLA_SUB, (jb + 1) * GLA_SUB)
        ob_ref[rb, :] = _gla_direction(
            qb_ref[rb, :], keb_ref[rb, :], klb_ref[rb, :], vb_ref[rb, :], vtb_ref[:, rb],
            decb_ref[jb * CHUNKS_PER_SUB:(jb + 1) * CHUNKS_PER_SUB, 256:], masks, state_ref.at[1], True)

    @pl.when(i < NB_CTX)
    def _():
        for d in range(2):
            for p in range(2):
                sp = state_ref[d, p].T
                sfin_ref[0, d, 2 * p] = sp[:GLA_DK, :GLA_DV]
                sfin_ref[0, d, 2 * p + 1] = sp[GLA_DK:, GLA_DV:]


def _bwd_block(i):
    k = (i - NB_CTX) % LAT_BLOCKS_PER_SEQ
    return jnp.where(i < NB_CTX, i, i - k + (LAT_BLOCKS_PER_SEQ - 1 - k))


def _gla(l, qef, kef, klf, qeb, keb, klb, dec, av, avt, state_gla):
    def bwd(width):
        return pl.BlockSpec((TM, width), lambda i: (_bwd_block(i), 0))

    def lat_seq(i):
        return jnp.maximum(i - NB_CTX, 0) // LAT_BLOCKS_PER_SEQ

    vt_fwd = pl.BlockSpec((512, TM), lambda i: (0, i))
    vt_bwd = pl.BlockSpec((512, TM), lambda i: (0, _bwd_block(i)))
    dec_fwd = pl.BlockSpec((TM // GLA_CHUNK, 512), lambda i: (i, 0))
    dec_bwd = pl.BlockSpec((TM // GLA_CHUNK, 512), lambda i: (_bwd_block(i), 0))
    state_dims = (2, GLA_HEADS, GLA_DK, GLA_DV)
    consts = _gla_mask_constants()
    return pl.pallas_call(
        _gla_kernel,
        grid=(NB,),
        in_specs=[pl.BlockSpec(t.shape, lambda i, nd=t.ndim: (0,) * nd) for t in consts]
                 + [_tok(256), _tok(256), _tok(256), _tok(512), vt_fwd, dec_fwd,
                  bwd(256), bwd(256), bwd(256), bwd(512), vt_bwd, dec_bwd,
                  pl.BlockSpec((1, None) + state_dims, lambda i: (lat_seq(i), l, 0, 0, 0, 0))],
        out_specs=[_tok(512), bwd(512),
                   pl.BlockSpec((1,) + state_dims, lambda i: (jnp.minimum(i, NB_CTX - 1), 0, 0, 0, 0))],
        out_shape=[jax.ShapeDtypeStruct((TOKENS, 512), F32), jax.ShapeDtypeStruct((TOKENS, 512), F32),
                   jax.ShapeDtypeStruct((BATCH,) + state_dims, F32)],
        scratch_shapes=[pltpu.VMEM((2, 2, 2 * GLA_DV, LANES), F32)],
        compiler_params=_cp(("arbitrary",)),
        name="gla_scan",
    )(*consts, qef, kef, klf, av, avt, dec, qeb, keb, klb, av, avt, dec, state_gla)


def _attend_group(q_refs, keys, values_t, masks, sink_ref, g, o_ref):
    rows = q_refs[0].shape[0]
    lane = lax.broadcasted_iota(jnp.int32, (rows, LANES), 1)
    stacked = []
    for q in q_refs:
        q32 = q.astype(F32)
        stacked.append(jnp.where(lane < 64, q32, 0.0).astype(BF16))
        stacked.append(jnp.where(lane >= 64, q32, 0.0).astype(BF16))
    qs = jnp.concatenate(stacked, axis=0)
    col_head = lax.broadcasted_iota(jnp.int32, (1, ATTN_GROUP * rows), 1) // rows
    sink = jnp.zeros((1, ATTN_GROUP * rows), F32)
    for hh in range(ATTN_GROUP):
        sink = jnp.where(col_head == hh, sink_ref[ATTN_GROUP * g + hh][:, :1], sink)
    ss = []
    m = sink
    for kk, mk in zip(keys, masks):
        s = _dot_nt(kk, qs)
        if mk is not None:
            s = jnp.where(jnp.concatenate([mk] * ATTN_GROUP, axis=1), s, -jnp.inf)
        ss.append(s)
        m = jnp.maximum(m, jnp.max(s, axis=0, keepdims=True))
    den = jnp.exp(sink - m)
    acc = jnp.zeros((ATTN_HEAD_DIM, ATTN_GROUP * rows), F32)
    for s, vt in zip(ss, values_t):
        e = jnp.exp(s - m)
        den = den + jnp.sum(e, axis=0, keepdims=True)
        acc = acc + _dot(vt, e.astype(BF16))
    o = (acc / den).astype(BF16)
    for hh in range(ATTN_GROUP):
        h = ATTN_GROUP * g + hh
        o_ref[h * ATTN_HEAD_DIM:(h + 1) * ATTN_HEAD_DIM, :] = o[:, hh * rows:(hh + 1) * rows]


def _ctx_attn_kernel(q_ref, kd_ref, vt_ref, sink_ref, o_ref):
    for g in range(ATTN_KV_HEADS):
        gl = slice(g * LANES, (g + 1) * LANES)
        q_refs = [q_ref[:, (2 * g + pp) * LANES:(2 * g + pp + 1) * LANES] for pp in range(2)]
        vt = vt_ref[g * ATTN_HEAD_DIM:(g + 1) * ATTN_HEAD_DIM, :]
        _attend_group(q_refs, [kd_ref[:, gl]], [vt], [None], sink_ref, g, o_ref)


def _context_attention(l, cq, ckd, cvt, sink_t):
    return pl.pallas_call(
        _ctx_attn_kernel,
        grid=(BATCH,),
        in_specs=[pl.BlockSpec((SEQ, 512), lambda b: (b, 0)),
                  pl.BlockSpec((SEQ, 256), lambda b: (b, 0)),
                  pl.BlockSpec((LANES, SEQ), lambda b: (0, b)),
                  _layer_spec(l, (ATTN_HEADS, 1, LANES))],
        out_specs=pl.BlockSpec((512, SEQ), lambda b: (0, b)),
        out_shape=jax.ShapeDtypeStruct((512, CTX_TOKENS), BF16),
        compiler_params=_cp(("arbitrary",)),
        name="context_attention",
    )(cq, ckd, cvt, sink_t)


def _lat_attn_kernel(q_ref, kd_ref, vt_ref, kc_ref, vct_ref, sink_ref, o_ref):
    n = pl.program_id(1)
    nq = DEC_SEQ // ATTN_BLOCK
    prev0 = pl.multiple_of(jnp.maximum(n - 1, 0) * ATTN_BLOCK, ATTN_BLOCK)
    cur0 = pl.multiple_of(n * ATTN_BLOCK, ATTN_BLOCK)
    next0 = pl.multiple_of(jnp.minimum(n + 1, nq - 1) * ATTN_BLOCK, ATTN_BLOCK)
    kj = lax.broadcasted_iota(jnp.int32, (ATTN_BLOCK, ATTN_BLOCK), 0)
    qi = lax.broadcasted_iota(jnp.int32, (ATTN_BLOCK, ATTN_BLOCK), 1)
    m_prev = jnp.logical_and(kj >= qi, n > 0)
    m_next = jnp.logical_and(kj <= qi, n < nq - 1)
    for g in range(ATTN_KV_HEADS):
        gl = slice(g * LANES, (g + 1) * LANES)
        gr = slice(g * ATTN_HEAD_DIM, (g + 1) * ATTN_HEAD_DIM)
        q_refs = [q_ref[:, (2 * g + pp) * LANES:(2 * g + pp + 1) * LANES] for pp in range(2)]
        keys = [kd_ref[pl.ds(prev0, ATTN_BLOCK), gl], kd_ref[pl.ds(cur0, ATTN_BLOCK), gl],
                kd_ref[pl.ds(next0, ATTN_BLOCK), gl], kc_ref[0, :, gl]]
        vals = [vt_ref[gr, pl.ds(prev0, ATTN_BLOCK)], vt_ref[gr, pl.ds(cur0, ATTN_BLOCK)],
                vt_ref[gr, pl.ds(next0, ATTN_BLOCK)], vct_ref[0, gr, :]]
        _attend_group(q_refs, keys, vals, [m_prev, None, m_next, None], sink_ref, g, o_ref)


def _latent_attention(l, cq, ckd, cvt, kc, vct, sink_t):
    nq = DEC_SEQ // ATTN_BLOCK
    q0 = CTX_TOKENS // ATTN_BLOCK
    s0 = CTX_TOKENS // DEC_SEQ
    return pl.pallas_call(
        _lat_attn_kernel,
        grid=(DEC_BATCH, nq),
        in_specs=[pl.BlockSpec((ATTN_BLOCK, 512), lambda b, n: (q0 + b * nq + n, 0)),
                  pl.BlockSpec((DEC_SEQ, 256), lambda b, n: (s0 + b, 0)),
                  pl.BlockSpec((LANES, DEC_SEQ), lambda b, n: (0, s0 + b)),
                  pl.BlockSpec((1, None, PAST_LEN, 256), lambda b, n: (b, l, 0, 0)),
                  pl.BlockSpec((1, None, LANES, PAST_LEN), lambda b, n: (b, l, 0, 0)),
                  _layer_spec(l, (ATTN_HEADS, 1, LANES))],
        out_specs=pl.BlockSpec((512, ATTN_BLOCK), lambda b, n: (0, b * nq + n)),
        out_shape=jax.ShapeDtypeStruct((512, LAT_TOKENS), BF16),
        compiler_params=_cp(("arbitrary", "arbitrary")),
        name="latent_attention",
    )(cq, ckd, cvt, kc, vct, sink_t)


def _gelu(x):
    return 0.5 * x * (1.0 + jnp.tanh(0.7978845608028654 * (x + 0.044715 * (x * x * x))))


def _merge_kernel(alpha, xc_ref, xl_ref, mod_ref, of_ref, ob_ref, sag_ref, ng_ref, zu_ref, zv_ref,
                  ws_ref, bs_ref, cc_ref, cl_ref, gate_ref, wb_ref, wo_ref, g1_ref, b1_ref,
                  rwh_ref, rwl_ref, rb_ref, utri_ref,
                  x1_ref, slot_ref, meta_ref, hs_ref,
                  h2_ref, cw_ref, gid_ref, stage_ref, cnt_ref, sem_ref):
    parts = [_merge_rows(alpha, slice(p * MERGE_ROWS, (p + 1) * MERGE_ROWS), xc_ref, xl_ref, mod_ref,
                         of_ref, ob_ref, sag_ref, ng_ref, zu_ref, zv_ref, ws_ref, bs_ref, cc_ref,
                         cl_ref, gate_ref, wb_ref, wo_ref, g1_ref, b1_ref, rwh_ref, rwl_ref, rb_ref,
                         x1_ref, h2_ref, cw_ref, gid_ref) for p in range(TM // MERGE_ROWS)]
    for step in range(MERGE_STAGES + len(parts) - 1):
        for lag, part in enumerate(parts):
            if 0 <= step - lag < MERGE_STAGES:
                next(part)
    _dispatch(h2_ref, cw_ref, gid_ref, utri_ref, slot_ref, meta_ref, hs_ref, stage_ref, cnt_ref, sem_ref)


def _aligned(row):
    return row if isinstance(row, int) else pl.multiple_of(row, HS_ALIGN)


def _window_copy(stage_ref, buf, src_row, hs_ref, dst_row, sem_ref, rows):
    return pltpu.make_async_copy(stage_ref.at[buf, pl.ds(_aligned(src_row), rows), :],
                                 hs_ref.at[pl.ds(_aligned(dst_row), rows), :], sem_ref.at[0])


def _by_window(small, fn):
    pl.when(small)(lambda: fn(SMALL_WIN))
    pl.when(jnp.logical_not(small))(lambda: fn(TM))


def _dispatch(h2_ref, cw_ref, gid_ref, utri_ref, slot_ref, meta_ref, hs_ref, stage_ref, cnt_ref, sem_ref):
    i = pl.program_id(0)

    @pl.when(i == 0)
    def _():
        for g in range(N_GROUPS + 1):
            cnt_ref[g] = 0

    gid = gid_ref[0:1, :]
    grp_i = lax.broadcasted_iota(jnp.int32, (8, TM), 0)
    onehot = jnp.where(grp_i.astype(F32) == gid, 1.0, 0.0)
    rank = _dot(onehot.astype(BF16), utri_ref[...])
    ncol = (rank[:, TM - 1:TM] + onehot[:, TM - 1:TM]).astype(jnp.int32)
    padc = ((ncol + (HS_ALIGN - 1)) // HS_ALIGN) * HS_ALIGN
    row8 = lax.broadcasted_iota(jnp.int32, (8, 1), 0)
    startc = jnp.zeros((8, 1), jnp.int32)
    run = jnp.zeros((1, 1), jnp.int32)
    for g in range(1, N_GROUPS):
        run = run + padc[g - 1:g, :]
        startc = jnp.where(row8 == g, run, startc)
    slot = jnp.sum(onehot * (rank + startc.astype(F32)), axis=0, keepdims=True)
    back = jnp.sum(onehot * (rank + (row8 * TM).astype(F32)), axis=0, keepdims=True)
    slot_ref[...] = jnp.broadcast_to(back, (LANES, TM)).T
    sel = jnp.where(lax.broadcasted_iota(jnp.int32, (STAGE_ROWS, TM), 0).astype(F32) == slot, 1.0, 0.0)
    cw = cw_ref[...]
    cw_hi = cw.astype(BF16)
    r1 = cw - cw_hi.astype(F32)
    cw_mid = r1.astype(BF16)
    cw_lo = (r1 - cw_mid.astype(F32)).astype(BF16)
    rowdata = jnp.concatenate([h2_ref[...], cw_hi, cw_mid, cw_lo], axis=1)
    buf = i % 2
    stage_ref[buf] = _dot(sel.astype(BF16), rowdata).astype(BF16)

    small = jnp.max(ncol, axis=0, keepdims=True)[0, 0] <= SMALL_WIN
    prev_small = cnt_ref[N_GROUPS] == 1

    def wait_windows(rows):
        for _ in range(N_GROUPS):
            _window_copy(stage_ref, buf, 0, hs_ref, 0, sem_ref, rows).wait()

    @pl.when(i > 0)
    def _():
        _by_window(prev_small, wait_windows)

    lane = lax.broadcasted_iota(jnp.int32, (8, LANES), 1)
    meta = jnp.where(lane == 2 * N_GROUPS, small.astype(jnp.int32), jnp.zeros((8, LANES), jnp.int32))
    starts, dsts = [], []
    for g in range(N_GROUPS):
        starts.append(startc[g, 0])
        dsts.append(g * HS_CAP + cnt_ref[g])
        cnt_ref[g] = cnt_ref[g] + padc[g, 0]
        meta = jnp.where(lane == g, dsts[g], meta)
        meta = jnp.where(lane == N_GROUPS + g, cnt_ref[g], meta)
    meta_ref[0] = meta
    cnt_ref[N_GROUPS] = small.astype(jnp.int32)

    def start_windows(rows):
        for g in range(N_GROUPS):
            _window_copy(stage_ref, buf, starts[g], hs_ref, dsts[g], sem_ref, rows).start()

    _by_window(small, start_windows)

    @pl.when(i == NB - 1)
    def _():
        _by_window(small, wait_windows)
        for g in range(N_GROUPS):
            for w in range(2):
                _window_copy(stage_ref, buf, STAGE_ROWS - TM, hs_ref, g * HS_CAP + cnt_ref[g] + w * TM,
                             sem_ref, TM).start()
        for _ in range(2 * N_GROUPS):
            _window_copy(stage_ref, buf, 0, hs_ref, 0, sem_ref, TM).wait()
        ends, total = [], 0
        for g in range(N_GROUPS):
            total = total + lax.shift_right_logical(cnt_ref[g] + (2 * TM - 1), TM.bit_length() - 1)
            ends.append(total)
        step = jnp.minimum(lax.broadcasted_iota(jnp.int32, (1, LANES), 1), total - 1)
        grp = jnp.zeros((1, LANES), jnp.int32)
        first = jnp.zeros((1, LANES), jnp.int32)
        for g in range(N_GROUPS - 1):
            grp = grp + jnp.where(step >= ends[g], 1, 0)
            first = jnp.where(step >= ends[g], ends[g], first)
        meta_ref[0, 1:2, :] = grp * (HS_CAP // TM) + step - first
        meta_ref[0, 2:3, :] = grp
        meta_ref[0, 3:4, :] = jnp.zeros((1, LANES), jnp.int32) + total


def _merge_rows(alpha, rows, xc_ref, xl_ref, mod_ref, of_ref, ob_ref, sag_ref, ng_ref, zu_ref, zv_ref,
                ws_ref, bs_ref, cc_ref, cl_ref, gate_ref, wb_ref, wo_ref, g1_ref, b1_ref,
                rwh_ref, rwl_ref, rb_ref, x1_ref, h2_ref, cw_ref, gid_ref):
    i = pl.program_id(0)
    m = mod_ref[0]
    n_rows = rows.stop - rows.start
    o = of_ref[rows, :] + ob_ref[rows, :]
    parts = []
    for h in range(GLA_HEADS):
        oh = o[:, h * GLA_DV:(h + 1) * GLA_DV]
        parts.append(oh * lax.rsqrt(jnp.mean(oh * oh, axis=-1, keepdims=True) + LN_EPS))
    br_a = (jnp.concatenate(parts, axis=1) * ng_ref[...] * sag_ref[rows, :]).astype(BF16)
    u = zu_ref[rows, :]
    v = zv_ref[rows, :]
    yield
    chunks = []
    for c in range(n_rows // GMLP_CHUNK):
        vc = v[c * GMLP_CHUNK:(c + 1) * GMLP_CHUNK]
        cols = [_dot(ws_ref[g], vc[:, g * GMLP_CH:(g + 1) * GMLP_CH]) for g in range(GMLP_GROUPS)]
        chunks.append(jnp.concatenate(cols, axis=1) + bs_ref[...])
    br_b = (u * jnp.concatenate(chunks, axis=0)).astype(BF16)
    br_ct = jnp.where(i < NB_CTX, cc_ref[:, rows], cl_ref[:, rows])
    projs = (_dot(br_a, wb_ref[0]), _dot(br_b, wb_ref[1]), _dot_tn(br_ct, wb_ref[2]))
    yield
    y = jnp.zeros((n_rows, D_MODEL), F32)
    for nbr, proj in enumerate(projs):
        y = y + gate_ref[rows, nbr * D_MODEL:(nbr + 1) * D_MODEL] * proj
    y = y.astype(BF16)
    yield
    y = _dot(y, wo_ref[...])
    yield
    x = jnp.where(i < NB_CTX, xc_ref[rows, :], xl_ref[rows, :])
    x1 = _layer_norm(alpha * x + m[:, 2 * D_MODEL:3 * D_MODEL] * y, g1_ref[...], b1_ref[...])
    x1_ref[rows, :] = x1
    h2 = x1 * (1.0 + m[:, 4 * D_MODEL:5 * D_MODEL]) + m[:, 3 * D_MODEL:4 * D_MODEL]
    h2_ref[rows, :] = h2.astype(BF16)
    h_hi, h_lo = _split(h2)
    logits = (_dot_nt(rwh_ref[...], h_hi) + _dot_nt(rwh_ref[...], h_lo) + _dot_nt(rwl_ref[...], h_hi)
              + rb_ref[:, :n_rows])[:ROUTER_ROWS]
    row_i = lax.broadcasted_iota(jnp.int32, (ROUTER_ROWS, n_rows), 0)
    row = row_i.astype(F32)
    row_group = (row_i >> 2).astype(F32)
    big = float(LANES)
    neg = -jnp.inf
    gl = jnp.where((row_i >= N_EXPERTS) & (row_i < N_EXPERTS + N_GROUPS), logits, neg)
    gmax = jnp.max(gl, axis=0, keepdims=True)
    gsum = jnp.sum(jnp.exp(gl - gmax), axis=0, keepdims=True)
    g_p = 1.0 / gsum
    g_i = jnp.min(jnp.where(gl == gmax, row, big), axis=0, keepdims=True) - float(N_EXPERTS)
    in_group = (row_i < N_EXPERTS) & (row_group == g_i)
    el = jnp.where(in_group, logits, neg)
    emax = jnp.max(el, axis=0, keepdims=True)
    ee = jnp.exp(el - emax)
    e_prob = ee / jnp.sum(ee, axis=0, keepdims=True)
    p1 = jnp.max(jnp.where(in_group, e_prob, neg), axis=0, keepdims=True)
    i1 = jnp.min(jnp.where(in_group & (e_prob == p1), row, big), axis=0, keepdims=True)
    rest = in_group & (row != i1)
    p2 = jnp.max(jnp.where(rest, e_prob, neg), axis=0, keepdims=True)
    i2 = jnp.min(jnp.where(rest & (e_prob == p2), row, big), axis=0, keepdims=True)
    tot = p1 + p2
    cw_t = (jnp.where(row == i1, g_p * p1 / tot, 0.0) + jnp.where(row == i2, g_p * p2 / tot, 0.0))
    cw_ref[rows, :] = jnp.concatenate([cw_t, jnp.zeros((LANES - ROUTER_ROWS, n_rows), F32)], axis=0).T
    gid_ref[0:1, rows] = g_i
    yield


def _merge(l, alpha, xc, xl, mod3, o_f, o_b, sag, norm_g, zu, zv, ws, bs_t, brc_ctx, brc_lat, gates,
           wb, wo, g1, b1, rw_hi, rw_lo, rb):
    utri = jnp.asarray(np.arange(TM)[:, None] < np.arange(TM)[None, :], dtype=BF16)
    return pl.pallas_call(
        functools.partial(_merge_kernel, alpha),
        grid=(NB,),
        in_specs=[_ctx_tok(D_MODEL), _lat_tok(D_MODEL), _mod_spec(l),
                  _tok(512), _tok(512), _tok(512), _layer_spec(l, (1, 512)),
                  _tok(512), _tok(512),
                  _layer_spec(l, (GMLP_GROUPS, GMLP_CHUNK, GMLP_CHUNK)), _layer_spec(l, (GMLP_CHUNK, 512)),
                  pl.BlockSpec((512, TM), lambda i: (0, jnp.minimum(i, NB_CTX - 1))),
                  pl.BlockSpec((512, TM), lambda i: (0, jnp.maximum(i - NB_CTX, 0))),
                  _tok(3072), _layer_spec(l, (N_BRANCH, BRANCH_W, D_MODEL)), _layer_spec(l, (D_MODEL, D_MODEL)),
                  _layer_spec(l, (1, D_MODEL)), _layer_spec(l, (1, D_MODEL)),
                  _layer_spec(l, (LANES, D_MODEL)), _layer_spec(l, (LANES, D_MODEL)), _layer_spec(l, (LANES, TM)),
                  pl.BlockSpec((TM, TM), lambda i: (0, 0))],
        out_specs=[_tok(D_MODEL), _tok(LANES), pl.BlockSpec((1, 8, LANES), lambda i: (i, 0, 0)),
                   pl.BlockSpec(memory_space=pl.ANY)],
        out_shape=[jax.ShapeDtypeStruct((TOKENS, D_MODEL), F32),
                   jax.ShapeDtypeStruct((TOKENS, LANES), F32),
                   jax.ShapeDtypeStruct((NB, 8, LANES), jnp.int32),
                   jax.ShapeDtypeStruct((N_GROUPS * HS_CAP, HS_W), BF16)],
        scratch_shapes=[pltpu.VMEM((TM, D_MODEL), BF16), pltpu.VMEM((TM, LANES), F32), pltpu.VMEM((8, TM), F32),
                        pltpu.VMEM((2, STAGE_ROWS, HS_W), BF16), pltpu.SMEM((N_GROUPS + 1,), jnp.int32),
                        pltpu.SemaphoreType.DMA((1,))],
        compiler_params=_cp(("arbitrary",)),
        name="merge_ln1_router",
    )(xc, xl, mod3, o_f, o_b, sag, norm_g, zu, zv, ws, bs_t, brc_ctx, brc_lat, gates,
      wb, wo, g1, b1, rw_hi, rw_lo, rb, utri)


def _experts_kernel(blk_ref, grp_ref, ntile_ref, hs_ref, wg32_ref, wu32_ref, wd32_ref, ys_ref,
                    wg_ref, wu_ref, wd_ref):
    s = pl.program_id(0)

    @pl.when(jnp.logical_or(s == 0, grp_ref[s] != grp_ref[jnp.maximum(s - 1, 0)]))
    def _():
        wg_ref[...] = wg32_ref[...].astype(BF16)
        wu_ref[...] = wu32_ref[...].astype(BF16)
        wd_ref[...] = wd32_ref[...].astype(BF16)

    @pl.when(s < ntile_ref[0])
    def _():
        g = grp_ref[s]
        x = hs_ref[:, :D_MODEL]
        cw = (hs_ref[:, D_MODEL:D_MODEL + LANES].astype(F32)
              + hs_ref[:, D_MODEL + LANES:D_MODEL + 2 * LANES].astype(F32)
              + hs_ref[:, D_MODEL + 2 * LANES:].astype(F32))
        lane = lax.broadcasted_iota(jnp.int32, (TM, LANES), 1)
        hid = []
        for k in range(EXPERTS_PER_GROUP):
            w_e = jnp.sum(jnp.where(lane == g * EXPERTS_PER_GROUP + k, cw, 0.0), axis=-1, keepdims=True)
            hid.append((_silu(_dot(x, wg_ref[k])) * _dot(x, wu_ref[k]) * w_e).astype(BF16))
        y = _dot(jnp.concatenate(hid, axis=1), wd_ref[...].reshape(EXPERTS_PER_GROUP * EXPERT_FF, D_MODEL))
        ys_ref[...] = y.astype(BF16)


def _experts(l, tile_blk, tile_grp, n_tiles, hs, wg, wu, wd):
    e = EXPERTS_PER_GROUP
    return pl.pallas_call(
        _experts_kernel,
        grid_spec=pltpu.PrefetchScalarGridSpec(
            num_scalar_prefetch=3,
            grid=(MOE_TILES,),
            in_specs=[pl.BlockSpec((TM, HS_W), lambda s, blk, grp, nt: (blk[s], 0)),
                      pl.BlockSpec((None, e, D_MODEL, EXPERT_FF), lambda s, blk, grp, nt: (l, grp[s], 0, 0)),
                      pl.BlockSpec((None, e, D_MODEL, EXPERT_FF), lambda s, blk, grp, nt: (l, grp[s], 0, 0)),
                      pl.BlockSpec((None, e, EXPERT_FF, D_MODEL), lambda s, blk, grp, nt: (l, grp[s], 0, 0))],
            out_specs=pl.BlockSpec((TM, D_MODEL), lambda s, blk, grp, nt: (blk[s], 0)),
            scratch_shapes=[pltpu.VMEM((e, D_MODEL, EXPERT_FF), BF16), pltpu.VMEM((e, D_MODEL, EXPERT_FF), BF16),
                            pltpu.VMEM((e, EXPERT_FF, D_MODEL), BF16)],
        ),
        out_shape=jax.ShapeDtypeStruct((N_GROUPS * HS_CAP, D_MODEL), BF16),
        compiler_params=_cp(("arbitrary",)),
        name="moe_experts",
    )(tile_blk, tile_grp, n_tiles, hs, wg, wu, wd)


def _window_fetch(ys_ref, offs_ref, blk, win_ref, buf, sem_ref, rows):
    return [pltpu.make_async_copy(
        ys_ref.at[pl.ds(pl.multiple_of(offs_ref[blk * N_GROUPS + g], HS_ALIGN), rows), :],
        win_ref.at[buf, pl.ds(g * rows, rows), :], sem_ref.at[buf]) for g in range(N_GROUPS)]


def _combine_kernel(alpha, offs_ref, small_ref, ys_ref, slot_ref, x1_ref, mod_ref, g2_ref, b2_ref,
                    oc_ref, ol_ref, win_ref, y_ref, sem_ref):
    i = pl.program_id(0)
    buf = i % 2

    def start(blk, to_buf):
        def go(rows):
            for cp in _window_fetch(ys_ref, offs_ref, blk, win_ref, to_buf, sem_ref, rows):
                cp.start()
        _by_window(small_ref[blk] == 1, go)

    pl.when(i == 0)(lambda: start(0, 0))
    pl.when(i + 1 < NB)(lambda: start(i + 1, 1 - buf))

    def gather(rows):
        for cp in _window_fetch(ys_ref, offs_ref, i, win_ref, buf, sem_ref, rows):
            cp.wait()
        back = slot_ref[...]
        packed = back - jnp.floor(back * (1.0 / TM)) * float(TM - rows)
        slot = jnp.concatenate([packed] * (N_GROUPS * rows // LANES), axis=1)
        lane = lax.broadcasted_iota(jnp.int32, (TM, N_GROUPS * rows), 1).astype(F32)
        sel = jnp.where(lane == slot, 1.0, 0.0).astype(BF16)
        y_ref[...] = _dot(sel, win_ref[buf, :N_GROUPS * rows, :])

    _by_window(small_ref[i] == 1, gather)
    m = mod_ref[0]
    out = _layer_norm(alpha * x1_ref[...] + m[:, 5 * D_MODEL:6 * D_MODEL] * y_ref[...],
                      g2_ref[...], b2_ref[...])

    @pl.when(i < NB_CTX)
    def _():
        oc_ref[...] = out

    @pl.when(i >= NB_CTX)
    def _():
        ol_ref[...] = out


def _combine(l, alpha, offs, small, ys, slot, x1, mod3, g2, b2):
    return pl.pallas_call(
        functools.partial(_combine_kernel, alpha),
        grid_spec=pltpu.PrefetchScalarGridSpec(
            num_scalar_prefetch=2,
            grid=(NB,),
            in_specs=[pl.BlockSpec(memory_space=pl.ANY),
                      pl.BlockSpec((TM, LANES), lambda i, *_: (i, 0)),
                      pl.BlockSpec((TM, D_MODEL), lambda i, *_: (i, 0)),
                      _mod_spec(l),
                      pl.BlockSpec((None, 1, D_MODEL), lambda i, *_: (l, 0, 0)),
                      pl.BlockSpec((None, 1, D_MODEL), lambda i, *_: (l, 0, 0))],
            out_specs=[pl.BlockSpec((TM, D_MODEL), lambda i, *_: (jnp.minimum(i, NB_CTX - 1), 0)),
                       pl.BlockSpec((TM, D_MODEL), lambda i, *_: (jnp.maximum(i - NB_CTX, 0), 0))],
            scratch_shapes=[pltpu.VMEM((2, N_GROUPS * TM, D_MODEL), BF16), pltpu.VMEM((TM, D_MODEL), F32),
                            pltpu.SemaphoreType.DMA((2,))],
        ),
        out_shape=[jax.ShapeDtypeStruct((CTX_TOKENS, D_MODEL), F32),
                   jax.ShapeDtypeStruct((LAT_TOKENS, D_MODEL), F32)],
        compiler_params=_cp(("arbitrary",)),
        name="combine_ln2",
    )(offs, small, ys, slot, x1, mod3, g2, b2)


def _rope_tables():
    f32 = np.float32
    pos = np.arange(DEC_SEQ)
    row = (pos // GRID_W).astype(f32)
    col = (pos % GRID_W).astype(f32)
    quarter = ATTN_HEAD_DIM // 4
    inv_freq = np.power(f32(ROPE_BASE), -np.arange(quarter, dtype=f32) / f32(quarter)).astype(f32)
    j = np.arange(ATTN_HEAD_DIM)
    p = np.where((j // (2 * quarter))[None, :] == 0, row[:, None], col[:, None]).astype(f32)
    ang = (p * inv_freq[j % quarter][None, :]).astype(f32)
    sign = np.where((j % (2 * quarter)) < quarter, -1.0, 1.0).astype(f32)
    cos = np.tile(np.cos(ang).astype(f32), (1, ATTN_HEADS))
    sin = np.tile((np.sin(ang) * sign[None, :]).astype(f32), (1, ATTN_HEADS))
    cos = np.concatenate([np.ones((TM, 512), f32), cos], axis=0)
    sin = np.concatenate([np.zeros((TM, 512), f32), sin], axis=0)
    return jnp.asarray(cos), jnp.asarray(sin)


def _dup_heads(t):
    h0, h1 = t[..., :ATTN_HEAD_DIM], t[..., ATTN_HEAD_DIM:]
    return jnp.concatenate([h0, h0, h1, h1], axis=-1)


def _split_w_in(w):
    gate_scale = np.where(np.arange(IN_COLS) >= W_MG.start, 0.5, 1.0).astype(np.float32)
    return [(jnp.swapaxes(w, 1, 2) * gate_scale[:, None]).astype(BF16)]


def kernel(x_prompt, x_sample, state_gla, cache_k, cache_v, c, c_ctx, ada_w, ada_b, w_in, gla_gate_w, gla_gate_b, gla_norm_g, gmlp_ln_g, gmlp_ln_b, gmlp_ws, gmlp_bs, attn_sink, w_branch, w_out, ln1_g, ln1_b, ln2_g, ln2_b, router_group_w, router_group_b, router_expert_w, router_expert_b, expert_w_gate, expert_w_up, expert_w_down):
    alpha = (2.0 * DEPTH) ** 0.25
    xc = x_prompt.reshape(CTX_TOKENS, D_MODEL)
    xl = x_sample.reshape(LAT_TOKENS, D_MODEL)
    cond = jnp.concatenate([c_ctx[None, :], c, jnp.zeros((MOD_ROWS - 1 - DEC_BATCH, D_MODEL), F32)], axis=0)
    cos_t, sin_t = _rope_tables()

    ada_b3 = ada_b[:, None, :]
    w_parts = _split_w_in(w_in)
    zrow = jnp.zeros((DEPTH, GLA_RANK, 256), F32)
    zpad = jnp.zeros((DEPTH, LANES - 2 * GLA_RANK, 256), F32)
    gw_p = jnp.stack([jnp.concatenate([gla_gate_w[:, 0], zrow, zpad], axis=1),
                      jnp.concatenate([zrow, gla_gate_w[:, 1], zpad], axis=1)], axis=1).astype(BF16)
    gb_p = gla_gate_b[:, :, None, :]
    sink_t = jnp.broadcast_to(attn_sink[:, :, None, None], (DEPTH, ATTN_HEADS, 1, LANES))
    kc = _dup_heads(cache_k.reshape(DEC_BATCH, DEPTH, PAST_LEN, 128)).astype(BF16)
    vct = jnp.swapaxes(cache_v.reshape(DEC_BATCH, DEPTH, PAST_LEN, 128), 2, 3).astype(BF16)
    rw = jnp.swapaxes(jnp.concatenate(
        [router_expert_w, router_group_w,
         jnp.zeros((DEPTH, D_MODEL, LANES - N_EXPERTS - N_GROUPS), F32)], axis=2), 1, 2)
    rw_hi = rw.astype(BF16)
    rw_lo = (rw - rw_hi.astype(F32)).astype(BF16)
    rb = jnp.concatenate([router_expert_b, router_group_b,
                          jnp.zeros((DEPTH, LANES - N_EXPERTS - N_GROUPS), F32)], axis=1)
    rb = jnp.broadcast_to(rb[:, :, None], (DEPTH, LANES, TM))
    bs_t = jnp.repeat(jnp.swapaxes(gmlp_bs, 1, 2), GMLP_CH, axis=2)
    ws_b = gmlp_ws.astype(BF16)
    wb_b = (0.5 * w_branch).astype(BF16)
    wo_b = w_out.astype(BF16)
    row = lambda t: t[:, None, :]

    mod3 = _modulation(cond, ada_w, ada_b3).reshape(DEPTH, MOD_ROWS, 1, 6 * D_MODEL)
    states, keys, values = [], [], []
    for l in range(DEPTH):
        (qef, kef, klf, qeb, keb, klb, dec, av, avt, sag, zu, zv, cq, ckd, cvt, ck, cv, gates) = _in_projection(
            l, xc, xl, mod3, cos_t, sin_t, gw_p, gb_p, row(gmlp_ln_g), row(gmlp_ln_b), w_parts)
        o_f, o_b, s_fin = _gla(l, qef, kef, klf, qeb, keb, klb, dec, av, avt, state_gla)
        brc_ctx = _context_attention(l, cq, ckd, cvt, sink_t)
        brc_lat = _latent_attention(l, cq, ckd, cvt, kc, vct, sink_t)
        x1, slot, meta, hs = _merge(l, alpha, xc, xl, mod3, o_f, o_b, sag, row(gla_norm_g), zu, zv,
                                    ws_b, bs_t, brc_ctx, brc_lat, gates, wb_b,
                                    wo_b, row(ln1_g), row(ln1_b), rw_hi, rw_lo, rb)
        offs = meta[:, 0, :N_GROUPS].reshape(NB * N_GROUPS)
        tiles = meta[NB - 1]
        ys = _experts(l, tiles[1, :MOE_TILES], tiles[2, :MOE_TILES], tiles[3, :1], hs,
                      expert_w_gate, expert_w_up, expert_w_down)
        xc, xl = _combine(l, alpha, offs, meta[:, 0, 2 * N_GROUPS], ys, slot, x1, mod3,
                          row(ln2_g), row(ln2_b))
        states.append(s_fin)
        keys.append(ck.reshape(BATCH, SEQ, ATTN_KV_HEADS, ATTN_HEAD_DIM))
        values.append(cv.reshape(BATCH, SEQ, ATTN_KV_HEADS, ATTN_HEAD_DIM))
    return (xc.reshape(BATCH, SEQ, D_MODEL), xl.reshape(DEC_BATCH, DEC_SEQ, D_MODEL),
            jnp.stack(states, axis=1), jnp.stack(keys, axis=1), jnp.stack(values, axis=1))
```

```python
import functools

import jax
import jax.numpy as jnp
import numpy as np
from jax import lax
from jax.experimental import pallas as pl
from jax.experimental.pallas import tpu as pltpu

F32 = jnp.float32
BF16 = jnp.bfloat16

D_MODEL = 1024
BATCH = 32
SEQ = 256
DEPTH = 2
DEC_BATCH = 8
DEC_SEQ = 1024
PAST_LEN = 512
GRID_W = 64
GLA_HEADS = 4
GLA_DV = 128
GLA_DK = 64
GLA_RANK = 16
GLA_TAU = 16.0
GLA_CHUNK = 32
GMLP_CHUNK = 128
GMLP_CH = 128
GMLP_GROUPS = 4
ATTN_HEADS = 8
ATTN_KV_HEADS = 2
ATTN_GROUP = ATTN_HEADS // ATTN_KV_HEADS
ATTN_HEAD_DIM = 64
WINDOW = 128
ATTN_BLOCK = 128
ROPE_BASE = 10000.0
BRANCH_W = 512
N_BRANCH = 3
N_GROUPS = 4
EXPERTS_PER_GROUP = 4
N_EXPERTS = 16
EXPERT_FF = 256
LN_EPS = 1e-5

LANES = 128
TM = 256
CTX_TOKENS = BATCH * SEQ
LAT_TOKENS = DEC_BATCH * DEC_SEQ
TOKENS = CTX_TOKENS + LAT_TOKENS
NB_CTX = CTX_TOKENS // TM
NB_LAT = LAT_TOKENS // TM
NB = NB_CTX + NB_LAT
LAT_BLOCKS_PER_SEQ = DEC_SEQ // TM
MOD_ROWS = 16
GLA_SUB = 128
CHUNKS_PER_SUB = GLA_SUB // GLA_CHUNK
GLA_STAGES = 3
MERGE_ROWS = 128
MERGE_STAGES = 5
ROUTER_ROWS = 24
HS_ALIGN = 16
SMALL_WIN = 128
HS_W = D_MODEL + 3 * LANES
STAGE_ROWS = TM + N_GROUPS * HS_ALIGN + TM
_MAX_PAD = NB * (HS_ALIGN - 1)
HS_CAP = -(-(TOKENS + _MAX_PAD + 2 * TM) // TM) * TM
MOE_TILES = (TOKENS + N_GROUPS * _MAX_PAD) // TM + 2 * N_GROUPS

IN_COLS = 6432
W_AQ, W_AK, W_AV, W_AG = slice(0, 256), slice(256, 512), slice(512, 1024), slice(1024, 1536)
W_LR = slice(1536, 1664)
W_BZ, W_CQ, W_CK, W_CV = slice(1568, 2592), slice(2592, 3104), slice(3104, 3232), slice(3232, 3360)
W_MG = slice(3360, 6432)

VMEM_LIMIT = 56 * 1024 * 1024


def _cp(sem):
    return pltpu.CompilerParams(dimension_semantics=sem, vmem_limit_bytes=VMEM_LIMIT)


def _dot(a, b):
    return jnp.dot(a, b, preferred_element_type=F32)


def _dot_nt(a, b):
    return lax.dot_general(a, b, (((1,), (1,)), ((), ())), preferred_element_type=F32)


def _dot_tn(a, b):
    return lax.dot_general(a, b, (((0,), (0,)), ((), ())), preferred_element_type=F32)


def _split(x):
    hi = x.astype(BF16)
    lo = (x - hi.astype(F32)).astype(BF16)
    return hi, lo


def _sigmoid(x):
    return 0.5 * (jnp.tanh(0.5 * x) + 1.0)


def _silu(x):
    return x * _sigmoid(x)


def _layer_norm(x, g, b):
    mu = jnp.mean(x, axis=-1, keepdims=True)
    xc = x - mu
    var = jnp.mean(xc * xc, axis=-1, keepdims=True)
    return xc * lax.rsqrt(var + LN_EPS) * g + b


def _stagger(parts, n_stages):
    for step in range(n_stages + len(parts) - 1):
        for lag, part in enumerate(parts):
            if 0 <= step - lag < n_stages:
                next(part)


def _mod_block(i):
    return jnp.where(i < NB_CTX, 0, 1 + (i - NB_CTX) // LAT_BLOCKS_PER_SEQ)


def _tok(width):
    return pl.BlockSpec((TM, width), lambda i: (i, 0))


def _ctx_tok(width):
    return pl.BlockSpec((TM, width), lambda i: (jnp.minimum(i, NB_CTX - 1), 0))


def _lat_tok(width):
    return pl.BlockSpec((TM, width), lambda i: (jnp.maximum(i - NB_CTX, 0), 0))


def _layer_spec(l, shape):
    return pl.BlockSpec((None,) + shape, lambda *_: (l,) + (0,) * len(shape))


def _mod_spec(l):
    return pl.BlockSpec((None, 1, 1, 6 * D_MODEL), lambda i, *_: (l, _mod_block(i), 0, 0))


def _mod_kernel(cond_ref, w_ref, b_ref, o_ref):
    s_hi, s_lo = _split(_silu(cond_ref[...]))
    w_hi, w_lo = _split(w_ref[...])
    o_ref[...] = _dot(s_hi, w_hi) + _dot(s_lo, w_hi) + _dot(s_hi, w_lo) + b_ref[...]


def _modulation(cond, ada_w, ada_b3):
    tn = 1536
    return pl.pallas_call(
        _mod_kernel,
        grid=(DEPTH, 6 * D_MODEL // tn),
        in_specs=[pl.BlockSpec((MOD_ROWS, D_MODEL), lambda l, j: (0, 0)),
                  pl.BlockSpec((None, D_MODEL, tn), lambda l, j: (l, 0, j)),
                  pl.BlockSpec((None, 1, tn), lambda l, j: (l, 0, j))],
        out_specs=pl.BlockSpec((None, MOD_ROWS, tn), lambda l, j: (l, 0, j)),
        out_shape=jax.ShapeDtypeStruct((DEPTH, MOD_ROWS, 6 * D_MODEL), F32),
        compiler_params=_cp(("arbitrary", "arbitrary")),
        name="modulation",
    )(cond, ada_w, ada_b3)


def _rope(x, cos, sin):
    n = x.shape[1]
    lane = lax.broadcasted_iota(jnp.int32, x.shape, 1)
    partner = jnp.where((lane & 31) < 16, pltpu.roll(x, n - 16, 1), pltpu.roll(x, 16, 1))
    return x * cos + partner * sin


def _inproj_kernel(xc_ref, xl_ref, mod_ref, cos_ref, sin_ref, gw_ref, gb_ref, lg_ref, lb_ref,
                   wt_ref,
                   qef_ref, kef_ref, klf_ref, qeb_ref, keb_ref, klb_ref, dec_ref,
                   av_ref, avt_ref, sag_ref, zu_ref, zv_ref,
                   cq_ref, ckd_ref, cvt_ref, ck_ref, cv_ref, gate_ref):
    i = pl.program_id(0)
    m = mod_ref[0]
    x = jnp.where(i < NB_CTX, xc_ref[...], xl_ref[...])
    h = (x * (1.0 + m[:, D_MODEL:2 * D_MODEL]) + m[:, 0:D_MODEL]).astype(BF16)

    def proj(rows):
        return _dot_nt(h, wt_ref[rows, :])

    aq = proj(W_AQ)
    ak = proj(W_AK)
    lr = proj(W_LR).astype(BF16)
    zs = [_dot(lr, gw_ref[d]) + gb_ref[d] for d in range(2)]
    for nbr in range(N_BRANCH):
        cols = slice(W_MG.start + nbr * D_MODEL, W_MG.start + (nbr + 1) * D_MODEL)
        gate_ref[:, nbr * D_MODEL:(nbr + 1) * D_MODEL] = jnp.tanh(proj(cols)) + 1.0
    r = lax.broadcasted_iota(jnp.int32, (TM, TM), 0)
    c = lax.broadcasted_iota(jnp.int32, (TM, TM), 1)
    same = (r >> 5) == (c >> 5)
    bs = []
    for d in range(2):
        g = (jnp.minimum(zs[d], 0.0) - jnp.log(1.0 + jnp.exp(-jnp.abs(zs[d])))) / GLA_TAU
        tri_b = jnp.where(same & ((c <= r) if d == 0 else (c >= r)), 1.0, 0.0).astype(BF16)
        g_hi, g_lo = _split(g)
        bs.append(_dot(tri_b, g_hi) + _dot(tri_b, g_lo))
    z = _gelu(proj(W_BZ))
    zu_ref[...] = z[:, :BRANCH_W]
    zv_ref[...] = _layer_norm(z[:, BRANCH_W:], lg_ref[...], lb_ref[...]).astype(BF16)
    sag_ref[...] = _silu(proj(W_AG))
    av = proj(W_AV)
    av_ref[...] = av.astype(BF16)
    avt_ref[...] = av.T.astype(BF16)
    decs = []
    for d, (qe_ref, ke_ref, kl_ref) in enumerate(((qef_ref, kef_ref, klf_ref), (qeb_ref, keb_ref, klb_ref))):
        b = bs[d]
        last = GLA_CHUNK - 1 if d == 0 else 0
        tot = [b[n * GLA_CHUNK + last:n * GLA_CHUNK + last + 1, :] for n in range(TM // GLA_CHUNK)]
        bl = jnp.concatenate([jnp.broadcast_to(t, (GLA_CHUNK, t.shape[1])) for t in tot], axis=0)
        qe_ref[...] = (aq * jnp.exp(b) * (GLA_DK ** -0.5)).astype(BF16)
        ke_ref[...] = (ak * jnp.exp(-b)).astype(BF16)
        kl_ref[...] = (ak * jnp.exp(bl - b)).astype(BF16)
        decs.append(jnp.exp(jnp.concatenate(tot, axis=0)))
    dec_ref[...] = jnp.concatenate(decs, axis=1)

    cos = cos_ref[...]
    sin = sin_ref[...]
    cq_ref[...] = (_rope(proj(W_CQ), cos, sin) * (ATTN_HEAD_DIM ** -0.5)).astype(BF16)
    ck = proj(W_CK)
    cv = proj(W_CV)
    kr = _rope(ck, cos[:, :LANES], sin[:, :LANES])
    swapped = pltpu.roll(kr, ATTN_HEAD_DIM, 1)
    lane = lax.broadcasted_iota(jnp.int32, (TM, LANES), 1)
    ckd_ref[...] = jnp.concatenate([jnp.where(lane < 64, kr, swapped), jnp.where(lane < 64, swapped, kr)],
                                   axis=1).astype(BF16)
    cvt_ref[...] = cv.T.astype(BF16)

    @pl.when(i < NB_CTX)
    def _():
        ck_ref[...] = ck
        cv_ref[...] = cv


def _rope_block(i):
    return jnp.where(i < NB_CTX, 0, 1 + (i - NB_CTX) % LAT_BLOCKS_PER_SEQ)


def _in_projection(l, xc, xl, mod3, cos_t, sin_t, gw_p, gb_p, ln_g, ln_b, w_parts):
    def tok_out(width, dt):
        return _tok(width), jax.ShapeDtypeStruct((TOKENS, width), dt)

    def feat_out(width):
        return (pl.BlockSpec((width, TM), lambda i: (0, i)), jax.ShapeDtypeStruct((width, TOKENS), BF16))

    def ctx_out(width):
        return _ctx_tok(width), jax.ShapeDtypeStruct((CTX_TOKENS, width), F32)

    dec_out = (pl.BlockSpec((TM // GLA_CHUNK, 512), lambda i: (i, 0)),
               jax.ShapeDtypeStruct((TOKENS // GLA_CHUNK, 512), F32))
    outs = [tok_out(256, BF16)] * 6 + [dec_out] + [
        tok_out(512, BF16), feat_out(512), tok_out(512, F32), tok_out(512, F32), tok_out(512, BF16),
        tok_out(512, BF16), tok_out(256, BF16), feat_out(LANES), ctx_out(LANES), ctx_out(LANES),
        tok_out(3072, F32)]
    return pl.pallas_call(
        _inproj_kernel,
        grid=(NB,),
        in_specs=[_ctx_tok(D_MODEL), _lat_tok(D_MODEL), _mod_spec(l),
                  pl.BlockSpec((TM, 512), lambda i: (_rope_block(i), 0)),
                  pl.BlockSpec((TM, 512), lambda i: (_rope_block(i), 0)),
                  _layer_spec(l, (2, LANES, 256)), _layer_spec(l, (2, 1, 256)),
                  _layer_spec(l, (1, 512)), _layer_spec(l, (1, 512))]
                 + [_layer_spec(l, w.shape[1:]) for w in w_parts],
        out_specs=[o[0] for o in outs],
        out_shape=[o[1] for o in outs],
        compiler_params=_cp(("arbitrary",)),
        name="in_projection",
    )(xc, xl, mod3, cos_t, sin_t, gw_p, gb_p, ln_g, ln_b, *w_parts)


def _gla_direction(qe, ke, kl, v, vt, dec, masks, state_ref, reverse, o_ref, rows):
    tri, half_masks, chunk_mask, head_diag = masks
    tri = tri[int(reverse)]
    pairs = range(GLA_HEADS // 2)
    halves, q_exps, kvts = [], [], []
    for p in pairs:
        ps = slice(p * LANES, (p + 1) * LANES)
        qp = qe[:, ps]
        vp = v[:, p * 2 * GLA_DV:(p + 1) * 2 * GLA_DV]
        for hh in range(2):
            s = jnp.where(tri, _dot_nt(qp * half_masks[hh], ke[:, ps]), 0.0).astype(BF16)
            halves.append(_dot(s, vp[:, hh * GLA_DV:(hh + 1) * GLA_DV]))
        k_exp = jnp.concatenate([kl[:, ps]] * CHUNKS_PER_SUB, axis=1) * chunk_mask
        q_exps.append(jnp.concatenate([qp] * CHUNKS_PER_SUB, axis=1) * chunk_mask)
        kvts.append(_dot(vt[p * 2 * GLA_DV:(p + 1) * 2 * GLA_DV, :], k_exp))
    yield
    st_stacks = []
    for p in pairs:
        ps = slice(p * LANES, (p + 1) * LANES)
        st = state_ref[p]
        entering = [None] * CHUNKS_PER_SUB
        order = range(CHUNKS_PER_SUB - 1, -1, -1) if reverse else range(CHUNKS_PER_SUB)
        for n in order:
            entering[n] = st
            st = dec[n:n + 1, ps] * st + jnp.where(head_diag, kvts[p][:, n * LANES:(n + 1) * LANES], 0.0)
        state_ref[p] = st
        st_stacks.append(jnp.concatenate(entering, axis=1).astype(BF16))
    yield
    outs = []
    for p in pairs:
        inter = _dot_nt(q_exps[p], st_stacks[p])
        outs.append(halves[2 * p] + inter[:, :GLA_DV])
        outs.append(halves[2 * p + 1] + inter[:, GLA_DV:])
    o_ref[rows, :] = jnp.concatenate(outs, axis=1)
    yield


def _gla_mask_constants():
    r = np.arange(GLA_SUB)[:, None]
    c = np.arange(GLA_SUB)[None, :]
    same = (r // GLA_CHUNK) == (c // GLA_CHUNK)
    tri = np.stack([same & (c <= r), same & (c >= r)]).astype(np.float32)
    lane = np.arange(LANES)[None, :]
    halves = np.stack([np.broadcast_to(lane < 64, (GLA_SUB, LANES)),
                       np.broadcast_to(lane >= 64, (GLA_SUB, LANES))]).astype(np.float32)
    col_chunk = np.arange(CHUNKS_PER_SUB * LANES)[None, :] // LANES
    chunk_mask = ((r // GLA_CHUNK) == col_chunk).astype(np.float32)
    head_diag = ((np.arange(2 * GLA_DV)[:, None] // GLA_DV) == (lane // GLA_DK)).astype(np.float32)
    return (jnp.asarray(tri), jnp.asarray(halves, dtype=BF16), jnp.asarray(chunk_mask, dtype=BF16),
            jnp.asarray(head_diag))


def _gla_kernel(tri_ref, half_ref, cmask_ref, hdiag_ref,
                qf_ref, kef_ref, klf_ref, vf_ref, vtf_ref, decf_ref,
                qb_ref, keb_ref, klb_ref, vb_ref, vtb_ref, decb_ref,
                s0_ref, of_ref, ob_ref, sfin_ref, state_ref):
    i = pl.program_id(0)
    @pl.when(i < NB_CTX)
    def _():
        state_ref[...] = jnp.zeros_like(state_ref)

    @pl.when(jnp.logical_and(i >= NB_CTX, (i - NB_CTX) % LAT_BLOCKS_PER_SEQ == 0))
    def _():
        zero = jnp.zeros((GLA_DK, GLA_DV), F32)
        for d in range(2):
            for p in range(2):
                pair = jnp.concatenate(
                    [jnp.concatenate([s0_ref[0, d, 2 * p], zero], axis=1),
                     jnp.concatenate([zero, s0_ref[0, d, 2 * p + 1]], axis=1)], axis=0)
                state_ref[d, p] = pair.T

    masks = ((tri_ref[0] > 0.5, tri_ref[1] > 0.5), (half_ref[0], half_ref[1]), cmask_ref[...],
             hdiag_ref[...] > 0.5)
    n_sub = TM // GLA_SUB
    parts = []
    for j in range(n_sub):
        rs = slice(j * GLA_SUB, (j + 1) * GLA_SUB)
        parts.append(_gla_direction(
            qf_ref[rs, :], kef_ref[rs, :], klf_ref[rs, :], vf_ref[rs, :], vtf_ref[:, rs],
            decf_ref[j * CHUNKS_PER_SUB:(j + 1) * CHUNKS_PER_SUB, :256], masks, state_ref.at[0], False,
            of_ref, rs))
        jb = n_sub - 1 - j
        rb = slice(jb * GLA_SUB, (jb + 1) * GLA_SUB)
        parts.append(_gla_direction(
            qb_ref[rb, :], keb_ref[rb, :], klb_ref[rb, :], vb_ref[rb, :], vtb_ref[:, rb],
            decb_ref[jb * CHUNKS_PER_SUB:(jb + 1) * CHUNKS_PER_SUB, 256:], masks, state_ref.at[1], True,
            ob_ref, rb))
    _stagger(parts, GLA_STAGES)

    @pl.when(i < NB_CTX)
    def _():
        for d in range(2):
            for p in range(2):
                sp = state_ref[d, p].T
                sfin_ref[0, d, 2 * p] = sp[:GLA_DK, :GLA_DV]
                sfin_ref[0, d, 2 * p + 1] = sp[GLA_DK:, GLA_DV:]


def _bwd_block(i):
    k = (i - NB_CTX) % LAT_BLOCKS_PER_SEQ
    return jnp.where(i < NB_CTX, i, i - k + (LAT_BLOCKS_PER_SEQ - 1 - k))


def _gla(l, qef, kef, klf, qeb, keb, klb, dec, av, avt, state_gla):
    def bwd(width):
        return pl.BlockSpec((TM, width), lambda i: (_bwd_block(i), 0))

    def lat_seq(i):
        return jnp.maximum(i - NB_CTX, 0) // LAT_BLOCKS_PER_SEQ

    vt_fwd = pl.BlockSpec((512, TM), lambda i: (0, i))
    vt_bwd = pl.BlockSpec((512, TM), lambda i: (0, _bwd_block(i)))
    dec_fwd = pl.BlockSpec((TM // GLA_CHUNK, 512), lambda i: (i, 0))
    dec_bwd = pl.BlockSpec((TM // GLA_CHUNK, 512), lambda i: (_bwd_block(i), 0))
    state_dims = (2, GLA_HEADS, GLA_DK, GLA_DV)
    consts = _gla_mask_constants()
    return pl.pallas_call(
        _gla_kernel,
        grid=(NB,),
        in_specs=[pl.BlockSpec(t.shape, lambda i, nd=t.ndim: (0,) * nd) for t in consts]
                 + [_tok(256), _tok(256), _tok(256), _tok(512), vt_fwd, dec_fwd,
                  bwd(256), bwd(256), bwd(256), bwd(512), vt_bwd, dec_bwd,
                  pl.BlockSpec((1, None) + state_dims, lambda i: (lat_seq(i), l, 0, 0, 0, 0))],
        out_specs=[_tok(512), bwd(512),
                   pl.BlockSpec((1,) + state_dims, lambda i: (jnp.minimum(i, NB_CTX - 1), 0, 0, 0, 0))],
        out_shape=[jax.ShapeDtypeStruct((TOKENS, 512), F32), jax.ShapeDtypeStruct((TOKENS, 512), F32),
                   jax.ShapeDtypeStruct((BATCH,) + state_dims, F32)],
        scratch_shapes=[pltpu.VMEM((2, 2, 2 * GLA_DV, LANES), F32)],
        compiler_params=_cp(("arbitrary",)),
        name="gla_scan",
    )(*consts, qef, kef, klf, av, avt, dec, qeb, keb, klb, av, avt, dec, state_gla)


def _attend_group(q_refs, keys, values_t, masks, sink_ref, g, o_ref):
    rows = q_refs[0].shape[0]
    lane = lax.broadcasted_iota(jnp.int32, (rows, LANES), 1)
    stacked = []
    for q in q_refs:
        q32 = q.astype(F32)
        stacked.append(jnp.where(lane < 64, q32, 0.0).astype(BF16))
        stacked.append(jnp.where(lane >= 64, q32, 0.0).astype(BF16))
    qs = jnp.concatenate(stacked, axis=0)
    col_head = lax.broadcasted_iota(jnp.int32, (1, ATTN_GROUP * rows), 1) // rows
    sink = jnp.zeros((1, ATTN_GROUP * rows), F32)
    for hh in range(ATTN_GROUP):
        sink = jnp.where(col_head == hh, sink_ref[ATTN_GROUP * g + hh][:, :1], sink)
    ss = []
    m = sink
    for kk, mk in zip(keys, masks):
        s = _dot_nt(kk, qs)
        if mk is not None:
            s = jnp.where(jnp.concatenate([mk] * ATTN_GROUP, axis=1), s, -jnp.inf)
        ss.append(s)
        m = jnp.maximum(m, jnp.max(s, axis=0, keepdims=True))
    den = jnp.exp(sink - m)
    acc = jnp.zeros((ATTN_HEAD_DIM, ATTN_GROUP * rows), F32)
    for s, vt in zip(ss, values_t):
        e = jnp.exp(s - m)
        den = den + jnp.sum(e, axis=0, keepdims=True)
        acc = acc + _dot(vt, e.astype(BF16))
    o = (acc / den).astype(BF16)
    for hh in range(ATTN_GROUP):
        h = ATTN_GROUP * g + hh
        o_ref[h * ATTN_HEAD_DIM:(h + 1) * ATTN_HEAD_DIM, :] = o[:, hh * rows:(hh + 1) * rows]


def _ctx_attn_kernel(q_ref, kd_ref, vt_ref, sink_ref, o_ref):
    for g in range(ATTN_KV_HEADS):
        gl = slice(g * LANES, (g + 1) * LANES)
        q_refs = [q_ref[:, (2 * g + pp) * LANES:(2 * g + pp + 1) * LANES] for pp in range(2)]
        vt = vt_ref[g * ATTN_HEAD_DIM:(g + 1) * ATTN_HEAD_DIM, :]
        _attend_group(q_refs, [kd_ref[:, gl]], [vt], [None], sink_ref, g, o_ref)


def _context_attention(l, cq, ckd, cvt, sink_t):
    return pl.pallas_call(
        _ctx_attn_kernel,
        grid=(BATCH,),
        in_specs=[pl.BlockSpec((SEQ, 512), lambda b: (b, 0)),
                  pl.BlockSpec((SEQ, 256), lambda b: (b, 0)),
                  pl.BlockSpec((LANES, SEQ), lambda b: (0, b)),
                  _layer_spec(l, (ATTN_HEADS, 1, LANES))],
        out_specs=pl.BlockSpec((512, SEQ), lambda b: (0, b)),
        out_shape=jax.ShapeDtypeStruct((512, CTX_TOKENS), BF16),
        compiler_params=_cp(("arbitrary",)),
        name="context_attention",
    )(cq, ckd, cvt, sink_t)


def _lat_attn_kernel(q_ref, kd_ref, vt_ref, kc_ref, vct_ref, sink_ref, o_ref):
    n = pl.program_id(1)
    nq = DEC_SEQ // ATTN_BLOCK
    prev0 = pl.multiple_of(jnp.maximum(n - 1, 0) * ATTN_BLOCK, ATTN_BLOCK)
    cur0 = pl.multiple_of(n * ATTN_BLOCK, ATTN_BLOCK)
    next0 = pl.multiple_of(jnp.minimum(n + 1, nq - 1) * ATTN_BLOCK, ATTN_BLOCK)
    kj = lax.broadcasted_iota(jnp.int32, (ATTN_BLOCK, ATTN_BLOCK), 0)
    qi = lax.broadcasted_iota(jnp.int32, (ATTN_BLOCK, ATTN_BLOCK), 1)
    m_prev = jnp.logical_and(kj >= qi, n > 0)
    m_next = jnp.logical_and(kj <= qi, n < nq - 1)
    for g in range(ATTN_KV_HEADS):
        gl = slice(g * LANES, (g + 1) * LANES)
        gr = slice(g * ATTN_HEAD_DIM, (g + 1) * ATTN_HEAD_DIM)
        q_refs = [q_ref[:, (2 * g + pp) * LANES:(2 * g + pp + 1) * LANES] for pp in range(2)]
        keys = [kd_ref[pl.ds(prev0, ATTN_BLOCK), gl], kd_ref[pl.ds(cur0, ATTN_BLOCK), gl],
                kd_ref[pl.ds(next0, ATTN_BLOCK), gl], kc_ref[0, :, gl]]
        vals = [vt_ref[gr, pl.ds(prev0, ATTN_BLOCK)], vt_ref[gr, pl.ds(cur0, ATTN_BLOCK)],
                vt_ref[gr, pl.ds(next0, ATTN_BLOCK)], vct_ref[0, gr, :]]
        _attend_group(q_refs, keys, vals, [m_prev, None, m_next, None], sink_ref, g, o_ref)


def _latent_attention(l, cq, ckd, cvt, kc, vct, sink_t):
    nq = DEC_SEQ // ATTN_BLOCK
    q0 = CTX_TOKENS // ATTN_BLOCK
    s0 = CTX_TOKENS // DEC_SEQ
    return pl.pallas_call(
        _lat_attn_kernel,
        grid=(DEC_BATCH, nq),
        in_specs=[pl.BlockSpec((ATTN_BLOCK, 512), lambda b, n: (q0 + b * nq + n, 0)),
                  pl.BlockSpec((DEC_SEQ, 256), lambda b, n: (s0 + b, 0)),
                  pl.BlockSpec((LANES, DEC_SEQ), lambda b, n: (0, s0 + b)),
                  pl.BlockSpec((1, None, PAST_LEN, 256), lambda b, n: (b, l, 0, 0)),
                  pl.BlockSpec((1, None, LANES, PAST_LEN), lambda b, n: (b, l, 0, 0)),
                  _layer_spec(l, (ATTN_HEADS, 1, LANES))],
        out_specs=pl.BlockSpec((512, ATTN_BLOCK), lambda b, n: (0, b * nq + n)),
        out_shape=jax.ShapeDtypeStruct((512, LAT_TOKENS), BF16),
        compiler_params=_cp(("arbitrary", "arbitrary")),
        name="latent_attention",
    )(cq, ckd, cvt, kc, vct, sink_t)


def _gelu(x):
    return 0.5 * x * (1.0 + jnp.tanh(0.7978845608028654 * (x + 0.044715 * (x * x * x))))


def _merge_kernel(alpha, xc_ref, xl_ref, mod_ref, of_ref, ob_ref, sag_ref, ng_ref, zu_ref, zv_ref,
                  ws_ref, bs_ref, cc_ref, cl_ref, gate_ref, wb_ref, wo_ref, g1_ref, b1_ref,
                  rwh_ref, rwl_ref, rb_ref, utri_ref,
                  x1_ref, slot_ref, meta_ref, hs_ref,
                  h2_ref, cw_ref, gid_ref, stage_ref, cnt_ref, sem_ref):
    parts = [_merge_rows(alpha, slice(p * MERGE_ROWS, (p + 1) * MERGE_ROWS), xc_ref, xl_ref, mod_ref,
                         of_ref, ob_ref, sag_ref, ng_ref, zu_ref, zv_ref, ws_ref, bs_ref, cc_ref,
                         cl_ref, gate_ref, wb_ref, wo_ref, g1_ref, b1_ref, rwh_ref, rwl_ref, rb_ref,
                         x1_ref, h2_ref, cw_ref, gid_ref) for p in range(TM // MERGE_ROWS)]
    _stagger(parts, MERGE_STAGES)
    _dispatch(h2_ref, cw_ref, gid_ref, utri_ref, slot_ref, meta_ref, hs_ref, stage_ref, cnt_ref, sem_ref)


def _aligned(row):
    return row if isinstance(row, int) else pl.multiple_of(row, HS_ALIGN)


def _window_copy(stage_ref, buf, src_row, hs_ref, dst_row, sem_ref, rows):
    return pltpu.make_async_copy(stage_ref.at[buf, pl.ds(_aligned(src_row), rows), :],
                                 hs_ref.at[pl.ds(_aligned(dst_row), rows), :], sem_ref.at[0])


def _by_window(small, fn):
    pl.when(small)(lambda: fn(SMALL_WIN))
    pl.when(jnp.logical_not(small))(lambda: fn(TM))


def _dispatch(h2_ref, cw_ref, gid_ref, utri_ref, slot_ref, meta_ref, hs_ref, stage_ref, cnt_ref, sem_ref):
    i = pl.program_id(0)

    @pl.when(i == 0)
    def _():
        for g in range(N_GROUPS + 1):
            cnt_ref[g] = 0

    gid = gid_ref[0:1, :]
    grp_i = lax.broadcasted_iota(jnp.int32, (8, TM), 0)
    onehot = jnp.where(grp_i.astype(F32) == gid, 1.0, 0.0)
    rank = _dot(onehot.astype(BF16), utri_ref[...])
    ncol = (rank[:, TM - 1:TM] + onehot[:, TM - 1:TM]).astype(jnp.int32)
    padc = ((ncol + (HS_ALIGN - 1)) // HS_ALIGN) * HS_ALIGN
    row8 = lax.broadcasted_iota(jnp.int32, (8, 1), 0)
    startc = jnp.zeros((8, 1), jnp.int32)
    run = jnp.zeros((1, 1), jnp.int32)
    for g in range(1, N_GROUPS):
        run = run + padc[g - 1:g, :]
        startc = jnp.where(row8 == g, run, startc)
    slot = jnp.sum(onehot * (rank + startc.astype(F32)), axis=0, keepdims=True)
    back = jnp.sum(onehot * (rank + (row8 * TM).astype(F32)), axis=0, keepdims=True)
    slot_ref[...] = jnp.broadcast_to(back, (LANES, TM)).T
    sel = jnp.where(lax.broadcasted_iota(jnp.int32, (STAGE_ROWS, TM), 0).astype(F32) == slot, 1.0, 0.0)
    cw = cw_ref[...]
    cw_hi = cw.astype(BF16)
    r1 = cw - cw_hi.astype(F32)
    cw_mid = r1.astype(BF16)
    cw_lo = (r1 - cw_mid.astype(F32)).astype(BF16)
    rowdata = jnp.concatenate([h2_ref[...], cw_hi, cw_mid, cw_lo], axis=1)
    buf = i % 2
    stage_ref[buf] = _dot(sel.astype(BF16), rowdata).astype(BF16)

    small = jnp.max(ncol, axis=0, keepdims=True)[0, 0] <= SMALL_WIN
    prev_small = cnt_ref[N_GROUPS] == 1

    def wait_windows(rows):
        for _ in range(N_GROUPS):
            _window_copy(stage_ref, buf, 0, hs_ref, 0, sem_ref, rows).wait()

    @pl.when(i > 0)
    def _():
        _by_window(prev_small, wait_windows)

    lane = lax.broadcasted_iota(jnp.int32, (8, LANES), 1)
    meta = jnp.where(lane == 2 * N_GROUPS, small.astype(jnp.int32), jnp.zeros((8, LANES), jnp.int32))
    starts, dsts = [], []
    for g in range(N_GROUPS):
        starts.append(startc[g, 0])
        dsts.append(g * HS_CAP + cnt_ref[g])
        cnt_ref[g] = cnt_ref[g] + padc[g, 0]
        meta = jnp.where(lane == g, dsts[g], meta)
        meta = jnp.where(lane == N_GROUPS + g, cnt_ref[g], meta)
    meta_ref[0] = meta
    cnt_ref[N_GROUPS] = small.astype(jnp.int32)

    def start_windows(rows):
        for g in range(N_GROUPS):
            _window_copy(stage_ref, buf, starts[g], hs_ref, dsts[g], sem_ref, rows).start()

    _by_window(small, start_windows)

    @pl.when(i == NB - 1)
    def _():
        _by_window(small, wait_windows)
        for g in range(N_GROUPS):
            for w in range(2):
                _window_copy(stage_ref, buf, STAGE_ROWS - TM, hs_ref, g * HS_CAP + cnt_ref[g] + w * TM,
                             sem_ref, TM).start()
        for _ in range(2 * N_GROUPS):
            _window_copy(stage_ref, buf, 0, hs_ref, 0, sem_ref, TM).wait()
        ends, total = [], 0
        for g in range(N_GROUPS):
            total = total + lax.shift_right_logical(cnt_ref[g] + (2 * TM - 1), TM.bit_length() - 1)
            ends.append(total)
        step = jnp.minimum(lax.broadcasted_iota(jnp.int32, (1, LANES), 1), total - 1)
        grp = jnp.zeros((1, LANES), jnp.int32)
        first = jnp.zeros((1, LANES), jnp.int32)
        for g in range(N_GROUPS - 1):
            grp = grp + jnp.where(step >= ends[g], 1, 0)
            first = jnp.where(step >= ends[g], ends[g], first)
        meta_ref[0, 1:2, :] = grp * (HS_CAP // TM) + step - first
        meta_ref[0, 2:3, :] = grp
        meta_ref[0, 3:4, :] = jnp.zeros((1, LANES), jnp.int32) + total


def _merge_rows(alpha, rows, xc_ref, xl_ref, mod_ref, of_ref, ob_ref, sag_ref, ng_ref, zu_ref, zv_ref,
                ws_ref, bs_ref, cc_ref, cl_ref, gate_ref, wb_ref, wo_ref, g1_ref, b1_ref,
                rwh_ref, rwl_ref, rb_ref, x1_ref, h2_ref, cw_ref, gid_ref):
    i = pl.program_id(0)
    m = mod_ref[0]
    n_rows = rows.stop - rows.start
    o = of_ref[rows, :] + ob_ref[rows, :]
    parts = []
    for h in range(GLA_HEADS):
        oh = o[:, h * GLA_DV:(h + 1) * GLA_DV]
        parts.append(oh * lax.rsqrt(jnp.mean(oh * oh, axis=-1, keepdims=True) + LN_EPS))
    br_a = (jnp.concatenate(parts, axis=1) * ng_ref[...] * sag_ref[rows, :]).astype(BF16)
    u = zu_ref[rows, :]
    v = zv_ref[rows, :]
    yield
    chunks = []
    for c in range(n_rows // GMLP_CHUNK):
        vc = v[c * GMLP_CHUNK:(c + 1) * GMLP_CHUNK]
        cols = [_dot(ws_ref[g], vc[:, g * GMLP_CH:(g + 1) * GMLP_CH]) for g in range(GMLP_GROUPS)]
        chunks.append(jnp.concatenate(cols, axis=1) + bs_ref[...])
    br_b = (u * jnp.concatenate(chunks, axis=0)).astype(BF16)
    br_ct = jnp.where(i < NB_CTX, cc_ref[:, rows], cl_ref[:, rows])
    projs = (_dot(br_a, wb_ref[0]), _dot(br_b, wb_ref[1]), _dot_tn(br_ct, wb_ref[2]))
    yield
    y = jnp.zeros((n_rows, D_MODEL), F32)
    for nbr, proj in enumerate(projs):
        y = y + gate_ref[rows, nbr * D_MODEL:(nbr + 1) * D_MODEL] * proj
    y = y.astype(BF16)
    yield
    y = _dot(y, wo_ref[...])
    yield
    x = jnp.where(i < NB_CTX, xc_ref[rows, :], xl_ref[rows, :])
    x1 = _layer_norm(alpha * x + m[:, 2 * D_MODEL:3 * D_MODEL] * y, g1_ref[...], b1_ref[...])
    x1_ref[rows, :] = x1
    h2 = x1 * (1.0 + m[:, 4 * D_MODEL:5 * D_MODEL]) + m[:, 3 * D_MODEL:4 * D_MODEL]
    h2_ref[rows, :] = h2.astype(BF16)
    h_hi, h_lo = _split(h2)
    logits = (_dot_nt(rwh_ref[...], h_hi) + _dot_nt(rwh_ref[...], h_lo) + _dot_nt(rwl_ref[...], h_hi)
              + rb_ref[:, :n_rows])[:ROUTER_ROWS]
    row_i = lax.broadcasted_iota(jnp.int32, (ROUTER_ROWS, n_rows), 0)
    row = row_i.astype(F32)
    row_group = (row_i >> 2).astype(F32)
    big = float(LANES)
    neg = -jnp.inf
    gl = jnp.where((row_i >= N_EXPERTS) & (row_i < N_EXPERTS + N_GROUPS), logits, neg)
    gmax = jnp.max(gl, axis=0, keepdims=True)
    gsum = jnp.sum(jnp.exp(gl - gmax), axis=0, keepdims=True)
    g_p = 1.0 / gsum
    g_i = jnp.min(jnp.where(gl == gmax, row, big), axis=0, keepdims=True) - float(N_EXPERTS)
    in_group = (row_i < N_EXPERTS) & (row_group == g_i)
    el = jnp.where(in_group, logits, neg)
    emax = jnp.max(el, axis=0, keepdims=True)
    ee = jnp.exp(el - emax)
    e_prob = ee / jnp.sum(ee, axis=0, keepdims=True)
    p1 = jnp.max(jnp.where(in_group, e_prob, neg), axis=0, keepdims=True)
    i1 = jnp.min(jnp.where(in_group & (e_prob == p1), row, big), axis=0, keepdims=True)
    rest = in_group & (row != i1)
    p2 = jnp.max(jnp.where(rest, e_prob, neg), axis=0, keepdims=True)
    i2 = jnp.min(jnp.where(rest & (e_prob == p2), row, big), axis=0, keepdims=True)
    tot = p1 + p2
    cw_t = (jnp.where(row == i1, g_p * p1 / tot, 0.0) + jnp.where(row == i2, g_p * p2 / tot, 0.0))
    cw_ref[rows, :] = jnp.concatenate([cw_t, jnp.zeros((LANES - ROUTER_ROWS, n_rows), F32)], axis=0).T
    gid_ref[0:1, rows] = g_i
    yield


def _merge(l, alpha, xc, xl, mod3, o_f, o_b, sag, norm_g, zu, zv, ws, bs_t, brc_ctx, brc_lat, gates,
           wb, wo, g1, b1, rw_hi, rw_lo, rb):
    utri = jnp.asarray(np.arange(TM)[:, None] < np.arange(TM)[None, :], dtype=BF16)
    return pl.pallas_call(
        functools.partial(_merge_kernel, alpha),
        grid=(NB,),
        in_specs=[_ctx_tok(D_MODEL), _lat_tok(D_MODEL), _mod_spec(l),
                  _tok(512), _tok(512), _tok(512), _layer_spec(l, (1, 512)),
                  _tok(512), _tok(512),
                  _layer_spec(l, (GMLP_GROUPS, GMLP_CHUNK, GMLP_CHUNK)), _layer_spec(l, (GMLP_CHUNK, 512)),
                  pl.BlockSpec((512, TM), lambda i: (0, jnp.minimum(i, NB_CTX - 1))),
                  pl.BlockSpec((512, TM), lambda i: (0, jnp.maximum(i - NB_CTX, 0))),
                  _tok(3072), _layer_spec(l, (N_BRANCH, BRANCH_W, D_MODEL)), _layer_spec(l, (D_MODEL, D_MODEL)),
                  _layer_spec(l, (1, D_MODEL)), _layer_spec(l, (1, D_MODEL)),
                  _layer_spec(l, (LANES, D_MODEL)), _layer_spec(l, (LANES, D_MODEL)), _layer_spec(l, (LANES, TM)),
                  pl.BlockSpec((TM, TM), lambda i: (0, 0))],
        out_specs=[_tok(D_MODEL), _tok(LANES), pl.BlockSpec((1, 8, LANES), lambda i: (i, 0, 0)),
                   pl.BlockSpec(memory_space=pl.ANY)],
        out_shape=[jax.ShapeDtypeStruct((TOKENS, D_MODEL), F32),
                   jax.ShapeDtypeStruct((TOKENS, LANES), F32),
                   jax.ShapeDtypeStruct((NB, 8, LANES), jnp.int32),
                   jax.ShapeDtypeStruct((N_GROUPS * HS_CAP, HS_W), BF16)],
        scratch_shapes=[pltpu.VMEM((TM, D_MODEL), BF16), pltpu.VMEM((TM, LANES), F32), pltpu.VMEM((8, TM), F32),
                        pltpu.VMEM((2, STAGE_ROWS, HS_W), BF16), pltpu.SMEM((N_GROUPS + 1,), jnp.int32),
                        pltpu.SemaphoreType.DMA((1,))],
        compiler_params=_cp(("arbitrary",)),
        name="merge_ln1_router",
    )(xc, xl, mod3, o_f, o_b, sag, norm_g, zu, zv, ws, bs_t, brc_ctx, brc_lat, gates,
      wb, wo, g1, b1, rw_hi, rw_lo, rb, utri)


def _experts_kernel(blk_ref, grp_ref, ntile_ref, hs_ref, wg32_ref, wu32_ref, wd32_ref, ys_ref,
                    wg_ref, wu_ref, wd_ref):
    s = pl.program_id(0)

    @pl.when(jnp.logical_or(s == 0, grp_ref[s] != grp_ref[jnp.maximum(s - 1, 0)]))
    def _():
        wg_ref[...] = wg32_ref[...].astype(BF16)
        wu_ref[...] = wu32_ref[...].astype(BF16)
        wd_ref[...] = wd32_ref[...].astype(BF16)

    @pl.when(s < ntile_ref[0])
    def _():
        g = grp_ref[s]
        x = hs_ref[:, :D_MODEL]
        cw = (hs_ref[:, D_MODEL:D_MODEL + LANES].astype(F32)
              + hs_ref[:, D_MODEL + LANES:D_MODEL + 2 * LANES].astype(F32)
              + hs_ref[:, D_MODEL + 2 * LANES:].astype(F32))
        lane = lax.broadcasted_iota(jnp.int32, (TM, LANES), 1)
        hid = []
        for k in range(EXPERTS_PER_GROUP):
            w_e = jnp.sum(jnp.where(lane == g * EXPERTS_PER_GROUP + k, cw, 0.0), axis=-1, keepdims=True)
            hid.append((_silu(_dot(x, wg_ref[k])) * _dot(x, wu_ref[k]) * w_e).astype(BF16))
        y = _dot(jnp.concatenate(hid, axis=1), wd_ref[...].reshape(EXPERTS_PER_GROUP * EXPERT_FF, D_MODEL))
        ys_ref[...] = y.astype(BF16)


def _experts(l, tile_blk, tile_grp, n_tiles, hs, wg, wu, wd):
    e = EXPERTS_PER_GROUP
    return pl.pallas_call(
        _experts_kernel,
        grid_spec=pltpu.PrefetchScalarGridSpec(
            num_scalar_prefetch=3,
            grid=(MOE_TILES,),
            in_specs=[pl.BlockSpec((TM, HS_W), lambda s, blk, grp, nt: (blk[s], 0)),
                      pl.BlockSpec((None, e, D_MODEL, EXPERT_FF), lambda s, blk, grp, nt: (l, grp[s], 0, 0)),
                      pl.BlockSpec((None, e, D_MODEL, EXPERT_FF), lambda s, blk, grp, nt: (l, grp[s], 0, 0)),
                      pl.BlockSpec((None, e, EXPERT_FF, D_MODEL), lambda s, blk, grp, nt: (l, grp[s], 0, 0))],
            out_specs=pl.BlockSpec((TM, D_MODEL), lambda s, blk, grp, nt: (blk[s], 0)),
            scratch_shapes=[pltpu.VMEM((e, D_MODEL, EXPERT_FF), BF16), pltpu.VMEM((e, D_MODEL, EXPERT_FF), BF16),
                            pltpu.VMEM((e, EXPERT_FF, D_MODEL), BF16)],
        ),
        out_shape=jax.ShapeDtypeStruct((N_GROUPS * HS_CAP, D_MODEL), BF16),
        compiler_params=_cp(("arbitrary",)),
        name="moe_experts",
    )(tile_blk, tile_grp, n_tiles, hs, wg, wu, wd)


def _window_fetch(ys_ref, offs_ref, blk, win_ref, buf, sem_ref, rows):
    return [pltpu.make_async_copy(
        ys_ref.at[pl.ds(pl.multiple_of(offs_ref[blk * N_GROUPS + g], HS_ALIGN), rows), :],
        win_ref.at[buf, pl.ds(g * rows, rows), :], sem_ref.at[buf]) for g in range(N_GROUPS)]


def _combine_kernel(alpha, offs_ref, small_ref, ys_ref, slot_ref, x1_ref, mod_ref, g2_ref, b2_ref,
                    oc_ref, ol_ref, win_ref, y_ref, sem_ref):
    i = pl.program_id(0)
    buf = i % 2

    def start(blk, to_buf):
        def go(rows):
            for cp in _window_fetch(ys_ref, offs_ref, blk, win_ref, to_buf, sem_ref, rows):
                cp.start()
        _by_window(small_ref[blk] == 1, go)

    pl.when(i == 0)(lambda: start(0, 0))
    pl.when(i + 1 < NB)(lambda: start(i + 1, 1 - buf))

    def gather(rows):
        for cp in _window_fetch(ys_ref, offs_ref, i, win_ref, buf, sem_ref, rows):
            cp.wait()
        back = slot_ref[...]
        packed = back - jnp.floor(back * (1.0 / TM)) * float(TM - rows)
        slot = jnp.concatenate([packed] * (N_GROUPS * rows // LANES), axis=1)
        lane = lax.broadcasted_iota(jnp.int32, (TM, N_GROUPS * rows), 1).astype(F32)
        sel = jnp.where(lane == slot, 1.0, 0.0).astype(BF16)
        y_ref[...] = _dot(sel, win_ref[buf, :N_GROUPS * rows, :])

    _by_window(small_ref[i] == 1, gather)
    m = mod_ref[0]
    out = _layer_norm(alpha * x1_ref[...] + m[:, 5 * D_MODEL:6 * D_MODEL] * y_ref[...],
                      g2_ref[...], b2_ref[...])

    @pl.when(i < NB_CTX)
    def _():
        oc_ref[...] = out

    @pl.when(i >= NB_CTX)
    def _():
        ol_ref[...] = out


def _combine(l, alpha, offs, small, ys, slot, x1, mod3, g2, b2):
    return pl.pallas_call(
        functools.partial(_combine_kernel, alpha),
        grid_spec=pltpu.PrefetchScalarGridSpec(
            num_scalar_prefetch=2,
            grid=(NB,),
            in_specs=[pl.BlockSpec(memory_space=pl.ANY),
                      pl.BlockSpec((TM, LANES), lambda i, *_: (i, 0)),
                      pl.BlockSpec((TM, D_MODEL), lambda i, *_: (i, 0)),
                      _mod_spec(l),
                      pl.BlockSpec((None, 1, D_MODEL), lambda i, *_: (l, 0, 0)),
                      pl.BlockSpec((None, 1, D_MODEL), lambda i, *_: (l, 0, 0))],
            out_specs=[pl.BlockSpec((TM, D_MODEL), lambda i, *_: (jnp.minimum(i, NB_CTX - 1), 0)),
                       pl.BlockSpec((TM, D_MODEL), lambda i, *_: (jnp.maximum(i - NB_CTX, 0), 0))],
            scratch_shapes=[pltpu.VMEM((2, N_GROUPS * TM, D_MODEL), BF16), pltpu.VMEM((TM, D_MODEL), F32),
                            pltpu.SemaphoreType.DMA((2,))],
        ),
        out_shape=[jax.ShapeDtypeStruct((CTX_TOKENS, D_MODEL), F32),
                   jax.ShapeDtypeStruct((LAT_TOKENS, D_MODEL), F32)],
        compiler_params=_cp(("arbitrary",)),
        name="combine_ln2",
    )(offs, small, ys, slot, x1, mod3, g2, b2)


def _rope_tables():
    f32 = np.float32
    pos = np.arange(DEC_SEQ)
    row = (pos // GRID_W).astype(f32)
    col = (pos % GRID_W).astype(f32)
    quarter = ATTN_HEAD_DIM // 4
    inv_freq = np.power(f32(ROPE_BASE), -np.arange(quarter, dtype=f32) / f32(quarter)).astype(f32)
    j = np.arange(ATTN_HEAD_DIM)
    p = np.where((j // (2 * quarter))[None, :] == 0, row[:, None], col[:, None]).astype(f32)
    ang = (p * inv_freq[j % quarter][None, :]).astype(f32)
    sign = np.where((j % (2 * quarter)) < quarter, -1.0, 1.0).astype(f32)
    cos = np.tile(np.cos(ang).astype(f32), (1, ATTN_HEADS))
    sin = np.tile((np.sin(ang) * sign[None, :]).astype(f32), (1, ATTN_HEADS))
    cos = np.concatenate([np.ones((TM, 512), f32), cos], axis=0)
    sin = np.concatenate([np.zeros((TM, 512), f32), sin], axis=0)
    return jnp.asarray(cos), jnp.asarray(sin)


def _dup_heads(t):
    h0, h1 = t[..., :ATTN_HEAD_DIM], t[..., ATTN_HEAD_DIM:]
    return jnp.concatenate([h0, h0, h1, h1], axis=-1)


def _split_w_in(w):
    gate_scale = np.where(np.arange(IN_COLS) >= W_MG.start, 0.5, 1.0).astype(np.float32)
    return [(jnp.swapaxes(w, 1, 2) * gate_scale[:, None]).astype(BF16)]


def kernel(x_prompt, x_sample, state_gla, cache_k, cache_v, c, c_ctx, ada_w, ada_b, w_in, gla_gate_w, gla_gate_b, gla_norm_g, gmlp_ln_g, gmlp_ln_b, gmlp_ws, gmlp_bs, attn_sink, w_branch, w_out, ln1_g, ln1_b, ln2_g, ln2_b, router_group_w, router_group_b, router_expert_w, router_expert_b, expert_w_gate, expert_w_up, expert_w_down):
    alpha = (2.0 * DEPTH) ** 0.25
    xc = x_prompt.reshape(CTX_TOKENS, D_MODEL)
    xl = x_sample.reshape(LAT_TOKENS, D_MODEL)
    cond = jnp.concatenate([c_ctx[None, :], c, jnp.zeros((MOD_ROWS - 1 - DEC_BATCH, D_MODEL), F32)], axis=0)
    cos_t, sin_t = _rope_tables()

    ada_b3 = ada_b[:, None, :]
    w_parts = _split_w_in(w_in)
    zrow = jnp.zeros((DEPTH, GLA_RANK, 256), F32)
    zpad = jnp.zeros((DEPTH, LANES - 2 * GLA_RANK, 256), F32)
    gw_p = jnp.stack([jnp.concatenate([gla_gate_w[:, 0], zrow, zpad], axis=1),
                      jnp.concatenate([zrow, gla_gate_w[:, 1], zpad], axis=1)], axis=1).astype(BF16)
    gb_p = gla_gate_b[:, :, None, :]
    sink_t = jnp.broadcast_to(attn_sink[:, :, None, None], (DEPTH, ATTN_HEADS, 1, LANES))
    kc = _dup_heads(cache_k.reshape(DEC_BATCH, DEPTH, PAST_LEN, 128)).astype(BF16)
    vct = jnp.swapaxes(cache_v.reshape(DEC_BATCH, DEPTH, PAST_LEN, 128), 2, 3).astype(BF16)
    rw = jnp.swapaxes(jnp.concatenate(
        [router_expert_w, router_group_w,
         jnp.zeros((DEPTH, D_MODEL, LANES - N_EXPERTS - N_GROUPS), F32)], axis=2), 1, 2)
    rw_hi = rw.astype(BF16)
    rw_lo = (rw - rw_hi.astype(F32)).astype(BF16)
    rb = jnp.concatenate([router_expert_b, router_group_b,
                          jnp.zeros((DEPTH, LANES - N_EXPERTS - N_GROUPS), F32)], axis=1)
    rb = jnp.broadcast_to(rb[:, :, None], (DEPTH, LANES, TM))
    bs_t = jnp.repeat(jnp.swapaxes(gmlp_bs, 1, 2), GMLP_CH, axis=2)
    ws_b = gmlp_ws.astype(BF16)
    wb_b = (0.5 * w_branch).astype(BF16)
    wo_b = w_out.astype(BF16)
    row = lambda t: t[:, None, :]

    mod3 = _modulation(cond, ada_w, ada_b3).reshape(DEPTH, MOD_ROWS, 1, 6 * D_MODEL)
    states, keys, values = [], [], []
    for l in range(DEPTH):
        (qef, kef, klf, qeb, keb, klb, dec, av, avt, sag, zu, zv, cq, ckd, cvt, ck, cv, gates) = _in_projection(
            l, xc, xl, mod3, cos_t, sin_t, gw_p, gb_p, row(gmlp_ln_g), row(gmlp_ln_b), w_parts)
        o_f, o_b, s_fin = _gla(l, qef, kef, klf, qeb, keb, klb, dec, av, avt, state_gla)
        brc_ctx = _context_attention(l, cq, ckd, cvt, sink_t)
        brc_lat = _latent_attention(l, cq, ckd, cvt, kc, vct, sink_t)
        x1, slot, meta, hs = _merge(l, alpha, xc, xl, mod3, o_f, o_b, sag, row(gla_norm_g), zu, zv,
                                    ws_b, bs_t, brc_ctx, brc_lat, gates, wb_b,
                                    wo_b, row(ln1_g), row(ln1_b), rw_hi, rw_lo, rb)
        offs = meta[:, 0, :N_GROUPS].reshape(NB * N_GROUPS)
        tiles = meta[NB - 1]
        ys = _experts(l, tiles[1, :MOE_TILES], tiles[2, :MOE_TILES], tiles[3, :1], hs,
                      expert_w_gate, expert_w_up, expert_w_down)
        xc, xl = _combine(l, alpha, offs, meta[:, 0, 2 * N_GROUPS], ys, slot, x1, mod3,
                          row(ln2_g), row(ln2_b))
        states.append(s_fin)
        keys.append(ck.reshape(BATCH, SEQ, ATTN_KV_HEADS, ATTN_HEAD_DIM))
        values.append(cv.reshape(BATCH, SEQ, ATTN_KV_HEADS, ATTN_HEAD_DIM))
    return (xc.reshape(BATCH, SEQ, D_MODEL), xl.reshape(DEC_BATCH, DEC_SEQ, D_MODEL),
            jnp.stack(states, axis=1), jnp.stack(keys, axis=1), jnp.stack(values, axis=1))
```

```python
import functools

import jax
import jax.numpy as jnp
import numpy as np
from jax import lax
from jax.experimental import pallas as pl
from jax.experimental.pallas import tpu as pltpu

F32 = jnp.float32
BF16 = jnp.bfloat16

D_MODEL = 1024
BATCH = 32
SEQ = 256
DEPTH = 2
DEC_BATCH = 8
DEC_SEQ = 1024
PAST_LEN = 512
GRID_W = 64
GLA_HEADS = 4
GLA_DV = 128
GLA_DK = 64
GLA_RANK = 16
GLA_TAU = 16.0
GLA_CHUNK = 32
GMLP_CHUNK = 128
GMLP_CH = 128
GMLP_GROUPS = 4
ATTN_HEADS = 8
ATTN_KV_HEADS = 2
ATTN_GROUP = ATTN_HEADS // ATTN_KV_HEADS
ATTN_HEAD_DIM = 64
WINDOW = 128
ATTN_BLOCK = 128
ROPE_BASE = 10000.0
BRANCH_W = 512
N_BRANCH = 3
N_GROUPS = 4
EXPERTS_PER_GROUP = 4
N_EXPERTS = 16
EXPERT_FF = 256
LN_EPS = 1e-5

LANES = 128
TM = 256
CTX_TOKENS = BATCH * SEQ
LAT_TOKENS = DEC_BATCH * DEC_SEQ
TOKENS = CTX_TOKENS + LAT_TOKENS
NB_CTX = CTX_TOKENS // TM
NB_LAT = LAT_TOKENS // TM
NB = NB_CTX + NB_LAT
LAT_BLOCKS_PER_SEQ = DEC_SEQ // TM
MOD_ROWS = 16
GLA_SUB = 128
CHUNKS_PER_SUB = GLA_SUB // GLA_CHUNK
GLA_STAGES = 3
MERGE_ROWS = 128
MERGE_STAGES = 6
ROUTER_ROWS = 24
HS_ALIGN = 16
SMALL_WIN = 128
HS_W = D_MODEL + 3 * LANES
STAGE_ROWS = TM + N_GROUPS * HS_ALIGN + TM
_MAX_PAD = NB * (HS_ALIGN - 1)
HS_CAP = -(-(TOKENS + _MAX_PAD + 2 * TM) // TM) * TM
MOE_TILES = (TOKENS + N_GROUPS * _MAX_PAD) // TM + 2 * N_GROUPS

IN_COLS = 6432
W_AQ, W_AK, W_AV, W_AG = slice(0, 256), slice(256, 512), slice(512, 1024), slice(1024, 1536)
W_LR = slice(1536, 1664)
W_BZ, W_CQ, W_CK, W_CV = slice(1568, 2592), slice(2592, 3104), slice(3104, 3232), slice(3232, 3360)
W_MG = slice(3360, 6432)

VMEM_LIMIT = 56 * 1024 * 1024


def _cp(sem):
    return pltpu.CompilerParams(dimension_semantics=sem, vmem_limit_bytes=VMEM_LIMIT)


def _dot(a, b):
    return jnp.dot(a, b, preferred_element_type=F32)


def _dot_nt(a, b):
    return lax.dot_general(a, b, (((1,), (1,)), ((), ())), preferred_element_type=F32)


def _dot_tn(a, b):
    return lax.dot_general(a, b, (((0,), (0,)), ((), ())), preferred_element_type=F32)


def _split(x):
    hi = x.astype(BF16)
    lo = (x - hi.astype(F32)).astype(BF16)
    return hi, lo


def _sigmoid(x):
    return 0.5 * (jnp.tanh(0.5 * x) + 1.0)


def _silu(x):
    return x * _sigmoid(x)


def _layer_norm(x, g, b):
    mu = jnp.mean(x, axis=-1, keepdims=True)
    xc = x - mu
    var = jnp.mean(xc * xc, axis=-1, keepdims=True)
    return xc * lax.rsqrt(var + LN_EPS) * g + b


def _stagger(parts, n_stages):
    for step in range(n_stages + len(parts) - 1):
        for lag, part in enumerate(parts):
            if 0 <= step - lag < n_stages:
                next(part)


def _mod_block(i):
    return jnp.where(i < NB_CTX, 0, 1 + (i - NB_CTX) // LAT_BLOCKS_PER_SEQ)


def _tok(width):
    return pl.BlockSpec((TM, width), lambda i: (i, 0))


def _ctx_tok(width):
    return pl.BlockSpec((TM, width), lambda i: (jnp.minimum(i, NB_CTX - 1), 0))


def _lat_tok(width):
    return pl.BlockSpec((TM, width), lambda i: (jnp.maximum(i - NB_CTX, 0), 0))


def _layer_spec(l, shape):
    return pl.BlockSpec((None,) + shape, lambda *_: (l,) + (0,) * len(shape))


def _mod_spec(l):
    return pl.BlockSpec((None, 1, 1, 6 * D_MODEL), lambda i, *_: (l, _mod_block(i), 0, 0))


def _mod_kernel(cond_ref, w_ref, b_ref, o_ref):
    s_hi, s_lo = _split(_silu(cond_ref[...]))
    w_hi, w_lo = _split(w_ref[...])
    o_ref[...] = _dot(s_hi, w_hi) + _dot(s_lo, w_hi) + _dot(s_hi, w_lo) + b_ref[...]


def _modulation(cond, ada_w, ada_b3):
    tn = 1536
    return pl.pallas_call(
        _mod_kernel,
        grid=(DEPTH, 6 * D_MODEL // tn),
        in_specs=[pl.BlockSpec((MOD_ROWS, D_MODEL), lambda l, j: (0, 0)),
                  pl.BlockSpec((None, D_MODEL, tn), lambda l, j: (l, 0, j)),
                  pl.BlockSpec((None, 1, tn), lambda l, j: (l, 0, j))],
        out_specs=pl.BlockSpec((None, MOD_ROWS, tn), lambda l, j: (l, 0, j)),
        out_shape=jax.ShapeDtypeStruct((DEPTH, MOD_ROWS, 6 * D_MODEL), F32),
        compiler_params=_cp(("arbitrary", "arbitrary")),
        name="modulation",
    )(cond, ada_w, ada_b3)


def _rope(x, cos, sin):
    n = x.shape[1]
    lane = lax.broadcasted_iota(jnp.int32, x.shape, 1)
    partner = jnp.where((lane & 31) < 16, pltpu.roll(x, n - 16, 1), pltpu.roll(x, 16, 1))
    return x * cos + partner * sin


def _inproj_kernel(xc_ref, xl_ref, mod_ref, cos_ref, sin_ref, gw_ref, gb_ref, lg_ref, lb_ref,
                   wt_ref,
                   qef_ref, kef_ref, klf_ref, qeb_ref, keb_ref, klb_ref, dec_ref,
                   av_ref, avt_ref, sag_ref, zu_ref, zv_ref,
                   cq_ref, ckd_ref, cvt_ref, ck_ref, cv_ref, gate_ref):
    i = pl.program_id(0)
    m = mod_ref[0]
    x = jnp.where(i < NB_CTX, xc_ref[...], xl_ref[...])
    h = (x * (1.0 + m[:, D_MODEL:2 * D_MODEL]) + m[:, 0:D_MODEL]).astype(BF16)

    def proj(rows):
        return _dot_nt(h, wt_ref[rows, :])

    aq = proj(W_AQ)
    ak = proj(W_AK)
    lr = proj(W_LR).astype(BF16)
    zs = [_dot(lr, gw_ref[d]) + gb_ref[d] for d in range(2)]
    for nbr in range(N_BRANCH):
        cols = slice(W_MG.start + nbr * D_MODEL, W_MG.start + (nbr + 1) * D_MODEL)
        gate_ref[:, nbr * D_MODEL:(nbr + 1) * D_MODEL] = jnp.tanh(proj(cols)) + 1.0
    r = lax.broadcasted_iota(jnp.int32, (TM, TM), 0)
    c = lax.broadcasted_iota(jnp.int32, (TM, TM), 1)
    same = (r >> 5) == (c >> 5)
    bs = []
    for d in range(2):
        g = (jnp.minimum(zs[d], 0.0) - jnp.log(1.0 + jnp.exp(-jnp.abs(zs[d])))) / GLA_TAU
        tri_b = jnp.where(same & ((c <= r) if d == 0 else (c >= r)), 1.0, 0.0).astype(BF16)
        g_hi, g_lo = _split(g)
        bs.append(_dot(tri_b, g_hi) + _dot(tri_b, g_lo))
    z = _gelu(proj(W_BZ))
    zu_ref[...] = z[:, :BRANCH_W]
    zv_ref[...] = _layer_norm(z[:, BRANCH_W:], lg_ref[...], lb_ref[...]).astype(BF16)
    sag_ref[...] = _silu(proj(W_AG))
    av = proj(W_AV)
    av_ref[...] = av.astype(BF16)
    avt_ref[...] = av.T.astype(BF16)
    decs = []
    for d, (qe_ref, ke_ref, kl_ref) in enumerate(((qef_ref, kef_ref, klf_ref), (qeb_ref, keb_ref, klb_ref))):
        b = bs[d]
        last = GLA_CHUNK - 1 if d == 0 else 0
        tot = [b[n * GLA_CHUNK + last:n * GLA_CHUNK + last + 1, :] for n in range(TM // GLA_CHUNK)]
        bl = jnp.concatenate([jnp.broadcast_to(t, (GLA_CHUNK, t.shape[1])) for t in tot], axis=0)
        qe_ref[...] = (aq * jnp.exp(b) * (GLA_DK ** -0.5)).astype(BF16)
        ke_ref[...] = (ak * jnp.exp(-b)).astype(BF16)
        kl_ref[...] = (ak * jnp.exp(bl - b)).astype(BF16)
        decs.append(jnp.exp(jnp.concatenate(tot, axis=0)))
    dec_ref[...] = jnp.concatenate(decs, axis=1)

    cos = cos_ref[...]
    sin = sin_ref[...]
    cq_ref[...] = (_rope(proj(W_CQ), cos, sin) * (ATTN_HEAD_DIM ** -0.5)).astype(BF16)
    ck = proj(W_CK)
    cv = proj(W_CV)
    kr = _rope(ck, cos[:, :LANES], sin[:, :LANES])
    swapped = pltpu.roll(kr, ATTN_HEAD_DIM, 1)
    lane = lax.broadcasted_iota(jnp.int32, (TM, LANES), 1)
    ckd_ref[...] = jnp.concatenate([jnp.where(lane < 64, kr, swapped), jnp.where(lane < 64, swapped, kr)],
                                   axis=1).astype(BF16)
    cvt_ref[...] = cv.T.astype(BF16)

    @pl.when(i < NB_CTX)
    def _():
        ck_ref[...] = ck
        cv_ref[...] = cv


def _rope_block(i):
    return jnp.where(i < NB_CTX, 0, 1 + (i - NB_CTX) % LAT_BLOCKS_PER_SEQ)


def _in_projection(l, xc, xl, mod3, cos_t, sin_t, gw_p, gb_p, ln_g, ln_b, w_parts):
    def tok_out(width, dt):
        return _tok(width), jax.ShapeDtypeStruct((TOKENS, width), dt)

    def feat_out(width):
        return (pl.BlockSpec((width, TM), lambda i: (0, i)), jax.ShapeDtypeStruct((width, TOKENS), BF16))

    def ctx_out(width):
        return _ctx_tok(width), jax.ShapeDtypeStruct((CTX_TOKENS, width), F32)

    dec_out = (pl.BlockSpec((TM // GLA_CHUNK, 512), lambda i: (i, 0)),
               jax.ShapeDtypeStruct((TOKENS // GLA_CHUNK, 512), F32))
    outs = [tok_out(256, BF16)] * 6 + [dec_out] + [
        tok_out(512, BF16), feat_out(512), tok_out(512, F32), tok_out(512, F32), tok_out(512, BF16),
        tok_out(512, BF16), tok_out(256, BF16), feat_out(LANES), ctx_out(LANES), ctx_out(LANES),
        tok_out(3072, F32)]
    return pl.pallas_call(
        _inproj_kernel,
        grid=(NB,),
        in_specs=[_ctx_tok(D_MODEL), _lat_tok(D_MODEL), _mod_spec(l),
                  pl.BlockSpec((TM, 512), lambda i: (_rope_block(i), 0)),
                  pl.BlockSpec((TM, 512), lambda i: (_rope_block(i), 0)),
                  _layer_spec(l, (2, LANES, 256)), _layer_spec(l, (2, 1, 256)),
                  _layer_spec(l, (1, 512)), _layer_spec(l, (1, 512))]
                 + [_layer_spec(l, w.shape[1:]) for w in w_parts],
        out_specs=[o[0] for o in outs],
        out_shape=[o[1] for o in outs],
        compiler_params=_cp(("arbitrary",)),
        name="in_projection",
    )(xc, xl, mod3, cos_t, sin_t, gw_p, gb_p, ln_g, ln_b, *w_parts)


def _gla_direction(qe, ke, kl, v, vt, dec, masks, state_ref, reverse, o_ref, rows):
    tri, half_masks, chunk_mask, head_diag = masks
    tri = tri[int(reverse)]
    pairs = range(GLA_HEADS // 2)
    halves, q_exps, kvts = [], [], []
    for p in pairs:
        ps = slice(p * LANES, (p + 1) * LANES)
        qp = qe[:, ps]
        vp = v[:, p * 2 * GLA_DV:(p + 1) * 2 * GLA_DV]
        for hh in range(2):
            s = jnp.where(tri, _dot_nt(qp * half_masks[hh], ke[:, ps]), 0.0).astype(BF16)
            halves.append(_dot(s, vp[:, hh * GLA_DV:(hh + 1) * GLA_DV]))
        k_exp = jnp.concatenate([kl[:, ps]] * CHUNKS_PER_SUB, axis=1) * chunk_mask
        q_exps.append(jnp.concatenate([qp] * CHUNKS_PER_SUB, axis=1) * chunk_mask)
        kvts.append(_dot(vt[p * 2 * GLA_DV:(p + 1) * 2 * GLA_DV, :], k_exp))
    yield
    st_stacks = []
    for p in pairs:
        ps = slice(p * LANES, (p + 1) * LANES)
        st = state_ref[p]
        entering = [None] * CHUNKS_PER_SUB
        order = range(CHUNKS_PER_SUB - 1, -1, -1) if reverse else range(CHUNKS_PER_SUB)
        for n in order:
            entering[n] = st
            st = dec[n:n + 1, ps] * st + jnp.where(head_diag, kvts[p][:, n * LANES:(n + 1) * LANES], 0.0)
        state_ref[p] = st
        st_stacks.append(jnp.concatenate(entering, axis=1).astype(BF16))
    yield
    outs = []
    for p in pairs:
        inter = _dot_nt(q_exps[p], st_stacks[p])
        outs.append(halves[2 * p] + inter[:, :GLA_DV])
        outs.append(halves[2 * p + 1] + inter[:, GLA_DV:])
    o_ref[rows, :] = jnp.concatenate(outs, axis=1)
    yield


def _gla_mask_constants():
    r = np.arange(GLA_SUB)[:, None]
    c = np.arange(GLA_SUB)[None, :]
    same = (r // GLA_CHUNK) == (c // GLA_CHUNK)
    tri = np.stack([same & (c <= r), same & (c >= r)]).astype(np.float32)
    lane = np.arange(LANES)[None, :]
    halves = np.stack([np.broadcast_to(lane < 64, (GLA_SUB, LANES)),
                       np.broadcast_to(lane >= 64, (GLA_SUB, LANES))]).astype(np.float32)
    col_chunk = np.arange(CHUNKS_PER_SUB * LANES)[None, :] // LANES
    chunk_mask = ((r // GLA_CHUNK) == col_chunk).astype(np.float32)
    head_diag = ((np.arange(2 * GLA_DV)[:, None] // GLA_DV) == (lane // GLA_DK)).astype(np.float32)
    return (jnp.asarray(tri), jnp.asarray(halves, dtype=BF16), jnp.asarray(chunk_mask, dtype=BF16),
            jnp.asarray(head_diag))


def _gla_kernel(tri_ref, half_ref, cmask_ref, hdiag_ref,
                qf_ref, kef_ref, klf_ref, vf_ref, vtf_ref, decf_ref,
                qb_ref, keb_ref, klb_ref, vb_ref, vtb_ref, decb_ref,
                s0_ref, of_ref, ob_ref, sfin_ref, state_ref):
    i = pl.program_id(0)
    @pl.when(i < NB_CTX)
    def _():
        state_ref[...] = jnp.zeros_like(state_ref)

    @pl.when(jnp.logical_and(i >= NB_CTX, (i - NB_CTX) % LAT_BLOCKS_PER_SEQ == 0))
    def _():
        zero = jnp.zeros((GLA_DK, GLA_DV), F32)
        for d in range(2):
            for p in range(2):
                pair = jnp.concatenate(
                    [jnp.concatenate([s0_ref[0, d, 2 * p], zero], axis=1),
                     jnp.concatenate([zero, s0_ref[0, d, 2 * p + 1]], axis=1)], axis=0)
                state_ref[d, p] = pair.T

    masks = ((tri_ref[0] > 0.5, tri_ref[1] > 0.5), (half_ref[0], half_ref[1]), cmask_ref[...],
             hdiag_ref[...] > 0.5)
    n_sub = TM // GLA_SUB
    parts = []
    for j in range(n_sub):
        rs = slice(j * GLA_SUB, (j + 1) * GLA_SUB)
        parts.append(_gla_direction(
            qf_ref[rs, :], kef_ref[rs, :], klf_ref[rs, :], vf_ref[rs, :], vtf_ref[:, rs],
            decf_ref[j * CHUNKS_PER_SUB:(j + 1) * CHUNKS_PER_SUB, :256], masks, state_ref.at[0], False,
            of_ref, rs))
        jb = n_sub - 1 - j
        rb = slice(jb * GLA_SUB, (jb + 1) * GLA_SUB)
        parts.append(_gla_direction(
            qb_ref[rb, :], keb_ref[rb, :], klb_ref[rb, :], vb_ref[rb, :], vtb_ref[:, rb],
            decb_ref[jb * CHUNKS_PER_SUB:(jb + 1) * CHUNKS_PER_SUB, 256:], masks, state_ref.at[1], True,
            ob_ref, rb))
    _stagger(parts, GLA_STAGES)

    @pl.when(i < NB_CTX)
    def _():
        for d in range(2):
            for p in range(2):
                sp = state_ref[d, p].T
                sfin_ref[0, d, 2 * p] = sp[:GLA_DK, :GLA_DV]
                sfin_ref[0, d, 2 * p + 1] = sp[GLA_DK:, GLA_DV:]


def _bwd_block(i):
    k = (i - NB_CTX) % LAT_BLOCKS_PER_SEQ
    return jnp.where(i < NB_CTX, i, i - k + (LAT_BLOCKS_PER_SEQ - 1 - k))


def _gla(l, qef, kef, klf, qeb, keb, klb, dec, av, avt, state_gla):
    def bwd(width):
        return pl.BlockSpec((TM, width), lambda i: (_bwd_block(i), 0))

    def lat_seq(i):
        return jnp.maximum(i - NB_CTX, 0) // LAT_BLOCKS_PER_SEQ

    vt_fwd = pl.BlockSpec((512, TM), lambda i: (0, i))
    vt_bwd = pl.BlockSpec((512, TM), lambda i: (0, _bwd_block(i)))
    dec_fwd = pl.BlockSpec((TM // GLA_CHUNK, 512), lambda i: (i, 0))
    dec_bwd = pl.BlockSpec((TM // GLA_CHUNK, 512), lambda i: (_bwd_block(i), 0))
    state_dims = (2, GLA_HEADS, GLA_DK, GLA_DV)
    consts = _gla_mask_constants()
    return pl.pallas_call(
        _gla_kernel,
        grid=(NB,),
        in_specs=[pl.BlockSpec(t.shape, lambda i, nd=t.ndim: (0,) * nd) for t in consts]
                 + [_tok(256), _tok(256), _tok(256), _tok(512), vt_fwd, dec_fwd,
                  bwd(256), bwd(256), bwd(256), bwd(512), vt_bwd, dec_bwd,
                  pl.BlockSpec((1, None) + state_dims, lambda i: (lat_seq(i), l, 0, 0, 0, 0))],
        out_specs=[_tok(512), bwd(512),
                   pl.BlockSpec((1,) + state_dims, lambda i: (jnp.minimum(i, NB_CTX - 1), 0, 0, 0, 0))],
        out_shape=[jax.ShapeDtypeStruct((TOKENS, 512), F32), jax.ShapeDtypeStruct((TOKENS, 512), F32),
                   jax.ShapeDtypeStruct((BATCH,) + state_dims, F32)],
        scratch_shapes=[pltpu.VMEM((2, 2, 2 * GLA_DV, LANES), F32)],
        compiler_params=_cp(("arbitrary",)),
        name="gla_scan",
    )(*consts, qef, kef, klf, av, avt, dec, qeb, keb, klb, av, avt, dec, state_gla)


def _attend_group(q_refs, keys, values_t, masks, sink_ref, g, o_ref):
    rows = q_refs[0].shape[0]
    lane = lax.broadcasted_iota(jnp.int32, (rows, LANES), 1)
    stacked = []
    for q in q_refs:
        q32 = q.astype(F32)
        stacked.append(jnp.where(lane < 64, q32, 0.0).astype(BF16))
        stacked.append(jnp.where(lane >= 64, q32, 0.0).astype(BF16))
    qs = jnp.concatenate(stacked, axis=0)
    col_head = lax.broadcasted_iota(jnp.int32, (1, ATTN_GROUP * rows), 1) // rows
    sink = jnp.zeros((1, ATTN_GROUP * rows), F32)
    for hh in range(ATTN_GROUP):
        sink = jnp.where(col_head == hh, sink_ref[ATTN_GROUP * g + hh][:, :1], sink)
    ss = []
    m = sink
    for kk, mk in zip(keys, masks):
        s = _dot_nt(kk, qs)
        if mk is not None:
            s = jnp.where(jnp.concatenate([mk] * ATTN_GROUP, axis=1), s, -jnp.inf)
        ss.append(s)
        m = jnp.maximum(m, jnp.max(s, axis=0, keepdims=True))
    den = jnp.exp(sink - m)
    acc = jnp.zeros((ATTN_HEAD_DIM, ATTN_GROUP * rows), F32)
    for s, vt in zip(ss, values_t):
        e = jnp.exp(s - m)
        den = den + jnp.sum(e, axis=0, keepdims=True)
        acc = acc + _dot(vt, e.astype(BF16))
    o = (acc / den).astype(BF16)
    for hh in range(ATTN_GROUP):
        h = ATTN_GROUP * g + hh
        o_ref[h * ATTN_HEAD_DIM:(h + 1) * ATTN_HEAD_DIM, :] = o[:, hh * rows:(hh + 1) * rows]


def _ctx_attn_kernel(q_ref, kd_ref, vt_ref, sink_ref, o_ref):
    for g in range(ATTN_KV_HEADS):
        gl = slice(g * LANES, (g + 1) * LANES)
        q_refs = [q_ref[:, (2 * g + pp) * LANES:(2 * g + pp + 1) * LANES] for pp in range(2)]
        vt = vt_ref[g * ATTN_HEAD_DIM:(g + 1) * ATTN_HEAD_DIM, :]
        _attend_group(q_refs, [kd_ref[:, gl]], [vt], [None], sink_ref, g, o_ref)


def _context_attention(l, cq, ckd, cvt, sink_t):
    return pl.pallas_call(
        _ctx_attn_kernel,
        grid=(BATCH,),
        in_specs=[pl.BlockSpec((SEQ, 512), lambda b: (b, 0)),
                  pl.BlockSpec((SEQ, 256), lambda b: (b, 0)),
                  pl.BlockSpec((LANES, SEQ), lambda b: (0, b)),
                  _layer_spec(l, (ATTN_HEADS, 1, LANES))],
        out_specs=pl.BlockSpec((512, SEQ), lambda b: (0, b)),
        out_shape=jax.ShapeDtypeStruct((512, CTX_TOKENS), BF16),
        compiler_params=_cp(("arbitrary",)),
        name="context_attention",
    )(cq, ckd, cvt, sink_t)


def _lat_attn_kernel(q_ref, kd_ref, vt_ref, kc_ref, vct_ref, sink_ref, o_ref):
    n = pl.program_id(1)
    nq = DEC_SEQ // ATTN_BLOCK
    prev0 = pl.multiple_of(jnp.maximum(n - 1, 0) * ATTN_BLOCK, ATTN_BLOCK)
    cur0 = pl.multiple_of(n * ATTN_BLOCK, ATTN_BLOCK)
    next0 = pl.multiple_of(jnp.minimum(n + 1, nq - 1) * ATTN_BLOCK, ATTN_BLOCK)
    kj = lax.broadcasted_iota(jnp.int32, (ATTN_BLOCK, ATTN_BLOCK), 0)
    qi = lax.broadcasted_iota(jnp.int32, (ATTN_BLOCK, ATTN_BLOCK), 1)
    m_prev = jnp.logical_and(kj >= qi, n > 0)
    m_next = jnp.logical_and(kj <= qi, n < nq - 1)
    for g in range(ATTN_KV_HEADS):
        gl = slice(g * LANES, (g + 1) * LANES)
        gr = slice(g * ATTN_HEAD_DIM, (g + 1) * ATTN_HEAD_DIM)
        q_refs = [q_ref[:, (2 * g + pp) * LANES:(2 * g + pp + 1) * LANES] for pp in range(2)]
        keys = [kd_ref[pl.ds(prev0, ATTN_BLOCK), gl], kd_ref[pl.ds(cur0, ATTN_BLOCK), gl],
                kd_ref[pl.ds(next0, ATTN_BLOCK), gl], kc_ref[0, :, gl]]
        vals = [vt_ref[gr, pl.ds(prev0, ATTN_BLOCK)], vt_ref[gr, pl.ds(cur0, ATTN_BLOCK)],
                vt_ref[gr, pl.ds(next0, ATTN_BLOCK)], vct_ref[0, gr, :]]
        _attend_group(q_refs, keys, vals, [m_prev, None, m_next, None], sink_ref, g, o_ref)


def _latent_attention(l, cq, ckd, cvt, kc, vct, sink_t):
    nq = DEC_SEQ // ATTN_BLOCK
    q0 = CTX_TOKENS // ATTN_BLOCK
    s0 = CTX_TOKENS // DEC_SEQ
    return pl.pallas_call(
        _lat_attn_kernel,
        grid=(DEC_BATCH, nq),
        in_specs=[pl.BlockSpec((ATTN_BLOCK, 512), lambda b, n: (q0 + b * nq + n, 0)),
                  pl.BlockSpec((DEC_SEQ, 256), lambda b, n: (s0 + b, 0)),
                  pl.BlockSpec((LANES, DEC_SEQ), lambda b, n: (0, s0 + b)),
                  pl.BlockSpec((1, None, PAST_LEN, 256), lambda b, n: (b, l, 0, 0)),
                  pl.BlockSpec((1, None, LANES, PAST_LEN), lambda b, n: (b, l, 0, 0)),
                  _layer_spec(l, (ATTN_HEADS, 1, LANES))],
        out_specs=pl.BlockSpec((512, ATTN_BLOCK), lambda b, n: (0, b * nq + n)),
        out_shape=jax.ShapeDtypeStruct((512, LAT_TOKENS), BF16),
        compiler_params=_cp(("arbitrary", "arbitrary")),
        name="latent_attention",
    )(cq, ckd, cvt, kc, vct, sink_t)


def _gelu(x):
    return 0.5 * x * (1.0 + jnp.tanh(0.7978845608028654 * (x + 0.044715 * (x * x * x))))


def _merge_kernel(alpha, xc_ref, xl_ref, mod_ref, of_ref, ob_ref, sag_ref, ng_ref, zu_ref, zv_ref,
                  ws_ref, bs_ref, cc_ref, cl_ref, gate_ref, wb_ref, wo_ref, g1_ref, b1_ref,
                  rwh_ref, rwl_ref, rb_ref, utri_ref,
                  x1_ref, slot_ref, meta_ref, hs_ref,
                  h2_ref, cw_ref, gid_ref, stage_ref, cnt_ref, sem_ref):
    parts = [_merge_rows(alpha, slice(p * MERGE_ROWS, (p + 1) * MERGE_ROWS), xc_ref, xl_ref, mod_ref,
                         of_ref, ob_ref, sag_ref, ng_ref, zu_ref, zv_ref, ws_ref, bs_ref, cc_ref,
                         cl_ref, gate_ref, wb_ref, wo_ref, g1_ref, b1_ref, rwh_ref, rwl_ref, rb_ref,
                         x1_ref, h2_ref, cw_ref, gid_ref) for p in range(TM // MERGE_ROWS)]
    _stagger(parts, MERGE_STAGES)
    _dispatch(h2_ref, cw_ref, gid_ref, utri_ref, slot_ref, meta_ref, hs_ref, stage_ref, cnt_ref, sem_ref)


def _aligned(row):
    return row if isinstance(row, int) else pl.multiple_of(row, HS_ALIGN)


def _window_copy(stage_ref, buf, src_row, hs_ref, dst_row, sem_ref, rows):
    return pltpu.make_async_copy(stage_ref.at[buf, pl.ds(_aligned(src_row), rows), :],
                                 hs_ref.at[pl.ds(_aligned(dst_row), rows), :], sem_ref.at[0])


def _by_window(small, fn):
    pl.when(small)(lambda: fn(SMALL_WIN))
    pl.when(jnp.logical_not(small))(lambda: fn(TM))


def _dispatch(h2_ref, cw_ref, gid_ref, utri_ref, slot_ref, meta_ref, hs_ref, stage_ref, cnt_ref, sem_ref):
    i = pl.program_id(0)

    @pl.when(i == 0)
    def _():
        for g in range(N_GROUPS + 1):
            cnt_ref[g] = 0

    gid = gid_ref[0:1, :]
    grp_i = lax.broadcasted_iota(jnp.int32, (8, TM), 0)
    onehot = jnp.where(grp_i.astype(F32) == gid, 1.0, 0.0)
    rank = _dot(onehot.astype(BF16), utri_ref[...])
    ncol = (rank[:, TM - 1:TM] + onehot[:, TM - 1:TM]).astype(jnp.int32)
    padc = ((ncol + (HS_ALIGN - 1)) // HS_ALIGN) * HS_ALIGN
    row8 = lax.broadcasted_iota(jnp.int32, (8, 1), 0)
    startc = jnp.zeros((8, 1), jnp.int32)
    run = jnp.zeros((1, 1), jnp.int32)
    for g in range(1, N_GROUPS):
        run = run + padc[g - 1:g, :]
        startc = jnp.where(row8 == g, run, startc)
    slot = jnp.sum(onehot * (rank + startc.astype(F32)), axis=0, keepdims=True)
    back = jnp.sum(onehot * (rank + (row8 * TM).astype(F32)), axis=0, keepdims=True)
    slot_ref[...] = jnp.broadcast_to(back, (LANES, TM)).T
    sel = jnp.where(lax.broadcasted_iota(jnp.int32, (STAGE_ROWS, TM), 0).astype(F32) == slot, 1.0, 0.0)
    cw = cw_ref[...]
    cw_hi = cw.astype(BF16)
    r1 = cw - cw_hi.astype(F32)
    cw_mid = r1.astype(BF16)
    cw_lo = (r1 - cw_mid.astype(F32)).astype(BF16)
    rowdata = jnp.concatenate([h2_ref[...], cw_hi, cw_mid, cw_lo], axis=1)
    buf = i % 2
    stage_ref[buf] = _dot(sel.astype(BF16), rowdata).astype(BF16)

    small = jnp.max(ncol, axis=0, keepdims=True)[0, 0] <= SMALL_WIN
    prev_small = cnt_ref[N_GROUPS] == 1

    def wait_windows(rows):
        for _ in range(N_GROUPS):
            _window_copy(stage_ref, buf, 0, hs_ref, 0, sem_ref, rows).wait()

    @pl.when(i > 0)
    def _():
        _by_window(prev_small, wait_windows)

    lane = lax.broadcasted_iota(jnp.int32, (8, LANES), 1)
    meta = jnp.where(lane == 2 * N_GROUPS, small.astype(jnp.int32), jnp.zeros((8, LANES), jnp.int32))
    starts, dsts = [], []
    for g in range(N_GROUPS):
        starts.append(startc[g, 0])
        dsts.append(g * HS_CAP + cnt_ref[g])
        cnt_ref[g] = cnt_ref[g] + padc[g, 0]
        meta = jnp.where(lane == g, dsts[g], meta)
        meta = jnp.where(lane == N_GROUPS + g, cnt_ref[g], meta)
    meta_ref[0] = meta
    cnt_ref[N_GROUPS] = small.astype(jnp.int32)

    def start_windows(rows):
        for g in range(N_GROUPS):
            _window_copy(stage_ref, buf, starts[g], hs_ref, dsts[g], sem_ref, rows).start()

    _by_window(small, start_windows)

    @pl.when(i == NB - 1)
    def _():
        _by_window(small, wait_windows)
        for g in range(N_GROUPS):
            for w in range(2):
                _window_copy(stage_ref, buf, STAGE_ROWS - TM, hs_ref, g * HS_CAP + cnt_ref[g] + w * TM,
                             sem_ref, TM).start()
        for _ in range(2 * N_GROUPS):
            _window_copy(stage_ref, buf, 0, hs_ref, 0, sem_ref, TM).wait()
        ends, total = [], 0
        for g in range(N_GROUPS):
            total = total + lax.shift_right_logical(cnt_ref[g] + (2 * TM - 1), TM.bit_length() - 1)
            ends.append(total)
        step = jnp.minimum(lax.broadcasted_iota(jnp.int32, (1, LANES), 1), total - 1)
        grp = jnp.zeros((1, LANES), jnp.int32)
        first = jnp.zeros((1, LANES), jnp.int32)
        for g in range(N_GROUPS - 1):
            grp = grp + jnp.where(step >= ends[g], 1, 0)
            first = jnp.where(step >= ends[g], ends[g], first)
        meta_ref[0, 1:2, :] = grp * (HS_CAP // TM) + step - first
        meta_ref[0, 2:3, :] = grp
        meta_ref[0, 3:4, :] = jnp.zeros((1, LANES), jnp.int32) + total


def _merge_rows(alpha, rows, xc_ref, xl_ref, mod_ref, of_ref, ob_ref, sag_ref, ng_ref, zu_ref, zv_ref,
                ws_ref, bs_ref, cc_ref, cl_ref, gate_ref, wb_ref, wo_ref, g1_ref, b1_ref,
                rwh_ref, rwl_ref, rb_ref, x1_ref, h2_ref, cw_ref, gid_ref):
    i = pl.program_id(0)
    m = mod_ref[0]
    n_rows = rows.stop - rows.start
    o = of_ref[rows, :] + ob_ref[rows, :]
    parts = []
    for h in range(GLA_HEADS):
        oh = o[:, h * GLA_DV:(h + 1) * GLA_DV]
        parts.append(oh * lax.rsqrt(jnp.mean(oh * oh, axis=-1, keepdims=True) + LN_EPS))
    br_a = (jnp.concatenate(parts, axis=1) * ng_ref[...] * sag_ref[rows, :]).astype(BF16)
    u = zu_ref[rows, :]
    v = zv_ref[rows, :]
    yield
    chunks = []
    for c in range(n_rows // GMLP_CHUNK):
        vc = v[c * GMLP_CHUNK:(c + 1) * GMLP_CHUNK]
        cols = [_dot(ws_ref[g], vc[:, g * GMLP_CH:(g + 1) * GMLP_CH]) for g in range(GMLP_GROUPS)]
        chunks.append(jnp.concatenate(cols, axis=1) + bs_ref[...])
    br_b = (u * jnp.concatenate(chunks, axis=0)).astype(BF16)
    br_ct = jnp.where(i < NB_CTX, cc_ref[:, rows], cl_ref[:, rows])
    projs = (_dot(br_a, wb_ref[0]), _dot(br_b, wb_ref[1]), _dot_tn(br_ct, wb_ref[2]))
    yield
    y = jnp.zeros((n_rows, D_MODEL), F32)
    for nbr, proj in enumerate(projs):
        y = y + gate_ref[rows, nbr * D_MODEL:(nbr + 1) * D_MODEL] * proj
    y = y.astype(BF16)
    yield
    y = _dot(y, wo_ref[...])
    yield
    x = jnp.where(i < NB_CTX, xc_ref[rows, :], xl_ref[rows, :])
    x1 = _layer_norm(alpha * x + m[:, 2 * D_MODEL:3 * D_MODEL] * y, g1_ref[...], b1_ref[...])
    x1_ref[rows, :] = x1
    h2 = x1 * (1.0 + m[:, 4 * D_MODEL:5 * D_MODEL]) + m[:, 3 * D_MODEL:4 * D_MODEL]
    h2_ref[rows, :] = h2.astype(BF16)
    yield
    h_hi, h_lo = _split(h2)
    logits = (_dot_nt(rwh_ref[...], h_hi) + _dot_nt(rwh_ref[...], h_lo) + _dot_nt(rwl_ref[...], h_hi)
              + rb_ref[:, :n_rows])[:ROUTER_ROWS]
    row_i = lax.broadcasted_iota(jnp.int32, (ROUTER_ROWS, n_rows), 0)
    row = row_i.astype(F32)
    row_group = (row_i >> 2).astype(F32)
    big = float(LANES)
    neg = -jnp.inf
    gl = jnp.where((row_i >= N_EXPERTS) & (row_i < N_EXPERTS + N_GROUPS), logits, neg)
    gmax = jnp.max(gl, axis=0, keepdims=True)
    gsum = jnp.sum(jnp.exp(gl - gmax), axis=0, keepdims=True)
    g_p = 1.0 / gsum
    g_i = jnp.min(jnp.where(gl == gmax, row, big), axis=0, keepdims=True) - float(N_EXPERTS)
    in_group = (row_i < N_EXPERTS) & (row_group == g_i)
    el = jnp.where(in_group, logits, neg)
    emax = jnp.max(el, axis=0, keepdims=True)
    ee = jnp.exp(el - emax)
    e_prob = ee / jnp.sum(ee, axis=0, keepdims=True)
    p1 = jnp.max(jnp.where(in_group, e_prob, neg), axis=0, keepdims=True)
    i1 = jnp.min(jnp.where(in_group & (e_prob == p1), row, big), axis=0, keepdims=True)
    rest = in_group & (row != i1)
    p2 = jnp.max(jnp.where(rest, e_prob, neg), axis=0, keepdims=True)
    i2 = jnp.min(jnp.where(rest & (e_prob == p2), row, big), axis=0, keepdims=True)
    tot = p1 + p2
    cw_t = (jnp.where(row == i1, g_p * p1 / tot, 0.0) + jnp.where(row == i2, g_p * p2 / tot, 0.0))
    cw_ref[rows, :] = jnp.concatenate([cw_t, jnp.zeros((LANES - ROUTER_ROWS, n_rows), F32)], axis=0).T
    gid_ref[0:1, rows] = g_i
    yield


def _merge(l, alpha, xc, xl, mod3, o_f, o_b, sag, norm_g, zu, zv, ws, bs_t, brc_ctx, brc_lat, gates,
           wb, wo, g1, b1, rw_hi, rw_lo, rb):
    utri = jnp.asarray(np.arange(TM)[:, None] < np.arange(TM)[None, :], dtype=BF16)
    return pl.pallas_call(
        functools.partial(_merge_kernel, alpha),
        grid=(NB,),
        in_specs=[_ctx_tok(D_MODEL), _lat_tok(D_MODEL), _mod_spec(l),
                  _tok(512), _tok(512), _tok(512), _layer_spec(l, (1, 512)),
                  _tok(512), _tok(512),
                  _layer_spec(l, (GMLP_GROUPS, GMLP_CHUNK, GMLP_CHUNK)), _layer_spec(l, (GMLP_CHUNK, 512)),
                  pl.BlockSpec((512, TM), lambda i: (0, jnp.minimum(i, NB_CTX - 1))),
                  pl.BlockSpec((512, TM), lambda i: (0, jnp.maximum(i - NB_CTX, 0))),
                  _tok(3072), _layer_spec(l, (N_BRANCH, BRANCH_W, D_MODEL)), _layer_spec(l, (D_MODEL, D_MODEL)),
                  _layer_spec(l, (1, D_MODEL)), _layer_spec(l, (1, D_MODEL)),
                  _layer_spec(l, (LANES, D_MODEL)), _layer_spec(l, (LANES, D_MODEL)), _layer_spec(l, (LANES, TM)),
                  pl.BlockSpec((TM, TM), lambda i: (0, 0))],
        out_specs=[_tok(D_MODEL), _tok(LANES), pl.BlockSpec((1, 8, LANES), lambda i: (i, 0, 0)),
                   pl.BlockSpec(memory_space=pl.ANY)],
        out_shape=[jax.ShapeDtypeStruct((TOKENS, D_MODEL), F32),
                   jax.ShapeDtypeStruct((TOKENS, LANES), F32),
                   jax.ShapeDtypeStruct((NB, 8, LANES), jnp.int32),
                   jax.ShapeDtypeStruct((N_GROUPS * HS_CAP, HS_W), BF16)],
        scratch_shapes=[pltpu.VMEM((TM, D_MODEL), BF16), pltpu.VMEM((TM, LANES), F32), pltpu.VMEM((8, TM), F32),
                        pltpu.VMEM((2, STAGE_ROWS, HS_W), BF16), pltpu.SMEM((N_GROUPS + 1,), jnp.int32),
                        pltpu.SemaphoreType.DMA((1,))],
        compiler_params=_cp(("arbitrary",)),
        name="merge_ln1_router",
    )(xc, xl, mod3, o_f, o_b, sag, norm_g, zu, zv, ws, bs_t, brc_ctx, brc_lat, gates,
      wb, wo, g1, b1, rw_hi, rw_lo, rb, utri)


def _experts_kernel(blk_ref, grp_ref, ntile_ref, hs_ref, wg32_ref, wu32_ref, wd32_ref, ys_ref,
                    wg_ref, wu_ref, wd_ref):
    s = pl.program_id(0)

    @pl.when(jnp.logical_or(s == 0, grp_ref[s] != grp_ref[jnp.maximum(s - 1, 0)]))
    def _():
        wg_ref[...] = wg32_ref[...].astype(BF16)
        wu_ref[...] = wu32_ref[...].astype(BF16)
        wd_ref[...] = wd32_ref[...].astype(BF16)

    @pl.when(s < ntile_ref[0])
    def _():
        g = grp_ref[s]
        x = hs_ref[:, :D_MODEL]
        cw = (hs_ref[:, D_MODEL:D_MODEL + LANES].astype(F32)
              + hs_ref[:, D_MODEL + LANES:D_MODEL + 2 * LANES].astype(F32)
              + hs_ref[:, D_MODEL + 2 * LANES:].astype(F32))
        lane = lax.broadcasted_iota(jnp.int32, (TM, LANES), 1)
        hid = []
        for k in range(EXPERTS_PER_GROUP):
            w_e = jnp.sum(jnp.where(lane == g * EXPERTS_PER_GROUP + k, cw, 0.0), axis=-1, keepdims=True)
            hid.append((_silu(_dot(x, wg_ref[k])) * _dot(x, wu_ref[k]) * w_e).astype(BF16))
        y = _dot(jnp.concatenate(hid, axis=1), wd_ref[...].reshape(EXPERTS_PER_GROUP * EXPERT_FF, D_MODEL))
        ys_ref[...] = y.astype(BF16)


def _experts(l, tile_blk, tile_grp, n_tiles, hs, wg, wu, wd):
    e = EXPERTS_PER_GROUP
    return pl.pallas_call(
        _experts_kernel,
        grid_spec=pltpu.PrefetchScalarGridSpec(
            num_scalar_prefetch=3,
            grid=(MOE_TILES,),
            in_specs=[pl.BlockSpec((TM, HS_W), lambda s, blk, grp, nt: (blk[s], 0)),
                      pl.BlockSpec((None, e, D_MODEL, EXPERT_FF), lambda s, blk, grp, nt: (l, grp[s], 0, 0)),
                      pl.BlockSpec((None, e, D_MODEL, EXPERT_FF), lambda s, blk, grp, nt: (l, grp[s], 0, 0)),
                      pl.BlockSpec((None, e, EXPERT_FF, D_MODEL), lambda s, blk, grp, nt: (l, grp[s], 0, 0))],
            out_specs=pl.BlockSpec((TM, D_MODEL), lambda s, blk, grp, nt: (blk[s], 0)),
            scratch_shapes=[pltpu.VMEM((e, D_MODEL, EXPERT_FF), BF16), pltpu.VMEM((e, D_MODEL, EXPERT_FF), BF16),
                            pltpu.VMEM((e, EXPERT_FF, D_MODEL), BF16)],
        ),
        out_shape=jax.ShapeDtypeStruct((N_GROUPS * HS_CAP, D_MODEL), BF16),
        compiler_params=_cp(("arbitrary",)),
        name="moe_experts",
    )(tile_blk, tile_grp, n_tiles, hs, wg, wu, wd)


def _window_fetch(ys_ref, offs_ref, blk, win_ref, buf, sem_ref, rows):
    return [pltpu.make_async_copy(
        ys_ref.at[pl.ds(pl.multiple_of(offs_ref[blk * N_GROUPS + g], HS_ALIGN), rows), :],
        win_ref.at[buf, pl.ds(g * rows, rows), :], sem_ref.at[buf]) for g in range(N_GROUPS)]


def _combine_kernel(alpha, offs_ref, small_ref, ys_ref, slot_ref, x1_ref, mod_ref, g2_ref, b2_ref,
                    oc_ref, ol_ref, win_ref, y_ref, sem_ref):
    i = pl.program_id(0)
    buf = i % 2

    def start(blk, to_buf):
        def go(rows):
            for cp in _window_fetch(ys_ref, offs_ref, blk, win_ref, to_buf, sem_ref, rows):
                cp.start()
        _by_window(small_ref[blk] == 1, go)

    pl.when(i == 0)(lambda: start(0, 0))
    pl.when(i + 1 < NB)(lambda: start(i + 1, 1 - buf))

    def gather(rows):
        for cp in _window_fetch(ys_ref, offs_ref, i, win_ref, buf, sem_ref, rows):
            cp.wait()
        back = slot_ref[...]
        packed = back - jnp.floor(back * (1.0 / TM)) * float(TM - rows)
        slot = jnp.concatenate([packed] * (N_GROUPS * rows // LANES), axis=1)
        lane = lax.broadcasted_iota(jnp.int32, (TM, N_GROUPS * rows), 1).astype(F32)
        sel = jnp.where(lane == slot, 1.0, 0.0).astype(BF16)
        y_ref[...] = _dot(sel, win_ref[buf, :N_GROUPS * rows, :])

    _by_window(small_ref[i] == 1, gather)
    m = mod_ref[0]
    out = _layer_norm(alpha * x1_ref[...] + m[:, 5 * D_MODEL:6 * D_MODEL] * y_ref[...],
                      g2_ref[...], b2_ref[...])

    @pl.when(i < NB_CTX)
    def _():
        oc_ref[...] = out

    @pl.when(i >= NB_CTX)
    def _():
        ol_ref[...] = out


def _combine(l, alpha, offs, small, ys, slot, x1, mod3, g2, b2):
    return pl.pallas_call(
        functools.partial(_combine_kernel, alpha),
        grid_spec=pltpu.PrefetchScalarGridSpec(
            num_scalar_prefetch=2,
            grid=(NB,),
            in_specs=[pl.BlockSpec(memory_space=pl.ANY),
                      pl.BlockSpec((TM, LANES), lambda i, *_: (i, 0)),
                      pl.BlockSpec((TM, D_MODEL), lambda i, *_: (i, 0)),
                      _mod_spec(l),
                      pl.BlockSpec((None, 1, D_MODEL), lambda i, *_: (l, 0, 0)),
                      pl.BlockSpec((None, 1, D_MODEL), lambda i, *_: (l, 0, 0))],
            out_specs=[pl.BlockSpec((TM, D_MODEL), lambda i, *_: (jnp.minimum(i, NB_CTX - 1), 0)),
                       pl.BlockSpec((TM, D_MODEL), lambda i, *_: (jnp.maximum(i - NB_CTX, 0), 0))],
            scratch_shapes=[pltpu.VMEM((2, N_GROUPS * TM, D_MODEL), BF16), pltpu.VMEM((TM, D_MODEL), F32),
                            pltpu.SemaphoreType.DMA((2,))],
        ),
        out_shape=[jax.ShapeDtypeStruct((CTX_TOKENS, D_MODEL), F32),
                   jax.ShapeDtypeStruct((LAT_TOKENS, D_MODEL), F32)],
        compiler_params=_cp(("arbitrary",)),
        name="combine_ln2",
    )(offs, small, ys, slot, x1, mod3, g2, b2)


def _rope_tables():
    f32 = np.float32
    pos = np.arange(DEC_SEQ)
    row = (pos // GRID_W).astype(f32)
    col = (pos % GRID_W).astype(f32)
    quarter = ATTN_HEAD_DIM // 4
    inv_freq = np.power(f32(ROPE_BASE), -np.arange(quarter, dtype=f32) / f32(quarter)).astype(f32)
    j = np.arange(ATTN_HEAD_DIM)
    p = np.where((j // (2 * quarter))[None, :] == 0, row[:, None], col[:, None]).astype(f32)
    ang = (p * inv_freq[j % quarter][None, :]).astype(f32)
    sign = np.where((j % (2 * quarter)) < quarter, -1.0, 1.0).astype(f32)
    cos = np.tile(np.cos(ang).astype(f32), (1, ATTN_HEADS))
    sin = np.tile((np.sin(ang) * sign[None, :]).astype(f32), (1, ATTN_HEADS))
    cos = np.concatenate([np.ones((TM, 512), f32), cos], axis=0)
    sin = np.concatenate([np.zeros((TM, 512), f32), sin], axis=0)
    return jnp.asarray(cos), jnp.asarray(sin)


def _dup_heads(t):
    h0, h1 = t[..., :ATTN_HEAD_DIM], t[..., ATTN_HEAD_DIM:]
    return jnp.concatenate([h0, h0, h1, h1], axis=-1)


def _split_w_in(w):
    gate_scale = np.where(np.arange(IN_COLS) >= W_MG.start, 0.5, 1.0).astype(np.float32)
    return [(jnp.swapaxes(w, 1, 2) * gate_scale[:, None]).astype(BF16)]


def kernel(x_prompt, x_sample, state_gla, cache_k, cache_v, c, c_ctx, ada_w, ada_b, w_in, gla_gate_w, gla_gate_b, gla_norm_g, gmlp_ln_g, gmlp_ln_b, gmlp_ws, gmlp_bs, attn_sink, w_branch, w_out, ln1_g, ln1_b, ln2_g, ln2_b, router_group_w, router_group_b, router_expert_w, router_expert_b, expert_w_gate, expert_w_up, expert_w_down):
    alpha = (2.0 * DEPTH) ** 0.25
    xc = x_prompt.reshape(CTX_TOKENS, D_MODEL)
    xl = x_sample.reshape(LAT_TOKENS, D_MODEL)
    cond = jnp.concatenate([c_ctx[None, :], c, jnp.zeros((MOD_ROWS - 1 - DEC_BATCH, D_MODEL), F32)], axis=0)
    cos_t, sin_t = _rope_tables()

    ada_b3 = ada_b[:, None, :]
    w_parts = _split_w_in(w_in)
    zrow = jnp.zeros((DEPTH, GLA_RANK, 256), F32)
    zpad = jnp.zeros((DEPTH, LANES - 2 * GLA_RANK, 256), F32)
    gw_p = jnp.stack([jnp.concatenate([gla_gate_w[:, 0], zrow, zpad], axis=1),
                      jnp.concatenate([zrow, gla_gate_w[:, 1], zpad], axis=1)], axis=1).astype(BF16)
    gb_p = gla_gate_b[:, :, None, :]
    sink_t = jnp.broadcast_to(attn_sink[:, :, None, None], (DEPTH, ATTN_HEADS, 1, LANES))
    kc = _dup_heads(cache_k.reshape(DEC_BATCH, DEPTH, PAST_LEN, 128)).astype(BF16)
    vct = jnp.swapaxes(cache_v.reshape(DEC_BATCH, DEPTH, PAST_LEN, 128), 2, 3).astype(BF16)
    rw = jnp.swapaxes(jnp.concatenate(
        [router_expert_w, router_group_w,
         jnp.zeros((DEPTH, D_MODEL, LANES - N_EXPERTS - N_GROUPS), F32)], axis=2), 1, 2)
    rw_hi = rw.astype(BF16)
    rw_lo = (rw - rw_hi.astype(F32)).astype(BF16)
    rb = jnp.concatenate([router_expert_b, router_group_b,
                          jnp.zeros((DEPTH, LANES - N_EXPERTS - N_GROUPS), F32)], axis=1)
    rb = jnp.broadcast_to(rb[:, :, None], (DEPTH, LANES, TM))
    bs_t = jnp.repeat(jnp.swapaxes(gmlp_bs, 1, 2), GMLP_CH, axis=2)
    ws_b = gmlp_ws.astype(BF16)
    wb_b = (0.5 * w_branch).astype(BF16)
    wo_b = w_out.astype(BF16)
    row = lambda t: t[:, None, :]

    mod3 = _modulation(cond, ada_w, ada_b3).reshape(DEPTH, MOD_ROWS, 1, 6 * D_MODEL)
    states, keys, values = [], [], []
    for l in range(DEPTH):
        (qef, kef, klf, qeb, keb, klb, dec, av, avt, sag, zu, zv, cq, ckd, cvt, ck, cv, gates) = _in_projection(
            l, xc, xl, mod3, cos_t, sin_t, gw_p, gb_p, row(gmlp_ln_g), row(gmlp_ln_b), w_parts)
        o_f, o_b, s_fin = _gla(l, qef, kef, klf, qeb, keb, klb, dec, av, avt, state_gla)
        brc_ctx = _context_attention(l, cq, ckd, cvt, sink_t)
        brc_lat = _latent_attention(l, cq, ckd, cvt, kc, vct, sink_t)
        x1, slot, meta, hs = _merge(l, alpha, xc, xl, mod3, o_f, o_b, sag, row(gla_norm_g), zu, zv,
                                    ws_b, bs_t, brc_ctx, brc_lat, gates, wb_b,
                                    wo_b, row(ln1_g), row(ln1_b), rw_hi, rw_lo, rb)
        offs = meta[:, 0, :N_GROUPS].reshape(NB * N_GROUPS)
        tiles = meta[NB - 1]
        ys = _experts(l, tiles[1, :MOE_TILES], tiles[2, :MOE_TILES], tiles[3, :1], hs,
                      expert_w_gate, expert_w_up, expert_w_down)
        xc, xl = _combine(l, alpha, offs, meta[:, 0, 2 * N_GROUPS], ys, slot, x1, mod3,
                          row(ln2_g), row(ln2_b))
        states.append(s_fin)
        keys.append(ck.reshape(BATCH, SEQ, ATTN_KV_HEADS, ATTN_HEAD_DIM))
        values.append(cv.reshape(BATCH, SEQ, ATTN_KV_HEADS, ATTN_HEAD_DIM))
    return (xc.reshape(BATCH, SEQ, D_MODEL), xl.reshape(DEC_BATCH, DEC_SEQ, D_MODEL),
            jnp.stack(states, axis=1), jnp.stack(keys, axis=1), jnp.stack(values, axis=1))
```

```python
import functools

import jax
import jax.numpy as jnp
import numpy as np
from jax import lax
from jax.experimental import pallas as pl
from jax.experimental.pallas import tpu as pltpu

F32 = jnp.float32
BF16 = jnp.bfloat16

D_MODEL = 1024
BATCH = 32
SEQ = 256
DEPTH = 2
DEC_BATCH = 8
DEC_SEQ = 1024
PAST_LEN = 512
GRID_W = 64
GLA_HEADS = 4
GLA_DV = 128
GLA_DK = 64
GLA_RANK = 16
GLA_TAU = 16.0
GLA_CHUNK = 32
GMLP_CHUNK = 128
GMLP_CH = 128
GMLP_GROUPS = 4
ATTN_HEADS = 8
ATTN_KV_HEADS = 2
ATTN_GROUP = ATTN_HEADS // ATTN_KV_HEADS
ATTN_HEAD_DIM = 64
WINDOW = 128
ATTN_BLOCK = 128
ROPE_BASE = 10000.0
BRANCH_W = 512
N_BRANCH = 3
N_GROUPS = 4
EXPERTS_PER_GROUP = 4
N_EXPERTS = 16
EXPERT_FF = 256
LN_EPS = 1e-5

LANES = 128
TM = 256
CTX_TOKENS = BATCH * SEQ
LAT_TOKENS = DEC_BATCH * DEC_SEQ
TOKENS = CTX_TOKENS + LAT_TOKENS
NB_CTX = CTX_TOKENS // TM
NB_LAT = LAT_TOKENS // TM
NB = NB_CTX + NB_LAT
LAT_BLOCKS_PER_SEQ = DEC_SEQ // TM
MOD_ROWS = 16
GLA_SUB = 128
CHUNKS_PER_SUB = GLA_SUB // GLA_CHUNK
LAT_QB = 2
GLA_STAGES = 3
MERGE_ROWS = 128
MERGE_STAGES = 6
ROUTER_ROWS = 24
HS_ALIGN = 16
SMALL_WIN = 128
HS_W = D_MODEL + 3 * LANES
STAGE_ROWS = TM + N_GROUPS * HS_ALIGN + TM
_MAX_PAD = NB * (HS_ALIGN - 1)
HS_CAP = -(-(TOKENS + _MAX_PAD + 2 * TM) // TM) * TM
MOE_TILES = (TOKENS + N_GROUPS * _MAX_PAD) // TM + 2 * N_GROUPS

IN_COLS = 6432
W_AQ, W_AK, W_AV, W_AG = slice(0, 256), slice(256, 512), slice(512, 1024), slice(1024, 1536)
W_LR = slice(1536, 1664)
W_BZ, W_CQ, W_CK, W_CV = slice(1568, 2592), slice(2592, 3104), slice(3104, 3232), slice(3232, 3360)
W_MG = slice(3360, 6432)

VMEM_LIMIT = 56 * 1024 * 1024


def _cp(sem):
    return pltpu.CompilerParams(dimension_semantics=sem, vmem_limit_bytes=VMEM_LIMIT)


def _dot(a, b):
    return jnp.dot(a, b, preferred_element_type=F32)


def _dot_nt(a, b):
    return lax.dot_general(a, b, (((1,), (1,)), ((), ())), preferred_element_type=F32)


def _dot_tn(a, b):
    return lax.dot_general(a, b, (((0,), (0,)), ((), ())), preferred_element_type=F32)


def _split(x):
    hi = x.astype(BF16)
    lo = (x - hi.astype(F32)).astype(BF16)
    return hi, lo


def _sigmoid(x):
    return 0.5 * (jnp.tanh(0.5 * x) + 1.0)


def _silu(x):
    return x * _sigmoid(x)


def _layer_norm(x, g, b):
    mu = jnp.mean(x, axis=-1, keepdims=True)
    xc = x - mu
    var = jnp.mean(xc * xc, axis=-1, keepdims=True)
    return xc * lax.rsqrt(var + LN_EPS) * g + b


def _stagger(parts, n_stages):
    for step in range(n_stages + len(parts) - 1):
        for lag, part in enumerate(parts):
            if 0 <= step - lag < n_stages:
                next(part)


def _mod_block(i):
    return jnp.where(i < NB_CTX, 0, 1 + (i - NB_CTX) // LAT_BLOCKS_PER_SEQ)


def _tok(width):
    return pl.BlockSpec((TM, width), lambda i: (i, 0))


def _ctx_tok(width):
    return pl.BlockSpec((TM, width), lambda i: (jnp.minimum(i, NB_CTX - 1), 0))


def _lat_tok(width):
    return pl.BlockSpec((TM, width), lambda i: (jnp.maximum(i - NB_CTX, 0), 0))


def _layer_spec(l, shape):
    return pl.BlockSpec((None,) + shape, lambda *_: (l,) + (0,) * len(shape))


def _mod_spec(l):
    return pl.BlockSpec((None, 1, 1, 6 * D_MODEL), lambda i, *_: (l, _mod_block(i), 0, 0))


def _mod_kernel(cond_ref, w_ref, b_ref, o_ref):
    s_hi, s_lo = _split(_silu(cond_ref[...]))
    w_hi, w_lo = _split(w_ref[...])
    o_ref[...] = _dot(s_hi, w_hi) + _dot(s_lo, w_hi) + _dot(s_hi, w_lo) + b_ref[...]


def _modulation(cond, ada_w, ada_b3):
    tn = 1536
    return pl.pallas_call(
        _mod_kernel,
        grid=(DEPTH, 6 * D_MODEL // tn),
        in_specs=[pl.BlockSpec((MOD_ROWS, D_MODEL), lambda l, j: (0, 0)),
                  pl.BlockSpec((None, D_MODEL, tn), lambda l, j: (l, 0, j)),
                  pl.BlockSpec((None, 1, tn), lambda l, j: (l, 0, j))],
        out_specs=pl.BlockSpec((None, MOD_ROWS, tn), lambda l, j: (l, 0, j)),
        out_shape=jax.ShapeDtypeStruct((DEPTH, MOD_ROWS, 6 * D_MODEL), F32),
        compiler_params=_cp(("arbitrary", "arbitrary")),
        name="modulation",
    )(cond, ada_w, ada_b3)


def _rope(x, cos, sin):
    n = x.shape[1]
    lane = lax.broadcasted_iota(jnp.int32, x.shape, 1)
    partner = jnp.where((lane & 31) < 16, pltpu.roll(x, n - 16, 1), pltpu.roll(x, 16, 1))
    return x * cos + partner * sin


def _inproj_kernel(xc_ref, xl_ref, mod_ref, cos_ref, sin_ref, gw_ref, gb_ref, lg_ref, lb_ref,
                   wt_ref,
                   qef_ref, kef_ref, klf_ref, qeb_ref, keb_ref, klb_ref, dec_ref,
                   av_ref, avt_ref, sag_ref, zu_ref, zv_ref,
                   cq_ref, ckd_ref, cvt_ref, ck_ref, cv_ref, gate_ref):
    i = pl.program_id(0)
    m = mod_ref[0]
    x = jnp.where(i < NB_CTX, xc_ref[...], xl_ref[...])
    h = (x * (1.0 + m[:, D_MODEL:2 * D_MODEL]) + m[:, 0:D_MODEL]).astype(BF16)

    def proj(rows):
        return _dot_nt(h, wt_ref[rows, :])

    aq = proj(W_AQ)
    ak = proj(W_AK)
    lr = proj(W_LR).astype(BF16)
    zs = [_dot(lr, gw_ref[d]) + gb_ref[d] for d in range(2)]
    for nbr in range(N_BRANCH):
        cols = slice(W_MG.start + nbr * D_MODEL, W_MG.start + (nbr + 1) * D_MODEL)
        gate_ref[:, nbr * D_MODEL:(nbr + 1) * D_MODEL] = jnp.tanh(proj(cols)) + 1.0
    r = lax.broadcasted_iota(jnp.int32, (TM, TM), 0)
    c = lax.broadcasted_iota(jnp.int32, (TM, TM), 1)
    same = (r >> 5) == (c >> 5)
    bs = []
    for d in range(2):
        g = (jnp.minimum(zs[d], 0.0) - jnp.log(1.0 + jnp.exp(-jnp.abs(zs[d])))) / GLA_TAU
        tri_b = jnp.where(same & ((c <= r) if d == 0 else (c >= r)), 1.0, 0.0).astype(BF16)
        g_hi, g_lo = _split(g)
        bs.append(_dot(tri_b, g_hi) + _dot(tri_b, g_lo))
    z = _gelu(proj(W_BZ))
    zu_ref[...] = z[:, :BRANCH_W]
    zv_ref[...] = _layer_norm(z[:, BRANCH_W:], lg_ref[...], lb_ref[...]).astype(BF16)
    sag_ref[...] = _silu(proj(W_AG))
    av = proj(W_AV)
    av_ref[...] = av.astype(BF16)
    avt_ref[...] = av.T.astype(BF16)
    decs = []
    for d, (qe_ref, ke_ref, kl_ref) in enumerate(((qef_ref, kef_ref, klf_ref), (qeb_ref, keb_ref, klb_ref))):
        b = bs[d]
        last = GLA_CHUNK - 1 if d == 0 else 0
        tot = [b[n * GLA_CHUNK + last:n * GLA_CHUNK + last + 1, :] for n in range(TM // GLA_CHUNK)]
        bl = jnp.concatenate([jnp.broadcast_to(t, (GLA_CHUNK, t.shape[1])) for t in tot], axis=0)
        qe_ref[...] = (aq * jnp.exp(b) * (GLA_DK ** -0.5)).astype(BF16)
        ke_ref[...] = (ak * jnp.exp(-b)).astype(BF16)
        kl_ref[...] = (ak * jnp.exp(bl - b)).astype(BF16)
        decs.append(jnp.exp(jnp.concatenate(tot, axis=0)))
    dec_ref[...] = jnp.concatenate(decs, axis=1)

    cos = cos_ref[...]
    sin = sin_ref[...]
    cq_ref[...] = (_rope(proj(W_CQ), cos, sin) * (ATTN_HEAD_DIM ** -0.5)).astype(BF16)
    ck = proj(W_CK)
    cv = proj(W_CV)
    kr = _rope(ck, cos[:, :LANES], sin[:, :LANES])
    swapped = pltpu.roll(kr, ATTN_HEAD_DIM, 1)
    lane = lax.broadcasted_iota(jnp.int32, (TM, LANES), 1)
    ckd_ref[...] = jnp.concatenate([jnp.where(lane < 64, kr, swapped), jnp.where(lane < 64, swapped, kr)],
                                   axis=1).astype(BF16)
    cvt_ref[...] = cv.T.astype(BF16)

    @pl.when(i < NB_CTX)
    def _():
        ck_ref[...] = ck
        cv_ref[...] = cv


def _rope_block(i):
    return jnp.where(i < NB_CTX, 0, 1 + (i - NB_CTX) % LAT_BLOCKS_PER_SEQ)


def _in_projection(l, xc, xl, mod3, cos_t, sin_t, gw_p, gb_p, ln_g, ln_b, w_parts):
    def tok_out(width, dt):
        return _tok(width), jax.ShapeDtypeStruct((TOKENS, width), dt)

    def feat_out(width):
        return (pl.BlockSpec((width, TM), lambda i: (0, i)), jax.ShapeDtypeStruct((width, TOKENS), BF16))

    def ctx_out(width):
        return _ctx_tok(width), jax.ShapeDtypeStruct((CTX_TOKENS, width), F32)

    dec_out = (pl.BlockSpec((TM // GLA_CHUNK, 512), lambda i: (i, 0)),
               jax.ShapeDtypeStruct((TOKENS // GLA_CHUNK, 512), F32))
    outs = [tok_out(256, BF16)] * 6 + [dec_out] + [
        tok_out(512, BF16), feat_out(512), tok_out(512, F32), tok_out(512, F32), tok_out(512, BF16),
        tok_out(512, BF16), tok_out(256, BF16), feat_out(LANES), ctx_out(LANES), ctx_out(LANES),
        tok_out(3072, F32)]
    return pl.pallas_call(
        _inproj_kernel,
        grid=(NB,),
        in_specs=[_ctx_tok(D_MODEL), _lat_tok(D_MODEL), _mod_spec(l),
                  pl.BlockSpec((TM, 512), lambda i: (_rope_block(i), 0)),
                  pl.BlockSpec((TM, 512), lambda i: (_rope_block(i), 0)),
                  _layer_spec(l, (2, LANES, 256)), _layer_spec(l, (2, 1, 256)),
                  _layer_spec(l, (1, 512)), _layer_spec(l, (1, 512))]
                 + [_layer_spec(l, w.shape[1:]) for w in w_parts],
        out_specs=[o[0] for o in outs],
        out_shape=[o[1] for o in outs],
        compiler_params=_cp(("arbitrary",)),
        name="in_projection",
    )(xc, xl, mod3, cos_t, sin_t, gw_p, gb_p, ln_g, ln_b, *w_parts)


def _gla_direction(qe, ke, kl, v, vt, dec, masks, state_ref, reverse, o_ref, rows):
    tri, half_masks, chunk_mask, head_diag = masks
    tri = tri[int(reverse)]
    pairs = range(GLA_HEADS // 2)
    halves, q_exps, kvts = [], [], []
    for p in pairs:
        ps = slice(p * LANES, (p + 1) * LANES)
        qp = qe[:, ps]
        vp = v[:, p * 2 * GLA_DV:(p + 1) * 2 * GLA_DV]
        for hh in range(2):
            s = jnp.where(tri, _dot_nt(qp * half_masks[hh], ke[:, ps]), 0.0).astype(BF16)
            halves.append(_dot(s, vp[:, hh * GLA_DV:(hh + 1) * GLA_DV]))
        k_exp = jnp.concatenate([kl[:, ps]] * CHUNKS_PER_SUB, axis=1) * chunk_mask
        q_exps.append(jnp.concatenate([qp] * CHUNKS_PER_SUB, axis=1) * chunk_mask)
        kvts.append(_dot(vt[p * 2 * GLA_DV:(p + 1) * 2 * GLA_DV, :], k_exp))
    yield
    st_stacks = []
    for p in pairs:
        ps = slice(p * LANES, (p + 1) * LANES)
        st = state_ref[p]
        entering = [None] * CHUNKS_PER_SUB
        order = range(CHUNKS_PER_SUB - 1, -1, -1) if reverse else range(CHUNKS_PER_SUB)
        for n in order:
            entering[n] = st
            st = dec[n:n + 1, ps] * st + jnp.where(head_diag, kvts[p][:, n * LANES:(n + 1) * LANES], 0.0)
        state_ref[p] = st
        st_stacks.append(jnp.concatenate(entering, axis=1).astype(BF16))
    yield
    outs = []
    for p in pairs:
        inter = _dot_nt(q_exps[p], st_stacks[p])
        outs.append(halves[2 * p] + inter[:, :GLA_DV])
        outs.append(halves[2 * p + 1] + inter[:, GLA_DV:])
    o_ref[rows, :] = jnp.concatenate(outs, axis=1)
    yield


def _gla_mask_constants():
    r = np.arange(GLA_SUB)[:, None]
    c = np.arange(GLA_SUB)[None, :]
    same = (r // GLA_CHUNK) == (c // GLA_CHUNK)
    tri = np.stack([same & (c <= r), same & (c >= r)]).astype(np.float32)
    lane = np.arange(LANES)[None, :]
    halves = np.stack([np.broadcast_to(lane < 64, (GLA_SUB, LANES)),
                       np.broadcast_to(lane >= 64, (GLA_SUB, LANES))]).astype(np.float32)
    col_chunk = np.arange(CHUNKS_PER_SUB * LANES)[None, :] // LANES
    chunk_mask = ((r // GLA_CHUNK) == col_chunk).astype(np.float32)
    head_diag = ((np.arange(2 * GLA_DV)[:, None] // GLA_DV) == (lane // GLA_DK)).astype(np.float32)
    return (jnp.asarray(tri), jnp.asarray(halves, dtype=BF16), jnp.asarray(chunk_mask, dtype=BF16),
            jnp.asarray(head_diag))


def _gla_kernel(tri_ref, half_ref, cmask_ref, hdiag_ref,
                qf_ref, kef_ref, klf_ref, vf_ref, vtf_ref, decf_ref,
                qb_ref, keb_ref, klb_ref, vb_ref, vtb_ref, decb_ref,
                s0_ref, of_ref, ob_ref, sfin_ref, state_ref):
    i = pl.program_id(0)
    @pl.when(i < NB_CTX)
    def _():
        state_ref[...] = jnp.zeros_like(state_ref)

    @pl.when(jnp.logical_and(i >= NB_CTX, (i - NB_CTX) % LAT_BLOCKS_PER_SEQ == 0))
    def _():
        zero = jnp.zeros((GLA_DK, GLA_DV), F32)
        for d in range(2):
            for p in range(2):
                pair = jnp.concatenate(
                    [jnp.concatenate([s0_ref[0, d, 2 * p], zero], axis=1),
                     jnp.concatenate([zero, s0_ref[0, d, 2 * p + 1]], axis=1)], axis=0)
                state_ref[d, p] = pair.T

    masks = ((tri_ref[0] > 0.5, tri_ref[1] > 0.5), (half_ref[0], half_ref[1]), cmask_ref[...],
             hdiag_ref[...] > 0.5)
    n_sub = TM // GLA_SUB
    parts = []
    for j in range(n_sub):
        rs = slice(j * GLA_SUB, (j + 1) * GLA_SUB)
        parts.append(_gla_direction(
            qf_ref[rs, :], kef_ref[rs, :], klf_ref[rs, :], vf_ref[rs, :], vtf_ref[:, rs],
            decf_ref[j * CHUNKS_PER_SUB:(j + 1) * CHUNKS_PER_SUB, :256], masks, state_ref.at[0], False,
            of_ref, rs))
        jb = n_sub - 1 - j
        rb = slice(jb * GLA_SUB, (jb + 1) * GLA_SUB)
        parts.append(_gla_direction(
            qb_ref[rb, :], keb_ref[rb, :], klb_ref[rb, :], vb_ref[rb, :], vtb_ref[:, rb],
            decb_ref[jb * CHUNKS_PER_SUB:(jb + 1) * CHUNKS_PER_SUB, 256:], masks, state_ref.at[1], True,
            ob_ref, rb))
    _stagger(parts, GLA_STAGES)

    @pl.when(i < NB_CTX)
    def _():
        for d in range(2):
            for p in range(2):
                sp = state_ref[d, p].T
                sfin_ref[0, d, 2 * p] = sp[:GLA_DK, :GLA_DV]
                sfin_ref[0, d, 2 * p + 1] = sp[GLA_DK:, GLA_DV:]


def _bwd_block(i):
    k = (i - NB_CTX) % LAT_BLOCKS_PER_SEQ
    return jnp.where(i < NB_CTX, i, i - k + (LAT_BLOCKS_PER_SEQ - 1 - k))


def _gla(l, qef, kef, klf, qeb, keb, klb, dec, av, avt, state_gla):
    def bwd(width):
        return pl.BlockSpec((TM, width), lambda i: (_bwd_block(i), 0))

    def lat_seq(i):
        return jnp.maximum(i - NB_CTX, 0) // LAT_BLOCKS_PER_SEQ

    vt_fwd = pl.BlockSpec((512, TM), lambda i: (0, i))
    vt_bwd = pl.BlockSpec((512, TM), lambda i: (0, _bwd_block(i)))
    dec_fwd = pl.BlockSpec((TM // GLA_CHUNK, 512), lambda i: (i, 0))
    dec_bwd = pl.BlockSpec((TM // GLA_CHUNK, 512), lambda i: (_bwd_block(i), 0))
    state_dims = (2, GLA_HEADS, GLA_DK, GLA_DV)
    consts = _gla_mask_constants()
    return pl.pallas_call(
        _gla_kernel,
        grid=(NB,),
        in_specs=[pl.BlockSpec(t.shape, lambda i, nd=t.ndim: (0,) * nd) for t in consts]
                 + [_tok(256), _tok(256), _tok(256), _tok(512), vt_fwd, dec_fwd,
                  bwd(256), bwd(256), bwd(256), bwd(512), vt_bwd, dec_bwd,
                  pl.BlockSpec((1, None) + state_dims, lambda i: (lat_seq(i), l, 0, 0, 0, 0))],
        out_specs=[_tok(512), bwd(512),
                   pl.BlockSpec((1,) + state_dims, lambda i: (jnp.minimum(i, NB_CTX - 1), 0, 0, 0, 0))],
        out_shape=[jax.ShapeDtypeStruct((TOKENS, 512), F32), jax.ShapeDtypeStruct((TOKENS, 512), F32),
                   jax.ShapeDtypeStruct((BATCH,) + state_dims, F32)],
        scratch_shapes=[pltpu.VMEM((2, 2, 2 * GLA_DV, LANES), F32)],
        compiler_params=_cp(("arbitrary",)),
        name="gla_scan",
    )(*consts, qef, kef, klf, av, avt, dec, qeb, keb, klb, av, avt, dec, state_gla)


def _attend_group(q_refs, keys, values_t, masks, sink_ref, g, o_ref):
    rows = q_refs[0].shape[0]
    lane = lax.broadcasted_iota(jnp.int32, (rows, LANES), 1)
    stacked = []
    for q in q_refs:
        q32 = q.astype(F32)
        stacked.append(jnp.where(lane < 64, q32, 0.0).astype(BF16))
        stacked.append(jnp.where(lane >= 64, q32, 0.0).astype(BF16))
    qs = jnp.concatenate(stacked, axis=0)
    col_head = lax.broadcasted_iota(jnp.int32, (1, ATTN_GROUP * rows), 1) // rows
    sink = jnp.zeros((1, ATTN_GROUP * rows), F32)
    for hh in range(ATTN_GROUP):
        sink = jnp.where(col_head == hh, sink_ref[ATTN_GROUP * g + hh][:, :1], sink)
    ss = []
    m = sink
    for kk, mk in zip(keys, masks):
        s = _dot_nt(kk, qs)
        if mk is not None:
            s = jnp.where(jnp.concatenate([mk] * ATTN_GROUP, axis=1), s, -jnp.inf)
        ss.append(s)
        m = jnp.maximum(m, jnp.max(s, axis=0, keepdims=True))
    den = jnp.exp(sink - m)
    acc = jnp.zeros((ATTN_HEAD_DIM, ATTN_GROUP * rows), F32)
    for s, vt in zip(ss, values_t):
        e = jnp.exp(s - m)
        den = den + jnp.sum(e, axis=0, keepdims=True)
        acc = acc + _dot(vt, e.astype(BF16))
    o = (acc / den).astype(BF16)
    for hh in range(ATTN_GROUP):
        h = ATTN_GROUP * g + hh
        o_ref[h * ATTN_HEAD_DIM:(h + 1) * ATTN_HEAD_DIM, :] = o[:, hh * rows:(hh + 1) * rows]


def _ctx_attn_kernel(q_ref, kd_ref, vt_ref, sink_ref, o_ref):
    for g in range(ATTN_KV_HEADS):
        gl = slice(g * LANES, (g + 1) * LANES)
        q_refs = [q_ref[:, (2 * g + pp) * LANES:(2 * g + pp + 1) * LANES] for pp in range(2)]
        vt = vt_ref[g * ATTN_HEAD_DIM:(g + 1) * ATTN_HEAD_DIM, :]
        _attend_group(q_refs, [kd_ref[:, gl]], [vt], [None], sink_ref, g, o_ref)


def _context_attention(l, cq, ckd, cvt, sink_t):
    return pl.pallas_call(
        _ctx_attn_kernel,
        grid=(BATCH,),
        in_specs=[pl.BlockSpec((SEQ, 512), lambda b: (b, 0)),
                  pl.BlockSpec((SEQ, 256), lambda b: (b, 0)),
                  pl.BlockSpec((LANES, SEQ), lambda b: (0, b)),
                  _layer_spec(l, (ATTN_HEADS, 1, LANES))],
        out_specs=pl.BlockSpec((512, SEQ), lambda b: (0, b)),
        out_shape=jax.ShapeDtypeStruct((512, CTX_TOKENS), BF16),
        compiler_params=_cp(("arbitrary",)),
        name="context_attention",
    )(cq, ckd, cvt, sink_t)


def _lat_attn_kernel(q_ref, kd_ref, vt_ref, kc_ref, vct_ref, sink_ref, o_ref):
    nq = DEC_SEQ // ATTN_BLOCK
    kj = lax.broadcasted_iota(jnp.int32, (ATTN_BLOCK, ATTN_BLOCK), 0)
    qi = lax.broadcasted_iota(jnp.int32, (ATTN_BLOCK, ATTN_BLOCK), 1)
    for j in range(LAT_QB):
        n = pl.program_id(1) * LAT_QB + j
        qrows = slice(j * ATTN_BLOCK, (j + 1) * ATTN_BLOCK)
        prev0 = pl.multiple_of(jnp.maximum(n - 1, 0) * ATTN_BLOCK, ATTN_BLOCK)
        cur0 = pl.multiple_of(n * ATTN_BLOCK, ATTN_BLOCK)
        next0 = pl.multiple_of(jnp.minimum(n + 1, nq - 1) * ATTN_BLOCK, ATTN_BLOCK)
        m_prev = jnp.logical_and(kj >= qi, n > 0)
        m_next = jnp.logical_and(kj <= qi, n < nq - 1)
        for g in range(ATTN_KV_HEADS):
            gl = slice(g * LANES, (g + 1) * LANES)
            gr = slice(g * ATTN_HEAD_DIM, (g + 1) * ATTN_HEAD_DIM)
            q_refs = [q_ref[qrows, (2 * g + pp) * LANES:(2 * g + pp + 1) * LANES] for pp in range(2)]
            keys = [kd_ref[pl.ds(prev0, ATTN_BLOCK), gl], kd_ref[pl.ds(cur0, ATTN_BLOCK), gl],
                    kd_ref[pl.ds(next0, ATTN_BLOCK), gl], kc_ref[0, :, gl]]
            vals = [vt_ref[gr, pl.ds(prev0, ATTN_BLOCK)], vt_ref[gr, pl.ds(cur0, ATTN_BLOCK)],
                    vt_ref[gr, pl.ds(next0, ATTN_BLOCK)], vct_ref[0, gr, :]]
            _attend_group(q_refs, keys, vals, [m_prev, None, m_next, None], sink_ref, g,
                          o_ref.at[:, qrows])


def _latent_attention(l, cq, ckd, cvt, kc, vct, sink_t):
    rows = LAT_QB * ATTN_BLOCK
    nq = DEC_SEQ // rows
    q0 = CTX_TOKENS // rows
    s0 = CTX_TOKENS // DEC_SEQ
    return pl.pallas_call(
        _lat_attn_kernel,
        grid=(DEC_BATCH, nq),
        in_specs=[pl.BlockSpec((rows, 512), lambda b, n: (q0 + b * nq + n, 0)),
                  pl.BlockSpec((DEC_SEQ, 256), lambda b, n: (s0 + b, 0)),
                  pl.BlockSpec((LANES, DEC_SEQ), lambda b, n: (0, s0 + b)),
                  pl.BlockSpec((1, None, PAST_LEN, 256), lambda b, n: (b, l, 0, 0)),
                  pl.BlockSpec((1, None, LANES, PAST_LEN), lambda b, n: (b, l, 0, 0)),
                  _layer_spec(l, (ATTN_HEADS, 1, LANES))],
        out_specs=pl.BlockSpec((512, rows), lambda b, n: (0, b * nq + n)),
        out_shape=jax.ShapeDtypeStruct((512, LAT_TOKENS), BF16),
        compiler_params=_cp(("arbitrary", "arbitrary")),
        name="latent_attention",
    )(cq, ckd, cvt, kc, vct, sink_t)


def _gelu(x):
    return 0.5 * x * (1.0 + jnp.tanh(0.7978845608028654 * (x + 0.044715 * (x * x * x))))


def _merge_kernel(alpha, xc_ref, xl_ref, mod_ref, of_ref, ob_ref, sag_ref, ng_ref, zu_ref, zv_ref,
                  ws_ref, bs_ref, cc_ref, cl_ref, gate_ref, wb_ref, wo_ref, g1_ref, b1_ref,
                  rwh_ref, rwl_ref, rb_ref, utri_ref,
                  x1_ref, slot_ref, meta_ref, hs_ref,
                  h2_ref, cw_ref, gid_ref, stage_ref, cnt_ref, sem_ref):
    parts = [_merge_rows(alpha, slice(p * MERGE_ROWS, (p + 1) * MERGE_ROWS), xc_ref, xl_ref, mod_ref,
                         of_ref, ob_ref, sag_ref, ng_ref, zu_ref, zv_ref, ws_ref, bs_ref, cc_ref,
                         cl_ref, gate_ref, wb_ref, wo_ref, g1_ref, b1_ref, rwh_ref, rwl_ref, rb_ref,
                         x1_ref, h2_ref, cw_ref, gid_ref) for p in range(TM // MERGE_ROWS)]
    _stagger(parts, MERGE_STAGES)
    _dispatch(h2_ref, cw_ref, gid_ref, utri_ref, slot_ref, meta_ref, hs_ref, stage_ref, cnt_ref, sem_ref)


def _aligned(row):
    return row if isinstance(row, int) else pl.multiple_of(row, HS_ALIGN)


def _window_copy(stage_ref, buf, src_row, hs_ref, dst_row, sem_ref, rows):
    return pltpu.make_async_copy(stage_ref.at[buf, pl.ds(_aligned(src_row), rows), :],
                                 hs_ref.at[pl.ds(_aligned(dst_row), rows), :], sem_ref.at[0])


def _by_window(small, fn):
    pl.when(small)(lambda: fn(SMALL_WIN))
    pl.when(jnp.logical_not(small))(lambda: fn(TM))


def _dispatch(h2_ref, cw_ref, gid_ref, utri_ref, slot_ref, meta_ref, hs_ref, stage_ref, cnt_ref, sem_ref):
    i = pl.program_id(0)

    @pl.when(i == 0)
    def _():
        for g in range(N_GROUPS + 1):
            cnt_ref[g] = 0

    gid = gid_ref[0:1, :]
    grp_i = lax.broadcasted_iota(jnp.int32, (8, TM), 0)
    onehot = jnp.where(grp_i.astype(F32) == gid, 1.0, 0.0)
    rank = _dot(onehot.astype(BF16), utri_ref[...])
    ncol = (rank[:, TM - 1:TM] + onehot[:, TM - 1:TM]).astype(jnp.int32)
    padc = ((ncol + (HS_ALIGN - 1)) // HS_ALIGN) * HS_ALIGN
    row8 = lax.broadcasted_iota(jnp.int32, (8, 1), 0)
    startc = jnp.zeros((8, 1), jnp.int32)
    run = jnp.zeros((1, 1), jnp.int32)
    for g in range(1, N_GROUPS):
        run = run + padc[g - 1:g, :]
        startc = jnp.where(row8 == g, run, startc)
    slot = jnp.sum(onehot * (rank + startc.astype(F32)), axis=0, keepdims=True)
    back = jnp.sum(onehot * (rank + (row8 * TM).astype(F32)), axis=0, keepdims=True)
    slot_ref[...] = jnp.broadcast_to(back, (LANES, TM)).T
    sel = jnp.where(lax.broadcasted_iota(jnp.int32, (STAGE_ROWS, TM), 0).astype(F32) == slot, 1.0, 0.0)
    cw = cw_ref[...]
    cw_hi = cw.astype(BF16)
    r1 = cw - cw_hi.astype(F32)
    cw_mid = r1.astype(BF16)
    cw_lo = (r1 - cw_mid.astype(F32)).astype(BF16)
    rowdata = jnp.concatenate([h2_ref[...], cw_hi, cw_mid, cw_lo], axis=1)
    buf = i % 2
    stage_ref[buf] = _dot(sel.astype(BF16), rowdata).astype(BF16)

    small = jnp.max(ncol, axis=0, keepdims=True)[0, 0] <= SMALL_WIN
    prev_small = cnt_ref[N_GROUPS] == 1

    def wait_windows(rows):
        for _ in range(N_GROUPS):
            _window_copy(stage_ref, buf, 0, hs_ref, 0, sem_ref, rows).wait()

    @pl.when(i > 0)
    def _():
        _by_window(prev_small, wait_windows)

    lane = lax.broadcasted_iota(jnp.int32, (8, LANES), 1)
    meta = jnp.where(lane == 2 * N_GROUPS, small.astype(jnp.int32), jnp.zeros((8, LANES), jnp.int32))
    starts, dsts = [], []
    for g in range(N_GROUPS):
        starts.append(startc[g, 0])
        dsts.append(g * HS_CAP + cnt_ref[g])
        cnt_ref[g] = cnt_ref[g] + padc[g, 0]
        meta = jnp.where(lane == g, dsts[g], meta)
        meta = jnp.where(lane == N_GROUPS + g, cnt_ref[g], meta)
    meta_ref[0] = meta
    cnt_ref[N_GROUPS] = small.astype(jnp.int32)

    def start_windows(rows):
        for g in range(N_GROUPS):
            _window_copy(stage_ref, buf, starts[g], hs_ref, dsts[g], sem_ref, rows).start()

    _by_window(small, start_windows)

    @pl.when(i == NB - 1)
    def _():
        _by_window(small, wait_windows)
        for g in range(N_GROUPS):
            for w in range(2):
                _window_copy(stage_ref, buf, STAGE_ROWS - TM, hs_ref, g * HS_CAP + cnt_ref[g] + w * TM,
                             sem_ref, TM).start()
        for _ in range(2 * N_GROUPS):
            _window_copy(stage_ref, buf, 0, hs_ref, 0, sem_ref, TM).wait()
        ends, total = [], 0
        for g in range(N_GROUPS):
            total = total + lax.shift_right_logical(cnt_ref[g] + (2 * TM - 1), TM.bit_length() - 1)
            ends.append(total)
        step = jnp.minimum(lax.broadcasted_iota(jnp.int32, (1, LANES), 1), total - 1)
        grp = jnp.zeros((1, LANES), jnp.int32)
        first = jnp.zeros((1, LANES), jnp.int32)
        for g in range(N_GROUPS - 1):
            grp = grp + jnp.where(step >= ends[g], 1, 0)
            first = jnp.where(step >= ends[g], ends[g], first)
        meta_ref[0, 1:2, :] = grp * (HS_CAP // TM) + step - first
        meta_ref[0, 2:3, :] = grp
        meta_ref[0, 3:4, :] = jnp.zeros((1, LANES), jnp.int32) + total


def _merge_rows(alpha, rows, xc_ref, xl_ref, mod_ref, of_ref, ob_ref, sag_ref, ng_ref, zu_ref, zv_ref,
                ws_ref, bs_ref, cc_ref, cl_ref, gate_ref, wb_ref, wo_ref, g1_ref, b1_ref,
                rwh_ref, rwl_ref, rb_ref, x1_ref, h2_ref, cw_ref, gid_ref):
    i = pl.program_id(0)
    m = mod_ref[0]
    n_rows = rows.stop - rows.start
    o = of_ref[rows, :] + ob_ref[rows, :]
    parts = []
    for h in range(GLA_HEADS):
        oh = o[:, h * GLA_DV:(h + 1) * GLA_DV]
        parts.append(oh * lax.rsqrt(jnp.mean(oh * oh, axis=-1, keepdims=True) + LN_EPS))
    br_a = (jnp.concatenate(parts, axis=1) * ng_ref[...] * sag_ref[rows, :]).astype(BF16)
    u = zu_ref[rows, :]
    v = zv_ref[rows, :]
    yield
    chunks = []
    for c in range(n_rows // GMLP_CHUNK):
        vc = v[c * GMLP_CHUNK:(c + 1) * GMLP_CHUNK]
        cols = [_dot(ws_ref[g], vc[:, g * GMLP_CH:(g + 1) * GMLP_CH]) for g in range(GMLP_GROUPS)]
        chunks.append(jnp.concatenate(cols, axis=1) + bs_ref[...])
    br_b = (u * jnp.concatenate(chunks, axis=0)).astype(BF16)
    br_ct = jnp.where(i < NB_CTX, cc_ref[:, rows], cl_ref[:, rows])
    projs = (_dot(br_a, wb_ref[0]), _dot(br_b, wb_ref[1]), _dot_tn(br_ct, wb_ref[2]))
    yield
    y = jnp.zeros((n_rows, D_MODEL), F32)
    for nbr, proj in enumerate(projs):
        y = y + gate_ref[rows, nbr * D_MODEL:(nbr + 1) * D_MODEL] * proj
    y = y.astype(BF16)
    yield
    y = _dot(y, wo_ref[...])
    yield
    x = jnp.where(i < NB_CTX, xc_ref[rows, :], xl_ref[rows, :])
    x1 = _layer_norm(alpha * x + m[:, 2 * D_MODEL:3 * D_MODEL] * y, g1_ref[...], b1_ref[...])
    x1_ref[rows, :] = x1
    h2 = x1 * (1.0 + m[:, 4 * D_MODEL:5 * D_MODEL]) + m[:, 3 * D_MODEL:4 * D_MODEL]
    h2_ref[rows, :] = h2.astype(BF16)
    yield
    h_hi, h_lo = _split(h2)
    logits = (_dot_nt(rwh_ref[...], h_hi) + _dot_nt(rwh_ref[...], h_lo) + _dot_nt(rwl_ref[...], h_hi)
              + rb_ref[:, :n_rows])[:ROUTER_ROWS]
    row_i = lax.broadcasted_iota(jnp.int32, (ROUTER_ROWS, n_rows), 0)
    row = row_i.astype(F32)
    row_group = (row_i >> 2).astype(F32)
    big = float(LANES)
    neg = -jnp.inf
    gl = jnp.where((row_i >= N_EXPERTS) & (row_i < N_EXPERTS + N_GROUPS), logits, neg)
    gmax = jnp.max(gl, axis=0, keepdims=True)
    gsum = jnp.sum(jnp.exp(gl - gmax), axis=0, keepdims=True)
    g_p = 1.0 / gsum
    g_i = jnp.min(jnp.where(gl == gmax, row, big), axis=0, keepdims=True) - float(N_EXPERTS)
    in_group = (row_i < N_EXPERTS) & (row_group == g_i)
    el = jnp.where(in_group, logits, neg)
    emax = jnp.max(el, axis=0, keepdims=True)
    ee = jnp.exp(el - emax)
    e_prob = ee / jnp.sum(ee, axis=0, keepdims=True)
    p1 = jnp.max(jnp.where(in_group, e_prob, neg), axis=0, keepdims=True)
    i1 = jnp.min(jnp.where(in_group & (e_prob == p1), row, big), axis=0, keepdims=True)
    rest = in_group & (row != i1)
    p2 = jnp.max(jnp.where(rest, e_prob, neg), axis=0, keepdims=True)
    i2 = jnp.min(jnp.where(rest & (e_prob == p2), row, big), axis=0, keepdims=True)
    tot = p1 + p2
    cw_t = (jnp.where(row == i1, g_p * p1 / tot, 0.0) + jnp.where(row == i2, g_p * p2 / tot, 0.0))
    cw_ref[rows, :] = jnp.concatenate([cw_t, jnp.zeros((LANES - ROUTER_ROWS, n_rows), F32)], axis=0).T
    gid_ref[0:1, rows] = g_i
    yield


def _merge(l, alpha, xc, xl, mod3, o_f, o_b, sag, norm_g, zu, zv, ws, bs_t, brc_ctx, brc_lat, gates,
           wb, wo, g1, b1, rw_hi, rw_lo, rb):
    utri = jnp.asarray(np.arange(TM)[:, None] < np.arange(TM)[None, :], dtype=BF16)
    return pl.pallas_call(
        functools.partial(_merge_kernel, alpha),
        grid=(NB,),
        in_specs=[_ctx_tok(D_MODEL), _lat_tok(D_MODEL), _mod_spec(l),
                  _tok(512), _tok(512), _tok(512), _layer_spec(l, (1, 512)),
                  _tok(512), _tok(512),
                  _layer_spec(l, (GMLP_GROUPS, GMLP_CHUNK, GMLP_CHUNK)), _layer_spec(l, (GMLP_CHUNK, 512)),
                  pl.BlockSpec((512, TM), lambda i: (0, jnp.minimum(i, NB_CTX - 1))),
                  pl.BlockSpec((512, TM), lambda i: (0, jnp.maximum(i - NB_CTX, 0))),
                  _tok(3072), _layer_spec(l, (N_BRANCH, BRANCH_W, D_MODEL)), _layer_spec(l, (D_MODEL, D_MODEL)),
                  _layer_spec(l, (1, D_MODEL)), _layer_spec(l, (1, D_MODEL)),
                  _layer_spec(l, (LANES, D_MODEL)), _layer_spec(l, (LANES, D_MODEL)), _layer_spec(l, (LANES, TM)),
                  pl.BlockSpec((TM, TM), lambda i: (0, 0))],
        out_specs=[_tok(D_MODEL), _tok(LANES), pl.BlockSpec((1, 8, LANES), lambda i: (i, 0, 0)),
                   pl.BlockSpec(memory_space=pl.ANY)],
        out_shape=[jax.ShapeDtypeStruct((TOKENS, D_MODEL), F32),
                   jax.ShapeDtypeStruct((TOKENS, LANES), F32),
                   jax.ShapeDtypeStruct((NB, 8, LANES), jnp.int32),
                   jax.ShapeDtypeStruct((N_GROUPS * HS_CAP, HS_W), BF16)],
        scratch_shapes=[pltpu.VMEM((TM, D_MODEL), BF16), pltpu.VMEM((TM, LANES), F32), pltpu.VMEM((8, TM), F32),
                        pltpu.VMEM((2, STAGE_ROWS, HS_W), BF16), pltpu.SMEM((N_GROUPS + 1,), jnp.int32),
                        pltpu.SemaphoreType.DMA((1,))],
        compiler_params=_cp(("arbitrary",)),
        name="merge_ln1_router",
    )(xc, xl, mod3, o_f, o_b, sag, norm_g, zu, zv, ws, bs_t, brc_ctx, brc_lat, gates,
      wb, wo, g1, b1, rw_hi, rw_lo, rb, utri)


def _experts_kernel(blk_ref, grp_ref, ntile_ref, hs_ref, wg32_ref, wu32_ref, wd32_ref, ys_ref,
                    wg_ref, wu_ref, wd_ref):
    s = pl.program_id(0)

    @pl.when(jnp.logical_or(s == 0, grp_ref[s] != grp_ref[jnp.maximum(s - 1, 0)]))
    def _():
        wg_ref[...] = wg32_ref[...].astype(BF16)
        wu_ref[...] = wu32_ref[...].astype(BF16)
        wd_ref[...] = wd32_ref[...].astype(BF16)

    @pl.when(s < ntile_ref[0])
    def _():
        g = grp_ref[s]
        x = hs_ref[:, :D_MODEL]
        cw = (hs_ref[:, D_MODEL:D_MODEL + LANES].astype(F32)
              + hs_ref[:, D_MODEL + LANES:D_MODEL + 2 * LANES].astype(F32)
              + hs_ref[:, D_MODEL + 2 * LANES:].astype(F32))
        lane = lax.broadcasted_iota(jnp.int32, (TM, LANES), 1)
        hid = []
        for k in range(EXPERTS_PER_GROUP):
            w_e = jnp.sum(jnp.where(lane == g * EXPERTS_PER_GROUP + k, cw, 0.0), axis=-1, keepdims=True)
            hid.append((_silu(_dot(x, wg_ref[k])) * _dot(x, wu_ref[k]) * w_e).astype(BF16))
        y = _dot(jnp.concatenate(hid, axis=1), wd_ref[...].reshape(EXPERTS_PER_GROUP * EXPERT_FF, D_MODEL))
        ys_ref[...] = y.astype(BF16)


def _experts(l, tile_blk, tile_grp, n_tiles, hs, wg, wu, wd):
    e = EXPERTS_PER_GROUP
    return pl.pallas_call(
        _experts_kernel,
        grid_spec=pltpu.PrefetchScalarGridSpec(
            num_scalar_prefetch=3,
            grid=(MOE_TILES,),
            in_specs=[pl.BlockSpec((TM, HS_W), lambda s, blk, grp, nt: (blk[s], 0)),
                      pl.BlockSpec((None, e, D_MODEL, EXPERT_FF), lambda s, blk, grp, nt: (l, grp[s], 0, 0)),
                      pl.BlockSpec((None, e, D_MODEL, EXPERT_FF), lambda s, blk, grp, nt: (l, grp[s], 0, 0)),
                      pl.BlockSpec((None, e, EXPERT_FF, D_MODEL), lambda s, blk, grp, nt: (l, grp[s], 0, 0))],
            out_specs=pl.BlockSpec((TM, D_MODEL), lambda s, blk, grp, nt: (blk[s], 0)),
            scratch_shapes=[pltpu.VMEM((e, D_MODEL, EXPERT_FF), BF16), pltpu.VMEM((e, D_MODEL, EXPERT_FF), BF16),
                            pltpu.VMEM((e, EXPERT_FF, D_MODEL), BF16)],
        ),
        out_shape=jax.ShapeDtypeStruct((N_GROUPS * HS_CAP, D_MODEL), BF16),
        compiler_params=_cp(("arbitrary",)),
        name="moe_experts",
    )(tile_blk, tile_grp, n_tiles, hs, wg, wu, wd)


def _window_fetch(ys_ref, offs_ref, blk, win_ref, buf, sem_ref, rows):
    return [pltpu.make_async_copy(
        ys_ref.at[pl.ds(pl.multiple_of(offs_ref[blk * N_GROUPS + g], HS_ALIGN), rows), :],
        win_ref.at[buf, pl.ds(g * rows, rows), :], sem_ref.at[buf]) for g in range(N_GROUPS)]


def _combine_kernel(alpha, offs_ref, small_ref, ys_ref, slot_ref, x1_ref, mod_ref, g2_ref, b2_ref,
                    oc_ref, ol_ref, win_ref, y_ref, sem_ref):
    i = pl.program_id(0)
    buf = i % 2

    def start(blk, to_buf):
        def go(rows):
            for cp in _window_fetch(ys_ref, offs_ref, blk, win_ref, to_buf, sem_ref, rows):
                cp.start()
        _by_window(small_ref[blk] == 1, go)

    pl.when(i == 0)(lambda: start(0, 0))
    pl.when(i + 1 < NB)(lambda: start(i + 1, 1 - buf))

    def gather(rows):
        for cp in _window_fetch(ys_ref, offs_ref, i, win_ref, buf, sem_ref, rows):
            cp.wait()
        back = slot_ref[...]
        packed = back - jnp.floor(back * (1.0 / TM)) * float(TM - rows)
        slot = jnp.concatenate([packed] * (N_GROUPS * rows // LANES), axis=1)
        lane = lax.broadcasted_iota(jnp.int32, (TM, N_GROUPS * rows), 1).astype(F32)
        sel = jnp.where(lane == slot, 1.0, 0.0).astype(BF16)
        y_ref[...] = _dot(sel, win_ref[buf, :N_GROUPS * rows, :])

    _by_window(small_ref[i] == 1, gather)
    m = mod_ref[0]
    out = _layer_norm(alpha * x1_ref[...] + m[:, 5 * D_MODEL:6 * D_MODEL] * y_ref[...],
                      g2_ref[...], b2_ref[...])

    @pl.when(i < NB_CTX)
    def _():
        oc_ref[...] = out

    @pl.when(i >= NB_CTX)
    def _():
        ol_ref[...] = out


def _combine(l, alpha, offs, small, ys, slot, x1, mod3, g2, b2):
    return pl.pallas_call(
        functools.partial(_combine_kernel, alpha),
        grid_spec=pltpu.PrefetchScalarGridSpec(
            num_scalar_prefetch=2,
            grid=(NB,),
            in_specs=[pl.BlockSpec(memory_space=pl.ANY),
                      pl.BlockSpec((TM, LANES), lambda i, *_: (i, 0)),
                      pl.BlockSpec((TM, D_MODEL), lambda i, *_: (i, 0)),
                      _mod_spec(l),
                      pl.BlockSpec((None, 1, D_MODEL), lambda i, *_: (l, 0, 0)),
                      pl.BlockSpec((None, 1, D_MODEL), lambda i, *_: (l, 0, 0))],
            out_specs=[pl.BlockSpec((TM, D_MODEL), lambda i, *_: (jnp.minimum(i, NB_CTX - 1), 0)),
                       pl.BlockSpec((TM, D_MODEL), lambda i, *_: (jnp.maximum(i - NB_CTX, 0), 0))],
            scratch_shapes=[pltpu.VMEM((2, N_GROUPS * TM, D_MODEL), BF16), pltpu.VMEM((TM, D_MODEL), F32),
                            pltpu.SemaphoreType.DMA((2,))],
        ),
        out_shape=[jax.ShapeDtypeStruct((CTX_TOKENS, D_MODEL), F32),
                   jax.ShapeDtypeStruct((LAT_TOKENS, D_MODEL), F32)],
        compiler_params=_cp(("arbitrary",)),
        name="combine_ln2",
    )(offs, small, ys, slot, x1, mod3, g2, b2)


def _rope_tables():
    f32 = np.float32
    pos = np.arange(DEC_SEQ)
    row = (pos // GRID_W).astype(f32)
    col = (pos % GRID_W).astype(f32)
    quarter = ATTN_HEAD_DIM // 4
    inv_freq = np.power(f32(ROPE_BASE), -np.arange(quarter, dtype=f32) / f32(quarter)).astype(f32)
    j = np.arange(ATTN_HEAD_DIM)
    p = np.where((j // (2 * quarter))[None, :] == 0, row[:, None], col[:, None]).astype(f32)
    ang = (p * inv_freq[j % quarter][None, :]).astype(f32)
    sign = np.where((j % (2 * quarter)) < quarter, -1.0, 1.0).astype(f32)
    cos = np.tile(np.cos(ang).astype(f32), (1, ATTN_HEADS))
    sin = np.tile((np.sin(ang) * sign[None, :]).astype(f32), (1, ATTN_HEADS))
    cos = np.concatenate([np.ones((TM, 512), f32), cos], axis=0)
    sin = np.concatenate([np.zeros((TM, 512), f32), sin], axis=0)
    return jnp.asarray(cos), jnp.asarray(sin)


def _dup_heads(t):
    h0, h1 = t[..., :ATTN_HEAD_DIM], t[..., ATTN_HEAD_DIM:]
    return jnp.concatenate([h0, h0, h1, h1], axis=-1)


def _split_w_in(w):
    gate_scale = np.where(np.arange(IN_COLS) >= W_MG.start, 0.5, 1.0).astype(np.float32)
    return [(jnp.swapaxes(w, 1, 2) * gate_scale[:, None]).astype(BF16)]


def kernel(x_prompt, x_sample, state_gla, cache_k, cache_v, c, c_ctx, ada_w, ada_b, w_in, gla_gate_w, gla_gate_b, gla_norm_g, gmlp_ln_g, gmlp_ln_b, gmlp_ws, gmlp_bs, attn_sink, w_branch, w_out, ln1_g, ln1_b, ln2_g, ln2_b, router_group_w, router_group_b, router_expert_w, router_expert_b, expert_w_gate, expert_w_up, expert_w_down):
    alpha = (2.0 * DEPTH) ** 0.25
    xc = x_prompt.reshape(CTX_TOKENS, D_MODEL)
    xl = x_sample.reshape(LAT_TOKENS, D_MODEL)
    cond = jnp.concatenate([c_ctx[None, :], c, jnp.zeros((MOD_ROWS - 1 - DEC_BATCH, D_MODEL), F32)], axis=0)
    cos_t, sin_t = _rope_tables()

    ada_b3 = ada_b[:, None, :]
    w_parts = _split_w_in(w_in)
    zrow = jnp.zeros((DEPTH, GLA_RANK, 256), F32)
    zpad = jnp.zeros((DEPTH, LANES - 2 * GLA_RANK, 256), F32)
    gw_p = jnp.stack([jnp.concatenate([gla_gate_w[:, 0], zrow, zpad], axis=1),
                      jnp.concatenate([zrow, gla_gate_w[:, 1], zpad], axis=1)], axis=1).astype(BF16)
    gb_p = gla_gate_b[:, :, None, :]
    sink_t = jnp.broadcast_to(attn_sink[:, :, None, None], (DEPTH, ATTN_HEADS, 1, LANES))
    kc = _dup_heads(cache_k.reshape(DEC_BATCH, DEPTH, PAST_LEN, 128)).astype(BF16)
    vct = jnp.swapaxes(cache_v.reshape(DEC_BATCH, DEPTH, PAST_LEN, 128), 2, 3).astype(BF16)
    rw = jnp.swapaxes(jnp.concatenate(
        [router_expert_w, router_group_w,
         jnp.zeros((DEPTH, D_MODEL, LANES - N_EXPERTS - N_GROUPS), F32)], axis=2), 1, 2)
    rw_hi = rw.astype(BF16)
    rw_lo = (rw - rw_hi.astype(F32)).astype(BF16)
    rb = jnp.concatenate([router_expert_b, router_group_b,
                          jnp.zeros((DEPTH, LANES - N_EXPERTS - N_GROUPS), F32)], axis=1)
    rb = jnp.broadcast_to(rb[:, :, None], (DEPTH, LANES, TM))
    bs_t = jnp.repeat(jnp.swapaxes(gmlp_bs, 1, 2), GMLP_CH, axis=2)
    ws_b = gmlp_ws.astype(BF16)
    wb_b = (0.5 * w_branch).astype(BF16)
    wo_b = w_out.astype(BF16)
    row = lambda t: t[:, None, :]

    mod3 = _modulation(cond, ada_w, ada_b3).reshape(DEPTH, MOD_ROWS, 1, 6 * D_MODEL)
    states, keys, values = [], [], []
    for l in range(DEPTH):
        (qef, kef, klf, qeb, keb, klb, dec, av, avt, sag, zu, zv, cq, ckd, cvt, ck, cv, gates) = _in_projection(
            l, xc, xl, mod3, cos_t, sin_t, gw_p, gb_p, row(gmlp_ln_g), row(gmlp_ln_b), w_parts)
        o_f, o_b, s_fin = _gla(l, qef, kef, klf, qeb, keb, klb, dec, av, avt, state_gla)
        brc_ctx = _context_attention(l, cq, ckd, cvt, sink_t)
        brc_lat = _latent_attention(l, cq, ckd, cvt, kc, vct, sink_t)
        x1, slot, meta, hs = _merge(l, alpha, xc, xl, mod3, o_f, o_b, sag, row(gla_norm_g), zu, zv,
                                    ws_b, bs_t, brc_ctx, brc_lat, gates, wb_b,
                                    wo_b, row(ln1_g), row(ln1_b), rw_hi, rw_lo, rb)
        offs = meta[:, 0, :N_GROUPS].reshape(NB * N_GROUPS)
        tiles = meta[NB - 1]
        ys = _experts(l, tiles[1, :MOE_TILES], tiles[2, :MOE_TILES], tiles[3, :1], hs,
                      expert_w_gate, expert_w_up, expert_w_down)
        xc, xl = _combine(l, alpha, offs, meta[:, 0, 2 * N_GROUPS], ys, slot, x1, mod3,
                          row(ln2_g), row(ln2_b))
        states.append(s_fin)
        keys.append(ck.reshape(BATCH, SEQ, ATTN_KV_HEADS, ATTN_HEAD_DIM))
        values.append(cv.reshape(BATCH, SEQ, ATTN_KV_HEADS, ATTN_HEAD_DIM))
    return (xc.reshape(BATCH, SEQ, D_MODEL), xl.reshape(DEC_BATCH, DEC_SEQ, D_MODEL),
            jnp.stack(states, axis=1), jnp.stack(keys, axis=1), jnp.stack(values, axis=1))
```

```python
import functools

import jax
import jax.numpy as jnp
import numpy as np
from jax import lax
from jax.experimental import pallas as pl
from jax.experimental.pallas import tpu as pltpu

F32 = jnp.float32
BF16 = jnp.bfloat16

D_MODEL = 1024
BATCH = 32
SEQ = 256
DEPTH = 2
DEC_BATCH = 8
DEC_SEQ = 1024
PAST_LEN = 512
GRID_W = 64
GLA_HEADS = 4
GLA_DV = 128
GLA_DK = 64
GLA_RANK = 16
GLA_TAU = 16.0
GLA_CHUNK = 32
GMLP_CHUNK = 128
GMLP_CH = 128
GMLP_GROUPS = 4
ATTN_HEADS = 8
ATTN_KV_HEADS = 2
ATTN_GROUP = ATTN_HEADS // ATTN_KV_HEADS
ATTN_HEAD_DIM = 64
WINDOW = 128
ATTN_BLOCK = 128
ROPE_BASE = 10000.0
BRANCH_W = 512
N_BRANCH = 3
N_GROUPS = 4
EXPERTS_PER_GROUP = 4
N_EXPERTS = 16
EXPERT_FF = 256
LN_EPS = 1e-5

LANES = 128
TM = 256
CTX_TOKENS = BATCH * SEQ
LAT_TOKENS = DEC_BATCH * DEC_SEQ
TOKENS = CTX_TOKENS + LAT_TOKENS
NB_CTX = CTX_TOKENS // TM
NB_LAT = LAT_TOKENS // TM
NB = NB_CTX + NB_LAT
LAT_BLOCKS_PER_SEQ = DEC_SEQ // TM
MOD_ROWS = 16
GLA_SUB = 128
CHUNKS_PER_SUB = GLA_SUB // GLA_CHUNK
CTX_SEQS = 4
LAT_QB = 2
GLA_STAGES = 3
MERGE_ROWS = 128
MERGE_STAGES = 6
ROUTER_ROWS = 24
HS_ALIGN = 16
SMALL_WIN = 128
HS_W = D_MODEL + 3 * LANES
STAGE_ROWS = TM + N_GROUPS * HS_ALIGN + TM
_MAX_PAD = NB * (HS_ALIGN - 1)
HS_CAP = -(-(TOKENS + _MAX_PAD + 2 * TM) // TM) * TM
MOE_TILES = (TOKENS + N_GROUPS * _MAX_PAD) // TM + 2 * N_GROUPS

IN_COLS = 6432
W_AQ, W_AK, W_AV, W_AG = slice(0, 256), slice(256, 512), slice(512, 1024), slice(1024, 1536)
W_LR = slice(1536, 1664)
W_BZ, W_CQ, W_CK, W_CV = slice(1568, 2592), slice(2592, 3104), slice(3104, 3232), slice(3232, 3360)
W_MG = slice(3360, 6432)

VMEM_LIMIT = 56 * 1024 * 1024


def _cp(sem):
    return pltpu.CompilerParams(dimension_semantics=sem, vmem_limit_bytes=VMEM_LIMIT)


def _dot(a, b):
    return jnp.dot(a, b, preferred_element_type=F32)


def _dot_nt(a, b):
    return lax.dot_general(a, b, (((1,), (1,)), ((), ())), preferred_element_type=F32)


def _dot_tn(a, b):
    return lax.dot_general(a, b, (((0,), (0,)), ((), ())), preferred_element_type=F32)


def _split(x):
    hi = x.astype(BF16)
    lo = (x - hi.astype(F32)).astype(BF16)
    return hi, lo


def _sigmoid(x):
    return 0.5 * (jnp.tanh(0.5 * x) + 1.0)


def _silu(x):
    return x * _sigmoid(x)


def _layer_norm(x, g, b):
    mu = jnp.mean(x, axis=-1, keepdims=True)
    xc = x - mu
    var = jnp.mean(xc * xc, axis=-1, keepdims=True)
    return xc * lax.rsqrt(var + LN_EPS) * g + b


def _stagger(parts, n_stages):
    for step in range(n_stages + len(parts) - 1):
        for lag, part in enumerate(parts):
            if 0 <= step - lag < n_stages:
                next(part)


def _mod_block(i):
    return jnp.where(i < NB_CTX, 0, 1 + (i - NB_CTX) // LAT_BLOCKS_PER_SEQ)


def _tok(width):
    return pl.BlockSpec((TM, width), lambda i: (i, 0))


def _ctx_tok(width):
    return pl.BlockSpec((TM, width), lambda i: (jnp.minimum(i, NB_CTX - 1), 0))


def _lat_tok(width):
    return pl.BlockSpec((TM, width), lambda i: (jnp.maximum(i - NB_CTX, 0), 0))


def _layer_spec(l, shape):
    return pl.BlockSpec((None,) + shape, lambda *_: (l,) + (0,) * len(shape))


def _mod_spec(l):
    return pl.BlockSpec((None, 1, 1, 6 * D_MODEL), lambda i, *_: (l, _mod_block(i), 0, 0))


def _mod_kernel(cond_ref, w_ref, b_ref, o_ref):
    s_hi, s_lo = _split(_silu(cond_ref[...]))
    w_hi, w_lo = _split(w_ref[...])
    o_ref[...] = _dot(s_hi, w_hi) + _dot(s_lo, w_hi) + _dot(s_hi, w_lo) + b_ref[...]


def _modulation(cond, ada_w, ada_b3):
    tn = 1536
    return pl.pallas_call(
        _mod_kernel,
        grid=(DEPTH, 6 * D_MODEL // tn),
        in_specs=[pl.BlockSpec((MOD_ROWS, D_MODEL), lambda l, j: (0, 0)),
                  pl.BlockSpec((None, D_MODEL, tn), lambda l, j: (l, 0, j)),
                  pl.BlockSpec((None, 1, tn), lambda l, j: (l, 0, j))],
        out_specs=pl.BlockSpec((None, MOD_ROWS, tn), lambda l, j: (l, 0, j)),
        out_shape=jax.ShapeDtypeStruct((DEPTH, MOD_ROWS, 6 * D_MODEL), F32),
        compiler_params=_cp(("arbitrary", "arbitrary")),
        name="modulation",
    )(cond, ada_w, ada_b3)


def _rope(x, cos, sin):
    n = x.shape[1]
    lane = lax.broadcasted_iota(jnp.int32, x.shape, 1)
    partner = jnp.where((lane & 31) < 16, pltpu.roll(x, n - 16, 1), pltpu.roll(x, 16, 1))
    return x * cos + partner * sin


def _inproj_kernel(xc_ref, xl_ref, mod_ref, cos_ref, sin_ref, gw_ref, gb_ref, lg_ref, lb_ref,
                   wt_ref,
                   qef_ref, kef_ref, klf_ref, qeb_ref, keb_ref, klb_ref, dec_ref,
                   av_ref, avt_ref, sag_ref, zu_ref, zv_ref,
                   cq_ref, ckd_ref, cvt_ref, ck_ref, cv_ref, gate_ref):
    i = pl.program_id(0)
    m = mod_ref[0]
    x = jnp.where(i < NB_CTX, xc_ref[...], xl_ref[...])
    h = (x * (1.0 + m[:, D_MODEL:2 * D_MODEL]) + m[:, 0:D_MODEL]).astype(BF16)

    def proj(rows):
        return _dot_nt(h, wt_ref[rows, :])

    aq = proj(W_AQ)
    ak = proj(W_AK)
    lr = proj(W_LR).astype(BF16)
    zs = [_dot(lr, gw_ref[d]) + gb_ref[d] for d in range(2)]
    for nbr in range(N_BRANCH):
        cols = slice(W_MG.start + nbr * D_MODEL, W_MG.start + (nbr + 1) * D_MODEL)
        gate_ref[:, nbr * D_MODEL:(nbr + 1) * D_MODEL] = jnp.tanh(proj(cols)) + 1.0
    r = lax.broadcasted_iota(jnp.int32, (TM, TM), 0)
    c = lax.broadcasted_iota(jnp.int32, (TM, TM), 1)
    same = (r >> 5) == (c >> 5)
    bs = []
    for d in range(2):
        g = (jnp.minimum(zs[d], 0.0) - jnp.log(1.0 + jnp.exp(-jnp.abs(zs[d])))) / GLA_TAU
        tri_b = jnp.where(same & ((c <= r) if d == 0 else (c >= r)), 1.0, 0.0).astype(BF16)
        g_hi, g_lo = _split(g)
        bs.append(_dot(tri_b, g_hi) + _dot(tri_b, g_lo))
    z = _gelu(proj(W_BZ))
    zu_ref[...] = z[:, :BRANCH_W]
    zv_ref[...] = _layer_norm(z[:, BRANCH_W:], lg_ref[...], lb_ref[...]).astype(BF16)
    sag_ref[...] = _silu(proj(W_AG))
    av = proj(W_AV)
    av_ref[...] = av.astype(BF16)
    avt_ref[...] = av.T.astype(BF16)
    decs = []
    for d, (qe_ref, ke_ref, kl_ref) in enumerate(((qef_ref, kef_ref, klf_ref), (qeb_ref, keb_ref, klb_ref))):
        b = bs[d]
        last = GLA_CHUNK - 1 if d == 0 else 0
        tot = [b[n * GLA_CHUNK + last:n * GLA_CHUNK + last + 1, :] for n in range(TM // GLA_CHUNK)]
        bl = jnp.concatenate([jnp.broadcast_to(t, (GLA_CHUNK, t.shape[1])) for t in tot], axis=0)
        qe_ref[...] = (aq * jnp.exp(b) * (GLA_DK ** -0.5)).astype(BF16)
        ke_ref[...] = (ak * jnp.exp(-b)).astype(BF16)
        kl_ref[...] = (ak * jnp.exp(bl - b)).astype(BF16)
        decs.append(jnp.exp(jnp.concatenate(tot, axis=0)))
    dec_ref[...] = jnp.concatenate(decs, axis=1)

    cos = cos_ref[...]
    sin = sin_ref[...]
    cq_ref[...] = (_rope(proj(W_CQ), cos, sin) * (ATTN_HEAD_DIM ** -0.5)).astype(BF16)
    ck = proj(W_CK)
    cv = proj(W_CV)
    kr = _rope(ck, cos[:, :LANES], sin[:, :LANES])
    swapped = pltpu.roll(kr, ATTN_HEAD_DIM, 1)
    lane = lax.broadcasted_iota(jnp.int32, (TM, LANES), 1)
    ckd_ref[...] = jnp.concatenate([jnp.where(lane < 64, kr, swapped), jnp.where(lane < 64, swapped, kr)],
                                   axis=1).astype(BF16)
    cvt_ref[...] = cv.T.astype(BF16)

    @pl.when(i < NB_CTX)
    def _():
        ck_ref[...] = ck
        cv_ref[...] = cv


def _rope_block(i):
    return jnp.where(i < NB_CTX, 0, 1 + (i - NB_CTX) % LAT_BLOCKS_PER_SEQ)


def _in_projection(l, xc, xl, mod3, cos_t, sin_t, gw_p, gb_p, ln_g, ln_b, w_parts):
    def tok_out(width, dt):
        return _tok(width), jax.ShapeDtypeStruct((TOKENS, width), dt)

    def feat_out(width):
        return (pl.BlockSpec((width, TM), lambda i: (0, i)), jax.ShapeDtypeStruct((width, TOKENS), BF16))

    def ctx_out(width):
        return _ctx_tok(width), jax.ShapeDtypeStruct((CTX_TOKENS, width), F32)

    dec_out = (pl.BlockSpec((TM // GLA_CHUNK, 512), lambda i: (i, 0)),
               jax.ShapeDtypeStruct((TOKENS // GLA_CHUNK, 512), F32))
    outs = [tok_out(256, BF16)] * 6 + [dec_out] + [
        tok_out(512, BF16), feat_out(512), tok_out(512, F32), tok_out(512, F32), tok_out(512, BF16),
        tok_out(512, BF16), tok_out(256, BF16), feat_out(LANES), ctx_out(LANES), ctx_out(LANES),
        tok_out(3072, F32)]
    return pl.pallas_call(
        _inproj_kernel,
        grid=(NB,),
        in_specs=[_ctx_tok(D_MODEL), _lat_tok(D_MODEL), _mod_spec(l),
                  pl.BlockSpec((TM, 512), lambda i: (_rope_block(i), 0)),
                  pl.BlockSpec((TM, 512), lambda i: (_rope_block(i), 0)),
                  _layer_spec(l, (2, LANES, 256)), _layer_spec(l, (2, 1, 256)),
                  _layer_spec(l, (1, 512)), _layer_spec(l, (1, 512))]
                 + [_layer_spec(l, w.shape[1:]) for w in w_parts],
        out_specs=[o[0] for o in outs],
        out_shape=[o[1] for o in outs],
        compiler_params=_cp(("arbitrary",)),
        name="in_projection",
    )(xc, xl, mod3, cos_t, sin_t, gw_p, gb_p, ln_g, ln_b, *w_parts)


def _gla_direction(qe, ke, kl, v, vt, dec, masks, state_ref, reverse, o_ref, rows):
    tri, half_masks, chunk_mask, head_diag = masks
    tri = tri[int(reverse)]
    pairs = range(GLA_HEADS // 2)
    halves, q_exps, kvts = [], [], []
    for p in pairs:
        ps = slice(p * LANES, (p + 1) * LANES)
        qp = qe[:, ps]
        vp = v[:, p * 2 * GLA_DV:(p + 1) * 2 * GLA_DV]
        for hh in range(2):
            s = jnp.where(tri, _dot_nt(qp * half_masks[hh], ke[:, ps]), 0.0).astype(BF16)
            halves.append(_dot(s, vp[:, hh * GLA_DV:(hh + 1) * GLA_DV]))
        k_exp = jnp.concatenate([kl[:, ps]] * CHUNKS_PER_SUB, axis=1) * chunk_mask
        q_exps.append(jnp.concatenate([qp] * CHUNKS_PER_SUB, axis=1) * chunk_mask)
        kvts.append(_dot(vt[p * 2 * GLA_DV:(p + 1) * 2 * GLA_DV, :], k_exp))
    yield
    st_stacks = []
    for p in pairs:
        ps = slice(p * LANES, (p + 1) * LANES)
        st = state_ref[p]
        entering = [None] * CHUNKS_PER_SUB
        order = range(CHUNKS_PER_SUB - 1, -1, -1) if reverse else range(CHUNKS_PER_SUB)
        for n in order:
            entering[n] = st
            st = dec[n:n + 1, ps] * st + jnp.where(head_diag, kvts[p][:, n * LANES:(n + 1) * LANES], 0.0)
        state_ref[p] = st
        st_stacks.append(jnp.concatenate(entering, axis=1).astype(BF16))
    yield
    outs = []
    for p in pairs:
        inter = _dot_nt(q_exps[p], st_stacks[p])
        outs.append(halves[2 * p] + inter[:, :GLA_DV])
        outs.append(halves[2 * p + 1] + inter[:, GLA_DV:])
    o_ref[rows, :] = jnp.concatenate(outs, axis=1)
    yield


def _gla_mask_constants():
    r = np.arange(GLA_SUB)[:, None]
    c = np.arange(GLA_SUB)[None, :]
    same = (r // GLA_CHUNK) == (c // GLA_CHUNK)
    tri = np.stack([same & (c <= r), same & (c >= r)]).astype(np.float32)
    lane = np.arange(LANES)[None, :]
    halves = np.stack([np.broadcast_to(lane < 64, (GLA_SUB, LANES)),
                       np.broadcast_to(lane >= 64, (GLA_SUB, LANES))]).astype(np.float32)
    col_chunk = np.arange(CHUNKS_PER_SUB * LANES)[None, :] // LANES
    chunk_mask = ((r // GLA_CHUNK) == col_chunk).astype(np.float32)
    head_diag = ((np.arange(2 * GLA_DV)[:, None] // GLA_DV) == (lane // GLA_DK)).astype(np.float32)
    return (jnp.asarray(tri), jnp.asarray(halves, dtype=BF16), jnp.asarray(chunk_mask, dtype=BF16),
            jnp.asarray(head_diag))


def _gla_kernel(tri_ref, half_ref, cmask_ref, hdiag_ref,
                qf_ref, kef_ref, klf_ref, vf_ref, vtf_ref, decf_ref,
                qb_ref, keb_ref, klb_ref, vb_ref, vtb_ref, decb_ref,
                s0_ref, of_ref, ob_ref, sfin_ref, state_ref):
    i = pl.program_id(0)
    @pl.when(i < NB_CTX)
    def _():
        state_ref[...] = jnp.zeros_like(state_ref)

    @pl.when(jnp.logical_and(i >= NB_CTX, (i - NB_CTX) % LAT_BLOCKS_PER_SEQ == 0))
    def _():
        zero = jnp.zeros((GLA_DK, GLA_DV), F32)
        for d in range(2):
            for p in range(2):
                pair = jnp.concatenate(
                    [jnp.concatenate([s0_ref[0, d, 2 * p], zero], axis=1),
                     jnp.concatenate([zero, s0_ref[0, d, 2 * p + 1]], axis=1)], axis=0)
                state_ref[d, p] = pair.T

    masks = ((tri_ref[0] > 0.5, tri_ref[1] > 0.5), (half_ref[0], half_ref[1]), cmask_ref[...],
             hdiag_ref[...] > 0.5)
    n_sub = TM // GLA_SUB
    parts = []
    for j in range(n_sub):
        rs = slice(j * GLA_SUB, (j + 1) * GLA_SUB)
        parts.append(_gla_direction(
            qf_ref[rs, :], kef_ref[rs, :], klf_ref[rs, :], vf_ref[rs, :], vtf_ref[:, rs],
            decf_ref[j * CHUNKS_PER_SUB:(j + 1) * CHUNKS_PER_SUB, :256], masks, state_ref.at[0], False,
            of_ref, rs))
        jb = n_sub - 1 - j
        rb = slice(jb * GLA_SUB, (jb + 1) * GLA_SUB)
        parts.append(_gla_direction(
            qb_ref[rb, :], keb_ref[rb, :], klb_ref[rb, :], vb_ref[rb, :], vtb_ref[:, rb],
            decb_ref[jb * CHUNKS_PER_SUB:(jb + 1) * CHUNKS_PER_SUB, 256:], masks, state_ref.at[1], True,
            ob_ref, rb))
    _stagger(parts, GLA_STAGES)

    @pl.when(i < NB_CTX)
    def _():
        for d in range(2):
            for p in range(2):
                sp = state_ref[d, p].T
                sfin_ref[0, d, 2 * p] = sp[:GLA_DK, :GLA_DV]
                sfin_ref[0, d, 2 * p + 1] = sp[GLA_DK:, GLA_DV:]


def _bwd_block(i):
    k = (i - NB_CTX) % LAT_BLOCKS_PER_SEQ
    return jnp.where(i < NB_CTX, i, i - k + (LAT_BLOCKS_PER_SEQ - 1 - k))


def _gla(l, qef, kef, klf, qeb, keb, klb, dec, av, avt, state_gla):
    def bwd(width):
        return pl.BlockSpec((TM, width), lambda i: (_bwd_block(i), 0))

    def lat_seq(i):
        return jnp.maximum(i - NB_CTX, 0) // LAT_BLOCKS_PER_SEQ

    vt_fwd = pl.BlockSpec((512, TM), lambda i: (0, i))
    vt_bwd = pl.BlockSpec((512, TM), lambda i: (0, _bwd_block(i)))
    dec_fwd = pl.BlockSpec((TM // GLA_CHUNK, 512), lambda i: (i, 0))
    dec_bwd = pl.BlockSpec((TM // GLA_CHUNK, 512), lambda i: (_bwd_block(i), 0))
    state_dims = (2, GLA_HEADS, GLA_DK, GLA_DV)
    consts = _gla_mask_constants()
    return pl.pallas_call(
        _gla_kernel,
        grid=(NB,),
        in_specs=[pl.BlockSpec(t.shape, lambda i, nd=t.ndim: (0,) * nd) for t in consts]
                 + [_tok(256), _tok(256), _tok(256), _tok(512), vt_fwd, dec_fwd,
                  bwd(256), bwd(256), bwd(256), bwd(512), vt_bwd, dec_bwd,
                  pl.BlockSpec((1, None) + state_dims, lambda i: (lat_seq(i), l, 0, 0, 0, 0))],
        out_specs=[_tok(512), bwd(512),
                   pl.BlockSpec((1,) + state_dims, lambda i: (jnp.minimum(i, NB_CTX - 1), 0, 0, 0, 0))],
        out_shape=[jax.ShapeDtypeStruct((TOKENS, 512), F32), jax.ShapeDtypeStruct((TOKENS, 512), F32),
                   jax.ShapeDtypeStruct((BATCH,) + state_dims, F32)],
        scratch_shapes=[pltpu.VMEM((2, 2, 2 * GLA_DV, LANES), F32)],
        compiler_params=_cp(("arbitrary",)),
        name="gla_scan",
    )(*consts, qef, kef, klf, av, avt, dec, qeb, keb, klb, av, avt, dec, state_gla)


def _attend_group(q_refs, keys, values_t, masks, sink_ref, g, o_ref):
    rows = q_refs[0].shape[0]
    lane = lax.broadcasted_iota(jnp.int32, (rows, LANES), 1)
    stacked = []
    for q in q_refs:
        q32 = q.astype(F32)
        stacked.append(jnp.where(lane < 64, q32, 0.0).astype(BF16))
        stacked.append(jnp.where(lane >= 64, q32, 0.0).astype(BF16))
    qs = jnp.concatenate(stacked, axis=0)
    col_head = lax.broadcasted_iota(jnp.int32, (1, ATTN_GROUP * rows), 1) // rows
    sink = jnp.zeros((1, ATTN_GROUP * rows), F32)
    for hh in range(ATTN_GROUP):
        sink = jnp.where(col_head == hh, sink_ref[ATTN_GROUP * g + hh][:, :1], sink)
    ss = []
    m = sink
    for kk, mk in zip(keys, masks):
        s = _dot_nt(kk, qs)
        if mk is not None:
            s = jnp.where(jnp.concatenate([mk] * ATTN_GROUP, axis=1), s, -jnp.inf)
        ss.append(s)
        m = jnp.maximum(m, jnp.max(s, axis=0, keepdims=True))
    den = jnp.exp(sink - m)
    acc = jnp.zeros((ATTN_HEAD_DIM, ATTN_GROUP * rows), F32)
    for s, vt in zip(ss, values_t):
        e = jnp.exp(s - m)
        den = den + jnp.sum(e, axis=0, keepdims=True)
        acc = acc + _dot(vt, e.astype(BF16))
    o = (acc / den).astype(BF16)
    for hh in range(ATTN_GROUP):
        h = ATTN_GROUP * g + hh
        o_ref[h * ATTN_HEAD_DIM:(h + 1) * ATTN_HEAD_DIM, :] = o[:, hh * rows:(hh + 1) * rows]


def _ctx_attn_kernel(q_ref, kd_ref, vt_ref, sink_ref, o_ref):
    for j in range(CTX_SEQS):
        rows = slice(j * SEQ, (j + 1) * SEQ)
        for g in range(ATTN_KV_HEADS):
            gl = slice(g * LANES, (g + 1) * LANES)
            q_refs = [q_ref[rows, (2 * g + pp) * LANES:(2 * g + pp + 1) * LANES] for pp in range(2)]
            vt = vt_ref[g * ATTN_HEAD_DIM:(g + 1) * ATTN_HEAD_DIM, rows]
            _attend_group(q_refs, [kd_ref[rows, gl]], [vt], [None], sink_ref, g, o_ref.at[:, rows])


def _context_attention(l, cq, ckd, cvt, sink_t):
    rows = CTX_SEQS * SEQ
    return pl.pallas_call(
        _ctx_attn_kernel,
        grid=(BATCH // CTX_SEQS,),
        in_specs=[pl.BlockSpec((rows, 512), lambda b: (b, 0)),
                  pl.BlockSpec((rows, 256), lambda b: (b, 0)),
                  pl.BlockSpec((LANES, rows), lambda b: (0, b)),
                  _layer_spec(l, (ATTN_HEADS, 1, LANES))],
        out_specs=pl.BlockSpec((512, rows), lambda b: (0, b)),
        out_shape=jax.ShapeDtypeStruct((512, CTX_TOKENS), BF16),
        compiler_params=_cp(("arbitrary",)),
        name="context_attention",
    )(cq, ckd, cvt, sink_t)


def _lat_attn_kernel(q_ref, kd_ref, vt_ref, kc_ref, vct_ref, sink_ref, o_ref):
    nq = DEC_SEQ // ATTN_BLOCK
    kj = lax.broadcasted_iota(jnp.int32, (ATTN_BLOCK, ATTN_BLOCK), 0)
    qi = lax.broadcasted_iota(jnp.int32, (ATTN_BLOCK, ATTN_BLOCK), 1)
    for j in range(LAT_QB):
        n = pl.program_id(1) * LAT_QB + j
        qrows = slice(j * ATTN_BLOCK, (j + 1) * ATTN_BLOCK)
        prev0 = pl.multiple_of(jnp.maximum(n - 1, 0) * ATTN_BLOCK, ATTN_BLOCK)
        cur0 = pl.multiple_of(n * ATTN_BLOCK, ATTN_BLOCK)
        next0 = pl.multiple_of(jnp.minimum(n + 1, nq - 1) * ATTN_BLOCK, ATTN_BLOCK)
        m_prev = jnp.logical_and(kj >= qi, n > 0)
        m_next = jnp.logical_and(kj <= qi, n < nq - 1)
        for g in range(ATTN_KV_HEADS):
            gl = slice(g * LANES, (g + 1) * LANES)
            gr = slice(g * ATTN_HEAD_DIM, (g + 1) * ATTN_HEAD_DIM)
            q_refs = [q_ref[qrows, (2 * g + pp) * LANES:(2 * g + pp + 1) * LANES] for pp in range(2)]
            keys = [kd_ref[pl.ds(prev0, ATTN_BLOCK), gl], kd_ref[pl.ds(cur0, ATTN_BLOCK), gl],
                    kd_ref[pl.ds(next0, ATTN_BLOCK), gl], kc_ref[0, :, gl]]
            vals = [vt_ref[gr, pl.ds(prev0, ATTN_BLOCK)], vt_ref[gr, pl.ds(cur0, ATTN_BLOCK)],
                    vt_ref[gr, pl.ds(next0, ATTN_BLOCK)], vct_ref[0, gr, :]]
            _attend_group(q_refs, keys, vals, [m_prev, None, m_next, None], sink_ref, g,
                          o_ref.at[:, qrows])


def _latent_attention(l, cq, ckd, cvt, kc, vct, sink_t):
    rows = LAT_QB * ATTN_BLOCK
    nq = DEC_SEQ // rows
    q0 = CTX_TOKENS // rows
    s0 = CTX_TOKENS // DEC_SEQ
    return pl.pallas_call(
        _lat_attn_kernel,
        grid=(DEC_BATCH, nq),
        in_specs=[pl.BlockSpec((rows, 512), lambda b, n: (q0 + b * nq + n, 0)),
                  pl.BlockSpec((DEC_SEQ, 256), lambda b, n: (s0 + b, 0)),
                  pl.BlockSpec((LANES, DEC_SEQ), lambda b, n: (0, s0 + b)),
                  pl.BlockSpec((1, None, PAST_LEN, 256), lambda b, n: (b, l, 0, 0)),
                  pl.BlockSpec((1, None, LANES, PAST_LEN), lambda b, n: (b, l, 0, 0)),
                  _layer_spec(l, (ATTN_HEADS, 1, LANES))],
        out_specs=pl.BlockSpec((512, rows), lambda b, n: (0, b * nq + n)),
        out_shape=jax.ShapeDtypeStruct((512, LAT_TOKENS), BF16),
        compiler_params=_cp(("arbitrary", "arbitrary")),
        name="latent_attention",
    )(cq, ckd, cvt, kc, vct, sink_t)


def _gelu(x):
    return 0.5 * x * (1.0 + jnp.tanh(0.7978845608028654 * (x + 0.044715 * (x * x * x))))


def _merge_kernel(alpha, xc_ref, xl_ref, mod_ref, of_ref, ob_ref, sag_ref, ng_ref, zu_ref, zv_ref,
                  ws_ref, bs_ref, cc_ref, cl_ref, gate_ref, wb_ref, wo_ref, g1_ref, b1_ref,
                  rwh_ref, rwl_ref, rb_ref, utri_ref,
                  x1_ref, slot_ref, meta_ref, hs_ref,
                  h2_ref, cw_ref, gid_ref, stage_ref, cnt_ref, sem_ref):
    parts = [_merge_rows(alpha, slice(p * MERGE_ROWS, (p + 1) * MERGE_ROWS), xc_ref, xl_ref, mod_ref,
                         of_ref, ob_ref, sag_ref, ng_ref, zu_ref, zv_ref, ws_ref, bs_ref, cc_ref,
                         cl_ref, gate_ref, wb_ref, wo_ref, g1_ref, b1_ref, rwh_ref, rwl_ref, rb_ref,
                         x1_ref, h2_ref, cw_ref, gid_ref) for p in range(TM // MERGE_ROWS)]
    _stagger(parts, MERGE_STAGES)
    _dispatch(h2_ref, cw_ref, gid_ref, utri_ref, slot_ref, meta_ref, hs_ref, stage_ref, cnt_ref, sem_ref)


def _aligned(row):
    return row if isinstance(row, int) else pl.multiple_of(row, HS_ALIGN)


def _window_copy(stage_ref, buf, src_row, hs_ref, dst_row, sem_ref, rows):
    return pltpu.make_async_copy(stage_ref.at[buf, pl.ds(_aligned(src_row), rows), :],
                                 hs_ref.at[pl.ds(_aligned(dst_row), rows), :], sem_ref.at[0])


def _by_window(small, fn):
    pl.when(small)(lambda: fn(SMALL_WIN))
    pl.when(jnp.logical_not(small))(lambda: fn(TM))


def _dispatch(h2_ref, cw_ref, gid_ref, utri_ref, slot_ref, meta_ref, hs_ref, stage_ref, cnt_ref, sem_ref):
    i = pl.program_id(0)

    @pl.when(i == 0)
    def _():
        for g in range(N_GROUPS + 1):
            cnt_ref[g] = 0

    gid = gid_ref[0:1, :]
    grp_i = lax.broadcasted_iota(jnp.int32, (8, TM), 0)
    onehot = jnp.where(grp_i.astype(F32) == gid, 1.0, 0.0)
    rank = _dot(onehot.astype(BF16), utri_ref[...])
    ncol = (rank[:, TM - 1:TM] + onehot[:, TM - 1:TM]).astype(jnp.int32)
    padc = ((ncol + (HS_ALIGN - 1)) // HS_ALIGN) * HS_ALIGN
    row8 = lax.broadcasted_iota(jnp.int32, (8, 1), 0)
    startc = jnp.zeros((8, 1), jnp.int32)
    run = jnp.zeros((1, 1), jnp.int32)
    for g in range(1, N_GROUPS):
        run = run + padc[g - 1:g, :]
        startc = jnp.where(row8 == g, run, startc)
    slot = jnp.sum(onehot * (rank + startc.astype(F32)), axis=0, keepdims=True)
    back = jnp.sum(onehot * (rank + (row8 * TM).astype(F32)), axis=0, keepdims=True)
    slot_ref[...] = jnp.broadcast_to(back, (LANES, TM)).T
    sel = jnp.where(lax.broadcasted_iota(jnp.int32, (STAGE_ROWS, TM), 0).astype(F32) == slot, 1.0, 0.0)
    cw = cw_ref[...]
    cw_hi = cw.astype(BF16)
    r1 = cw - cw_hi.astype(F32)
    cw_mid = r1.astype(BF16)
    cw_lo = (r1 - cw_mid.astype(F32)).astype(BF16)
    rowdata = jnp.concatenate([h2_ref[...], cw_hi, cw_mid, cw_lo], axis=1)
    buf = i % 2
    stage_ref[buf] = _dot(sel.astype(BF16), rowdata).astype(BF16)

    small = jnp.max(ncol, axis=0, keepdims=True)[0, 0] <= SMALL_WIN
    prev_small = cnt_ref[N_GROUPS] == 1

    def wait_windows(rows):
        for _ in range(N_GROUPS):
            _window_copy(stage_ref, buf, 0, hs_ref, 0, sem_ref, rows).wait()

    @pl.when(i > 0)
    def _():
        _by_window(prev_small, wait_windows)

    lane = lax.broadcasted_iota(jnp.int32, (8, LANES), 1)
    meta = jnp.where(lane == 2 * N_GROUPS, small.astype(jnp.int32), jnp.zeros((8, LANES), jnp.int32))
    starts, dsts = [], []
    for g in range(N_GROUPS):
        starts.append(startc[g, 0])
        dsts.append(g * HS_CAP + cnt_ref[g])
        cnt_ref[g] = cnt_ref[g] + padc[g, 0]
        meta = jnp.where(lane == g, dsts[g], meta)
        meta = jnp.where(lane == N_GROUPS + g, cnt_ref[g], meta)
    meta_ref[0] = meta
    cnt_ref[N_GROUPS] = small.astype(jnp.int32)

    def start_windows(rows):
        for g in range(N_GROUPS):
            _window_copy(stage_ref, buf, starts[g], hs_ref, dsts[g], sem_ref, rows).start()

    _by_window(small, start_windows)

    @pl.when(i == NB - 1)
    def _():
        _by_window(small, wait_windows)
        for g in range(N_GROUPS):
            for w in range(2):
                _window_copy(stage_ref, buf, STAGE_ROWS - TM, hs_ref, g * HS_CAP + cnt_ref[g] + w * TM,
                             sem_ref, TM).start()
        for _ in range(2 * N_GROUPS):
            _window_copy(stage_ref, buf, 0, hs_ref, 0, sem_ref, TM).wait()
        ends, total = [], 0
        for g in range(N_GROUPS):
            total = total + lax.shift_right_logical(cnt_ref[g] + (2 * TM - 1), TM.bit_length() - 1)
            ends.append(total)
        step = jnp.minimum(lax.broadcasted_iota(jnp.int32, (1, LANES), 1), total - 1)
        grp = jnp.zeros((1, LANES), jnp.int32)
        first = jnp.zeros((1, LANES), jnp.int32)
        for g in range(N_GROUPS - 1):
            grp = grp + jnp.where(step >= ends[g], 1, 0)
            first = jnp.where(step >= ends[g], ends[g], first)
        meta_ref[0, 1:2, :] = grp * (HS_CAP // TM) + step - first
        meta_ref[0, 2:3, :] = grp
        meta_ref[0, 3:4, :] = jnp.zeros((1, LANES), jnp.int32) + total


def _merge_rows(alpha, rows, xc_ref, xl_ref, mod_ref, of_ref, ob_ref, sag_ref, ng_ref, zu_ref, zv_ref,
                ws_ref, bs_ref, cc_ref, cl_ref, gate_ref, wb_ref, wo_ref, g1_ref, b1_ref,
                rwh_ref, rwl_ref, rb_ref, x1_ref, h2_ref, cw_ref, gid_ref):
    i = pl.program_id(0)
    m = mod_ref[0]
    n_rows = rows.stop - rows.start
    o = of_ref[rows, :] + ob_ref[rows, :]
    parts = []
    for h in range(GLA_HEADS):
        oh = o[:, h * GLA_DV:(h + 1) * GLA_DV]
        parts.append(oh * lax.rsqrt(jnp.mean(oh * oh, axis=-1, keepdims=True) + LN_EPS))
    br_a = (jnp.concatenate(parts, axis=1) * ng_ref[...] * sag_ref[rows, :]).astype(BF16)
    u = zu_ref[rows, :]
    v = zv_ref[rows, :]
    yield
    chunks = []
    for c in range(n_rows // GMLP_CHUNK):
        vc = v[c * GMLP_CHUNK:(c + 1) * GMLP_CHUNK]
        cols = [_dot(ws_ref[g], vc[:, g * GMLP_CH:(g + 1) * GMLP_CH]) for g in range(GMLP_GROUPS)]
        chunks.append(jnp.concatenate(cols, axis=1) + bs_ref[...])
    br_b = (u * jnp.concatenate(chunks, axis=0)).astype(BF16)
    br_ct = jnp.where(i < NB_CTX, cc_ref[:, rows], cl_ref[:, rows])
    projs = (_dot(br_a, wb_ref[0]), _dot(br_b, wb_ref[1]), _dot_tn(br_ct, wb_ref[2]))
    yield
    y = jnp.zeros((n_rows, D_MODEL), F32)
    for nbr, proj in enumerate(projs):
        y = y + gate_ref[rows, nbr * D_MODEL:(nbr + 1) * D_MODEL] * proj
    y = y.astype(BF16)
    yield
    y = _dot(y, wo_ref[...])
    yield
    x = jnp.where(i < NB_CTX, xc_ref[rows, :], xl_ref[rows, :])
    x1 = _layer_norm(alpha * x + m[:, 2 * D_MODEL:3 * D_MODEL] * y, g1_ref[...], b1_ref[...])
    x1_ref[rows, :] = x1
    h2 = x1 * (1.0 + m[:, 4 * D_MODEL:5 * D_MODEL]) + m[:, 3 * D_MODEL:4 * D_MODEL]
    h2_ref[rows, :] = h2.astype(BF16)
    yield
    h_hi, h_lo = _split(h2)
    logits = (_dot_nt(rwh_ref[...], h_hi) + _dot_nt(rwh_ref[...], h_lo) + _dot_nt(rwl_ref[...], h_hi)
              + rb_ref[:, :n_rows])[:ROUTER_ROWS]
    row_i = lax.broadcasted_iota(jnp.int32, (ROUTER_ROWS, n_rows), 0)
    row = row_i.astype(F32)
    row_group = (row_i >> 2).astype(F32)
    big = float(LANES)
    neg = -jnp.inf
    gl = jnp.where((row_i >= N_EXPERTS) & (row_i < N_EXPERTS + N_GROUPS), logits, neg)
    gmax = jnp.max(gl, axis=0, keepdims=True)
    gsum = jnp.sum(jnp.exp(gl - gmax), axis=0, keepdims=True)
    g_p = 1.0 / gsum
    g_i = jnp.min(jnp.where(gl == gmax, row, big), axis=0, keepdims=True) - float(N_EXPERTS)
    in_group = (row_i < N_EXPERTS) & (row_group == g_i)
    el = jnp.where(in_group, logits, neg)
    emax = jnp.max(el, axis=0, keepdims=True)
    ee = jnp.exp(el - emax)
    e_prob = ee / jnp.sum(ee, axis=0, keepdims=True)
    p1 = jnp.max(jnp.where(in_group, e_prob, neg), axis=0, keepdims=True)
    i1 = jnp.min(jnp.where(in_group & (e_prob == p1), row, big), axis=0, keepdims=True)
    rest = in_group & (row != i1)
    p2 = jnp.max(jnp.where(rest, e_prob, neg), axis=0, keepdims=True)
    i2 = jnp.min(jnp.where(rest & (e_prob == p2), row, big), axis=0, keepdims=True)
    tot = p1 + p2
    cw_t = (jnp.where(row == i1, g_p * p1 / tot, 0.0) + jnp.where(row == i2, g_p * p2 / tot, 0.0))
    cw_ref[rows, :] = jnp.concatenate([cw_t, jnp.zeros((LANES - ROUTER_ROWS, n_rows), F32)], axis=0).T
    gid_ref[0:1, rows] = g_i
    yield


def _merge(l, alpha, xc, xl, mod3, o_f, o_b, sag, norm_g, zu, zv, ws, bs_t, brc_ctx, brc_lat, gates,
           wb, wo, g1, b1, rw_hi, rw_lo, rb):
    utri = jnp.asarray(np.arange(TM)[:, None] < np.arange(TM)[None, :], dtype=BF16)
    return pl.pallas_call(
        functools.partial(_merge_kernel, alpha),
        grid=(NB,),
        in_specs=[_ctx_tok(D_MODEL), _lat_tok(D_MODEL), _mod_spec(l),
                  _tok(512), _tok(512), _tok(512), _layer_spec(l, (1, 512)),
                  _tok(512), _tok(512),
                  _layer_spec(l, (GMLP_GROUPS, GMLP_CHUNK, GMLP_CHUNK)), _layer_spec(l, (GMLP_CHUNK, 512)),
                  pl.BlockSpec((512, TM), lambda i: (0, jnp.minimum(i, NB_CTX - 1))),
                  pl.BlockSpec((512, TM), lambda i: (0, jnp.maximum(i - NB_CTX, 0))),
                  _tok(3072), _layer_spec(l, (N_BRANCH, BRANCH_W, D_MODEL)), _layer_spec(l, (D_MODEL, D_MODEL)),
                  _layer_spec(l, (1, D_MODEL)), _layer_spec(l, (1, D_MODEL)),
                  _layer_spec(l, (LANES, D_MODEL)), _layer_spec(l, (LANES, D_MODEL)), _layer_spec(l, (LANES, TM)),
                  pl.BlockSpec((TM, TM), lambda i: (0, 0))],
        out_specs=[_tok(D_MODEL), _tok(LANES), pl.BlockSpec((1, 8, LANES), lambda i: (i, 0, 0)),
                   pl.BlockSpec(memory_space=pl.ANY)],
        out_shape=[jax.ShapeDtypeStruct((TOKENS, D_MODEL), F32),
                   jax.ShapeDtypeStruct((TOKENS, LANES), F32),
                   jax.ShapeDtypeStruct((NB, 8, LANES), jnp.int32),
                   jax.ShapeDtypeStruct((N_GROUPS * HS_CAP, HS_W), BF16)],
        scratch_shapes=[pltpu.VMEM((TM, D_MODEL), BF16), pltpu.VMEM((TM, LANES), F32), pltpu.VMEM((8, TM), F32),
                        pltpu.VMEM((2, STAGE_ROWS, HS_W), BF16), pltpu.SMEM((N_GROUPS + 1,), jnp.int32),
                        pltpu.SemaphoreType.DMA((1,))],
        compiler_params=_cp(("arbitrary",)),
        name="merge_ln1_router",
    )(xc, xl, mod3, o_f, o_b, sag, norm_g, zu, zv, ws, bs_t, brc_ctx, brc_lat, gates,
      wb, wo, g1, b1, rw_hi, rw_lo, rb, utri)


def _experts_kernel(blk_ref, grp_ref, ntile_ref, hs_ref, wg32_ref, wu32_ref, wd32_ref, ys_ref,
                    wg_ref, wu_ref, wd_ref):
    s = pl.program_id(0)

    @pl.when(jnp.logical_or(s == 0, grp_ref[s] != grp_ref[jnp.maximum(s - 1, 0)]))
    def _():
        wg_ref[...] = wg32_ref[...].astype(BF16)
        wu_ref[...] = wu32_ref[...].astype(BF16)
        wd_ref[...] = wd32_ref[...].astype(BF16)

    @pl.when(s < ntile_ref[0])
    def _():
        g = grp_ref[s]
        x = hs_ref[:, :D_MODEL]
        cw = (hs_ref[:, D_MODEL:D_MODEL + LANES].astype(F32)
              + hs_ref[:, D_MODEL + LANES:D_MODEL + 2 * LANES].astype(F32)
              + hs_ref[:, D_MODEL + 2 * LANES:].astype(F32))
        lane = lax.broadcasted_iota(jnp.int32, (TM, LANES), 1)
        hid = []
        for k in range(EXPERTS_PER_GROUP):
            w_e = jnp.sum(jnp.where(lane == g * EXPERTS_PER_GROUP + k, cw, 0.0), axis=-1, keepdims=True)
            hid.append((_silu(_dot(x, wg_ref[k])) * _dot(x, wu_ref[k]) * w_e).astype(BF16))
        y = _dot(jnp.concatenate(hid, axis=1), wd_ref[...].reshape(EXPERTS_PER_GROUP * EXPERT_FF, D_MODEL))
        ys_ref[...] = y.astype(BF16)


def _experts(l, tile_blk, tile_grp, n_tiles, hs, wg, wu, wd):
    e = EXPERTS_PER_GROUP
    return pl.pallas_call(
        _experts_kernel,
        grid_spec=pltpu.PrefetchScalarGridSpec(
            num_scalar_prefetch=3,
            grid=(MOE_TILES,),
            in_specs=[pl.BlockSpec((TM, HS_W), lambda s, blk, grp, nt: (blk[s], 0)),
                      pl.BlockSpec((None, e, D_MODEL, EXPERT_FF), lambda s, blk, grp, nt: (l, grp[s], 0, 0)),
                      pl.BlockSpec((None, e, D_MODEL, EXPERT_FF), lambda s, blk, grp, nt: (l, grp[s], 0, 0)),
                      pl.BlockSpec((None, e, EXPERT_FF, D_MODEL), lambda s, blk, grp, nt: (l, grp[s], 0, 0))],
            out_specs=pl.BlockSpec((TM, D_MODEL), lambda s, blk, grp, nt: (blk[s], 0)),
            scratch_shapes=[pltpu.VMEM((e, D_MODEL, EXPERT_FF), BF16), pltpu.VMEM((e, D_MODEL, EXPERT_FF), BF16),
                            pltpu.VMEM((e, EXPERT_FF, D_MODEL), BF16)],
        ),
        out_shape=jax.ShapeDtypeStruct((N_GROUPS * HS_CAP, D_MODEL), BF16),
        compiler_params=_cp(("arbitrary",)),
        name="moe_experts",
    )(tile_blk, tile_grp, n_tiles, hs, wg, wu, wd)


def _window_fetch(ys_ref, offs_ref, blk, win_ref, buf, sem_ref, rows):
    return [pltpu.make_async_copy(
        ys_ref.at[pl.ds(pl.multiple_of(offs_ref[blk * N_GROUPS + g], HS_ALIGN), rows), :],
        win_ref.at[buf, pl.ds(g * rows, rows), :], sem_ref.at[buf]) for g in range(N_GROUPS)]


def _combine_kernel(alpha, offs_ref, small_ref, ys_ref, slot_ref, x1_ref, mod_ref, g2_ref, b2_ref,
                    oc_ref, ol_ref, win_ref, y_ref, sem_ref):
    i = pl.program_id(0)
    buf = i % 2

    def start(blk, to_buf):
        def go(rows):
            for cp in _window_fetch(ys_ref, offs_ref, blk, win_ref, to_buf, sem_ref, rows):
                cp.start()
        _by_window(small_ref[blk] == 1, go)

    pl.when(i == 0)(lambda: start(0, 0))
    pl.when(i + 1 < NB)(lambda: start(i + 1, 1 - buf))

    def gather(rows):
        for cp in _window_fetch(ys_ref, offs_ref, i, win_ref, buf, sem_ref, rows):
            cp.wait()
        back = slot_ref[...]
        packed = back - jnp.floor(back * (1.0 / TM)) * float(TM - rows)
        slot = jnp.concatenate([packed] * (N_GROUPS * rows // LANES), axis=1)
        lane = lax.broadcasted_iota(jnp.int32, (TM, N_GROUPS * rows), 1).astype(F32)
        sel = jnp.where(lane == slot, 1.0, 0.0).astype(BF16)
        y_ref[...] = _dot(sel, win_ref[buf, :N_GROUPS * rows, :])

    _by_window(small_ref[i] == 1, gather)
    m = mod_ref[0]
    out = _layer_norm(alpha * x1_ref[...] + m[:, 5 * D_MODEL:6 * D_MODEL] * y_ref[...],
                      g2_ref[...], b2_ref[...])

    @pl.when(i < NB_CTX)
    def _():
        oc_ref[...] = out

    @pl.when(i >= NB_CTX)
    def _():
        ol_ref[...] = out


def _combine(l, alpha, offs, small, ys, slot, x1, mod3, g2, b2):
    return pl.pallas_call(
        functools.partial(_combine_kernel, alpha),
        grid_spec=pltpu.PrefetchScalarGridSpec(
            num_scalar_prefetch=2,
            grid=(NB,),
            in_specs=[pl.BlockSpec(memory_space=pl.ANY),
                      pl.BlockSpec((TM, LANES), lambda i, *_: (i, 0)),
                      pl.BlockSpec((TM, D_MODEL), lambda i, *_: (i, 0)),
                      _mod_spec(l),
                      pl.BlockSpec((None, 1, D_MODEL), lambda i, *_: (l, 0, 0)),
                      pl.BlockSpec((None, 1, D_MODEL), lambda i, *_: (l, 0, 0))],
            out_specs=[pl.BlockSpec((TM, D_MODEL), lambda i, *_: (jnp.minimum(i, NB_CTX - 1), 0)),
                       pl.BlockSpec((TM, D_MODEL), lambda i, *_: (jnp.maximum(i - NB_CTX, 0), 0))],
            scratch_shapes=[pltpu.VMEM((2, N_GROUPS * TM, D_MODEL), BF16), pltpu.VMEM((TM, D_MODEL), F32),
                            pltpu.SemaphoreType.DMA((2,))],
        ),
        out_shape=[jax.ShapeDtypeStruct((CTX_TOKENS, D_MODEL), F32),
                   jax.ShapeDtypeStruct((LAT_TOKENS, D_MODEL), F32)],
        compiler_params=_cp(("arbitrary",)),
        name="combine_ln2",
    )(offs, small, ys, slot, x1, mod3, g2, b2)


def _rope_tables():
    f32 = np.float32
    pos = np.arange(DEC_SEQ)
    row = (pos // GRID_W).astype(f32)
    col = (pos % GRID_W).astype(f32)
    quarter = ATTN_HEAD_DIM // 4
    inv_freq = np.power(f32(ROPE_BASE), -np.arange(quarter, dtype=f32) / f32(quarter)).astype(f32)
    j = np.arange(ATTN_HEAD_DIM)
    p = np.where((j // (2 * quarter))[None, :] == 0, row[:, None], col[:, None]).astype(f32)
    ang = (p * inv_freq[j % quarter][None, :]).astype(f32)
    sign = np.where((j % (2 * quarter)) < quarter, -1.0, 1.0).astype(f32)
    cos = np.tile(np.cos(ang).astype(f32), (1, ATTN_HEADS))
    sin = np.tile((np.sin(ang) * sign[None, :]).astype(f32), (1, ATTN_HEADS))
    cos = np.concatenate([np.ones((TM, 512), f32), cos], axis=0)
    sin = np.concatenate([np.zeros((TM, 512), f32), sin], axis=0)
    return jnp.asarray(cos), jnp.asarray(sin)


def _dup_heads(t):
    h0, h1 = t[..., :ATTN_HEAD_DIM], t[..., ATTN_HEAD_DIM:]
    return jnp.concatenate([h0, h0, h1, h1], axis=-1)


def _split_w_in(w):
    gate_scale = np.where(np.arange(IN_COLS) >= W_MG.start, 0.5, 1.0).astype(np.float32)
    return [(jnp.swapaxes(w, 1, 2) * gate_scale[:, None]).astype(BF16)]


def kernel(x_prompt, x_sample, state_gla, cache_k, cache_v, c, c_ctx, ada_w, ada_b, w_in, gla_gate_w, gla_gate_b, gla_norm_g, gmlp_ln_g, gmlp_ln_b, gmlp_ws, gmlp_bs, attn_sink, w_branch, w_out, ln1_g, ln1_b, ln2_g, ln2_b, router_group_w, router_group_b, router_expert_w, router_expert_b, expert_w_gate, expert_w_up, expert_w_down):
    alpha = (2.0 * DEPTH) ** 0.25
    xc = x_prompt.reshape(CTX_TOKENS, D_MODEL)
    xl = x_sample.reshape(LAT_TOKENS, D_MODEL)
    cond = jnp.concatenate([c_ctx[None, :], c, jnp.zeros((MOD_ROWS - 1 - DEC_BATCH, D_MODEL), F32)], axis=0)
    cos_t, sin_t = _rope_tables()

    ada_b3 = ada_b[:, None, :]
    w_parts = _split_w_in(w_in)
    zrow = jnp.zeros((DEPTH, GLA_RANK, 256), F32)
    zpad = jnp.zeros((DEPTH, LANES - 2 * GLA_RANK, 256), F32)
    gw_p = jnp.stack([jnp.concatenate([gla_gate_w[:, 0], zrow, zpad], axis=1),
                      jnp.concatenate([zrow, gla_gate_w[:, 1], zpad], axis=1)], axis=1).astype(BF16)
    gb_p = gla_gate_b[:, :, None, :]
    sink_t = jnp.broadcast_to(attn_sink[:, :, None, None], (DEPTH, ATTN_HEADS, 1, LANES))
    kc = _dup_heads(cache_k.reshape(DEC_BATCH, DEPTH, PAST_LEN, 128)).astype(BF16)
    vct = jnp.swapaxes(cache_v.reshape(DEC_BATCH, DEPTH, PAST_LEN, 128), 2, 3).astype(BF16)
    rw = jnp.swapaxes(jnp.concatenate(
        [router_expert_w, router_group_w,
         jnp.zeros((DEPTH, D_MODEL, LANES - N_EXPERTS - N_GROUPS), F32)], axis=2), 1, 2)
    rw_hi = rw.astype(BF16)
    rw_lo = (rw - rw_hi.astype(F32)).astype(BF16)
    rb = jnp.concatenate([router_expert_b, router_group_b,
                          jnp.zeros((DEPTH, LANES - N_EXPERTS - N_GROUPS), F32)], axis=1)
    rb = jnp.broadcast_to(rb[:, :, None], (DEPTH, LANES, TM))
    bs_t = jnp.repeat(jnp.swapaxes(gmlp_bs, 1, 2), GMLP_CH, axis=2)
    ws_b = gmlp_ws.astype(BF16)
    wb_b = (0.5 * w_branch).astype(BF16)
    wo_b = w_out.astype(BF16)
    row = lambda t: t[:, None, :]

    mod3 = _modulation(cond, ada_w, ada_b3).reshape(DEPTH, MOD_ROWS, 1, 6 * D_MODEL)
    states, keys, values = [], [], []
    for l in range(DEPTH):
        (qef, kef, klf, qeb, keb, klb, dec, av, avt, sag, zu, zv, cq, ckd, cvt, ck, cv, gates) = _in_projection(
            l, xc, xl, mod3, cos_t, sin_t, gw_p, gb_p, row(gmlp_ln_g), row(gmlp_ln_b), w_parts)
        o_f, o_b, s_fin = _gla(l, qef, kef, klf, qeb, keb, klb, dec, av, avt, state_gla)
        brc_ctx = _context_attention(l, cq, ckd, cvt, sink_t)
        brc_lat = _latent_attention(l, cq, ckd, cvt, kc, vct, sink_t)
        x1, slot, meta, hs = _merge(l, alpha, xc, xl, mod3, o_f, o_b, sag, row(gla_norm_g), zu, zv,
                                    ws_b, bs_t, brc_ctx, brc_lat, gates, wb_b,
                                    wo_b, row(ln1_g), row(ln1_b), rw_hi, rw_lo, rb)
        offs = meta[:, 0, :N_GROUPS].reshape(NB * N_GROUPS)
        tiles = meta[NB - 1]
        ys = _experts(l, tiles[1, :MOE_TILES], tiles[2, :MOE_TILES], tiles[3, :1], hs,
                      expert_w_gate, expert_w_up, expert_w_down)
        xc, xl = _combine(l, alpha, offs, meta[:, 0, 2 * N_GROUPS], ys, slot, x1, mod3,
                          row(ln2_g), row(ln2_b))
        states.append(s_fin)
        keys.append(ck.reshape(BATCH, SEQ, ATTN_KV_HEADS, ATTN_HEAD_DIM))
        values.append(cv.reshape(BATCH, SEQ, ATTN_KV_HEADS, ATTN_HEAD_DIM))
    return (xc.reshape(BATCH, SEQ, D_MODEL), xl.reshape(DEC_BATCH, DEC_SEQ, D_MODEL),
            jnp.stack(states, axis=1), jnp.stack(keys, axis=1), jnp.stack(values, axis=1))
```
